```python
import math
import jax
import jax.numpy as jnp
from jax import lax
import numpy as np

D_MODEL = 1024
BATCH = 4
SEQ = 8192
DEPTH = 2

GRID_W = 64
CTX_LEN = 256
EPS = 1e-6
LRU_WIDTH = D_MODEL // 2
LRU_BLOCKS = 8
LRU_BLOCK_W = LRU_WIDTH // LRU_BLOCKS
LRU_C = 8.0
CONV_W = 4
CONV_LEFT = 2
ATT_HEADS = 4
ATT_DH = 64
ATT_DV = 2 * ATT_DH
ATT_QK_W = ATT_HEADS * 2 * ATT_DH
ATT_WIDTH = ATT_HEADS * ATT_DV
Q_BLOCK = 128
ROPE_BASE = 10000.0
EVEN_IN = 2 * LRU_WIDTH + 2 * ATT_QK_W + ATT_WIDTH
EVEN_MIX = LRU_WIDTH + ATT_WIDTH
S5_GROUP = 16
S5_GROUPS = D_MODEL // S5_GROUP
S5_STATE = 64
S5_CHUNK = 128
N_EXPERTS = 64
TOP_K = 6
D_EXPERT = 256
ROUTED_SCALE = 2.5
MOE_BLOCK = 128

kernel_name = 'hybrid_rglru_diffattn_s5_moe_prefix_dit'


def rms_norm(x, g):
    xf = x.astype(jnp.float32)
    y = xf * lax.rsqrt(jnp.mean(xf * xf, axis=-1, keepdims=True) + EPS)
    return (y * g.astype(jnp.float32)).astype(x.dtype)


def modulate(x, shift, scale):
    return x * (1 + scale) + shift


def swiglu(x, w_gate, w_up, w_down):
    return (jax.nn.silu(x @ w_gate) * (x @ w_up)) @ w_down


def linear_scan(a, b, h0, reverse):
    first = -1 if reverse else 0
    b = b.at[:, first].add(a[:, first] * h0)

    def combine(left, right):
        a_l, b_l = left
        a_r, b_r = right
        return a_l * a_r, a_r * b_l + b_r

    _, h = lax.associative_scan(combine, (a, b), axis=1, reverse=reverse)
    return h, (h[:, 0] if reverse else h[:, -1])


def centred_depthwise_conv(x, w, b):
    n = x.shape[1]
    xp = jnp.pad(x, ((0, 0), (CONV_LEFT, CONV_W - 1 - CONV_LEFT), (0, 0)))
    out = xp[:, 0:n] * w[0]
    for k in range(1, CONV_W):
        out = out + xp[:, k:k + n] * w[k]
    return out + b


def rglru_coeffs(x, lam, r_w, r_b, i_w, i_b):
    xf = x.astype(jnp.float32)
    xb = xf.reshape(xf.shape[:-1] + (LRU_BLOCKS, LRU_BLOCK_W))

    def block_diag(w, bias):
        y = jnp.einsum('blnd,nde->blne', xb, w.astype(jnp.float32))
        return y.reshape(xf.shape) + bias.astype(jnp.float32)

    r = jax.nn.sigmoid(block_diag(r_w, r_b))
    i = jax.nn.sigmoid(block_diag(i_w, i_b))
    log_a = -LRU_C * r * jax.nn.softplus(-lam.astype(jnp.float32))
    a = jnp.exp(log_a)
    b = jnp.sqrt(-jnp.expm1(2.0 * log_a)) * (i * xf)
    return a, b


def bi_rglru(x_lat, x_ctx, lam, r_w, r_b, i_w, i_b):
    y_lat = 0.0
    y_ctx = 0.0
    for d, rev in enumerate((False, True)):
        a_c, b_c = rglru_coeffs(x_ctx, lam[d], r_w[d], r_b[d], i_w[d], i_b[d])
        h0 = jnp.zeros((x_ctx.shape[0], LRU_WIDTH), jnp.float32)
        h_c, h_last = linear_scan(a_c, b_c, h0, rev)
        a_l, b_l = rglru_coeffs(x_lat, lam[d], r_w[d], r_b[d], i_w[d], i_b[d])
        h_l, _ = linear_scan(a_l, b_l, h_last, rev)
        y_lat = y_lat + h_l
        y_ctx = y_ctx + h_c
    return y_lat, y_ctx


def axial_rope_tables(rows):
    r, col = jnp.meshgrid(jnp.arange(rows), jnp.arange(GRID_W), indexing='ij')
    pos = jnp.stack([r.reshape(-1), col.reshape(-1)], axis=-1).astype(jnp.float32)
    n_freq = ATT_DH // 4
    inv_freq = ROPE_BASE ** (-jnp.arange(n_freq, dtype=jnp.float32) / n_freq)
    ang = pos[:, :, None] * inv_freq
    return jnp.cos(ang), jnp.sin(ang)


def apply_axial_rope(x, cos, sin):
    shp = x.shape
    xr = x.reshape(shp[:-1] + (2, 2, ATT_DH // 4))
    x1, x2 = xr[..., 0, :], xr[..., 1, :]
    c = cos[:, None, None].astype(x.dtype)
    s = sin[:, None, None].astype(x.dtype)
    return jnp.stack([x1 * c - x2 * s, x2 * c + x1 * s], axis=-2).reshape(shp)


def diff_attend(q, k, v, lam):
    s = jnp.einsum('bqhmd,bkhmd->bhmqk', q, k).astype(jnp.float32) * (ATT_DH ** -0.5)
    p = jax.nn.softmax(s, axis=-1)
    p = p[:, :, 0] - lam * p[:, :, 1]
    return jnp.einsum('bhqk,bkhd->bqhd', p.astype(v.dtype), v)


def rglru_diffattn_mixer(h, hc, w_in, w_out, conv_w, conv_b, lru_lam, lru_r_w, lru_r_b, lru_i_w, lru_i_b,
                         q_g, k_g, lam_q1, lam_k1, lam_q2, lam_k2, subln_g, lam_init, cos, sin, need_ctx):
    f32 = jnp.float32
    bsz, n, _ = h.shape
    n_c = hc.shape[1]
    cuts = [LRU_WIDTH, 2 * LRU_WIDTH, 2 * LRU_WIDTH + ATT_QK_W, 2 * LRU_WIDTH + 2 * ATT_QK_W]
    gate, xa, q, k, v = jnp.split(h @ w_in, cuts, axis=-1)
    gate_c, xa_c, q_c, k_c, v_c = jnp.split(hc @ w_in, cuts, axis=-1)
    xa = centred_depthwise_conv(xa, conv_w, conv_b)
    xa_c = centred_depthwise_conv(xa_c, conv_w, conv_b)
    r_lat, r_ctx = bi_rglru(xa, xa_c, lru_lam, lru_r_w, lru_r_b, lru_i_w, lru_i_b)
    ya = (r_lat * jax.nn.gelu(gate.astype(f32))).astype(h.dtype)
    split_heads = lambda t: t.reshape(t.shape[:2] + (ATT_HEADS, 2, ATT_DH))
    q = apply_axial_rope(rms_norm(split_heads(q), q_g), cos, sin)
    k = apply_axial_rope(rms_norm(split_heads(k), k_g), cos, sin)
    k_c = rms_norm(split_heads(k_c), k_g)
    v = v.reshape(bsz, n, ATT_HEADS, ATT_DV)
    v_c = v_c.reshape(bsz, n_c, ATT_HEADS, ATT_DV)
    lam = (jnp.exp(jnp.sum(lam_q1.astype(f32) * lam_k1.astype(f32)))
           - jnp.exp(jnp.sum(lam_q2.astype(f32) * lam_k2.astype(f32))) + lam_init)
    k_all = jnp.concatenate([k_c, k], axis=1)
    v_all = jnp.concatenate([v_c, v], axis=1)
    n_blk = n // Q_BLOCK
    q_blocks = q.reshape(bsz, n_blk, Q_BLOCK, ATT_HEADS, 2, ATT_DH).swapaxes(0, 1)
    o = lax.map(lambda qb: diff_attend(qb, k_all, v_all, lam), q_blocks)
    o = o.swapaxes(0, 1).reshape(bsz, n, ATT_HEADS, ATT_DV)
    yb = (rms_norm(o, subln_g) * (1.0 - lam_init)).reshape(bsz, n, ATT_WIDTH)
    y = jnp.concatenate([ya, yb], axis=-1) @ w_out
    if not need_ctx:
        return y, None
    q_c = rms_norm(split_heads(q_c), q_g)
    o_c = diff_attend(q_c, k_c, v_c, lam)
    ya_c = (r_ctx * jax.nn.gelu(gate_c.astype(f32))).astype(h.dtype)
    yb_c = (rms_norm(o_c, subln_g) * (1.0 - lam_init)).reshape(bsz, n_c, ATT_WIDTH)
    y_c = jnp.concatenate([ya_c, yb_c], axis=-1) @ w_out
    return y, y_c


def s5_scan(u, lam_bar, b_bar, c_mat, h0, reverse, with_output):
    bsz, n, g, i = u.shape
    chunks = u.reshape(bsz, n // S5_CHUNK, S5_CHUNK, g, i).swapaxes(0, 1)

    def step(h, u_blk):
        bu = jnp.einsum('blgi,gpi->blgp', u_blk.astype(jnp.complex64), b_bar)
        a = jnp.broadcast_to(lam_bar, bu.shape)
        hs, h_last = linear_scan(a, bu, h, reverse)
        y = jnp.real(jnp.einsum('blgp,gip->blgi', hs, c_mat)) if with_output else None
        return h_last, y

    h_last, ys = lax.scan(step, h0, chunks, reverse=reverse)
    if with_output:
        ys = ys.swapaxes(0, 1).reshape(bsz, n, g, i)
    return ys, h_last


def s5_mixer(h, hc, a_re, a_im, log_step, b_re, b_im, c_re, c_im, d_skip, glu_w, glu_b, need_ctx):
    f32 = jnp.float32
    bsz, n, dm = h.shape
    grp = lambda t: t.astype(f32).reshape(t.shape[:2] + (S5_GROUPS, S5_GROUP))
    u, u_c = grp(h), grp(hc)
    d = d_skip.astype(f32).reshape(S5_GROUPS, S5_GROUP)
    y = u * d
    y_c = u_c * d if need_ctx else None
    for di, rev in enumerate((False, True)):
        lam = lax.complex(jnp.minimum(a_re[di].astype(f32), -1e-4), a_im[di].astype(f32))
        lam_bar = jnp.exp(lam * jnp.exp(log_step[di].astype(f32)))
        b_bar = ((lam_bar - 1.0) / lam)[..., None] * lax.complex(b_re[di].astype(f32), b_im[di].astype(f32))
        c_mat = lax.complex(c_re[di].astype(f32), c_im[di].astype(f32))
        h0 = jnp.zeros((bsz, S5_GROUPS, S5_STATE), jnp.complex64)
        yc_d, h_ctx = s5_scan(u_c, lam_bar, b_bar, c_mat, h0, rev, need_ctx)
        yl_d, _ = s5_scan(u, lam_bar, b_bar, c_mat, h_ctx, rev, True)
        y = y + yl_d
        if need_ctx:
            y_c = y_c + yc_d

    def glu(t):
        z = jax.nn.gelu(t.reshape(t.shape[:2] + (dm,))).astype(h.dtype)
        val, gt = jnp.split(z @ glu_w + glu_b, 2, axis=-1)
        return val * jax.nn.sigmoid(gt)

    return glu(y), (glu(y_c) if need_ctx else None)


def moe_ffn(h, router_w, router_bias, w_gate, w_up, w_down, ws_gate, ws_up, ws_down):
    n_tok, dm = h.shape
    f32 = jnp.float32
    affinity = jax.nn.sigmoid((h @ router_w).astype(f32))
    _, idx = lax.top_k(affinity + router_bias.astype(f32), TOP_K)
    s_sel = jnp.take_along_axis(affinity, idx, axis=-1)
    gates = s_sel / jnp.sum(s_sel, axis=-1, keepdims=True) * ROUTED_SCALE
    n_assign = n_tok * TOP_K
    e_flat = idx.reshape(-1)
    t_flat = jnp.repeat(jnp.arange(n_tok, dtype=jnp.int32), TOP_K)
    g_flat = gates.reshape(-1).astype(h.dtype)
    order = jnp.argsort(e_flat)
    e_s, t_s, g_s = e_flat[order], t_flat[order], g_flat[order]
    counts = jnp.bincount(e_flat, length=N_EXPERTS)
    padded = (counts + MOE_BLOCK - 1) // MOE_BLOCK * MOE_BLOCK
    pad_end = jnp.cumsum(padded)
    pad_start = pad_end - padded
    sort_start = jnp.cumsum(counts) - counts
    dest = pad_start[e_s] + jnp.arange(n_assign, dtype=jnp.int32) - sort_start[e_s]
    n_blocks = -(-(n_assign + N_EXPERTS * (MOE_BLOCK - 1)) // MOE_BLOCK)
    tok_pad = jnp.full((n_blocks * MOE_BLOCK,), n_tok, jnp.int32).at[dest].set(t_s)
    g_pad = jnp.zeros((n_blocks * MOE_BLOCK,), h.dtype).at[dest].set(g_s)
    blk_start = jnp.arange(n_blocks, dtype=jnp.int32) * MOE_BLOCK
    blk_expert = jnp.minimum(jnp.searchsorted(pad_end, blk_start, side='right'), N_EXPERTS - 1)
    h_ext = jnp.concatenate([h, jnp.zeros((1, dm), h.dtype)], axis=0)

    def expert_block(acc, blk):
        tok, g, e = blk
        y = swiglu(h_ext[tok], w_gate[e], w_up[e], w_down[e])
        return acc.at[tok].add(y * g[:, None]), None

    acc, _ = lax.scan(expert_block, jnp.zeros_like(h_ext),
                      (tok_pad.reshape(n_blocks, MOE_BLOCK), g_pad.reshape(n_blocks, MOE_BLOCK), blk_expert))
    return acc[:n_tok] + swiglu(h, ws_gate, ws_up, ws_down)


def setup_inputs(seed: int = 0) -> dict:
    key = jax.random.key(seed)
    keys = iter(jax.random.split(key, 64))
    f32 = jnp.float32
    dm = D_MODEL
    n_even = (DEPTH + 1) // 2
    n_odd = DEPTH // 2

    def normal(shape, scale):
        return jax.random.normal(next(keys), shape, f32) * scale

    def gain(shape):
        return 1.0 + normal(shape, 0.02)

    def uniform(shape, lo, hi):
        return jax.random.uniform(next(keys), shape, f32, lo, hi)

    a8 = uniform((n_even, 2, LRU_WIDTH), 0.9, 0.999)
    a_lru = a8 ** (1.0 / LRU_C)
    s5_n = jnp.arange(S5_STATE, dtype=f32)
    s5_shape = (n_odd, 2, S5_GROUPS, S5_STATE)
    return {
        'x': normal((BATCH, SEQ, dm), 1.0),
        'c': normal((BATCH, dm), 1.0),
        'ctx': normal((BATCH, CTX_LEN, dm), 1.0),
        'c_ctx': normal((dm,), 1.0),
        'mod_w': normal((DEPTH, dm, 6 * dm), 0.5 * dm ** -0.5),
        'mod_b': normal((DEPTH, 6 * dm), 0.02),
        'norm1_g': gain((DEPTH, dm)),
        'norm2_g': gain((DEPTH, dm)),
        'ar_w_in': normal((n_even, dm, EVEN_IN), dm ** -0.5),
        'ar_w_out': normal((n_even, EVEN_MIX, dm), EVEN_MIX ** -0.5),
        'lru_conv_w': normal((n_even, CONV_W, LRU_WIDTH), CONV_W ** -0.5),
        'lru_conv_b': normal((n_even, LRU_WIDTH), 0.02),
        'lru_lam': jnp.log(a_lru) - jnp.log1p(-a_lru),
        'lru_r_w': normal((n_even, 2, LRU_BLOCKS, LRU_BLOCK_W, LRU_BLOCK_W), LRU_BLOCK_W ** -0.5),
        'lru_r_b': normal((n_even, 2, LRU_WIDTH), 0.02),
        'lru_i_w': normal((n_even, 2, LRU_BLOCKS, LRU_BLOCK_W, LRU_BLOCK_W), LRU_BLOCK_W ** -0.5),
        'lru_i_b': normal((n_even, 2, LRU_WIDTH), 0.02),
        'attn_q_g': gain((n_even, ATT_DH)),
        'attn_k_g': gain((n_even, ATT_DH)),
        'attn_lam_q1': normal((n_even, ATT_DH), 0.1),
        'attn_lam_k1': normal((n_even, ATT_DH), 0.1),
        'attn_lam_q2': normal((n_even, ATT_DH), 0.1),
        'attn_lam_k2': normal((n_even, ATT_DH), 0.1),
        'attn_subln_g': gain((n_even, ATT_DV)),
        's5_a_re': -0.5 + normal(s5_shape, 0.01),
        's5_a_im': math.pi * s5_n + normal(s5_shape, 0.01),
        's5_log_step': uniform(s5_shape, math.log(1e-3), math.log(1e-1)),
        's5_b_re': normal(s5_shape + (S5_GROUP,), (2 * S5_GROUP) ** -0.5),
        's5_b_im': normal(s5_shape + (S5_GROUP,), (2 * S5_GROUP) ** -0.5),
        's5_c_re': normal((n_odd, 2, S5_GROUPS, S5_GROUP, S5_STATE), (2 * S5_STATE) ** -0.5),
        's5_c_im': normal((n_odd, 2, S5_GROUPS, S5_GROUP, S5_STATE), (2 * S5_STATE) ** -0.5),
        's5_d': normal((n_odd, dm), 1.0),
        's5_glu_w': normal((n_odd, dm, 2 * dm), dm ** -0.5),
        's5_glu_b': normal((n_odd, 2 * dm), 0.02),
        'router_w': normal((DEPTH, dm, N_EXPERTS), dm ** -0.5),
        'router_bias': normal((DEPTH, N_EXPERTS), 0.01),
        'exp_w_gate': normal((DEPTH, N_EXPERTS, dm, D_EXPERT), dm ** -0.5),
        'exp_w_up': normal((DEPTH, N_EXPERTS, dm, D_EXPERT), dm ** -0.5),
        'exp_w_down': normal((DEPTH, N_EXPERTS, D_EXPERT, dm), D_EXPERT ** -0.5),
        'sh_w_gate': normal((DEPTH, dm, D_EXPERT), dm ** -0.5),
        'sh_w_up': normal((DEPTH, dm, D_EXPERT), dm ** -0.5),
        'sh_w_down': normal((DEPTH, D_EXPERT, dm), D_EXPERT ** -0.5),
    }


def reference(x, c, ctx, c_ctx, mod_w, mod_b, norm1_g, norm2_g,
              ar_w_in, ar_w_out, lru_conv_w, lru_conv_b, lru_lam, lru_r_w, lru_r_b, lru_i_w, lru_i_b,
              attn_q_g, attn_k_g, attn_lam_q1, attn_lam_k1, attn_lam_q2, attn_lam_k2, attn_subln_g,
              s5_a_re, s5_a_im, s5_log_step, s5_b_re, s5_b_im, s5_c_re, s5_c_im, s5_d, s5_glu_w, s5_glu_b,
              router_w, router_bias, exp_w_gate, exp_w_up, exp_w_down, sh_w_gate, sh_w_up, sh_w_down):
    bsz, n, dm = x.shape
    rows = n // GRID_W
    cos, sin = axial_rope_tables(rows)
    mod_in = jax.nn.silu(c)
    mod_in_c = jax.nn.silu(c_ctx)
    for layer in range(DEPTH):
        last = layer == DEPTH - 1
        j = layer // 2
        m = (mod_in @ mod_w[layer] + mod_b[layer])[:, None, :]
        mc = mod_in_c @ mod_w[layer] + mod_b[layer]
        sh1, sc1, g1, sh2, sc2, g2 = jnp.split(m, 6, axis=-1)
        sh1c, sc1c, g1c, sh2c, sc2c, g2c = jnp.split(mc, 6, axis=-1)
        h = modulate(rms_norm(x, norm1_g[layer]), sh1, sc1)
        hc = modulate(rms_norm(ctx, norm1_g[layer]), sh1c, sc1c)
        if layer % 2 == 0:
            lam_init = 0.8 - 0.6 * math.exp(-0.3 * layer)
            y, y_c = rglru_diffattn_mixer(
                h, hc, ar_w_in[j], ar_w_out[j], lru_conv_w[j], lru_conv_b[j], lru_lam[j],
                lru_r_w[j], lru_r_b[j], lru_i_w[j], lru_i_b[j], attn_q_g[j], attn_k_g[j],
                attn_lam_q1[j], attn_lam_k1[j], attn_lam_q2[j], attn_lam_k2[j], attn_subln_g[j],
                lam_init, cos, sin, not last)
        else:
            y, y_c = s5_mixer(
                h, hc, s5_a_re[j], s5_a_im[j], s5_log_step[j], s5_b_re[j], s5_b_im[j],
                s5_c_re[j], s5_c_im[j], s5_d[j], s5_glu_w[j], s5_glu_b[j], not last)
        x = x + g1 * y
        moe_w = (router_w[layer], router_bias[layer], exp_w_gate[layer], exp_w_up[layer], exp_w_down[layer],
                 sh_w_gate[layer], sh_w_up[layer], sh_w_down[layer])
        h2 = modulate(rms_norm(x, norm2_g[layer]), sh2, sc2)
        if last:
            x = x + g2 * moe_ffn(h2.reshape(-1, dm), *moe_w).reshape(x.shape)
        else:
            ctx = ctx + g1c * y_c
            h2c = modulate(rms_norm(ctx, norm2_g[layer]), sh2c, sc2c)
            n_ctx_tok = h2c.shape[0] * h2c.shape[1]
            out = moe_ffn(jnp.concatenate([h2c.reshape(-1, dm), h2.reshape(-1, dm)], axis=0), *moe_w)
            ctx = ctx + g2c * out[:n_ctx_tok].reshape(ctx.shape)
            x = x + g2 * out[n_ctx_tok:].reshape(x.shape)
    return x
```

```python
import functools
import math

import jax
import jax.numpy as jnp
from jax import lax
from jax.experimental import pallas as pl
from jax.experimental.pallas import tpu as pltpu

F32, BF16, I32 = jnp.float32, jnp.bfloat16, jnp.int32
HIGHEST = lax.Precision.HIGHEST

D_MODEL = 1024
DEPTH = 2
GRID_W = 64
CTX_LEN = 256
EPS = 1e-6
LRU_WIDTH = 512
LRU_BLOCKS = 8
LRU_C = 8.0
ATT_HEADS = 4
ATT_DH = 64
ATT_DV = 128
ATT_QK_W = 512
ATT_WIDTH = 512
ROPE_BASE = 10000.0
EVEN_IN = 2 * LRU_WIDTH + 2 * ATT_QK_W + ATT_WIDTH
S5_GROUP = 16
S5_GROUPS = 64
S5_STATE = 64
S5_TC = 16
S5_ROWS = 8
S5_GB = 2
N_EXPERTS = 64
TOP_K = 6
D_EXPERT = 256
ROUTED_SCALE = 2.5

TM = 256
SEG = 16
RMAX = 2560
TR = 512
ATT_TK = 768
VMEM_LIMIT = 56 * 1024 * 1024


def _cparams(sem):
    return pltpu.CompilerParams(dimension_semantics=sem, vmem_limit_bytes=VMEM_LIMIT)


def _norm_mod(x, g, shift, scale):
    y = x * lax.rsqrt(jnp.mean(x * x, axis=-1, keepdims=True) + EPS) * g
    return y * (1.0 + scale) + shift


def _mod_kernel(c_ref, w_ref, b_ref, o_ref):
    c = c_ref[...]
    s = c * jax.nn.sigmoid(c)
    o_ref[0] = jnp.dot(s, w_ref[0], precision=HIGHEST, preferred_element_type=F32) + b_ref[0]


def _modulation(c, c_ctx, mod_w, mod_b):
    bsz, dm = c.shape
    cc = jnp.concatenate([c, c_ctx[None, :], jnp.zeros((8 - bsz - 1, dm), F32)], axis=0)
    out = pl.pallas_call(
        _mod_kernel,
        grid=(DEPTH, 6),
        in_specs=[pl.BlockSpec((8, dm), lambda l, j: (0, 0)),
                  pl.BlockSpec((1, dm, dm), lambda l, j: (l, 0, j)),
                  pl.BlockSpec((1, 1, dm), lambda l, j: (l, 0, j))],
        out_specs=pl.BlockSpec((1, 8, dm), lambda l, j: (l, 0, j)),
        out_shape=jax.ShapeDtypeStruct((DEPTH, 8, 6 * dm), F32),
        compiler_params=_cparams(("arbitrary", "arbitrary")),
        name="modulation",
    )(cc, mod_w, mod_b.reshape(DEPTH, 1, 6 * dm))
    return out.reshape(DEPTH * 8 * 6, 1, dm)


def _mod_spec(layer, part, row_fn):
    return pl.BlockSpec((1, 1, D_MODEL), lambda *ids: ((layer * 8 + row_fn(*ids)) * 6 + part, 0, 0))


def _rope_tables(n):
    rows = n // GRID_W
    r, col = jnp.meshgrid(jnp.arange(rows), jnp.arange(GRID_W), indexing="ij")
    pos = jnp.stack([r.reshape(-1), col.reshape(-1)], axis=-1).astype(F32)
    n_freq = ATT_DH // 4
    inv_freq = ROPE_BASE ** (-jnp.arange(n_freq, dtype=F32) / n_freq)
    ang = pos[:, :, None] * inv_freq
    cos, sin = jnp.cos(ang), jnp.sin(ang)
    zero = jnp.zeros_like(sin)
    cos64 = jnp.stack([cos, cos], axis=2).reshape(n, ATT_DH)
    sin_lo = jnp.stack([zero, sin], axis=2).reshape(n, ATT_DH)
    sin_hi = jnp.stack([-sin, zero], axis=2).reshape(n, ATT_DH)

    def full(tab, ctx_val):
        tab = jnp.concatenate([jnp.full((CTX_LEN, ATT_DH), ctx_val, F32), tab], axis=0)
        return jnp.concatenate([tab, tab], axis=1)

    return full(cos64, 1.0), full(sin_lo, 0.0), full(sin_hi, 0.0)


def _qk_post(t, gain, ones_bd, cos, sin_lo, sin_hi):
    ss = jnp.dot((t * t).astype(BF16), ones_bd, preferred_element_type=F32) * (1.0 / ATT_DH)
    tn = t * lax.rsqrt(ss + EPS) * gain
    w = tn.shape[1]
    return tn * cos + pltpu.roll(tn, 16, 1) * sin_lo + pltpu.roll(tn, w - 16, 1) * sin_hi


def _inproj_kernel(x_ref, c_ref, sh_ref, sc_ref, g_ref, w_ref, qg_ref, kg_ref, ones_ref,
                   cos_ref, slo_ref, shi_ref, gate_ref, xa_ref, q_ref, k_ref, v_ref):
    t = pl.program_id(1)
    x = jnp.where(t == 0, c_ref[0], x_ref[0])
    h = _norm_mod(x, g_ref[0], sh_ref[0], sc_ref[0])
    z = jnp.dot(h.astype(BF16), w_ref[...], preferred_element_type=F32)
    lw, qw = LRU_WIDTH, ATT_QK_W
    gate_ref[0] = z[:, 0:lw].astype(BF16)
    xa_ref[0] = z[:, lw:2 * lw].astype(BF16)
    tile4 = lambda a: jnp.concatenate([a, a, a, a], axis=1)
    cos, slo, shi = tile4(cos_ref[...]), tile4(slo_ref[...]), tile4(shi_ref[...])
    ones_bd = ones_ref[...]
    q = _qk_post(z[:, 2 * lw:2 * lw + qw], qg_ref[...], ones_bd, cos, slo, shi)
    q_ref[0] = (q * (ATT_DH ** -0.5)).astype(BF16)
    k = _qk_post(z[:, 2 * lw + qw:2 * lw + 2 * qw], kg_ref[...], ones_bd, cos, slo, shi)
    k_ref[0] = k.astype(BF16)
    v_ref[0] = z[:, 2 * lw + 2 * qw:].astype(BF16)


def _inproj(x, ctx, modtab, norm_g, w_in, q_g, k_g):
    bsz, n, dm = x.shape
    ntb = (CTX_LEN + n) // TM
    s = CTX_LEN + n
    row = lambda b, t: jnp.where(t == 0, bsz, b)
    cos, slo, shi = _rope_tables(n)
    ones_bd = jnp.kron(jnp.eye(ATT_QK_W // ATT_DH, dtype=F32), jnp.ones((ATT_DH, ATT_DH), F32)).astype(BF16)
    tile_g = lambda g: jnp.tile(g, ATT_QK_W // ATT_DH)[None, :]
    tab_spec = pl.BlockSpec((TM, 2 * ATT_DH), lambda b, t: (t, 0))
    out_spec = pl.BlockSpec((1, TM, LRU_WIDTH), lambda b, t: (b, t, 0))
    out_sds = jax.ShapeDtypeStruct((bsz, s, LRU_WIDTH), BF16)
    return pl.pallas_call(
        _inproj_kernel,
        grid=(bsz, ntb),
        in_specs=[pl.BlockSpec((1, TM, dm), lambda b, t: (b, jnp.maximum(t - 1, 0), 0)),
                  pl.BlockSpec((1, TM, dm), lambda b, t: (b, 0, 0)),
                  _mod_spec(0, 0, row), _mod_spec(0, 1, row),
                  pl.BlockSpec((1, 1, dm), lambda b, t: (0, 0, 0)),
                  pl.BlockSpec((dm, EVEN_IN), lambda b, t: (0, 0)),
                  pl.BlockSpec((1, ATT_QK_W), lambda b, t: (0, 0)),
                  pl.BlockSpec((1, ATT_QK_W), lambda b, t: (0, 0)),
                  pl.BlockSpec((ATT_QK_W, ATT_QK_W), lambda b, t: (0, 0)),
                  tab_spec, tab_spec, tab_spec],
        out_specs=[out_spec] * 5,
        out_shape=[out_sds] * 5,
        compiler_params=_cparams(("arbitrary", "arbitrary")),
        name="inproj",
    )(x, ctx, modtab, modtab, norm_g.reshape(1, 1, dm), w_in.astype(BF16), tile_g(q_g), tile_g(k_g), ones_bd,
      cos, slo, shi)


def _lru_kernel(xa_ref, xp_ref, xn_ref, cw_ref, cb_ref, lam_ref, w_ref, bias_ref, h_ref,
                a_scr, b_scr, c_scr, *, ntb, reverse):
    s = pl.program_id(1)
    ti = jnp.where(s == 0, 0, ntb - s) if reverse else s
    lw = LRU_WIDTH
    x = xa_ref[0].astype(F32)
    row = lax.broadcasted_iota(I32, (TM, lw), 0)
    has_prev = jnp.where(ti > 1, 1.0, 0.0)
    has_next = jnp.where((ti > 0) & (ti < ntb - 1), 1.0, 0.0)
    prev = xp_ref[0].astype(F32) * has_prev
    nxt = xn_ref[0].astype(F32) * has_next
    xm1 = jnp.where(row == 0, prev[7:8], pltpu.roll(x, 1, 0))
    xm2 = jnp.where(row == 0, prev[6:7], jnp.where(row == 1, prev[7:8], pltpu.roll(x, 2, 0)))
    xp1 = jnp.where(row == TM - 1, nxt[0:1], pltpu.roll(x, TM - 1, 0))
    cw = cw_ref[...]
    xc = cw[0:1] * xm2 + cw[1:2] * xm1 + cw[2:3] * x + cw[3:4] * xp1 + cb_ref[...]

    z = jnp.dot(xc.astype(BF16), w_ref[...], preferred_element_type=F32) + bias_ref[...]
    r = jax.nn.sigmoid(z[:, :lw])
    ig = jax.nn.sigmoid(z[:, lw:])
    neg_lam = -lam_ref[...]
    softplus = jnp.maximum(neg_lam, 0.0) + jnp.log1p(jnp.exp(-jnp.abs(neg_lam)))
    log_a = (-LRU_C) * r * softplus
    a = jnp.exp(log_a)
    b = jnp.sqrt(-jnp.tanh(log_a) * (a * a + 1.0)) * (ig * xc)

    r8 = row & 7
    for sft in (1, 2, 4):
        if reverse:
            a_s, b_s, m = pltpu.roll(a, TM - sft, 0), pltpu.roll(b, TM - sft, 0), r8 < 8 - sft
        else:
            a_s, b_s, m = pltpu.roll(a, sft, 0), pltpu.roll(b, sft, 0), r8 >= sft
        b = jnp.where(m, a * b_s + b, b)
        a = jnp.where(m, a * a_s, a)
    a_scr[...] = a
    b_scr[...] = b

    @pl.when(s == 0)
    def _():
        c_scr[...] = jnp.zeros_like(c_scr)

    ng = TM // 8

    def body(j, carry):
        g = (ng - 1 - j) if reverse else j
        off = pl.multiple_of(g * 8, 8)
        h = a_scr[pl.ds(off, 8), :] * carry + b_scr[pl.ds(off, 8), :]
        h_ref[0, pl.ds(off, 8), :] = h
        last = h[0:1] if reverse else h[7:8]
        return jnp.broadcast_to(last, (8, lw))

    c_scr[...] = lax.fori_loop(0, ng, body, c_scr[...], unroll=4)


def _lru(xa, conv_w, conv_b, lam, r_w, r_b, i_w, i_b, reverse):
    bsz, s, lw = xa.shape
    ntb = s // TM
    d = 1 if reverse else 0
    bd = lambda w: jax.scipy.linalg.block_diag(*[w[i] for i in range(LRU_BLOCKS)])
    w = jnp.concatenate([bd(r_w[d]), bd(i_w[d])], axis=1).astype(BF16)
    bias = jnp.concatenate([r_b[d], i_b[d]])[None, :]
    cw = jnp.concatenate([conv_w, jnp.zeros((4, lw), F32)], axis=0)
    if reverse:
        tile = lambda t: jnp.where(t == 0, 0, ntb - t)
    else:
        tile = lambda t: t
    nb8 = s // 8
    return pl.pallas_call(
        functools.partial(_lru_kernel, ntb=ntb, reverse=reverse),
        grid=(bsz, ntb),
        in_specs=[pl.BlockSpec((1, TM, lw), lambda b, t: (b, tile(t), 0)),
                  pl.BlockSpec((1, 8, lw), lambda b, t: (b, jnp.maximum(tile(t) * (TM // 8) - 1, 0), 0)),
                  pl.BlockSpec((1, 8, lw), lambda b, t: (b, jnp.minimum((tile(t) + 1) * (TM // 8), nb8 - 1), 0)),
                  pl.BlockSpec((8, lw), lambda b, t: (0, 0)),
                  pl.BlockSpec((1, lw), lambda b, t: (0, 0)),
                  pl.BlockSpec((1, lw), lambda b, t: (0, 0)),
                  pl.BlockSpec((lw, 2 * lw), lambda b, t: (0, 0)),
                  pl.BlockSpec((1, 2 * lw), lambda b, t: (0, 0))],
        out_specs=pl.BlockSpec((1, TM, lw), lambda b, t: (b, tile(t), 0)),
        out_shape=jax.ShapeDtypeStruct((bsz, s, lw), F32),
        scratch_shapes=[pltpu.VMEM((TM, lw), F32), pltpu.VMEM((TM, lw), F32), pltpu.VMEM((8, lw), F32)],
        compiler_params=_cparams(("arbitrary", "arbitrary")),
        name="lru_rev" if reverse else "lru_fwd",
    )(xa, xa, xa, cw, conv_b[None, :], lam[d][None, :], w, bias)


def _attn_kernel(lamv_ref, q_ref, k_ref, v_ref, sg_ref, o_ref, m_scr, l_scr, acc_scr, *, nkv, lam_init):
    t = pl.program_id(2)
    q = q_ref[0]
    dh = ATT_DH
    m_scr[...] = jnp.full_like(m_scr, -jnp.inf)
    l_scr[...] = jnp.zeros_like(l_scr)
    acc_scr[...] = jnp.zeros_like(acc_scr)

    def chunk(off, size):
        kc = k_ref[0, pl.ds(off, size), :]
        vc = v_ref[0, pl.ds(off, size), :]
        for mi in range(2):
            sc = lax.dot_general(q[:, mi * dh:(mi + 1) * dh], kc[:, mi * dh:(mi + 1) * dh],
                                 (((1,), (1,)), ((), ())), preferred_element_type=F32)
            m_old = m_scr[mi]
            m_new = jnp.maximum(m_old, jnp.max(sc, axis=1, keepdims=True))
            alpha = jnp.exp(m_old - m_new)
            p = jnp.exp(sc - m_new[:, 0:1])
            l_scr[mi] = alpha * l_scr[mi] + jnp.sum(p, axis=1, keepdims=True)
            acc_scr[mi] = alpha * acc_scr[mi] + jnp.dot(p.astype(BF16), vc, preferred_element_type=F32)
            m_scr[mi] = m_new

    @pl.when(t == 0)
    def _():
        chunk(0, CTX_LEN)

    @pl.when(t > 0)
    def _():
        def body(j, carry):
            chunk(pl.multiple_of(j * ATT_TK, ATT_TK), ATT_TK)
            return carry
        lax.fori_loop(0, nkv // ATT_TK, body, 0)

    lv = lamv_ref[...]
    lam = (jnp.exp(jnp.sum(lv[0:1] * lv[1:2], axis=1, keepdims=True))
           - jnp.exp(jnp.sum(lv[2:3] * lv[3:4], axis=1, keepdims=True)) + lam_init)
    o = acc_scr[0] / l_scr[0] - lam * (acc_scr[1] / l_scr[1])
    y = o * lax.rsqrt(jnp.mean(o * o, axis=-1, keepdims=True) + EPS) * sg_ref[...] * (1.0 - lam_init)
    o_ref[0] = y.astype(BF16)


def _attention(q, k, v, lam_vecs, subln_g, lam_init):
    bsz, s, _ = q.shape
    ntb = s // TM
    assert s % ATT_TK == 0
    dv = ATT_DV
    lamv = jnp.concatenate([jnp.pad(lam_vecs, ((0, 0), (0, dv - ATT_DH))), jnp.zeros((4, dv), F32)], axis=0)
    return pl.pallas_call(
        functools.partial(_attn_kernel, nkv=s, lam_init=lam_init),
        grid=(bsz, ATT_HEADS, ntb),
        in_specs=[pl.BlockSpec((8, dv), lambda b, h, t: (0, 0)),
                  pl.BlockSpec((1, TM, dv), lambda b, h, t: (b, t, h)),
                  pl.BlockSpec((1, s, dv), lambda b, h, t: (b, 0, h)),
                  pl.BlockSpec((1, s, dv), lambda b, h, t: (b, 0, h)),
                  pl.BlockSpec((1, dv), lambda b, h, t: (0, 0))],
        out_specs=pl.BlockSpec((1, TM, dv), lambda b, h, t: (b, t, h)),
        out_shape=jax.ShapeDtypeStruct((bsz, s, ATT_WIDTH), BF16),
        scratch_shapes=[pltpu.VMEM((2, TM, dv), F32), pltpu.VMEM((2, TM, dv), F32), pltpu.VMEM((2, TM, dv), F32)],
        compiler_params=_cparams(("arbitrary", "arbitrary", "arbitrary")),
        name="diff_attention",
    )(lamv, q, k, v, subln_g[None, :])


def _post_mixer(y, x, g1, sh2, sc2, n2g, rwt, x1_ref, h2_ref, lg_ref):
    x1 = x + g1 * y
    x1_ref[0] = x1
    h2 = _norm_mod(x1, n2g, sh2, sc2)
    h2_ref[0] = h2.astype(BF16)
    lg_ref[...] = lax.dot_general(rwt, h2, (((1,), (1,)), ((), ())), precision=HIGHEST,
                                  preferred_element_type=F32)


def _outproj_kernel(hf_ref, hr_ref, gate_ref, yb_ref, x_ref, c_ref, g1_ref, sh_ref, sc_ref, n2g_ref, w_ref,
                    rwt_ref, x1_ref, h2_ref, lg_ref):
    t = pl.program_id(1)
    x = jnp.where(t == 0, c_ref[0], x_ref[0])
    ya = ((hf_ref[0] + hr_ref[0]) * jax.nn.gelu(gate_ref[0].astype(F32))).astype(BF16)
    lw = LRU_WIDTH
    y = (jnp.dot(ya, w_ref[0:lw, :], preferred_element_type=F32)
         + jnp.dot(yb_ref[0], w_ref[lw:, :], preferred_element_type=F32))
    _post_mixer(y, x, g1_ref[0], sh_ref[0], sc_ref[0], n2g_ref[0], rwt_ref[...], x1_ref, h2_ref, lg_ref)


def _outproj(hf, hr, gate, yb, x, ctx, modtab, norm_g, w_out, router_w):
    bsz, s, lw = hf.shape
    dm = D_MODEL
    ntb = s // TM
    row = lambda b, t: jnp.where(t == 0, bsz, b)
    half = pl.BlockSpec((1, TM, lw), lambda b, t: (b, t, 0))
    return pl.pallas_call(
        _outproj_kernel,
        grid=(bsz, ntb),
        in_specs=[half, half, half, half,
                  pl.BlockSpec((1, TM, dm), lambda b, t: (b, jnp.maximum(t - 1, 0), 0)),
                  pl.BlockSpec((1, TM, dm), lambda b, t: (b, 0, 0)),
                  _mod_spec(0, 2, row), _mod_spec(0, 3, row), _mod_spec(0, 4, row),
                  pl.BlockSpec((1, 1, dm), lambda b, t: (0, 0, 0)),
                  pl.BlockSpec((2 * lw, dm), lambda b, t: (0, 0)),
                  pl.BlockSpec((N_EXPERTS, dm), lambda b, t: (0, 0))],
        out_specs=[pl.BlockSpec((1, TM, dm), lambda b, t: (b, t, 0)),
                   pl.BlockSpec((1, TM, dm), lambda b, t: (b, t, 0)),
                   pl.BlockSpec((N_EXPERTS, TM), lambda b, t: (0, b * ntb + t))],
        out_shape=[jax.ShapeDtypeStruct((bsz, s, dm), F32),
                   jax.ShapeDtypeStruct((bsz, s, dm), BF16),
                   jax.ShapeDtypeStruct((N_EXPERTS, bsz * s), F32)],
        compiler_params=_cparams(("arbitrary", "arbitrary")),
        name="outproj",
    )(hf, hr, gate, yb, x, ctx, modtab, modtab, modtab, norm_g.reshape(1, 1, dm), w_out.astype(BF16),
      router_w.T)


def _route_kernel(lg_ref, bias_ref, tri_ref, low_ref, pos_ref, gate_ref, cnt_ref):
    ne = N_EXPERTS
    aff = jax.nn.sigmoid(lg_ref[...])
    work = aff + bias_ref[:, 0:1]
    eidx = lax.broadcasted_iota(I32, (ne, TM), 0)
    sels = []
    for _ in range(TOP_K):
        mx = jnp.max(work, axis=0, keepdims=True)
        am = jnp.min(jnp.where(work == mx, eidx, ne), axis=0, keepdims=True)
        sk = eidx == am
        sels.append(sk)
        work = jnp.where(sk, -jnp.inf, work)
    sel = sels[0]
    for sk in sels[1:]:
        sel = sel | sk
    self = jnp.where(sel, 1.0, 0.0)
    s_sel = aff * self
    gates = s_sel / jnp.sum(s_sel, axis=0, keepdims=True) * ROUTED_SCALE
    rank = jnp.dot(self.astype(BF16), tri_ref[...], preferred_element_type=F32)
    cnt = jnp.sum(self, axis=1, keepdims=True)
    cnt_seg = jnp.floor((cnt + (SEG - 1)) * (1.0 / SEG)) * SEG
    cnt_b = jnp.broadcast_to(cnt_seg, (ne, 128))
    seg_off = jnp.dot(low_ref[...], cnt_b.astype(BF16), preferred_element_type=F32)
    lpos = seg_off[:, 0:1] + rank
    pos_rows, gate_rows = [], []
    for sk in sels:
        pos_rows.append(jnp.sum(jnp.where(sk, lpos, 0.0), axis=0, keepdims=True))
        gate_rows.append(jnp.sum(jnp.where(sk, gates, 0.0), axis=0, keepdims=True))
    for _ in range(8 - TOP_K):
        pos_rows.append(jnp.full((1, TM), -1.0, F32))
        gate_rows.append(jnp.zeros((1, TM), F32))
    pos_ref[...] = jnp.concatenate(pos_rows, axis=0).astype(I32)
    gate_ref[...] = jnp.concatenate(gate_rows, axis=0)
    cnt_ref[0] = cnt_b.astype(I32)


def _route(lgt, router_bias):
    ne, t = lgt.shape
    nt = t // TM
    tri = jnp.triu(jnp.ones((TM, TM), F32), k=1).astype(BF16)
    low = jnp.tril(jnp.ones((ne, ne), F32), k=-1).astype(BF16)
    bias = jnp.broadcast_to(router_bias[:, None], (ne, 128))
    pos, gate, cnt = pl.pallas_call(
        _route_kernel,
        grid=(nt,),
        in_specs=[pl.BlockSpec((ne, TM), lambda i: (0, i)),
                  pl.BlockSpec((ne, 128), lambda i: (0, 0)),
                  pl.BlockSpec((TM, TM), lambda i: (0, 0)),
                  pl.BlockSpec((ne, ne), lambda i: (0, 0))],
        out_specs=[pl.BlockSpec((8, TM), lambda i: (0, i)),
                   pl.BlockSpec((8, TM), lambda i: (0, i)),
                   pl.BlockSpec((1, ne, 128), lambda i: (i, 0, 0))],
        out_shape=[jax.ShapeDtypeStruct((8, t), I32),
                   jax.ShapeDtypeStruct((8, t), F32),
                   jax.ShapeDtypeStruct((nt, ne, 128), I32)],
        compiler_params=_cparams(("arbitrary",)),
        name="route",
    )(lgt, bias, tri, low)
    return pos, gate, cnt[:, :, 0]


def _moe_layout(cnt_seg, nblk_max):
    tot = jnp.sum(cnt_seg, axis=0)
    region = (tot + TR - 1) // TR * TR
    region_end = jnp.cumsum(region)
    goff = (region_end - region)[None, :] + jnp.cumsum(cnt_seg, axis=0) - cnt_seg
    packed = ((goff // SEG) << 5) | (cnt_seg // SEG)
    nblk = region_end[-1] // TR
    blk = jnp.arange(nblk_max, dtype=I32)
    blk_e = jnp.sum((region_end[None, :] // TR <= blk[:, None]).astype(I32), axis=1)
    blk_e = jnp.minimum(blk_e, N_EXPERTS - 1).astype(I32)
    return packed.reshape(-1).astype(I32), blk_e, nblk.astype(I32).reshape(1)


def _segment_copies(pk_ref, tile, src, dst, sem, to_sorted, fn):
    loff = 0
    for e in range(N_EXPERTS):
        pk = pk_ref[tile * N_EXPERTS + e]
        cnt = (pk & 31) * SEG
        goff = (pk >> 5) * SEG
        n64 = cnt >> 6

        def copy(lo, go, size):
            lo, go = pl.multiple_of(lo, SEG), pl.multiple_of(go, SEG)
            stage, hbm = src if to_sorted else dst, dst if to_sorted else src
            a, b = stage.at[pl.ds(lo, size)], hbm.at[pl.ds(go, size)]
            return pltpu.make_async_copy(a, b, sem) if to_sorted else pltpu.make_async_copy(b, a, sem)

        def body(j, carry, loff=loff, goff=goff):
            fn(copy(loff + j * 64, goff + j * 64, 64))
            return carry

        lax.fori_loop(0, n64, body, 0)
        done = n64 * 64
        for size in (32, 16):
            bit = (cnt & size) != 0

            @pl.when(bit)
            def _(loff=loff, goff=goff, done=done, size=size):
                fn(copy(loff + done, goff + done, size))

            done = done + jnp.where(bit, size, 0)
        loff = loff + cnt


def _slot_matrix(pos, weight_rows=None):
    riota = lax.broadcasted_iota(I32, (RMAX, TM), 0)
    out = None
    for k in range(TOP_K):
        hit = riota == pos[k:k + 1]
        if weight_rows is None:
            out = hit if out is None else (out | hit)
        else:
            term = jnp.where(hit, weight_rows[k:k + 1], 0.0)
            out = term if out is None else out + term
    return out


def _dispatch_kernel(pk_ref, pos_ref, h_ref, xs_in, xs_ref, stage, sem):
    i = pl.program_id(0)
    p = jnp.where(_slot_matrix(pos_ref[...]), 1.0, 0.0).astype(BF16)
    stage[...] = jnp.dot(p, h_ref[...], preferred_element_type=F32).astype(BF16)
    _segment_copies(pk_ref, i, stage, xs_ref, sem, True, lambda c: c.start())
    _segment_copies(pk_ref, i, stage, xs_ref, sem, True, lambda c: c.wait())


def _dispatch(packed, pos, h2, nrows):
    t, dm = h2.shape
    nt = t // TM
    xs0 = jnp.zeros((nrows, dm), BF16)
    return pl.pallas_call(
        _dispatch_kernel,
        grid_spec=pltpu.PrefetchScalarGridSpec(
            num_scalar_prefetch=1,
            grid=(nt,),
            in_specs=[pl.BlockSpec((8, TM), lambda i, pk: (0, i)),
                      pl.BlockSpec((TM, dm), lambda i, pk: (i, 0)),
                      pl.BlockSpec(memory_space=pl.ANY)],
            out_specs=pl.BlockSpec(memory_space=pl.ANY),
            scratch_shapes=[pltpu.VMEM((RMAX, dm), BF16), pltpu.SemaphoreType.DMA]),
        out_shape=jax.ShapeDtypeStruct((nrows, dm), BF16),
        input_output_aliases={3: 0},
        compiler_params=_cparams(("arbitrary",)),
        name="moe_dispatch",
    )(packed, pos, h2, xs0)


def _expert_kernel(be_ref, nb_ref, x_ref, wg_ref, wu_ref, wd_ref, y_ref, wg_s, wu_s, wd_s):
    j = pl.program_id(0)
    changed = be_ref[j] != be_ref[jnp.maximum(j - 1, 0)]

    @pl.when((j == 0) | changed)
    def _():
        wg_s[...] = wg_ref[0].astype(BF16)
        wu_s[...] = wu_ref[0].astype(BF16)
        wd_s[...] = wd_ref[0].astype(BF16)

    @pl.when(j < nb_ref[0])
    def _():
        x = x_ref[...]
        g = jnp.dot(x, wg_s[...], preferred_element_type=F32)
        u = jnp.dot(x, wu_s[...], preferred_element_type=F32)
        a = (g * jax.nn.sigmoid(g) * u).astype(BF16)
        y_ref[...] = jnp.dot(a, wd_s[...], preferred_element_type=F32).astype(BF16)


def _experts(blk_e, nblk, xs, w_gate, w_up, w_down):
    nrows, dm = xs.shape
    nblk_max = nrows // TR
    de = D_EXPERT
    row_blk = lambda j, be, nb: (jnp.minimum(j, nb[0] - 1), 0)
    return pl.pallas_call(
        _expert_kernel,
        grid_spec=pltpu.PrefetchScalarGridSpec(
            num_scalar_prefetch=2,
            grid=(nblk_max,),
            in_specs=[pl.BlockSpec((TR, dm), row_blk),
                      pl.BlockSpec((1, dm, de), lambda j, be, nb: (be[j], 0, 0)),
                      pl.BlockSpec((1, dm, de), lambda j, be, nb: (be[j], 0, 0)),
                      pl.BlockSpec((1, de, dm), lambda j, be, nb: (be[j], 0, 0))],
            out_specs=pl.BlockSpec((TR, dm), row_blk),
            scratch_shapes=[pltpu.VMEM((dm, de), BF16), pltpu.VMEM((dm, de), BF16), pltpu.VMEM((de, dm), BF16)]),
        out_shape=jax.ShapeDtypeStruct((nrows, dm), BF16),
        input_output_aliases={2: 0},
        compiler_params=_cparams(("arbitrary",)),
        name="moe_experts",
    )(blk_e, nblk, xs, w_gate, w_up, w_down)


def _combine_kernel(pk_ref, pos_ref, gate_ref, h_ref, x1_ref, g2_ref, wsg_ref, wsu_ref, wsd_ref, ys_ref,
                    o_ref, stage, sem):
    i = pl.program_id(0)

    @pl.when(i == 0)
    def _():
        stage[...] = jnp.zeros_like(stage)

    _segment_copies(pk_ref, i, ys_ref, stage, sem, False, lambda c: c.start())
    pos = pos_ref[...]
    p = jnp.where(_slot_matrix(pos), 1.0, 0.0).astype(BF16)
    row_gate = jnp.sum(_slot_matrix(pos, gate_ref[...]), axis=1, keepdims=True)
    h = h_ref[...]
    g = jnp.dot(h, wsg_ref[...], preferred_element_type=F32)
    u = jnp.dot(h, wsu_ref[...], preferred_element_type=F32)
    shared = jnp.dot((g * jax.nn.sigmoid(g) * u).astype(BF16), wsd_ref[...], preferred_element_type=F32)
    _segment_copies(pk_ref, i, ys_ref, stage, sem, False, lambda c: c.wait())
    yg = (stage[...].astype(F32) * row_gate).astype(BF16)
    routed = lax.dot_general(p, yg, (((0,), (0,)), ((), ())), preferred_element_type=F32)
    o_ref[...] = x1_ref[...] + g2_ref[0] * (routed + shared)


def _combine(packed, pos, gate, h2, x1, modtab, layer, row_fn, ys, ws_gate, ws_up, ws_down):
    t, dm = h2.shape
    nt = t // TM
    de = D_EXPERT
    return pl.pallas_call(
        _combine_kernel,
        grid_spec=pltpu.PrefetchScalarGridSpec(
            num_scalar_prefetch=1,
            grid=(nt,),
            in_specs=[pl.BlockSpec((8, TM), lambda i, pk: (0, i)),
                      pl.BlockSpec((8, TM), lambda i, pk: (0, i)),
                      pl.BlockSpec((TM, dm), lambda i, pk: (i, 0)),
                      pl.BlockSpec((TM, dm), lambda i, pk: (i, 0)),
                      pl.BlockSpec((1, 1, dm), lambda i, pk: ((layer * 8 + row_fn(i)) * 6 + 5, 0, 0)),
                      pl.BlockSpec((dm, de), lambda i, pk: (0, 0)),
                      pl.BlockSpec((dm, de), lambda i, pk: (0, 0)),
                      pl.BlockSpec((de, dm), lambda i, pk: (0, 0)),
                      pl.BlockSpec(memory_space=pl.ANY)],
            out_specs=pl.BlockSpec((TM, dm), lambda i, pk: (i, 0)),
            scratch_shapes=[pltpu.VMEM((RMAX, dm), BF16), pltpu.SemaphoreType.DMA]),
        out_shape=jax.ShapeDtypeStruct((t, dm), F32),
        compiler_params=_cparams(("arbitrary",)),
        name="moe_combine",
    )(packed, pos, gate, h2, x1, modtab, ws_gate.astype(BF16), ws_up.astype(BF16), ws_down.astype(BF16), ys)


def _moe(h2, lgt, x1, modtab, layer, row_fn, router_bias, w_gate, w_up, w_down, ws_gate, ws_up, ws_down):
    t = h2.shape[0]
    nt = t // TM
    max_rows = t * TOP_K + nt * N_EXPERTS * (SEG - 1) + N_EXPERTS * (TR - 1)
    nblk_max = -(-max_rows // TR)
    pos, gate, cnt_seg = _route(lgt, router_bias)
    packed, blk_e, nblk = _moe_layout(cnt_seg, nblk_max)
    xs = _dispatch(packed, pos, h2, nblk_max * TR)
    ys = _experts(blk_e, nblk, xs, w_gate, w_up, w_down)
    return _combine(packed, pos, gate, h2, x1, modtab, layer, row_fn, ys, ws_gate, ws_up, ws_down)


def _prenorm_kernel(x_ref, sh_ref, sc_ref, g_ref, u_ref):
    u_ref[0] = _norm_mod(x_ref[0], g_ref[0], sh_ref[0], sc_ref[0]).astype(BF16)


def _prenorm(xall, modtab, layer, norm_g):
    bsz, s, dm = xall.shape
    ntb = s // TM
    row = lambda b, t: jnp.where(t == 0, bsz, b)
    return pl.pallas_call(
        _prenorm_kernel,
        grid=(bsz, ntb),
        in_specs=[pl.BlockSpec((1, TM, dm), lambda b, t: (b, t, 0)),
                  _mod_spec(layer, 0, row), _mod_spec(layer, 1, row),
                  pl.BlockSpec((1, 1, dm), lambda b, t: (0, 0, 0))],
        out_specs=pl.BlockSpec((1, TM, dm), lambda b, t: (b, t, 0)),
        out_shape=jax.ShapeDtypeStruct((bsz, s, dm), BF16),
        compiler_params=_cparams(("arbitrary", "arbitrary")),
        name="prenorm",
    )(xall, modtab, modtab, norm_g.reshape(1, 1, dm))


def _cmul(x, y):
    return x[0] * y[0] - x[1] * y[1], x[0] * y[1] + x[1] * y[0]


def _s5_weights(a_re, a_im, log_step, b_re, b_im, c_re, c_im, d_skip):
    tc, g, p, ch = S5_TC, S5_GROUPS, S5_STATE, S5_GROUP
    lam = (jnp.minimum(a_re, -1e-4), a_im)
    step = jnp.exp(log_step)
    mag = jnp.exp(lam[0] * step)
    lam_bar = (mag * jnp.cos(lam[1] * step), mag * jnp.sin(lam[1] * step))
    inv = 1.0 / (lam[0] * lam[0] + lam[1] * lam[1])
    coef = _cmul((lam_bar[0] - 1.0, lam_bar[1]), (lam[0] * inv, -lam[1] * inv))
    b_bar = _cmul((coef[0][..., None], coef[1][..., None]), (b_re, b_im))
    pw = [(jnp.ones_like(mag), jnp.zeros_like(mag))]
    for _ in range(tc):
        pw.append(_cmul(pw[-1], lam_bar))
    pw = (jnp.stack([q[0] for q in pw], axis=1), jnp.stack([q[1] for q in pw], axis=1))
    at = lambda d, idx: (pw[0][d, idx], pw[1][d, idx])
    cp = _cmul((c_re[:, None], c_im[:, None]), (pw[0][:, :tc, :, None, :], pw[1][:, :tc, :, None, :]))
    kern = (jnp.einsum("dkgop,dgpi->dkgoi", cp[0], b_bar[0], precision=HIGHEST)
            - jnp.einsum("dkgop,dgpi->dkgoi", cp[1], b_bar[1], precision=HIGHEST))
    s_idx = jnp.arange(tc)[:, None]
    t_idx = jnp.arange(tc)[None, :]

    def toeplitz(kd, tau):
        return jnp.where((tau >= 0)[:, :, None, None, None], kd[jnp.clip(tau, 0, tc - 1)], 0.0)

    m = toeplitz(kern[0], t_idx - s_idx) + toeplitz(kern[1], s_idx - t_idx)
    m = jnp.transpose(m, (2, 0, 4, 1, 3))
    eye_t = jnp.eye(tc, dtype=F32)[None, :, None, :, None]
    eye_c = jnp.eye(ch, dtype=F32)[None, None, :, None, :]
    m = m + eye_t * eye_c * d_skip.reshape(g, 1, ch, 1, 1)
    m = m.reshape(g, tc * ch, tc * ch)
    steps = jnp.arange(tc)
    lift = lambda z: (z[0][..., None], z[1][..., None])
    e_f = _cmul(lift(at(0, tc - 1 - steps)), (b_bar[0][0][None], b_bar[1][0][None]))
    e_r = _cmul(lift(at(1, steps)), (b_bar[0][1][None], b_bar[1][1][None]))
    w_in = jnp.stack([e_f[0], e_r[0], e_f[1], e_r[1]], axis=0)
    w_in = jnp.transpose(w_in, (2, 1, 4, 0, 3)).reshape(g, tc * ch, 4 * p)
    mid = lambda z: (z[0][:, :, None, :], z[1][:, :, None, :])
    g_f = _cmul((c_re[0][None], c_im[0][None]), mid(at(0, 1 + steps)))
    g_r = _cmul((c_re[1][None], c_im[1][None]), mid(at(1, tc - steps)))
    w_re = jnp.stack([g_f[0], g_r[0]], axis=0)
    w_im = -jnp.stack([g_f[1], g_r[1]], axis=0)
    to_rows = lambda w: jnp.transpose(w, (2, 0, 4, 1, 3)).reshape(g, 2 * p, tc * ch)
    a_rows = jnp.stack([jnp.concatenate([pw[0][0, tc], pw[0][1, tc]], axis=-1),
                        jnp.concatenate([pw[1][0, tc], pw[1][1, tc]], axis=-1)], axis=1)
    a_rows = jnp.concatenate([a_rows, jnp.zeros((g, 6, 2 * p), F32)], axis=1)
    return m.astype(BF16), w_in.astype(BF16), to_rows(w_re).astype(BF16), to_rows(w_im).astype(BF16), a_rows


def _s5_kernel(x_ref, m_ref, win_ref, wre_ref, wim_ref, a_ref, y_ref, v_scr, sre_scr, sim_scr, *,
               nchunk, nctx):
    p2 = 2 * S5_STATE
    rows = S5_ROWS
    for g in range(S5_GB):
        v_scr[g] = jnp.dot(x_ref[g], win_ref[g], preferred_element_type=F32)
    fwd_lane = lax.broadcasted_iota(I32, (rows, p2), 1) < S5_STATE
    a_re = [jnp.broadcast_to(a_ref[g, 0:1, :], (rows, p2)) for g in range(S5_GB)]
    a_im = [jnp.broadcast_to(a_ref[g, 1:2, :], (rows, p2)) for g in range(S5_GB)]

    def body(s, carry):
        cf = s
        cr = jnp.where(s < nctx, nctx - 1 - s, nchunk - 1 + nctx - s)
        of = pl.multiple_of(cf * rows, rows)
        orv = pl.multiple_of(cr * rows, rows)
        new = []
        for g in range(S5_GB):
            sre, sim = carry[g]
            sre_scr[g, pl.ds(of, rows), 0:S5_STATE] = sre[:, 0:S5_STATE]
            sre_scr[g, pl.ds(orv, rows), S5_STATE:p2] = sre[:, S5_STATE:p2]
            sim_scr[g, pl.ds(of, rows), 0:S5_STATE] = sim[:, 0:S5_STATE]
            sim_scr[g, pl.ds(orv, rows), S5_STATE:p2] = sim[:, S5_STATE:p2]
            vf = v_scr[g, pl.ds(of, rows), :]
            vr = v_scr[g, pl.ds(orv, rows), :]
            vre = jnp.where(fwd_lane, vf[:, 0:p2], vr[:, 0:p2])
            vim = jnp.where(fwd_lane, vf[:, p2:2 * p2], vr[:, p2:2 * p2])
            new.append((a_re[g] * sre - a_im[g] * sim + vre, a_re[g] * sim + a_im[g] * sre + vim))
        return tuple(new)

    zero = jnp.zeros((rows, p2), F32)
    lax.fori_loop(0, nchunk, body, tuple((zero, zero) for _ in range(S5_GB)))
    for g in range(S5_GB):
        y = (jnp.dot(x_ref[g], m_ref[g], preferred_element_type=F32)
             + jnp.dot(sre_scr[g].astype(BF16), wre_ref[g], preferred_element_type=F32)
             + jnp.dot(sim_scr[g].astype(BF16), wim_ref[g], preferred_element_type=F32))
        y_ref[g] = y.astype(BF16)


def _s5(u, weights):
    bsz, s, dm = u.shape
    tc, g, ch = S5_TC, S5_GROUPS, S5_GROUP
    nchunk = s // tc
    nctx = CTX_LEN // tc
    rows = nchunk * S5_ROWS
    lanes = tc * ch
    xt = jnp.transpose(u.reshape(bsz, nchunk, tc, g, ch), (3, 1, 0, 2, 4))
    xt = jnp.pad(xt, ((0, 0), (0, 0), (0, S5_ROWS - bsz), (0, 0), (0, 0))).reshape(g, rows, lanes)
    m, w_in, w_re, w_im, a_rows = weights
    p2 = 2 * S5_STATE
    gb = S5_GB
    wspec = lambda r, c: pl.BlockSpec((gb, r, c), lambda i: (i, 0, 0))
    y = pl.pallas_call(
        functools.partial(_s5_kernel, nchunk=nchunk, nctx=nctx),
        grid=(g // gb,),
        in_specs=[wspec(rows, lanes), wspec(lanes, lanes), wspec(lanes, 2 * p2), wspec(p2, lanes),
                  wspec(p2, lanes), wspec(8, p2)],
        out_specs=wspec(rows, lanes),
        out_shape=jax.ShapeDtypeStruct((g, rows, lanes), BF16),
        scratch_shapes=[pltpu.VMEM((gb, rows, 2 * p2), F32), pltpu.VMEM((gb, rows, p2), F32),
                        pltpu.VMEM((gb, rows, p2), F32)],
        compiler_params=_cparams(("arbitrary",)),
        name="s5",
    )(xt, m, w_in, w_re, w_im, a_rows)
    y = y.reshape(g, nchunk, S5_ROWS, tc, ch)[:, nctx:, :bsz]
    return jnp.transpose(y, (2, 1, 3, 0, 4)).reshape(bsz, s - CTX_LEN, dm)


def _glu_kernel(y_ref, x_ref, g1_ref, sh_ref, sc_ref, n2g_ref, w_ref, b_ref, rwt_ref, x1_ref, h2_ref, lg_ref):
    dm = D_MODEL
    z = jax.nn.gelu(y_ref[0].astype(F32)).astype(BF16)
    zz = jnp.dot(z, w_ref[...], preferred_element_type=F32) + b_ref[...]
    y = zz[:, :dm] * jax.nn.sigmoid(zz[:, dm:])
    _post_mixer(y, x_ref[0], g1_ref[0], sh_ref[0], sc_ref[0], n2g_ref[0], rwt_ref[...], x1_ref, h2_ref, lg_ref)


def _glu(y, xall, modtab, layer, norm_g, glu_w, glu_b, router_w):
    bsz, n, dm = y.shape
    ntl = n // TM
    row = lambda b, t: b
    return pl.pallas_call(
        _glu_kernel,
        grid=(bsz, ntl),
        in_specs=[pl.BlockSpec((1, TM, dm), lambda b, t: (b, t, 0)),
                  pl.BlockSpec((1, TM, dm), lambda b, t: (b, t + CTX_LEN // TM, 0)),
                  _mod_spec(layer, 2, row), _mod_spec(layer, 3, row), _mod_spec(layer, 4, row),
                  pl.BlockSpec((1, 1, dm), lambda b, t: (0, 0, 0)),
                  pl.BlockSpec((dm, 2 * dm), lambda b, t: (0, 0)),
                  pl.BlockSpec((1, 2 * dm), lambda b, t: (0, 0)),
                  pl.BlockSpec((N_EXPERTS, dm), lambda b, t: (0, 0))],
        out_specs=[pl.BlockSpec((1, TM, dm), lambda b, t: (b, t, 0)),
                   pl.BlockSpec((1, TM, dm), lambda b, t: (b, t, 0)),
                   pl.BlockSpec((N_EXPERTS, TM), lambda b, t: (0, b * ntl + t))],
        out_shape=[jax.ShapeDtypeStruct((bsz, n, dm), F32),
                   jax.ShapeDtypeStruct((bsz, n, dm), BF16),
                   jax.ShapeDtypeStruct((N_EXPERTS, bsz * n), F32)],
        compiler_params=_cparams(("arbitrary", "arbitrary")),
        name="glu",
    )(y, xall, modtab, modtab, modtab, norm_g.reshape(1, 1, dm), glu_w.astype(BF16), glu_b[None, :], router_w.T)


def kernel(x, c, ctx, c_ctx, mod_w, mod_b, norm1_g, norm2_g, ar_w_in, ar_w_out, lru_conv_w, lru_conv_b, lru_lam, lru_r_w, lru_r_b, lru_i_w, lru_i_b, attn_q_g, attn_k_g, attn_lam_q1, attn_lam_k1, attn_lam_q2, attn_lam_k2, attn_subln_g, s5_a_re, s5_a_im, s5_log_step, s5_b_re, s5_b_im, s5_c_re, s5_c_im, s5_d, s5_glu_w, s5_glu_b, router_w, router_bias, exp_w_gate, exp_w_up, exp_w_down, sh_w_gate, sh_w_up, sh_w_down):
    bsz, n, dm = x.shape
    assert dm == D_MODEL and ctx.shape[1] == CTX_LEN == TM and n % TM == 0 and bsz < 8
    assert mod_w.shape[0] == DEPTH == 2
    s = CTX_LEN + n
    ntb = s // TM
    modtab = _modulation(c, c_ctx, mod_w, mod_b)

    gate, xa, q, k, v = _inproj(x, ctx, modtab, norm1_g[0], ar_w_in[0], attn_q_g[0], attn_k_g[0])
    lru_args = (lru_conv_w[0], lru_conv_b[0], lru_lam[0], lru_r_w[0], lru_r_b[0], lru_i_w[0], lru_i_b[0])
    hf = _lru(xa, *lru_args, reverse=False)
    hr = _lru(xa, *lru_args, reverse=True)
    lam_init = 0.8 - 0.6 * math.exp(-0.3 * 0)
    lam_vecs = jnp.stack([attn_lam_q1[0], attn_lam_k1[0], attn_lam_q2[0], attn_lam_k2[0]], axis=0)
    yb = _attention(q, k, v, lam_vecs, attn_subln_g[0], lam_init)
    x1, h2, lgt = _outproj(hf, hr, gate, yb, x, ctx, modtab, norm2_g[0], ar_w_out[0], router_w[0])
    row0 = lambda i: jnp.where(i % ntb == 0, bsz, i // ntb)
    xall = _moe(h2.reshape(bsz * s, dm), lgt, x1.reshape(bsz * s, dm), modtab, 0, row0, router_bias[0],
                exp_w_gate[0], exp_w_up[0], exp_w_down[0], sh_w_gate[0], sh_w_up[0], sh_w_down[0])
    xall = xall.reshape(bsz, s, dm)

    u = _prenorm(xall, modtab, 1, norm1_g[1])
    weights = _s5_weights(s5_a_re[0], s5_a_im[0], s5_log_step[0], s5_b_re[0], s5_b_im[0], s5_c_re[0],
                          s5_c_im[0], s5_d[0])
    y = _s5(u, weights)
    x1, h2, lgt = _glu(y, xall, modtab, 1, norm2_g[1], s5_glu_w[0], s5_glu_b[0], router_w[1])
    ntl = n // TM
    row1 = lambda i: i // ntl
    out = _moe(h2.reshape(bsz * n, dm), lgt, x1.reshape(bsz * n, dm), modtab, 1, row1, router_bias[1],
               exp_w_gate[1], exp_w_up[1], exp_w_down[1], sh_w_gate[1], sh_w_up[1], sh_w_down[1])
    return out.reshape(bsz, n, dm)
```

```python
import functools
import math

import jax
import jax.numpy as jnp
from jax import lax
from jax.experimental import pallas as pl
from jax.experimental.pallas import tpu as pltpu

F32, BF16, I32 = jnp.float32, jnp.bfloat16, jnp.int32
HIGHEST = lax.Precision.HIGHEST

D_MODEL = 1024
DEPTH = 2
GRID_W = 64
CTX_LEN = 256
EPS = 1e-6
LRU_WIDTH = 512
LRU_BLOCKS = 8
LRU_C = 8.0
ATT_HEADS = 4
ATT_DH = 64
ATT_DV = 128
ATT_QK_W = 512
ATT_WIDTH = 512
ROPE_BASE = 10000.0
EVEN_IN = 2 * LRU_WIDTH + 2 * ATT_QK_W + ATT_WIDTH
S5_GROUP = 16
S5_GROUPS = 64
S5_STATE = 64
S5_TC = 16
S5_ROWS = 8
S5_GB = 2
N_EXPERTS = 64
TOP_K = 6
D_EXPERT = 256
ROUTED_SCALE = 2.5

TM = 256
SEG = 16
RMAX = 2560
TR = 512
ATT_TK = 768
VMEM_LIMIT = 56 * 1024 * 1024


def _cparams(sem):
    return pltpu.CompilerParams(dimension_semantics=sem, vmem_limit_bytes=VMEM_LIMIT)


def _norm_mod(x, g, shift, scale):
    y = x * lax.rsqrt(jnp.mean(x * x, axis=-1, keepdims=True) + EPS) * g
    return y * (1.0 + scale) + shift


def _mod_kernel(c_ref, w_ref, b_ref, o_ref):
    c = c_ref[...]
    s = c * jax.nn.sigmoid(c)
    o_ref[0] = jnp.dot(s, w_ref[0], precision=HIGHEST, preferred_element_type=F32) + b_ref[0]


def _modulation(c, c_ctx, mod_w, mod_b):
    bsz, dm = c.shape
    cc = jnp.concatenate([c, c_ctx[None, :], jnp.zeros((8 - bsz - 1, dm), F32)], axis=0)
    out = pl.pallas_call(
        _mod_kernel,
        grid=(DEPTH, 6),
        in_specs=[pl.BlockSpec((8, dm), lambda l, j: (0, 0)),
                  pl.BlockSpec((1, dm, dm), lambda l, j: (l, 0, j)),
                  pl.BlockSpec((1, 1, dm), lambda l, j: (l, 0, j))],
        out_specs=pl.BlockSpec((1, 8, dm), lambda l, j: (l, 0, j)),
        out_shape=jax.ShapeDtypeStruct((DEPTH, 8, 6 * dm), F32),
        compiler_params=_cparams(("arbitrary", "arbitrary")),
        name="modulation",
    )(cc, mod_w, mod_b.reshape(DEPTH, 1, 6 * dm))
    return out.reshape(DEPTH * 8 * 6, 1, dm)


def _mod_spec(layer, part, row_fn):
    return pl.BlockSpec((1, 1, D_MODEL), lambda *ids: ((layer * 8 + row_fn(*ids)) * 6 + part, 0, 0))


def _rope_tables(n):
    rows = n // GRID_W
    r, col = jnp.meshgrid(jnp.arange(rows), jnp.arange(GRID_W), indexing="ij")
    pos = jnp.stack([r.reshape(-1), col.reshape(-1)], axis=-1).astype(F32)
    n_freq = ATT_DH // 4
    inv_freq = ROPE_BASE ** (-jnp.arange(n_freq, dtype=F32) / n_freq)
    ang = pos[:, :, None] * inv_freq
    cos, sin = jnp.cos(ang), jnp.sin(ang)
    zero = jnp.zeros_like(sin)
    cos64 = jnp.stack([cos, cos], axis=2).reshape(n, ATT_DH)
    sin_lo = jnp.stack([zero, sin], axis=2).reshape(n, ATT_DH)
    sin_hi = jnp.stack([-sin, zero], axis=2).reshape(n, ATT_DH)

    def full(tab, ctx_val):
        tab = jnp.concatenate([jnp.full((CTX_LEN, ATT_DH), ctx_val, F32), tab], axis=0)
        return jnp.concatenate([tab, tab], axis=1)

    return full(cos64, 1.0), full(sin_lo, 0.0), full(sin_hi, 0.0)


def _qk_post(t, gain, ones_bd, cos, sin_lo, sin_hi):
    ss = jnp.dot((t * t).astype(BF16), ones_bd, preferred_element_type=F32) * (1.0 / ATT_DH)
    tn = t * lax.rsqrt(ss + EPS) * gain
    w = tn.shape[1]
    return tn * cos + pltpu.roll(tn, 16, 1) * sin_lo + pltpu.roll(tn, w - 16, 1) * sin_hi


def _inproj_kernel(x_ref, c_ref, sh_ref, sc_ref, g_ref, w_ref, qg_ref, kg_ref, ones_ref,
                   cos_ref, slo_ref, shi_ref, gate_ref, xa_ref, q_ref, k_ref, v_ref):
    t = pl.program_id(1)
    x = jnp.where(t == 0, c_ref[0], x_ref[0])
    h = _norm_mod(x, g_ref[0], sh_ref[0], sc_ref[0])
    z = jnp.dot(h.astype(BF16), w_ref[...], preferred_element_type=F32)
    lw, qw = LRU_WIDTH, ATT_QK_W
    gate_ref[0] = z[:, 0:lw].astype(BF16)
    xa_ref[0] = z[:, lw:2 * lw].astype(BF16)
    tile4 = lambda a: jnp.concatenate([a, a, a, a], axis=1)
    cos, slo, shi = tile4(cos_ref[...]), tile4(slo_ref[...]), tile4(shi_ref[...])
    ones_bd = ones_ref[...]
    q = _qk_post(z[:, 2 * lw:2 * lw + qw], qg_ref[...], ones_bd, cos, slo, shi)
    q_ref[0] = (q * (ATT_DH ** -0.5 * math.log2(math.e))).astype(BF16)
    k = _qk_post(z[:, 2 * lw + qw:2 * lw + 2 * qw], kg_ref[...], ones_bd, cos, slo, shi)
    k_ref[0] = k.astype(BF16)
    v_ref[0] = z[:, 2 * lw + 2 * qw:].astype(BF16)


def _inproj(x, ctx, modtab, norm_g, w_in, q_g, k_g):
    bsz, n, dm = x.shape
    ntb = (CTX_LEN + n) // TM
    s = CTX_LEN + n
    row = lambda b, t: jnp.where(t == 0, bsz, b)
    cos, slo, shi = _rope_tables(n)
    ones_bd = jnp.kron(jnp.eye(ATT_QK_W // ATT_DH, dtype=F32), jnp.ones((ATT_DH, ATT_DH), F32)).astype(BF16)
    tile_g = lambda g: jnp.tile(g, ATT_QK_W // ATT_DH)[None, :]
    tab_spec = pl.BlockSpec((TM, 2 * ATT_DH), lambda b, t: (t, 0))
    out_spec = pl.BlockSpec((1, TM, LRU_WIDTH), lambda b, t: (b, t, 0))
    out_sds = jax.ShapeDtypeStruct((bsz, s, LRU_WIDTH), BF16)
    return pl.pallas_call(
        _inproj_kernel,
        grid=(bsz, ntb),
        in_specs=[pl.BlockSpec((1, TM, dm), lambda b, t: (b, jnp.maximum(t - 1, 0), 0)),
                  pl.BlockSpec((1, TM, dm), lambda b, t: (b, 0, 0)),
                  _mod_spec(0, 0, row), _mod_spec(0, 1, row),
                  pl.BlockSpec((1, 1, dm), lambda b, t: (0, 0, 0)),
                  pl.BlockSpec((dm, EVEN_IN), lambda b, t: (0, 0)),
                  pl.BlockSpec((1, ATT_QK_W), lambda b, t: (0, 0)),
                  pl.BlockSpec((1, ATT_QK_W), lambda b, t: (0, 0)),
                  pl.BlockSpec((ATT_QK_W, ATT_QK_W), lambda b, t: (0, 0)),
                  tab_spec, tab_spec, tab_spec],
        out_specs=[out_spec] * 5,
        out_shape=[out_sds] * 5,
        compiler_params=_cparams(("arbitrary", "arbitrary")),
        name="inproj",
    )(x, ctx, modtab, modtab, norm_g.reshape(1, 1, dm), w_in.astype(BF16), tile_g(q_g), tile_g(k_g), ones_bd,
      cos, slo, shi)


def _lru_kernel(xa_ref, xp_ref, xn_ref, cw_ref, cb_ref, lam_ref, w_ref, bias_ref, h_ref,
                a_scr, b_scr, c_scr, *, ntb, reverse):
    s = pl.program_id(1)
    ti = jnp.where(s == 0, 0, ntb - s) if reverse else s
    lw = LRU_WIDTH
    x = xa_ref[0].astype(F32)
    row = lax.broadcasted_iota(I32, (TM, lw), 0)
    has_prev = jnp.where(ti > 1, 1.0, 0.0)
    has_next = jnp.where((ti > 0) & (ti < ntb - 1), 1.0, 0.0)
    prev = xp_ref[0].astype(F32) * has_prev
    nxt = xn_ref[0].astype(F32) * has_next
    xm1 = jnp.where(row == 0, prev[7:8], pltpu.roll(x, 1, 0))
    xm2 = jnp.where(row == 0, prev[6:7], jnp.where(row == 1, prev[7:8], pltpu.roll(x, 2, 0)))
    xp1 = jnp.where(row == TM - 1, nxt[0:1], pltpu.roll(x, TM - 1, 0))
    cw = cw_ref[...]
    xc = cw[0:1] * xm2 + cw[1:2] * xm1 + cw[2:3] * x + cw[3:4] * xp1 + cb_ref[...]

    z = jnp.dot(xc.astype(BF16), w_ref[...], preferred_element_type=F32) + bias_ref[...]
    r = jax.nn.sigmoid(z[:, :lw])
    ig = jax.nn.sigmoid(z[:, lw:])
    neg_lam = -lam_ref[...]
    softplus = jnp.maximum(neg_lam, 0.0) + jnp.log1p(jnp.exp(-jnp.abs(neg_lam)))
    log_a = (-LRU_C) * r * softplus
    a = jnp.exp(log_a)
    b = jnp.sqrt(-jnp.tanh(log_a) * (a * a + 1.0)) * (ig * xc)

    r8 = row & 7
    for sft in (1, 2, 4):
        if reverse:
            a_s, b_s, m = pltpu.roll(a, TM - sft, 0), pltpu.roll(b, TM - sft, 0), r8 < 8 - sft
        else:
            a_s, b_s, m = pltpu.roll(a, sft, 0), pltpu.roll(b, sft, 0), r8 >= sft
        b = jnp.where(m, a * b_s + b, b)
        a = jnp.where(m, a * a_s, a)
    a_scr[...] = a
    b_scr[...] = b

    @pl.when(s == 0)
    def _():
        c_scr[...] = jnp.zeros_like(c_scr)

    ng = TM // 8

    def body(j, carry):
        g = (ng - 1 - j) if reverse else j
        off = pl.multiple_of(g * 8, 8)
        h = a_scr[pl.ds(off, 8), :] * carry + b_scr[pl.ds(off, 8), :]
        h_ref[0, pl.ds(off, 8), :] = h
        last = h[0:1] if reverse else h[7:8]
        return jnp.broadcast_to(last, (8, lw))

    c_scr[...] = lax.fori_loop(0, ng, body, c_scr[...], unroll=4)


def _lru(xa, conv_w, conv_b, lam, r_w, r_b, i_w, i_b, reverse):
    bsz, s, lw = xa.shape
    ntb = s // TM
    d = 1 if reverse else 0
    bd = lambda w: jax.scipy.linalg.block_diag(*[w[i] for i in range(LRU_BLOCKS)])
    w = jnp.concatenate([bd(r_w[d]), bd(i_w[d])], axis=1).astype(BF16)
    bias = jnp.concatenate([r_b[d], i_b[d]])[None, :]
    cw = jnp.concatenate([conv_w, jnp.zeros((4, lw), F32)], axis=0)
    if reverse:
        tile = lambda t: jnp.where(t == 0, 0, ntb - t)
    else:
        tile = lambda t: t
    nb8 = s // 8
    return pl.pallas_call(
        functools.partial(_lru_kernel, ntb=ntb, reverse=reverse),
        grid=(bsz, ntb),
        in_specs=[pl.BlockSpec((1, TM, lw), lambda b, t: (b, tile(t), 0)),
                  pl.BlockSpec((1, 8, lw), lambda b, t: (b, jnp.maximum(tile(t) * (TM // 8) - 1, 0), 0)),
                  pl.BlockSpec((1, 8, lw), lambda b, t: (b, jnp.minimum((tile(t) + 1) * (TM // 8), nb8 - 1), 0)),
                  pl.BlockSpec((8, lw), lambda b, t: (0, 0)),
                  pl.BlockSpec((1, lw), lambda b, t: (0, 0)),
                  pl.BlockSpec((1, lw), lambda b, t: (0, 0)),
                  pl.BlockSpec((lw, 2 * lw), lambda b, t: (0, 0)),
                  pl.BlockSpec((1, 2 * lw), lambda b, t: (0, 0))],
        out_specs=pl.BlockSpec((1, TM, lw), lambda b, t: (b, tile(t), 0)),
        out_shape=jax.ShapeDtypeStruct((bsz, s, lw), F32),
        scratch_shapes=[pltpu.VMEM((TM, lw), F32), pltpu.VMEM((TM, lw), F32), pltpu.VMEM((8, lw), F32)],
        compiler_params=_cparams(("arbitrary", "arbitrary")),
        name="lru_rev" if reverse else "lru_fwd",
    )(xa, xa, xa, cw, conv_b[None, :], lam[d][None, :], w, bias)


def _attn_kernel(lamv_ref, q_ref, k_ref, v_ref, sg_ref, o_ref, m_scr, l_scr, acc_scr, al_scr, s_scr, p_scr, *,
                 nkv, lam_init):
    t = pl.program_id(2)
    q = q_ref[0]
    dh = ATT_DH
    dv = ATT_DV
    m_scr[...] = jnp.full_like(m_scr, -jnp.inf)
    l_scr[...] = jnp.zeros_like(l_scr)
    acc_scr[...] = jnp.zeros_like(acc_scr)

    def scores(buf, off, size):
        kc = k_ref[0, pl.ds(off, size), :]
        for mi in range(2):
            s_scr[buf, mi, :, 0:size] = lax.dot_general(
                q[:, mi * dh:(mi + 1) * dh], kc[:, mi * dh:(mi + 1) * dh], (((1,), (1,)), ((), ())),
                preferred_element_type=F32)

    def softmax(buf, size):
        nlb = size // 128
        groups = [(slice(r * 16, (r + 1) * 16), mi) for r in range(TM // 16) for mi in range(2)]
        for rows, mi in groups:
            sc = s_scr[buf, mi, rows, 0:size]
            mx = functools.reduce(jnp.maximum, [sc[:, i * 128:(i + 1) * 128] for i in range(nlb)])
            m_old = m_scr[mi, rows, :]
            m_new = jnp.maximum(m_old, jnp.max(mx, axis=1, keepdims=True))
            m_scr[mi, rows, :] = m_new
            al_scr[mi, rows, :] = jnp.exp2(m_old - m_new)
        for rows, mi in groups:
            m_new = m_scr[mi, rows, :]
            ps = [jnp.exp2(s_scr[buf, mi, rows, i * 128:(i + 1) * 128] - m_new) for i in range(nlb)]
            l_scr[mi, rows, :] = al_scr[mi, rows, :] * l_scr[mi, rows, :] + functools.reduce(jnp.add, ps)
            p_scr[buf, mi, rows, 0:size] = jnp.concatenate(ps, axis=1).astype(BF16)

    def values(buf, off, size):
        vc = v_ref[0, pl.ds(off, size), :]
        pv = jnp.dot(p_scr[buf, :, :, 0:size].reshape(2 * TM, size), vc, preferred_element_type=F32)
        acc_scr[...] = al_scr[...] * acc_scr[...] + pv.reshape(2, TM, dv)

    @pl.when(t == 0)
    def _():
        scores(0, 0, CTX_LEN)
        softmax(0, CTX_LEN)
        values(0, 0, CTX_LEN)

    @pl.when(t > 0)
    def _():
        tk = ATT_TK
        nchunk = nkv // tk
        at = lambda j: pl.multiple_of(j * tk, tk)
        scores(0, 0, tk)

        def body(i, carry):
            j0 = 2 * i
            scores(1, at(j0 + 1), tk)
            softmax(0, tk)
            values(0, at(j0), tk)
            scores(0, at(jnp.minimum(j0 + 2, nchunk - 1)), tk)
            softmax(1, tk)
            values(1, at(j0 + 1), tk)
            return carry

        lax.fori_loop(0, nchunk // 2, body, 0)
        if nchunk % 2:
            softmax(0, tk)
            values(0, (nchunk - 1) * tk, tk)

    lv = lamv_ref[...]
    lam = (jnp.exp(jnp.sum(lv[0:1] * lv[1:2], axis=1, keepdims=True))
           - jnp.exp(jnp.sum(lv[2:3] * lv[3:4], axis=1, keepdims=True)) + lam_init)
    l0 = jnp.sum(l_scr[0], axis=1, keepdims=True)
    l1 = jnp.sum(l_scr[1], axis=1, keepdims=True)
    o = acc_scr[0] / l0 - lam * (acc_scr[1] / l1)
    y = o * lax.rsqrt(jnp.mean(o * o, axis=-1, keepdims=True) + EPS) * sg_ref[...] * (1.0 - lam_init)
    o_ref[0] = y.astype(BF16)


def _attention(q, k, v, lam_vecs, subln_g, lam_init):
    bsz, s, _ = q.shape
    ntb = s // TM
    assert s % ATT_TK == 0
    dv = ATT_DV
    lamv = jnp.concatenate([jnp.pad(lam_vecs, ((0, 0), (0, dv - ATT_DH))), jnp.zeros((4, dv), F32)], axis=0)
    return pl.pallas_call(
        functools.partial(_attn_kernel, nkv=s, lam_init=lam_init),
        grid=(bsz, ATT_HEADS, ntb),
        in_specs=[pl.BlockSpec((8, dv), lambda b, h, t: (0, 0)),
                  pl.BlockSpec((1, TM, dv), lambda b, h, t: (b, t, h)),
                  pl.BlockSpec((1, s, dv), lambda b, h, t: (b, 0, h)),
                  pl.BlockSpec((1, s, dv), lambda b, h, t: (b, 0, h)),
                  pl.BlockSpec((1, dv), lambda b, h, t: (0, 0))],
        out_specs=pl.BlockSpec((1, TM, dv), lambda b, h, t: (b, t, h)),
        out_shape=jax.ShapeDtypeStruct((bsz, s, ATT_WIDTH), BF16),
        scratch_shapes=[pltpu.VMEM((2, TM, dv), F32)] * 4
        + [pltpu.VMEM((2, 2, TM, ATT_TK), F32), pltpu.VMEM((2, 2, TM, ATT_TK), BF16)],
        compiler_params=_cparams(("arbitrary", "arbitrary", "arbitrary")),
        name="diff_attention",
    )(lamv, q, k, v, subln_g[None, :])


def _post_mixer(y, x, g1, sh2, sc2, n2g, rwt, x1_ref, h2_ref, lg_ref):
    x1 = x + g1 * y
    x1_ref[0] = x1
    h2 = _norm_mod(x1, n2g, sh2, sc2)
    h2_ref[0] = h2.astype(BF16)
    lg_ref[...] = lax.dot_general(rwt, h2, (((1,), (1,)), ((), ())), precision=HIGHEST,
                                  preferred_element_type=F32)


def _outproj_kernel(hf_ref, hr_ref, gate_ref, yb_ref, x_ref, c_ref, g1_ref, sh_ref, sc_ref, n2g_ref, w_ref,
                    rwt_ref, x1_ref, h2_ref, lg_ref):
    t = pl.program_id(1)
    x = jnp.where(t == 0, c_ref[0], x_ref[0])
    ya = ((hf_ref[0] + hr_ref[0]) * jax.nn.gelu(gate_ref[0].astype(F32))).astype(BF16)
    lw = LRU_WIDTH
    y = (jnp.dot(ya, w_ref[0:lw, :], preferred_element_type=F32)
         + jnp.dot(yb_ref[0], w_ref[lw:, :], preferred_element_type=F32))
    _post_mixer(y, x, g1_ref[0], sh_ref[0], sc_ref[0], n2g_ref[0], rwt_ref[...], x1_ref, h2_ref, lg_ref)


def _outproj(hf, hr, gate, yb, x, ctx, modtab, norm_g, w_out, router_w):
    bsz, s, lw = hf.shape
    dm = D_MODEL
    ntb = s // TM
    row = lambda b, t: jnp.where(t == 0, bsz, b)
    half = pl.BlockSpec((1, TM, lw), lambda b, t: (b, t, 0))
    return pl.pallas_call(
        _outproj_kernel,
        grid=(bsz, ntb),
        in_specs=[half, half, half, half,
                  pl.BlockSpec((1, TM, dm), lambda b, t: (b, jnp.maximum(t - 1, 0), 0)),
                  pl.BlockSpec((1, TM, dm), lambda b, t: (b, 0, 0)),
                  _mod_spec(0, 2, row), _mod_spec(0, 3, row), _mod_spec(0, 4, row),
                  pl.BlockSpec((1, 1, dm), lambda b, t: (0, 0, 0)),
                  pl.BlockSpec((2 * lw, dm), lambda b, t: (0, 0)),
                  pl.BlockSpec((N_EXPERTS, dm), lambda b, t: (0, 0))],
        out_specs=[pl.BlockSpec((1, TM, dm), lambda b, t: (b, t, 0)),
                   pl.BlockSpec((1, TM, dm), lambda b, t: (b, t, 0)),
                   pl.BlockSpec((N_EXPERTS, TM), lambda b, t: (0, b * ntb + t))],
        out_shape=[jax.ShapeDtypeStruct((bsz, s, dm), F32),
                   jax.ShapeDtypeStruct((bsz, s, dm), BF16),
                   jax.ShapeDtypeStruct((N_EXPERTS, bsz * s), F32)],
        compiler_params=_cparams(("arbitrary", "arbitrary")),
        name="outproj",
    )(hf, hr, gate, yb, x, ctx, modtab, modtab, modtab, norm_g.reshape(1, 1, dm), w_out.astype(BF16),
      router_w.T)


def _route_kernel(lg_ref, bias_ref, tri_ref, low_ref, pos_ref, gate_ref, cnt_ref):
    ne = N_EXPERTS
    aff = jax.nn.sigmoid(lg_ref[...])
    work = aff + bias_ref[:, 0:1]
    eidx = lax.broadcasted_iota(I32, (ne, TM), 0)
    sels = []
    for _ in range(TOP_K):
        mx = jnp.max(work, axis=0, keepdims=True)
        am = jnp.min(jnp.where(work == mx, eidx, ne), axis=0, keepdims=True)
        sk = eidx == am
        sels.append(sk)
        work = jnp.where(sk, -jnp.inf, work)
    sel = sels[0]
    for sk in sels[1:]:
        sel = sel | sk
    self = jnp.where(sel, 1.0, 0.0)
    s_sel = aff * self
    gates = s_sel / jnp.sum(s_sel, axis=0, keepdims=True) * ROUTED_SCALE
    rank = jnp.dot(self.astype(BF16), tri_ref[...], preferred_element_type=F32)
    cnt = jnp.sum(self, axis=1, keepdims=True)
    cnt_seg = jnp.floor((cnt + (SEG - 1)) * (1.0 / SEG)) * SEG
    cnt_b = jnp.broadcast_to(cnt_seg, (ne, 128))
    seg_off = jnp.dot(low_ref[...], cnt_b.astype(BF16), preferred_element_type=F32)
    lpos = seg_off[:, 0:1] + rank
    pos_rows, gate_rows = [], []
    for sk in sels:
        pos_rows.append(jnp.sum(jnp.where(sk, lpos, 0.0), axis=0, keepdims=True))
        gate_rows.append(jnp.sum(jnp.where(sk, gates, 0.0), axis=0, keepdims=True))
    for _ in range(8 - TOP_K):
        pos_rows.append(jnp.full((1, TM), -1.0, F32))
        gate_rows.append(jnp.zeros((1, TM), F32))
    pos_ref[...] = jnp.concatenate(pos_rows, axis=0).astype(I32)
    gate_ref[...] = jnp.concatenate(gate_rows, axis=0)
    cnt_ref[0] = cnt_b.astype(I32)


def _route(lgt, router_bias):
    ne, t = lgt.shape
    nt = t // TM
    tri = jnp.triu(jnp.ones((TM, TM), F32), k=1).astype(BF16)
    low = jnp.tril(jnp.ones((ne, ne), F32), k=-1).astype(BF16)
    bias = jnp.broadcast_to(router_bias[:, None], (ne, 128))
    pos, gate, cnt = pl.pallas_call(
        _route_kernel,
        grid=(nt,),
        in_specs=[pl.BlockSpec((ne, TM), lambda i: (0, i)),
                  pl.BlockSpec((ne, 128), lambda i: (0, 0)),
                  pl.BlockSpec((TM, TM), lambda i: (0, 0)),
                  pl.BlockSpec((ne, ne), lambda i: (0, 0))],
        out_specs=[pl.BlockSpec((8, TM), lambda i: (0, i)),
                   pl.BlockSpec((8, TM), lambda i: (0, i)),
                   pl.BlockSpec((1, ne, 128), lambda i: (i, 0, 0))],
        out_shape=[jax.ShapeDtypeStruct((8, t), I32),
                   jax.ShapeDtypeStruct((8, t), F32),
                   jax.ShapeDtypeStruct((nt, ne, 128), I32)],
        compiler_params=_cparams(("arbitrary",)),
        name="route",
    )(lgt, bias, tri, low)
    return pos, gate, cnt[:, :, 0]


def _moe_layout(cnt_seg, nblk_max):
    tot = jnp.sum(cnt_seg, axis=0)
    region = (tot + TR - 1) // TR * TR
    region_end = jnp.cumsum(region)
    goff = (region_end - region)[None, :] + jnp.cumsum(cnt_seg, axis=0) - cnt_seg
    packed = ((goff // SEG) << 5) | (cnt_seg // SEG)
    nblk = region_end[-1] // TR
    blk = jnp.arange(nblk_max, dtype=I32)
    blk_e = jnp.sum((region_end[None, :] // TR <= blk[:, None]).astype(I32), axis=1)
    blk_e = jnp.minimum(blk_e, N_EXPERTS - 1).astype(I32)
    tile_info = jnp.sum(cnt_seg, axis=1) | (jnp.any(cnt_seg >= 64, axis=1).astype(I32) << 16)
    return packed.reshape(-1).astype(I32), tile_info.astype(I32), blk_e, nblk.astype(I32).reshape(1)


def _segment_copy(stage, hbm, sem, to_sorted, lo, go, size):
    lo, go = pl.multiple_of(lo, SEG), pl.multiple_of(go, SEG)
    a, b = stage.at[pl.ds(lo, size)], hbm.at[pl.ds(go, size)]
    return pltpu.make_async_copy(a, b, sem) if to_sorted else pltpu.make_async_copy(b, a, sem)


def _segment_starts(pk_ref, tile, info, stage, hbm, sem, to_sorted):
    unpack = lambda pk: ((pk & 31) * SEG, (pk >> 5) * SEG)

    @pl.when((info >> 16) != 0)
    def _():
        def expert(e, loff):
            cnt, goff = unpack(pk_ref[tile * N_EXPERTS + e])

            def chunk(j, carry):
                _segment_copy(stage, hbm, sem, to_sorted, loff + j * 64, goff + j * 64, 64).start()
                return carry

            lax.fori_loop(0, cnt >> 6, chunk, 0)
            return loff + cnt

        lax.fori_loop(0, N_EXPERTS, expert, 0)

    loff = 0
    for e in range(N_EXPERTS):
        cnt, goff = unpack(pk_ref[tile * N_EXPERTS + e])
        done = (cnt >> 6) << 6
        for size in (32, 16):
            bit = (cnt & size) != 0

            @pl.when(bit)
            def _(lo=loff + done, go=goff + done, size=size):
                _segment_copy(stage, hbm, sem, to_sorted, lo, go, size).start()

            done = done + jnp.where(bit, size, 0)
        loff = loff + cnt


def _segment_wait(total, stage, hbm, sem, to_sorted):
    size = 1 << (RMAX.bit_length() - 1)
    while size >= SEG:
        @pl.when((total & size) != 0)
        def _(size=size):
            _segment_copy(stage, hbm, sem, to_sorted, 0, 0, size).wait()

        size //= 2


def _slot_matrix(pos, weight_rows=None):
    riota = lax.broadcasted_iota(I32, (RMAX, TM), 0)
    out = jnp.zeros((RMAX, TM), F32)
    for k in range(TOP_K):
        w = 1.0 if weight_rows is None else weight_rows[k:k + 1]
        out = jnp.where(riota == pos[k:k + 1], w, out)
    return out.astype(BF16)


def _dispatch_kernel(pk_ref, tot_ref, pos_ref, h_ref, xs_in, xs_ref, stage, sem):
    i = pl.program_id(0)
    slot = i % 2
    stage[slot] = jnp.dot(_slot_matrix(pos_ref[...]), h_ref[...], preferred_element_type=F32).astype(BF16)

    @pl.when(i > 0)
    def _():
        rows = tot_ref[jnp.maximum(i - 1, 0)] & 0xFFFF
        _segment_wait(rows, stage.at[1 - slot], xs_ref, sem.at[1 - slot], True)

    _segment_starts(pk_ref, i, tot_ref[i], stage.at[slot], xs_ref, sem.at[slot], True)

    @pl.when(i == pl.num_programs(0) - 1)
    def _():
        _segment_wait(tot_ref[i] & 0xFFFF, stage.at[slot], xs_ref, sem.at[slot], True)


def _dispatch(packed, tile_rows, pos, h2, nrows):
    t, dm = h2.shape
    nt = t // TM
    xs0 = jnp.zeros((nrows, dm), BF16)
    return pl.pallas_call(
        _dispatch_kernel,
        grid_spec=pltpu.PrefetchScalarGridSpec(
            num_scalar_prefetch=2,
            grid=(nt,),
            in_specs=[pl.BlockSpec((8, TM), lambda i, pk, tot: (0, i)),
                      pl.BlockSpec((TM, dm), lambda i, pk, tot: (i, 0)),
                      pl.BlockSpec(memory_space=pl.ANY)],
            out_specs=pl.BlockSpec(memory_space=pl.ANY),
            scratch_shapes=[pltpu.VMEM((2, RMAX, dm), BF16), pltpu.SemaphoreType.DMA((2,))]),
        out_shape=jax.ShapeDtypeStruct((nrows, dm), BF16),
        input_output_aliases={4: 0},
        compiler_params=_cparams(("arbitrary",)),
        name="moe_dispatch",
    )(packed, tile_rows, pos, h2, xs0)


def _expert_kernel(be_ref, nb_ref, x_ref, wg_ref, wu_ref, wd_ref, y_ref, wg_s, wu_s, wd_s):
    j = pl.program_id(0)
    changed = be_ref[j] != be_ref[jnp.maximum(j - 1, 0)]

    @pl.when((j == 0) | changed)
    def _():
        wg_s[...] = wg_ref[0].astype(BF16)
        wu_s[...] = wu_ref[0].astype(BF16)
        wd_s[...] = wd_ref[0].astype(BF16)

    @pl.when(j < nb_ref[0])
    def _():
        x = x_ref[...]
        g = jnp.dot(x, wg_s[...], preferred_element_type=F32)
        u = jnp.dot(x, wu_s[...], preferred_element_type=F32)
        a = (g * jax.nn.sigmoid(g) * u).astype(BF16)
        y_ref[...] = jnp.dot(a, wd_s[...], preferred_element_type=F32).astype(BF16)


def _experts(blk_e, nblk, xs, w_gate, w_up, w_down):
    nrows, dm = xs.shape
    nblk_max = nrows // TR
    de = D_EXPERT
    row_blk = lambda j, be, nb: (jnp.minimum(j, nb[0] - 1), 0)
    return pl.pallas_call(
        _expert_kernel,
        grid_spec=pltpu.PrefetchScalarGridSpec(
            num_scalar_prefetch=2,
            grid=(nblk_max,),
            in_specs=[pl.BlockSpec((TR, dm), row_blk),
                      pl.BlockSpec((1, dm, de), lambda j, be, nb: (be[j], 0, 0)),
                      pl.BlockSpec((1, dm, de), lambda j, be, nb: (be[j], 0, 0)),
                      pl.BlockSpec((1, de, dm), lambda j, be, nb: (be[j], 0, 0))],
            out_specs=pl.BlockSpec((TR, dm), row_blk),
            scratch_shapes=[pltpu.VMEM((dm, de), BF16), pltpu.VMEM((dm, de), BF16), pltpu.VMEM((de, dm), BF16)]),
        out_shape=jax.ShapeDtypeStruct((nrows, dm), BF16),
        input_output_aliases={2: 0},
        compiler_params=_cparams(("arbitrary",)),
        name="moe_experts",
    )(blk_e, nblk, xs, w_gate, w_up, w_down)


def _combine_kernel(pk_ref, tot_ref, pos_ref, gate_ref, h_ref, x1_ref, g2_ref, wsg_ref, wsu_ref, wsd_ref, ys_ref,
                    o_ref, stage, sem):
    i = pl.program_id(0)
    slot = i % 2

    @pl.when(i == 0)
    def _():
        stage[...] = jnp.zeros_like(stage)
        _segment_starts(pk_ref, 0, tot_ref[0], stage.at[0], ys_ref, sem.at[0], False)

    last = pl.num_programs(0) - 1

    @pl.when(i < last)
    def _():
        nxt = jnp.minimum(i + 1, last)
        _segment_starts(pk_ref, nxt, tot_ref[nxt], stage.at[1 - slot], ys_ref, sem.at[1 - slot], False)

    gates = _slot_matrix(pos_ref[...], gate_ref[...])
    h = h_ref[...]
    g = jnp.dot(h, wsg_ref[...], preferred_element_type=F32)
    u = jnp.dot(h, wsu_ref[...], preferred_element_type=F32)
    shared = jnp.dot((g * jax.nn.sigmoid(g) * u).astype(BF16), wsd_ref[...], preferred_element_type=F32)
    _segment_wait(tot_ref[i] & 0xFFFF, stage.at[slot], ys_ref, sem.at[slot], False)
    routed = lax.dot_general(gates, stage[slot], (((0,), (0,)), ((), ())), preferred_element_type=F32)
    o_ref[...] = x1_ref[...] + g2_ref[0] * (routed + shared)


def _combine(packed, tile_rows, pos, gate, h2, x1, modtab, layer, row_fn, ys, ws_gate, ws_up, ws_down):
    t, dm = h2.shape
    nt = t // TM
    de = D_EXPERT
    const = lambda shape: pl.BlockSpec(shape, lambda i, pk, tot: (0,) * len(shape))
    return pl.pallas_call(
        _combine_kernel,
        grid_spec=pltpu.PrefetchScalarGridSpec(
            num_scalar_prefetch=2,
            grid=(nt,),
            in_specs=[pl.BlockSpec((8, TM), lambda i, pk, tot: (0, i)),
                      pl.BlockSpec((8, TM), lambda i, pk, tot: (0, i)),
                      pl.BlockSpec((TM, dm), lambda i, pk, tot: (i, 0)),
                      pl.BlockSpec((TM, dm), lambda i, pk, tot: (i, 0)),
                      pl.BlockSpec((1, 1, dm), lambda i, pk, tot: ((layer * 8 + row_fn(i)) * 6 + 5, 0, 0)),
                      const((dm, de)), const((dm, de)), const((de, dm)),
                      pl.BlockSpec(memory_space=pl.ANY)],
            out_specs=pl.BlockSpec((TM, dm), lambda i, pk, tot: (i, 0)),
            scratch_shapes=[pltpu.VMEM((2, RMAX, dm), BF16), pltpu.SemaphoreType.DMA((2,))]),
        out_shape=jax.ShapeDtypeStruct((t, dm), F32),
        compiler_params=_cparams(("arbitrary",)),
        name="moe_combine",
    )(packed, tile_rows, pos, gate, h2, x1, modtab, ws_gate.astype(BF16), ws_up.astype(BF16),
      ws_down.astype(BF16), ys)


def _moe(h2, lgt, x1, modtab, layer, row_fn, router_bias, w_gate, w_up, w_down, ws_gate, ws_up, ws_down):
    t = h2.shape[0]
    nt = t // TM
    max_rows = t * TOP_K + nt * N_EXPERTS * (SEG - 1) + N_EXPERTS * (TR - 1)
    nblk_max = -(-max_rows // TR)
    pos, gate, cnt_seg = _route(lgt, router_bias)
    packed, tile_rows, blk_e, nblk = _moe_layout(cnt_seg, nblk_max)
    xs = _dispatch(packed, tile_rows, pos, h2, nblk_max * TR)
    ys = _experts(blk_e, nblk, xs, w_gate, w_up, w_down)
    return _combine(packed, tile_rows, pos, gate, h2, x1, modtab, layer, row_fn, ys, ws_gate, ws_up, ws_down)


def _prenorm_kernel(x_ref, sh_ref, sc_ref, g_ref, u_ref):
    u_ref[0] = _norm_mod(x_ref[0], g_ref[0], sh_ref[0], sc_ref[0]).astype(BF16)


def _prenorm(xall, modtab, layer, norm_g):
    bsz, s, dm = xall.shape
    ntb = s // TM
    row = lambda b, t: jnp.where(t == 0, bsz, b)
    return pl.pallas_call(
        _prenorm_kernel,
        grid=(bsz, ntb),
        in_specs=[pl.BlockSpec((1, TM, dm), lambda b, t: (b, t, 0)),
                  _mod_spec(layer, 0, row), _mod_spec(layer, 1, row),
                  pl.BlockSpec((1, 1, dm), lambda b, t: (0, 0, 0))],
        out_specs=pl.BlockSpec((1, TM, dm), lambda b, t: (b, t, 0)),
        out_shape=jax.ShapeDtypeStruct((bsz, s, dm), BF16),
        compiler_params=_cparams(("arbitrary", "arbitrary")),
        name="prenorm",
    )(xall, modtab, modtab, norm_g.reshape(1, 1, dm))


def _cmul(x, y):
    return x[0] * y[0] - x[1] * y[1], x[0] * y[1] + x[1] * y[0]


def _s5_weights(a_re, a_im, log_step, b_re, b_im, c_re, c_im, d_skip):
    tc, g, p, ch = S5_TC, S5_GROUPS, S5_STATE, S5_GROUP
    lam = (jnp.minimum(a_re, -1e-4), a_im)
    step = jnp.exp(log_step)
    mag = jnp.exp(lam[0] * step)
    lam_bar = (mag * jnp.cos(lam[1] * step), mag * jnp.sin(lam[1] * step))
    inv = 1.0 / (lam[0] * lam[0] + lam[1] * lam[1])
    coef = _cmul((lam_bar[0] - 1.0, lam_bar[1]), (lam[0] * inv, -lam[1] * inv))
    b_bar = _cmul((coef[0][..., None], coef[1][..., None]), (b_re, b_im))
    pw = [(jnp.ones_like(mag), jnp.zeros_like(mag))]
    for _ in range(tc):
        pw.append(_cmul(pw[-1], lam_bar))
    pw = (jnp.stack([q[0] for q in pw], axis=1), jnp.stack([q[1] for q in pw], axis=1))
    at = lambda d, idx: (pw[0][d, idx], pw[1][d, idx])
    cp = _cmul((c_re[:, None], c_im[:, None]), (pw[0][:, :tc, :, None, :], pw[1][:, :tc, :, None, :]))
    kern = (jnp.einsum("dkgop,dgpi->dkgoi", cp[0], b_bar[0], precision=HIGHEST)
            - jnp.einsum("dkgop,dgpi->dkgoi", cp[1], b_bar[1], precision=HIGHEST))
    s_idx = jnp.arange(tc)[:, None]
    t_idx = jnp.arange(tc)[None, :]

    def toeplitz(kd, tau):
        return jnp.where((tau >= 0)[:, :, None, None, None], kd[jnp.clip(tau, 0, tc - 1)], 0.0)

    m = toeplitz(kern[0], t_idx - s_idx) + toeplitz(kern[1], s_idx - t_idx)
    m = jnp.transpose(m, (2, 0, 4, 1, 3))
    eye_t = jnp.eye(tc, dtype=F32)[None, :, None, :, None]
    eye_c = jnp.eye(ch, dtype=F32)[None, None, :, None, :]
    m = m + eye_t * eye_c * d_skip.reshape(g, 1, ch, 1, 1)
    m = m.reshape(g, tc * ch, tc * ch)
    steps = jnp.arange(tc)
    lift = lambda z: (z[0][..., None], z[1][..., None])
    e_f = _cmul(lift(at(0, tc - 1 - steps)), (b_bar[0][0][None], b_bar[1][0][None]))
    e_r = _cmul(lift(at(1, steps)), (b_bar[0][1][None], b_bar[1][1][None]))
    w_in = jnp.stack([e_f[0], e_r[0], e_f[1], e_r[1]], axis=0)
    w_in = jnp.transpose(w_in, (2, 1, 4, 0, 3)).reshape(g, tc * ch, 4 * p)
    mid = lambda z: (z[0][:, :, None, :], z[1][:, :, None, :])
    g_f = _cmul((c_re[0][None], c_im[0][None]), mid(at(0, 1 + steps)))
    g_r = _cmul((c_re[1][None], c_im[1][None]), mid(at(1, tc - steps)))
    w_re = jnp.stack([g_f[0], g_r[0]], axis=0)
    w_im = -jnp.stack([g_f[1], g_r[1]], axis=0)
    to_rows = lambda w: jnp.transpose(w, (2, 0, 4, 1, 3)).reshape(g, 2 * p, tc * ch)
    a_rows = jnp.stack([jnp.concatenate([pw[0][0, tc], pw[0][1, tc]], axis=-1),
                        jnp.concatenate([pw[1][0, tc], pw[1][1, tc]], axis=-1)], axis=1)
    a_rows = jnp.concatenate([a_rows, jnp.zeros((g, 6, 2 * p), F32)], axis=1)
    return m.astype(BF16), w_in.astype(BF16), to_rows(w_re).astype(BF16), to_rows(w_im).astype(BF16), a_rows


def _s5_kernel(x_ref, m_ref, win_ref, wre_ref, wim_ref, a_ref, y_ref, v_scr, sre_scr, sim_scr, *,
               nchunk, nctx):
    p2 = 2 * S5_STATE
    rows = S5_ROWS
    for g in range(S5_GB):
        v_scr[g] = jnp.dot(x_ref[g], win_ref[g], preferred_element_type=F32)
    fwd_lane = lax.broadcasted_iota(I32, (rows, p2), 1) < S5_STATE
    a_re = [jnp.broadcast_to(a_ref[g, 0:1, :], (rows, p2)) for g in range(S5_GB)]
    a_im = [jnp.broadcast_to(a_ref[g, 1:2, :], (rows, p2)) for g in range(S5_GB)]

    def body(s, carry):
        cf = s
        cr = jnp.where(s < nctx, nctx - 1 - s, nchunk - 1 + nctx - s)
        of = pl.multiple_of(cf * rows, rows)
        orv = pl.multiple_of(cr * rows, rows)
        new = []
        for g in range(S5_GB):
            sre, sim = carry[g]
            sre_scr[g, pl.ds(of, rows), 0:S5_STATE] = sre[:, 0:S5_STATE]
            sre_scr[g, pl.ds(orv, rows), S5_STATE:p2] = sre[:, S5_STATE:p2]
            sim_scr[g, pl.ds(of, rows), 0:S5_STATE] = sim[:, 0:S5_STATE]
            sim_scr[g, pl.ds(orv, rows), S5_STATE:p2] = sim[:, S5_STATE:p2]
            vf = v_scr[g, pl.ds(of, rows), :]
            vr = v_scr[g, pl.ds(orv, rows), :]
            vre = jnp.where(fwd_lane, vf[:, 0:p2], vr[:, 0:p2])
            vim = jnp.where(fwd_lane, vf[:, p2:2 * p2], vr[:, p2:2 * p2])
            new.append((a_re[g] * sre - a_im[g] * sim + vre, a_re[g] * sim + a_im[g] * sre + vim))
        return tuple(new)

    zero = jnp.zeros((rows, p2), F32)
    lax.fori_loop(0, nchunk, body, tuple((zero, zero) for _ in range(S5_GB)))
    for g in range(S5_GB):
        y = (jnp.dot(x_ref[g], m_ref[g], preferred_element_type=F32)
             + jnp.dot(sre_scr[g].astype(BF16), wre_ref[g], preferred_element_type=F32)
             + jnp.dot(sim_scr[g].astype(BF16), wim_ref[g], preferred_element_type=F32))
        y_ref[g] = y.astype(BF16)


def _s5(u, weights):
    bsz, s, dm = u.shape
    tc, g, ch = S5_TC, S5_GROUPS, S5_GROUP
    nchunk = s // tc
    nctx = CTX_LEN // tc
    rows = nchunk * S5_ROWS
    lanes = tc * ch
    xt = jnp.transpose(u.reshape(bsz, nchunk, tc, g, ch), (3, 1, 0, 2, 4))
    xt = jnp.pad(xt, ((0, 0), (0, 0), (0, S5_ROWS - bsz), (0, 0), (0, 0))).reshape(g, rows, lanes)
    m, w_in, w_re, w_im, a_rows = weights
    p2 = 2 * S5_STATE
    gb = S5_GB
    wspec = lambda r, c: pl.BlockSpec((gb, r, c), lambda i: (i, 0, 0))
    y = pl.pallas_call(
        functools.partial(_s5_kernel, nchunk=nchunk, nctx=nctx),
        grid=(g // gb,),
        in_specs=[wspec(rows, lanes), wspec(lanes, lanes), wspec(lanes, 2 * p2), wspec(p2, lanes),
                  wspec(p2, lanes), wspec(8, p2)],
        out_specs=wspec(rows, lanes),
        out_shape=jax.ShapeDtypeStruct((g, rows, lanes), BF16),
        scratch_shapes=[pltpu.VMEM((gb, rows, 2 * p2), F32), pltpu.VMEM((gb, rows, p2), F32),
                        pltpu.VMEM((gb, rows, p2), F32)],
        compiler_params=_cparams(("arbitrary",)),
        name="s5",
    )(xt, m, w_in, w_re, w_im, a_rows)
    y = y.reshape(g, nchunk, S5_ROWS, tc, ch)[:, nctx:, :bsz]
    return jnp.transpose(y, (2, 1, 3, 0, 4)).reshape(bsz, s - CTX_LEN, dm)


def _glu_kernel(y_ref, x_ref, g1_ref, sh_ref, sc_ref, n2g_ref, w_ref, b_ref, rwt_ref, x1_ref, h2_ref, lg_ref):
    dm = D_MODEL
    z = jax.nn.gelu(y_ref[0].astype(F32)).astype(BF16)
    zz = jnp.dot(z, w_ref[...], preferred_element_type=F32) + b_ref[...]
    y = zz[:, :dm] * jax.nn.sigmoid(zz[:, dm:])
    _post_mixer(y, x_ref[0], g1_ref[0], sh_ref[0], sc_ref[0], n2g_ref[0], rwt_ref[...], x1_ref, h2_ref, lg_ref)


def _glu(y, xall, modtab, layer, norm_g, glu_w, glu_b, router_w):
    bsz, n, dm = y.shape
    ntl = n // TM
    row = lambda b, t: b
    return pl.pallas_call(
        _glu_kernel,
        grid=(bsz, ntl),
        in_specs=[pl.BlockSpec((1, TM, dm), lambda b, t: (b, t, 0)),
                  pl.BlockSpec((1, TM, dm), lambda b, t: (b, t + CTX_LEN // TM, 0)),
                  _mod_spec(layer, 2, row), _mod_spec(layer, 3, row), _mod_spec(layer, 4, row),
                  pl.BlockSpec((1, 1, dm), lambda b, t: (0, 0, 0)),
                  pl.BlockSpec((dm, 2 * dm), lambda b, t: (0, 0)),
                  pl.BlockSpec((1, 2 * dm), lambda b, t: (0, 0)),
                  pl.BlockSpec((N_EXPERTS, dm), lambda b, t: (0, 0))],
        out_specs=[pl.BlockSpec((1, TM, dm), lambda b, t: (b, t, 0)),
                   pl.BlockSpec((1, TM, dm), lambda b, t: (b, t, 0)),
                   pl.BlockSpec((N_EXPERTS, TM), lambda b, t: (0, b * ntl + t))],
        out_shape=[jax.ShapeDtypeStruct((bsz, n, dm), F32),
                   jax.ShapeDtypeStruct((bsz, n, dm), BF16),
                   jax.ShapeDtypeStruct((N_EXPERTS, bsz * n), F32)],
        compiler_params=_cparams(("arbitrary", "arbitrary")),
        name="glu",
    )(y, xall, modtab, modtab, modtab, norm_g.reshape(1, 1, dm), glu_w.astype(BF16), glu_b[None, :], router_w.T)


def kernel(x, c, ctx, c_ctx, mod_w, mod_b, norm1_g, norm2_g, ar_w_in, ar_w_out, lru_conv_w, lru_conv_b, lru_lam, lru_r_w, lru_r_b, lru_i_w, lru_i_b, attn_q_g, attn_k_g, attn_lam_q1, attn_lam_k1, attn_lam_q2, attn_lam_k2, attn_subln_g, s5_a_re, s5_a_im, s5_log_step, s5_b_re, s5_b_im, s5_c_re, s5_c_im, s5_d, s5_glu_w, s5_glu_b, router_w, router_bias, exp_w_gate, exp_w_up, exp_w_down, sh_w_gate, sh_w_up, sh_w_down):
    bsz, n, dm = x.shape
    assert dm == D_MODEL and ctx.shape[1] == CTX_LEN == TM and n % TM == 0 and bsz < 8
    assert mod_w.shape[0] == DEPTH == 2
    s = CTX_LEN + n
    ntb = s // TM
    modtab = _modulation(c, c_ctx, mod_w, mod_b)

    gate, xa, q, k, v = _inproj(x, ctx, modtab, norm1_g[0], ar_w_in[0], attn_q_g[0], attn_k_g[0])
    lru_args = (lru_conv_w[0], lru_conv_b[0], lru_lam[0], lru_r_w[0], lru_r_b[0], lru_i_w[0], lru_i_b[0])
    hf = _lru(xa, *lru_args, reverse=False)
    hr = _lru(xa, *lru_args, reverse=True)
    lam_init = 0.8 - 0.6 * math.exp(-0.3 * 0)
    lam_vecs = jnp.stack([attn_lam_q1[0], attn_lam_k1[0], attn_lam_q2[0], attn_lam_k2[0]], axis=0)
    yb = _attention(q, k, v, lam_vecs, attn_subln_g[0], lam_init)
    x1, h2, lgt = _outproj(hf, hr, gate, yb, x, ctx, modtab, norm2_g[0], ar_w_out[0], router_w[0])
    row0 = lambda i: jnp.where(i % ntb == 0, bsz, i // ntb)
    xall = _moe(h2.reshape(bsz * s, dm), lgt, x1.reshape(bsz * s, dm), modtab, 0, row0, router_bias[0],
                exp_w_gate[0], exp_w_up[0], exp_w_down[0], sh_w_gate[0], sh_w_up[0], sh_w_down[0])
    xall = xall.reshape(bsz, s, dm)

    u = _prenorm(xall, modtab, 1, norm1_g[1])
    weights = _s5_weights(s5_a_re[0], s5_a_im[0], s5_log_step[0], s5_b_re[0], s5_b_im[0], s5_c_re[0],
                          s5_c_im[0], s5_d[0])
    y = _s5(u, weights)
    x1, h2, lgt = _glu(y, xall, modtab, 1, norm2_g[1], s5_glu_w[0], s5_glu_b[0], router_w[1])
    ntl = n // TM
    row1 = lambda i: i // ntl
    out = _moe(h2.reshape(bsz * n, dm), lgt, x1.reshape(bsz * n, dm), modtab, 1, row1, router_bias[1],
               exp_w_gate[1], exp_w_up[1], exp_w_down[1], sh_w_gate[1], sh_w_up[1], sh_w_down[1])
    return out.reshape(bsz, n, dm)
```

```python
import functools
import math

import jax
import jax.numpy as jnp
from jax import lax
from jax.experimental import pallas as pl
from jax.experimental.pallas import tpu as pltpu

F32, BF16, I32 = jnp.float32, jnp.bfloat16, jnp.int32
HIGHEST = lax.Precision.HIGHEST

D_MODEL = 1024
DEPTH = 2
GRID_W = 64
CTX_LEN = 256
EPS = 1e-6
LRU_WIDTH = 512
LRU_BLOCKS = 8
LRU_C = 8.0
ATT_HEADS = 4
ATT_DH = 64
ATT_DV = 128
ATT_QK_W = 512
ATT_WIDTH = 512
ROPE_BASE = 10000.0
EVEN_IN = 2 * LRU_WIDTH + 2 * ATT_QK_W + ATT_WIDTH
S5_GROUP = 16
S5_GROUPS = 64
S5_STATE = 64
S5_TC = 16
S5_GB = 4
N_EXPERTS = 64
TOP_K = 6
D_EXPERT = 256
ROUTED_SCALE = 2.5

TM = 256
SEG = 16
RMAX = 2560
TR = 512
ATT_TK = 768
VMEM_LIMIT = 56 * 1024 * 1024


def _cparams(sem):
    return pltpu.CompilerParams(dimension_semantics=sem, vmem_limit_bytes=VMEM_LIMIT)


def _norm_mod(x, g, shift, scale):
    y = x * lax.rsqrt(jnp.mean(x * x, axis=-1, keepdims=True) + EPS) * g
    return y * (1.0 + scale) + shift


def _mod_kernel(c_ref, w_ref, b_ref, o_ref):
    c = c_ref[...]
    s = c * jax.nn.sigmoid(c)
    o_ref[0] = jnp.dot(s, w_ref[0], precision=HIGHEST, preferred_element_type=F32) + b_ref[0]


def _modulation(c, c_ctx, mod_w, mod_b):
    bsz, dm = c.shape
    cc = jnp.concatenate([c, c_ctx[None, :], jnp.zeros((8 - bsz - 1, dm), F32)], axis=0)
    out = pl.pallas_call(
        _mod_kernel,
        grid=(DEPTH, 6),
        in_specs=[pl.BlockSpec((8, dm), lambda l, j: (0, 0)),
                  pl.BlockSpec((1, dm, dm), lambda l, j: (l, 0, j)),
                  pl.BlockSpec((1, 1, dm), lambda l, j: (l, 0, j))],
        out_specs=pl.BlockSpec((1, 8, dm), lambda l, j: (l, 0, j)),
        out_shape=jax.ShapeDtypeStruct((DEPTH, 8, 6 * dm), F32),
        compiler_params=_cparams(("arbitrary", "arbitrary")),
        name="modulation",
    )(cc, mod_w, mod_b.reshape(DEPTH, 1, 6 * dm))
    return out.reshape(DEPTH * 8 * 6, 1, dm)


def _mod_spec(layer, part, row_fn):
    return pl.BlockSpec((1, 1, D_MODEL), lambda *ids: ((layer * 8 + row_fn(*ids)) * 6 + part, 0, 0))


def _rope_tables(n):
    rows = n // GRID_W
    r, col = jnp.meshgrid(jnp.arange(rows), jnp.arange(GRID_W), indexing="ij")
    pos = jnp.stack([r.reshape(-1), col.reshape(-1)], axis=-1).astype(F32)
    n_freq = ATT_DH // 4
    inv_freq = ROPE_BASE ** (-jnp.arange(n_freq, dtype=F32) / n_freq)
    ang = pos[:, :, None] * inv_freq
    cos, sin = jnp.cos(ang), jnp.sin(ang)
    zero = jnp.zeros_like(sin)
    cos64 = jnp.stack([cos, cos], axis=2).reshape(n, ATT_DH)
    sin_lo = jnp.stack([zero, sin], axis=2).reshape(n, ATT_DH)
    sin_hi = jnp.stack([-sin, zero], axis=2).reshape(n, ATT_DH)

    def full(tab, ctx_val):
        tab = jnp.concatenate([jnp.full((CTX_LEN, ATT_DH), ctx_val, F32), tab], axis=0)
        return jnp.concatenate([tab, tab], axis=1)

    return full(cos64, 1.0), full(sin_lo, 0.0), full(sin_hi, 0.0)


def _qk_post(t, gain, ones_bd, cos, sin_lo, sin_hi):
    ss = jnp.dot((t * t).astype(BF16), ones_bd, preferred_element_type=F32) * (1.0 / ATT_DH)
    tn = t * lax.rsqrt(ss + EPS) * gain
    w = tn.shape[1]
    return tn * cos + pltpu.roll(tn, 16, 1) * sin_lo + pltpu.roll(tn, w - 16, 1) * sin_hi


def _inproj_kernel(x_ref, c_ref, sh_ref, sc_ref, g_ref, w_ref, qg_ref, kg_ref, ones_ref,
                   cos_ref, slo_ref, shi_ref, gate_ref, xa_ref, q_ref, k_ref, v_ref):
    t = pl.program_id(1)
    x = jnp.where(t == 0, c_ref[0], x_ref[0])
    h = _norm_mod(x, g_ref[0], sh_ref[0], sc_ref[0])
    z = jnp.dot(h.astype(BF16), w_ref[...], preferred_element_type=F32)
    lw, qw = LRU_WIDTH, ATT_QK_W
    gate_ref[0] = z[:, 0:lw].astype(BF16)
    xa_ref[0] = z[:, lw:2 * lw].astype(BF16)
    tile4 = lambda a: jnp.concatenate([a, a, a, a], axis=1)
    cos, slo, shi = tile4(cos_ref[...]), tile4(slo_ref[...]), tile4(shi_ref[...])
    ones_bd = ones_ref[...]
    q = _qk_post(z[:, 2 * lw:2 * lw + qw], qg_ref[...], ones_bd, cos, slo, shi)
    q_ref[0] = (q * (ATT_DH ** -0.5 * math.log2(math.e))).astype(BF16)
    k = _qk_post(z[:, 2 * lw + qw:2 * lw + 2 * qw], kg_ref[...], ones_bd, cos, slo, shi)
    k_ref[0] = k.astype(BF16)
    v_ref[0] = z[:, 2 * lw + 2 * qw:].astype(BF16)


def _inproj(x, ctx, modtab, norm_g, w_in, q_g, k_g):
    bsz, n, dm = x.shape
    ntb = (CTX_LEN + n) // TM
    s = CTX_LEN + n
    row = lambda b, t: jnp.where(t == 0, bsz, b)
    cos, slo, shi = _rope_tables(n)
    ones_bd = jnp.kron(jnp.eye(ATT_QK_W // ATT_DH, dtype=F32), jnp.ones((ATT_DH, ATT_DH), F32)).astype(BF16)
    tile_g = lambda g: jnp.tile(g, ATT_QK_W // ATT_DH)[None, :]
    tab_spec = pl.BlockSpec((TM, 2 * ATT_DH), lambda b, t: (t, 0))
    out_spec = pl.BlockSpec((1, TM, LRU_WIDTH), lambda b, t: (b, t, 0))
    out_sds = jax.ShapeDtypeStruct((bsz, s, LRU_WIDTH), BF16)
    return pl.pallas_call(
        _inproj_kernel,
        grid=(bsz, ntb),
        in_specs=[pl.BlockSpec((1, TM, dm), lambda b, t: (b, jnp.maximum(t - 1, 0), 0)),
                  pl.BlockSpec((1, TM, dm), lambda b, t: (b, 0, 0)),
                  _mod_spec(0, 0, row), _mod_spec(0, 1, row),
                  pl.BlockSpec((1, 1, dm), lambda b, t: (0, 0, 0)),
                  pl.BlockSpec((dm, EVEN_IN), lambda b, t: (0, 0)),
                  pl.BlockSpec((1, ATT_QK_W), lambda b, t: (0, 0)),
                  pl.BlockSpec((1, ATT_QK_W), lambda b, t: (0, 0)),
                  pl.BlockSpec((ATT_QK_W, ATT_QK_W), lambda b, t: (0, 0)),
                  tab_spec, tab_spec, tab_spec],
        out_specs=[out_spec] * 5,
        out_shape=[out_sds] * 5,
        compiler_params=_cparams(("arbitrary", "arbitrary")),
        name="inproj",
    )(x, ctx, modtab, modtab, norm_g.reshape(1, 1, dm), w_in.astype(BF16), tile_g(q_g), tile_g(k_g), ones_bd,
      cos, slo, shi)


def _lru_kernel(xa_ref, xp_ref, xn_ref, cw_ref, cb_ref, lam_ref, w_ref, bias_ref, h_ref,
                a_scr, b_scr, c_scr, *, ntb, reverse):
    s = pl.program_id(1)
    ti = jnp.where(s == 0, 0, ntb - s) if reverse else s
    lw = LRU_WIDTH
    x = xa_ref[0].astype(F32)
    row = lax.broadcasted_iota(I32, (TM, lw), 0)
    has_prev = jnp.where(ti > 1, 1.0, 0.0)
    has_next = jnp.where((ti > 0) & (ti < ntb - 1), 1.0, 0.0)
    prev = xp_ref[0].astype(F32) * has_prev
    nxt = xn_ref[0].astype(F32) * has_next
    xm1 = jnp.where(row == 0, prev[7:8], pltpu.roll(x, 1, 0))
    xm2 = jnp.where(row == 0, prev[6:7], jnp.where(row == 1, prev[7:8], pltpu.roll(x, 2, 0)))
    xp1 = jnp.where(row == TM - 1, nxt[0:1], pltpu.roll(x, TM - 1, 0))
    cw = cw_ref[...]
    xc = cw[0:1] * xm2 + cw[1:2] * xm1 + cw[2:3] * x + cw[3:4] * xp1 + cb_ref[...]

    z = jnp.dot(xc.astype(BF16), w_ref[...], preferred_element_type=F32) + bias_ref[...]
    r = jax.nn.sigmoid(z[:, :lw])
    ig = jax.nn.sigmoid(z[:, lw:])
    neg_lam = -lam_ref[...]
    softplus = jnp.maximum(neg_lam, 0.0) + jnp.log1p(jnp.exp(-jnp.abs(neg_lam)))
    log_a = (-LRU_C) * r * softplus
    a = jnp.exp(log_a)
    b = jnp.sqrt(-jnp.tanh(log_a) * (a * a + 1.0)) * (ig * xc)

    r8 = row & 7
    for sft in (1, 2, 4):
        if reverse:
            a_s, b_s, m = pltpu.roll(a, TM - sft, 0), pltpu.roll(b, TM - sft, 0), r8 < 8 - sft
        else:
            a_s, b_s, m = pltpu.roll(a, sft, 0), pltpu.roll(b, sft, 0), r8 >= sft
        b = jnp.where(m, a * b_s + b, b)
        a = jnp.where(m, a * a_s, a)
    a_scr[...] = a
    b_scr[...] = b

    @pl.when(s == 0)
    def _():
        c_scr[...] = jnp.zeros_like(c_scr)

    ng = TM // 8

    def body(j, carry):
        g = (ng - 1 - j) if reverse else j
        off = pl.multiple_of(g * 8, 8)
        h = a_scr[pl.ds(off, 8), :] * carry + b_scr[pl.ds(off, 8), :]
        h_ref[0, pl.ds(off, 8), :] = h
        last = h[0:1] if reverse else h[7:8]
        return jnp.broadcast_to(last, (8, lw))

    c_scr[...] = lax.fori_loop(0, ng, body, c_scr[...], unroll=4)


def _lru(xa, conv_w, conv_b, lam, r_w, r_b, i_w, i_b, reverse):
    bsz, s, lw = xa.shape
    ntb = s // TM
    d = 1 if reverse else 0
    bd = lambda w: jax.scipy.linalg.block_diag(*[w[i] for i in range(LRU_BLOCKS)])
    w = jnp.concatenate([bd(r_w[d]), bd(i_w[d])], axis=1).astype(BF16)
    bias = jnp.concatenate([r_b[d], i_b[d]])[None, :]
    cw = jnp.concatenate([conv_w, jnp.zeros((4, lw), F32)], axis=0)
    if reverse:
        tile = lambda t: jnp.where(t == 0, 0, ntb - t)
    else:
        tile = lambda t: t
    nb8 = s // 8
    return pl.pallas_call(
        functools.partial(_lru_kernel, ntb=ntb, reverse=reverse),
        grid=(bsz, ntb),
        in_specs=[pl.BlockSpec((1, TM, lw), lambda b, t: (b, tile(t), 0)),
                  pl.BlockSpec((1, 8, lw), lambda b, t: (b, jnp.maximum(tile(t) * (TM // 8) - 1, 0), 0)),
                  pl.BlockSpec((1, 8, lw), lambda b, t: (b, jnp.minimum((tile(t) + 1) * (TM // 8), nb8 - 1), 0)),
                  pl.BlockSpec((8, lw), lambda b, t: (0, 0)),
                  pl.BlockSpec((1, lw), lambda b, t: (0, 0)),
                  pl.BlockSpec((1, lw), lambda b, t: (0, 0)),
                  pl.BlockSpec((lw, 2 * lw), lambda b, t: (0, 0)),
                  pl.BlockSpec((1, 2 * lw), lambda b, t: (0, 0))],
        out_specs=pl.BlockSpec((1, TM, lw), lambda b, t: (b, tile(t), 0)),
        out_shape=jax.ShapeDtypeStruct((bsz, s, lw), F32),
        scratch_shapes=[pltpu.VMEM((TM, lw), F32), pltpu.VMEM((TM, lw), F32), pltpu.VMEM((8, lw), F32)],
        compiler_params=_cparams(("arbitrary", "arbitrary")),
        name="lru_rev" if reverse else "lru_fwd",
    )(xa, xa, xa, cw, conv_b[None, :], lam[d][None, :], w, bias)


def _attn_kernel(lamv_ref, q_ref, k_ref, v_ref, sg_ref, o_ref, m_scr, l_scr, acc_scr, al_scr, s_scr, p_scr, *,
                 nkv, lam_init):
    t = pl.program_id(2)
    q = q_ref[0]
    dh = ATT_DH
    dv = ATT_DV
    m_scr[...] = jnp.full_like(m_scr, -jnp.inf)
    l_scr[...] = jnp.zeros_like(l_scr)
    acc_scr[...] = jnp.zeros_like(acc_scr)

    def scores(buf, off, size):
        kc = k_ref[0, pl.ds(off, size), :]
        for mi in range(2):
            s_scr[buf, mi, :, 0:size] = lax.dot_general(
                q[:, mi * dh:(mi + 1) * dh], kc[:, mi * dh:(mi + 1) * dh], (((1,), (1,)), ((), ())),
                preferred_element_type=F32)

    def softmax(buf, size):
        nlb = size // 128
        groups = [(slice(r * 16, (r + 1) * 16), mi) for r in range(TM // 16) for mi in range(2)]
        for rows, mi in groups:
            sc = s_scr[buf, mi, rows, 0:size]
            mx = functools.reduce(jnp.maximum, [sc[:, i * 128:(i + 1) * 128] for i in range(nlb)])
            m_old = m_scr[mi, rows, :]
            m_new = jnp.maximum(m_old, jnp.max(mx, axis=1, keepdims=True))
            m_scr[mi, rows, :] = m_new
            al_scr[mi, rows, :] = jnp.exp2(m_old - m_new)
        for rows, mi in groups:
            m_new = m_scr[mi, rows, :]
            ps = [jnp.exp2(s_scr[buf, mi, rows, i * 128:(i + 1) * 128] - m_new) for i in range(nlb)]
            l_scr[mi, rows, :] = al_scr[mi, rows, :] * l_scr[mi, rows, :] + functools.reduce(jnp.add, ps)
            p_scr[buf, mi, rows, 0:size] = jnp.concatenate(ps, axis=1).astype(BF16)

    def values(buf, off, size):
        vc = v_ref[0, pl.ds(off, size), :]
        pv = jnp.dot(p_scr[buf, :, :, 0:size].reshape(2 * TM, size), vc, preferred_element_type=F32)
        acc_scr[...] = al_scr[...] * acc_scr[...] + pv.reshape(2, TM, dv)

    @pl.when(t == 0)
    def _():
        scores(0, 0, CTX_LEN)
        softmax(0, CTX_LEN)
        values(0, 0, CTX_LEN)

    @pl.when(t > 0)
    def _():
        tk = ATT_TK
        nchunk = nkv // tk
        scores(0, 0, tk)
        for j in range(nchunk):
            if j + 1 < nchunk:
                scores((j + 1) % 2, (j + 1) * tk, tk)
            softmax(j % 2, tk)
            values(j % 2, j * tk, tk)

    lv = lamv_ref[...]
    lam = (jnp.exp(jnp.sum(lv[0:1] * lv[1:2], axis=1, keepdims=True))
           - jnp.exp(jnp.sum(lv[2:3] * lv[3:4], axis=1, keepdims=True)) + lam_init)
    l0 = jnp.sum(l_scr[0], axis=1, keepdims=True)
    l1 = jnp.sum(l_scr[1], axis=1, keepdims=True)
    o = acc_scr[0] / l0 - lam * (acc_scr[1] / l1)
    y = o * lax.rsqrt(jnp.mean(o * o, axis=-1, keepdims=True) + EPS) * sg_ref[...] * (1.0 - lam_init)
    o_ref[0] = y.astype(BF16)


def _attention(q, k, v, lam_vecs, subln_g, lam_init):
    bsz, s, _ = q.shape
    ntb = s // TM
    assert s % ATT_TK == 0
    dv = ATT_DV
    lamv = jnp.concatenate([jnp.pad(lam_vecs, ((0, 0), (0, dv - ATT_DH))), jnp.zeros((4, dv), F32)], axis=0)
    return pl.pallas_call(
        functools.partial(_attn_kernel, nkv=s, lam_init=lam_init),
        grid=(bsz, ATT_HEADS, ntb),
        in_specs=[pl.BlockSpec((8, dv), lambda b, h, t: (0, 0)),
                  pl.BlockSpec((1, TM, dv), lambda b, h, t: (b, t, h)),
                  pl.BlockSpec((1, s, dv), lambda b, h, t: (b, 0, h)),
                  pl.BlockSpec((1, s, dv), lambda b, h, t: (b, 0, h)),
                  pl.BlockSpec((1, dv), lambda b, h, t: (0, 0))],
        out_specs=pl.BlockSpec((1, TM, dv), lambda b, h, t: (b, t, h)),
        out_shape=jax.ShapeDtypeStruct((bsz, s, ATT_WIDTH), BF16),
        scratch_shapes=[pltpu.VMEM((2, TM, dv), F32)] * 4
        + [pltpu.VMEM((2, 2, TM, ATT_TK), F32), pltpu.VMEM((2, 2, TM, ATT_TK), BF16)],
        compiler_params=_cparams(("arbitrary", "arbitrary", "arbitrary")),
        name="diff_attention",
    )(lamv, q, k, v, subln_g[None, :])


def _post_mixer(y, x, g1, sh2, sc2, n2g, rwt, x1_ref, h2_ref, lg_ref):
    x1 = x + g1 * y
    x1_ref[0] = x1
    h2 = _norm_mod(x1, n2g, sh2, sc2)
    h2_ref[0] = h2.astype(BF16)
    lg_ref[...] = lax.dot_general(rwt, h2, (((1,), (1,)), ((), ())), precision=HIGHEST,
                                  preferred_element_type=F32)


def _outproj_kernel(hf_ref, hr_ref, gate_ref, yb_ref, x_ref, c_ref, g1_ref, sh_ref, sc_ref, n2g_ref, w_ref,
                    rwt_ref, x1_ref, h2_ref, lg_ref):
    t = pl.program_id(1)
    x = jnp.where(t == 0, c_ref[0], x_ref[0])
    ya = ((hf_ref[0] + hr_ref[0]) * jax.nn.gelu(gate_ref[0].astype(F32))).astype(BF16)
    lw = LRU_WIDTH
    y = (jnp.dot(ya, w_ref[0:lw, :], preferred_element_type=F32)
         + jnp.dot(yb_ref[0], w_ref[lw:, :], preferred_element_type=F32))
    _post_mixer(y, x, g1_ref[0], sh_ref[0], sc_ref[0], n2g_ref[0], rwt_ref[...], x1_ref, h2_ref, lg_ref)


def _outproj(hf, hr, gate, yb, x, ctx, modtab, norm_g, w_out, router_w):
    bsz, s, lw = hf.shape
    dm = D_MODEL
    ntb = s // TM
    row = lambda b, t: jnp.where(t == 0, bsz, b)
    half = pl.BlockSpec((1, TM, lw), lambda b, t: (b, t, 0))
    return pl.pallas_call(
        _outproj_kernel,
        grid=(bsz, ntb),
        in_specs=[half, half, half, half,
                  pl.BlockSpec((1, TM, dm), lambda b, t: (b, jnp.maximum(t - 1, 0), 0)),
                  pl.BlockSpec((1, TM, dm), lambda b, t: (b, 0, 0)),
                  _mod_spec(0, 2, row), _mod_spec(0, 3, row), _mod_spec(0, 4, row),
                  pl.BlockSpec((1, 1, dm), lambda b, t: (0, 0, 0)),
                  pl.BlockSpec((2 * lw, dm), lambda b, t: (0, 0)),
                  pl.BlockSpec((N_EXPERTS, dm), lambda b, t: (0, 0))],
        out_specs=[pl.BlockSpec((1, TM, dm), lambda b, t: (b, t, 0)),
                   pl.BlockSpec((1, TM, dm), lambda b, t: (b, t, 0)),
                   pl.BlockSpec((N_EXPERTS, TM), lambda b, t: (0, b * ntb + t))],
        out_shape=[jax.ShapeDtypeStruct((bsz, s, dm), F32),
                   jax.ShapeDtypeStruct((bsz, s, dm), BF16),
                   jax.ShapeDtypeStruct((N_EXPERTS, bsz * s), F32)],
        compiler_params=_cparams(("arbitrary", "arbitrary")),
        name="outproj",
    )(hf, hr, gate, yb, x, ctx, modtab, modtab, modtab, norm_g.reshape(1, 1, dm), w_out.astype(BF16),
      router_w.T)


def _route_kernel(lg_ref, bias_ref, tri_ref, low_ref, pos_ref, gate_ref, cnt_ref):
    ne = N_EXPERTS
    aff = jax.nn.sigmoid(lg_ref[...])
    work = aff + bias_ref[:, 0:1]
    eidx = lax.broadcasted_iota(I32, (ne, TM), 0)
    sels = []
    for _ in range(TOP_K):
        mx = jnp.max(work, axis=0, keepdims=True)
        am = jnp.min(jnp.where(work == mx, eidx, ne), axis=0, keepdims=True)
        sk = eidx == am
        sels.append(sk)
        work = jnp.where(sk, -jnp.inf, work)
    sel = sels[0]
    for sk in sels[1:]:
        sel = sel | sk
    self = jnp.where(sel, 1.0, 0.0)
    s_sel = aff * self
    gates = s_sel / jnp.sum(s_sel, axis=0, keepdims=True) * ROUTED_SCALE
    rank = jnp.dot(self.astype(BF16), tri_ref[...], preferred_element_type=F32)
    cnt = jnp.sum(self, axis=1, keepdims=True)
    cnt_seg = jnp.floor((cnt + (SEG - 1)) * (1.0 / SEG)) * SEG
    cnt_b = jnp.broadcast_to(cnt_seg, (ne, 128))
    seg_off = jnp.dot(low_ref[...], cnt_b.astype(BF16), preferred_element_type=F32)
    lpos = seg_off[:, 0:1] + rank
    pos_rows, gate_rows = [], []
    for sk in sels:
        pos_rows.append(jnp.sum(jnp.where(sk, lpos, 0.0), axis=0, keepdims=True))
        gate_rows.append(jnp.sum(jnp.where(sk, gates, 0.0), axis=0, keepdims=True))
    for _ in range(8 - TOP_K):
        pos_rows.append(jnp.full((1, TM), -1.0, F32))
        gate_rows.append(jnp.zeros((1, TM), F32))
    pos_ref[...] = jnp.concatenate(pos_rows, axis=0).astype(I32)
    gate_ref[...] = jnp.concatenate(gate_rows, axis=0)
    cnt_ref[0] = cnt_b.astype(I32)


def _route(lgt, router_bias):
    ne, t = lgt.shape
    nt = t // TM
    tri = jnp.triu(jnp.ones((TM, TM), F32), k=1).astype(BF16)
    low = jnp.tril(jnp.ones((ne, ne), F32), k=-1).astype(BF16)
    bias = jnp.broadcast_to(router_bias[:, None], (ne, 128))
    pos, gate, cnt = pl.pallas_call(
        _route_kernel,
        grid=(nt,),
        in_specs=[pl.BlockSpec((ne, TM), lambda i: (0, i)),
                  pl.BlockSpec((ne, 128), lambda i: (0, 0)),
                  pl.BlockSpec((TM, TM), lambda i: (0, 0)),
                  pl.BlockSpec((ne, ne), lambda i: (0, 0))],
        out_specs=[pl.BlockSpec((8, TM), lambda i: (0, i)),
                   pl.BlockSpec((8, TM), lambda i: (0, i)),
                   pl.BlockSpec((1, ne, 128), lambda i: (i, 0, 0))],
        out_shape=[jax.ShapeDtypeStruct((8, t), I32),
                   jax.ShapeDtypeStruct((8, t), F32),
                   jax.ShapeDtypeStruct((nt, ne, 128), I32)],
        compiler_params=_cparams(("arbitrary",)),
        name="route",
    )(lgt, bias, tri, low)
    return pos, gate, cnt[:, :, 0]


def _moe_layout(cnt_seg, nblk_max):
    tot = jnp.sum(cnt_seg, axis=0)
    region = (tot + TR - 1) // TR * TR
    region_end = jnp.cumsum(region)
    goff = (region_end - region)[None, :] + jnp.cumsum(cnt_seg, axis=0) - cnt_seg
    packed = ((goff // SEG) << 5) | (cnt_seg // SEG)
    nblk = region_end[-1] // TR
    blk = jnp.arange(nblk_max, dtype=I32)
    blk_e = jnp.sum((region_end[None, :] // TR <= blk[:, None]).astype(I32), axis=1)
    blk_e = jnp.minimum(blk_e, N_EXPERTS - 1).astype(I32)
    tile_info = jnp.sum(cnt_seg, axis=1) | (jnp.any(cnt_seg >= 64, axis=1).astype(I32) << 16)
    tails = (((region_end - region + tot) // SEG) << 5) | ((region - tot) // SEG)
    tails = jnp.concatenate([tails, nblk[None]]).astype(I32)
    return packed.reshape(-1).astype(I32), tile_info.astype(I32), tails, blk_e, nblk.astype(I32).reshape(1)


def _segment_copy(stage, hbm, sem, to_sorted, lo, go, size):
    lo, go = pl.multiple_of(lo, SEG), pl.multiple_of(go, SEG)
    a, b = stage.at[pl.ds(lo, size)], hbm.at[pl.ds(go, size)]
    return pltpu.make_async_copy(a, b, sem) if to_sorted else pltpu.make_async_copy(b, a, sem)


def _segment_starts(pk_ref, tile, info, stage, hbm, sem, to_sorted):
    unpack = lambda pk: ((pk & 31) * SEG, (pk >> 5) * SEG)

    @pl.when((info >> 16) != 0)
    def _():
        def expert(e, loff):
            cnt, goff = unpack(pk_ref[tile * N_EXPERTS + e])

            def chunk(j, carry):
                _segment_copy(stage, hbm, sem, to_sorted, loff + j * 64, goff + j * 64, 64).start()
                return carry

            lax.fori_loop(0, cnt >> 6, chunk, 0)
            return loff + cnt

        lax.fori_loop(0, N_EXPERTS, expert, 0)

    loff = 0
    for e in range(N_EXPERTS):
        cnt, goff = unpack(pk_ref[tile * N_EXPERTS + e])
        done = (cnt >> 6) << 6
        for size in (32, 16):
            bit = (cnt & size) != 0

            @pl.when(bit)
            def _(lo=loff + done, go=goff + done, size=size):
                _segment_copy(stage, hbm, sem, to_sorted, lo, go, size).start()

            done = done + jnp.where(bit, size, 0)
        loff = loff + cnt


def _segment_wait(total, stage, hbm, sem, to_sorted):
    size = 1 << (RMAX.bit_length() - 1)
    while size >= SEG:
        @pl.when((total & size) != 0)
        def _(size=size):
            _segment_copy(stage, hbm, sem, to_sorted, 0, 0, size).wait()

        size //= 2


def _slot_matrix(pos, weight_rows=None):
    riota = lax.broadcasted_iota(I32, (RMAX, TM), 0)
    out = jnp.zeros((RMAX, TM), F32)
    for k in range(TOP_K):
        w = 1.0 if weight_rows is None else weight_rows[k:k + 1]
        out = jnp.where(riota == pos[k:k + 1], w, out)
    return out.astype(BF16)


def _zero_fill(tail_ref, zeros, xs_ref, sem, fn):
    def expert(e, carry):
        pk = tail_ref[e]
        off = (pk >> 5) * SEG

        def chunk(j, c):
            dst = xs_ref.at[pl.ds(pl.multiple_of(off + j * SEG, SEG), SEG)]
            fn(pltpu.make_async_copy(zeros.at[pl.ds(0, SEG)], dst, sem))
            return c

        return lax.fori_loop(0, pk & 31, chunk, carry)

    lax.fori_loop(0, N_EXPERTS, expert, 0)

    def block(j, carry):
        fn(pltpu.make_async_copy(zeros, xs_ref.at[pl.ds(pl.multiple_of(j * TR, TR), TR)], sem))
        return carry

    lax.fori_loop(tail_ref[N_EXPERTS], xs_ref.shape[0] // TR, block, 0)


def _dispatch_kernel(pk_ref, tot_ref, tail_ref, pos_ref, h_ref, xs_ref, stage, zeros, sem, zsem):
    i = pl.program_id(0)
    slot = i % 2

    @pl.when(i == 0)
    def _():
        zeros[...] = jnp.zeros_like(zeros)
        _zero_fill(tail_ref, zeros, xs_ref, zsem, lambda c: c.start())

    stage[slot] = jnp.dot(_slot_matrix(pos_ref[...]), h_ref[...], preferred_element_type=F32).astype(BF16)

    @pl.when(i > 0)
    def _():
        rows = tot_ref[jnp.maximum(i - 1, 0)] & 0xFFFF
        _segment_wait(rows, stage.at[1 - slot], xs_ref, sem.at[1 - slot], True)

    _segment_starts(pk_ref, i, tot_ref[i], stage.at[slot], xs_ref, sem.at[slot], True)

    @pl.when(i == pl.num_programs(0) - 1)
    def _():
        _segment_wait(tot_ref[i] & 0xFFFF, stage.at[slot], xs_ref, sem.at[slot], True)
        _zero_fill(tail_ref, zeros, xs_ref, zsem, lambda c: c.wait())


def _dispatch(packed, tile_rows, tails, pos, h2, nrows):
    t, dm = h2.shape
    nt = t // TM
    return pl.pallas_call(
        _dispatch_kernel,
        grid_spec=pltpu.PrefetchScalarGridSpec(
            num_scalar_prefetch=3,
            grid=(nt,),
            in_specs=[pl.BlockSpec((8, TM), lambda i, pk, tot, tail: (0, i)),
                      pl.BlockSpec((TM, dm), lambda i, pk, tot, tail: (i, 0))],
            out_specs=pl.BlockSpec(memory_space=pl.ANY),
            scratch_shapes=[pltpu.VMEM((2, RMAX, dm), BF16), pltpu.VMEM((TR, dm), BF16),
                            pltpu.SemaphoreType.DMA((2,)), pltpu.SemaphoreType.DMA]),
        out_shape=jax.ShapeDtypeStruct((nrows, dm), BF16),
        compiler_params=_cparams(("arbitrary",)),
        name="moe_dispatch",
    )(packed, tile_rows, tails, pos, h2)


def _expert_kernel(be_ref, nb_ref, x_ref, wg_ref, wu_ref, wd_ref, y_ref, wg_s, wu_s, wd_s):
    j = pl.program_id(0)
    changed = be_ref[j] != be_ref[jnp.maximum(j - 1, 0)]

    @pl.when((j == 0) | changed)
    def _():
        wg_s[...] = wg_ref[0].astype(BF16)
        wu_s[...] = wu_ref[0].astype(BF16)
        wd_s[...] = wd_ref[0].astype(BF16)

    @pl.when(j < nb_ref[0])
    def _():
        x = x_ref[...]
        g = jnp.dot(x, wg_s[...], preferred_element_type=F32)
        u = jnp.dot(x, wu_s[...], preferred_element_type=F32)
        a = (g * jax.nn.sigmoid(g) * u).astype(BF16)
        y_ref[...] = jnp.dot(a, wd_s[...], preferred_element_type=F32).astype(BF16)


def _experts(blk_e, nblk, xs, w_gate, w_up, w_down):
    nrows, dm = xs.shape
    nblk_max = nrows // TR
    de = D_EXPERT
    row_blk = lambda j, be, nb: (jnp.minimum(j, nb[0] - 1), 0)
    return pl.pallas_call(
        _expert_kernel,
        grid_spec=pltpu.PrefetchScalarGridSpec(
            num_scalar_prefetch=2,
            grid=(nblk_max,),
            in_specs=[pl.BlockSpec((TR, dm), row_blk),
                      pl.BlockSpec((1, dm, de), lambda j, be, nb: (be[j], 0, 0)),
                      pl.BlockSpec((1, dm, de), lambda j, be, nb: (be[j], 0, 0)),
                      pl.BlockSpec((1, de, dm), lambda j, be, nb: (be[j], 0, 0))],
            out_specs=pl.BlockSpec((TR, dm), row_blk),
            scratch_shapes=[pltpu.VMEM((dm, de), BF16), pltpu.VMEM((dm, de), BF16), pltpu.VMEM((de, dm), BF16)]),
        out_shape=jax.ShapeDtypeStruct((nrows, dm), BF16),
        input_output_aliases={2: 0},
        compiler_params=_cparams(("arbitrary",)),
        name="moe_experts",
    )(blk_e, nblk, xs, w_gate, w_up, w_down)


def _combine_kernel(pk_ref, tot_ref, pos_ref, gate_ref, h_ref, x1_ref, g2_ref, wsg_ref, wsu_ref, wsd_ref, ys_ref,
                    o_ref, stage, sem):
    i = pl.program_id(0)
    slot = i % 2

    @pl.when(i == 0)
    def _():
        stage[...] = jnp.zeros_like(stage)
        _segment_starts(pk_ref, 0, tot_ref[0], stage.at[0], ys_ref, sem.at[0], False)

    last = pl.num_programs(0) - 1

    @pl.when(i < last)
    def _():
        nxt = jnp.minimum(i + 1, last)
        _segment_starts(pk_ref, nxt, tot_ref[nxt], stage.at[1 - slot], ys_ref, sem.at[1 - slot], False)

    gates = _slot_matrix(pos_ref[...], gate_ref[...])
    h = h_ref[...]
    g = jnp.dot(h, wsg_ref[...], preferred_element_type=F32)
    u = jnp.dot(h, wsu_ref[...], preferred_element_type=F32)
    shared = jnp.dot((g * jax.nn.sigmoid(g) * u).astype(BF16), wsd_ref[...], preferred_element_type=F32)
    _segment_wait(tot_ref[i] & 0xFFFF, stage.at[slot], ys_ref, sem.at[slot], False)
    routed = lax.dot_general(gates, stage[slot], (((0,), (0,)), ((), ())), preferred_element_type=F32)
    o_ref[...] = x1_ref[...] + g2_ref[0] * (routed + shared)


def _combine(packed, tile_rows, pos, gate, h2, x1, modtab, layer, row_fn, ys, ws_gate, ws_up, ws_down):
    t, dm = h2.shape
    nt = t // TM
    de = D_EXPERT
    const = lambda shape: pl.BlockSpec(shape, lambda i, pk, tot: (0,) * len(shape))
    return pl.pallas_call(
        _combine_kernel,
        grid_spec=pltpu.PrefetchScalarGridSpec(
            num_scalar_prefetch=2,
            grid=(nt,),
            in_specs=[pl.BlockSpec((8, TM), lambda i, pk, tot: (0, i)),
                      pl.BlockSpec((8, TM), lambda i, pk, tot: (0, i)),
                      pl.BlockSpec((TM, dm), lambda i, pk, tot: (i, 0)),
                      pl.BlockSpec((TM, dm), lambda i, pk, tot: (i, 0)),
                      pl.BlockSpec((1, 1, dm), lambda i, pk, tot: ((layer * 8 + row_fn(i)) * 6 + 5, 0, 0)),
                      const((dm, de)), const((dm, de)), const((de, dm)),
                      pl.BlockSpec(memory_space=pl.ANY)],
            out_specs=pl.BlockSpec((TM, dm), lambda i, pk, tot: (i, 0)),
            scratch_shapes=[pltpu.VMEM((2, RMAX, dm), BF16), pltpu.SemaphoreType.DMA((2,))]),
        out_shape=jax.ShapeDtypeStruct((t, dm), F32),
        compiler_params=_cparams(("arbitrary",)),
        name="moe_combine",
    )(packed, tile_rows, pos, gate, h2, x1, modtab, ws_gate.astype(BF16), ws_up.astype(BF16),
      ws_down.astype(BF16), ys)


def _moe(h2, lgt, x1, modtab, layer, row_fn, router_bias, w_gate, w_up, w_down, ws_gate, ws_up, ws_down):
    t = h2.shape[0]
    nt = t // TM
    max_rows = t * TOP_K + nt * N_EXPERTS * (SEG - 1) + N_EXPERTS * (TR - 1)
    nblk_max = -(-max_rows // TR)
    pos, gate, cnt_seg = _route(lgt, router_bias)
    packed, tile_rows, tails, blk_e, nblk = _moe_layout(cnt_seg, nblk_max)
    xs = _dispatch(packed, tile_rows, tails, pos, h2, nblk_max * TR)
    ys = _experts(blk_e, nblk, xs, w_gate, w_up, w_down)
    return _combine(packed, tile_rows, pos, gate, h2, x1, modtab, layer, row_fn, ys, ws_gate, ws_up, ws_down)


def _prenorm_kernel(x_ref, sh_ref, sc_ref, g_ref, u_ref):
    u_ref[0] = _norm_mod(x_ref[0], g_ref[0], sh_ref[0], sc_ref[0]).astype(BF16)


def _prenorm(xall, modtab, layer, norm_g):
    bsz, s, dm = xall.shape
    ntb = s // TM
    row = lambda b, t: jnp.where(t == 0, bsz, b)
    return pl.pallas_call(
        _prenorm_kernel,
        grid=(bsz, ntb),
        in_specs=[pl.BlockSpec((1, TM, dm), lambda b, t: (b, t, 0)),
                  _mod_spec(layer, 0, row), _mod_spec(layer, 1, row),
                  pl.BlockSpec((1, 1, dm), lambda b, t: (0, 0, 0))],
        out_specs=pl.BlockSpec((1, TM, dm), lambda b, t: (b, t, 0)),
        out_shape=jax.ShapeDtypeStruct((bsz, s, dm), BF16),
        compiler_params=_cparams(("arbitrary", "arbitrary")),
        name="prenorm",
    )(xall, modtab, modtab, norm_g.reshape(1, 1, dm))


def _cmul(x, y):
    return x[0] * y[0] - x[1] * y[1], x[0] * y[1] + x[1] * y[0]


def _s5_weights(a_re, a_im, log_step, b_re, b_im, c_re, c_im, d_skip):
    tc, g, p, ch = S5_TC, S5_GROUPS, S5_STATE, S5_GROUP
    lam = (jnp.minimum(a_re, -1e-4), a_im)
    step = jnp.exp(log_step)
    mag = jnp.exp(lam[0] * step)
    lam_bar = (mag * jnp.cos(lam[1] * step), mag * jnp.sin(lam[1] * step))
    inv = 1.0 / (lam[0] * lam[0] + lam[1] * lam[1])
    coef = _cmul((lam_bar[0] - 1.0, lam_bar[1]), (lam[0] * inv, -lam[1] * inv))
    b_bar = _cmul((coef[0][..., None], coef[1][..., None]), (b_re, b_im))
    pw = [(jnp.ones_like(mag), jnp.zeros_like(mag))]
    for _ in range(tc):
        pw.append(_cmul(pw[-1], lam_bar))
    pw = (jnp.stack([q[0] for q in pw], axis=1), jnp.stack([q[1] for q in pw], axis=1))
    at = lambda d, idx: (pw[0][d, idx], pw[1][d, idx])
    cp = _cmul((c_re[:, None], c_im[:, None]), (pw[0][:, :tc, :, None, :], pw[1][:, :tc, :, None, :]))
    kern = (jnp.einsum("dkgop,dgpi->dkgoi", cp[0], b_bar[0], precision=HIGHEST)
            - jnp.einsum("dkgop,dgpi->dkgoi", cp[1], b_bar[1], precision=HIGHEST))
    s_idx = jnp.arange(tc)[:, None]
    t_idx = jnp.arange(tc)[None, :]

    def toeplitz(kd, tau):
        return jnp.where((tau >= 0)[:, :, None, None, None], kd[jnp.clip(tau, 0, tc - 1)], 0.0)

    m = toeplitz(kern[0], t_idx - s_idx) + toeplitz(kern[1], s_idx - t_idx)
    m = jnp.transpose(m, (2, 0, 4, 1, 3))
    eye_t = jnp.eye(tc, dtype=F32)[None, :, None, :, None]
    eye_c = jnp.eye(ch, dtype=F32)[None, None, :, None, :]
    m = m + eye_t * eye_c * d_skip.reshape(g, 1, ch, 1, 1)
    m = m.reshape(g, tc * ch, tc * ch)
    steps = jnp.arange(tc)
    lift = lambda z: (z[0][..., None], z[1][..., None])
    e_f = _cmul(lift(at(0, tc - 1 - steps)), (b_bar[0][0][None], b_bar[1][0][None]))
    e_r = _cmul(lift(at(1, steps)), (b_bar[0][1][None], b_bar[1][1][None]))
    w_in = jnp.stack([e_f[0], e_r[0], e_f[1], e_r[1]], axis=0)
    w_in = jnp.transpose(w_in, (2, 1, 4, 0, 3)).reshape(g, tc * ch, 4 * p)
    mid = lambda z: (z[0][:, :, None, :], z[1][:, :, None, :])
    g_f = _cmul((c_re[0][None], c_im[0][None]), mid(at(0, 1 + steps)))
    g_r = _cmul((c_re[1][None], c_im[1][None]), mid(at(1, tc - steps)))
    w_re = jnp.stack([g_f[0], g_r[0]], axis=0)
    w_im = -jnp.stack([g_f[1], g_r[1]], axis=0)
    to_rows = lambda w: jnp.transpose(w, (2, 0, 4, 1, 3)).reshape(g, 2 * p, tc * ch)
    a1 = (jnp.concatenate([pw[0][0, tc], pw[0][1, tc]], axis=-1),
          jnp.concatenate([pw[1][0, tc], pw[1][1, tc]], axis=-1))
    a2 = _cmul(a1, a1)
    second = _s5_second_rows(8)
    par = tuple(jnp.where(second[None], a2[k][:, None, :], a1[k][:, None, :]) for k in range(2))
    one = tuple(jnp.broadcast_to(a1[k][:, None, :], (g, 8, 2 * p)) for k in range(2))
    a_rows = jnp.concatenate([jnp.concatenate(par, axis=-1), jnp.concatenate(one, axis=-1)], axis=1)
    return m.astype(BF16), w_in.astype(BF16), to_rows(w_re).astype(BF16), to_rows(w_im).astype(BF16), a_rows


def _s5_second_rows(nrows):
    row = lax.broadcasted_iota(I32, (nrows, 2 * S5_STATE), 0)
    lane = lax.broadcasted_iota(I32, (nrows, 2 * S5_STATE), 1)
    return (lane < S5_STATE) != ((row & 7) < 4)


def _s5_kernel(x_ref, m_ref, win_ref, wre_ref, wim_ref, a_ref, y_ref, ure_scr, uim_scr, sre_scr, sim_scr, *,
               nblock, nctx):
    p2 = 2 * S5_STATE
    nrows = nblock * 8
    second = _s5_second_rows(nrows)
    fwd = lax.broadcasted_iota(I32, (nrows, p2), 1) < S5_STATE
    for g in range(S5_GB):
        v = jnp.dot(x_ref[g], win_ref[g], preferred_element_type=F32)
        vre, vim = v[:, 0:p2], v[:, p2:2 * p2]
        a_re, a_im = a_ref[g, 8:9, 0:p2], a_ref[g, 8:9, p2:2 * p2]
        fre = jnp.where(fwd, pltpu.roll(vre, 4, 0), pltpu.roll(vre, nrows - 4, 0))
        fim = jnp.where(fwd, pltpu.roll(vim, 4, 0), pltpu.roll(vim, nrows - 4, 0))
        ure_scr[g] = vre + jnp.where(second, a_re * fre - a_im * fim, 0.0)
        uim_scr[g] = vim + jnp.where(second, a_re * fim + a_im * fre, 0.0)
    second8 = second[0:8]
    fwd8 = fwd[0:8]
    ap_re = [a_ref[g, 0:8, 0:p2] for g in range(S5_GB)]
    ap_im = [a_ref[g, 0:8, p2:2 * p2] for g in range(S5_GB)]

    def body(s, carry):
        jr = jnp.where(s < nctx, nctx - 1 - s, nblock - 1 + nctx - s)
        of = pl.multiple_of(s * 8, 8)
        orv = pl.multiple_of(jr * 8, 8)
        new = []
        for g in range(S5_GB):
            cre, cim = carry[g]
            ure = jnp.where(fwd8, ure_scr[g, pl.ds(of, 8), :], ure_scr[g, pl.ds(orv, 8), :])
            uim = jnp.where(fwd8, uim_scr[g, pl.ds(of, 8), :], uim_scr[g, pl.ds(orv, 8), :])
            zre = ap_re[g] * cre - ap_im[g] * cim + ure
            zim = ap_re[g] * cim + ap_im[g] * cre + uim
            rre, rim = pltpu.roll(zre, 4, 0), pltpu.roll(zim, 4, 0)
            ere, eim = jnp.where(second8, rre, cre), jnp.where(second8, rim, cim)
            sre_scr[g, pl.ds(of, 8), 0:S5_STATE] = ere[:, 0:S5_STATE]
            sre_scr[g, pl.ds(orv, 8), S5_STATE:p2] = ere[:, S5_STATE:p2]
            sim_scr[g, pl.ds(of, 8), 0:S5_STATE] = eim[:, 0:S5_STATE]
            sim_scr[g, pl.ds(orv, 8), S5_STATE:p2] = eim[:, S5_STATE:p2]
            new.append((jnp.where(second8, zre, rre), jnp.where(second8, zim, rim)))
        return tuple(new)

    zero = jnp.zeros((8, p2), F32)
    lax.fori_loop(0, nblock, body, tuple((zero, zero) for _ in range(S5_GB)))
    for g in range(S5_GB):
        y = (jnp.dot(x_ref[g], m_ref[g], preferred_element_type=F32)
             + jnp.dot(sre_scr[g].astype(BF16), wre_ref[g], preferred_element_type=F32)
             + jnp.dot(sim_scr[g].astype(BF16), wim_ref[g], preferred_element_type=F32))
        y_ref[g] = y.astype(BF16)


def _s5(u, weights):
    bsz, s, dm = u.shape
    assert 2 * bsz == 8
    tc, g, ch = S5_TC, S5_GROUPS, S5_GROUP
    nchunk = s // tc
    nctx = CTX_LEN // tc
    assert nchunk % 2 == 0 and nctx % 2 == 0
    rows = nchunk * bsz
    lanes = tc * ch
    xt = jnp.transpose(u.reshape(bsz, nchunk, tc, g, ch), (3, 1, 0, 2, 4)).reshape(g, rows, lanes)
    m, w_in, w_re, w_im, a_rows = weights
    p2 = 2 * S5_STATE
    gb = S5_GB
    wspec = lambda r, c: pl.BlockSpec((gb, r, c), lambda i: (i, 0, 0))
    y = pl.pallas_call(
        functools.partial(_s5_kernel, nblock=nchunk // 2, nctx=nctx // 2),
        grid=(g // gb,),
        in_specs=[wspec(rows, lanes), wspec(lanes, lanes), wspec(lanes, 2 * p2), wspec(p2, lanes),
                  wspec(p2, lanes), wspec(16, 2 * p2)],
        out_specs=wspec(rows, lanes),
        out_shape=jax.ShapeDtypeStruct((g, rows, lanes), BF16),
        scratch_shapes=[pltpu.VMEM((gb, rows, p2), F32)] * 4,
        compiler_params=_cparams(("arbitrary",)),
        name="s5",
    )(xt, m, w_in, w_re, w_im, a_rows)
    y = y.reshape(g, nchunk, bsz, tc, ch)[:, nctx:]
    return jnp.transpose(y, (2, 1, 3, 0, 4)).reshape(bsz, s - CTX_LEN, dm)


def _glu_kernel(y_ref, x_ref, g1_ref, sh_ref, sc_ref, n2g_ref, w_ref, b_ref, rwt_ref, x1_ref, h2_ref, lg_ref):
    dm = D_MODEL
    z = jax.nn.gelu(y_ref[0].astype(F32)).astype(BF16)
    zz = jnp.dot(z, w_ref[...], preferred_element_type=F32) + b_ref[...]
    y = zz[:, :dm] * jax.nn.sigmoid(zz[:, dm:])
    _post_mixer(y, x_ref[0], g1_ref[0], sh_ref[0], sc_ref[0], n2g_ref[0], rwt_ref[...], x1_ref, h2_ref, lg_ref)


def _glu(y, xall, modtab, layer, norm_g, glu_w, glu_b, router_w):
    bsz, n, dm = y.shape
    ntl = n // TM
    row = lambda b, t: b
    return pl.pallas_call(
        _glu_kernel,
        grid=(bsz, ntl),
        in_specs=[pl.BlockSpec((1, TM, dm), lambda b, t: (b, t, 0)),
                  pl.BlockSpec((1, TM, dm), lambda b, t: (b, t + CTX_LEN // TM, 0)),
                  _mod_spec(layer, 2, row), _mod_spec(layer, 3, row), _mod_spec(layer, 4, row),
                  pl.BlockSpec((1, 1, dm), lambda b, t: (0, 0, 0)),
                  pl.BlockSpec((dm, 2 * dm), lambda b, t: (0, 0)),
                  pl.BlockSpec((1, 2 * dm), lambda b, t: (0, 0)),
                  pl.BlockSpec((N_EXPERTS, dm), lambda b, t: (0, 0))],
        out_specs=[pl.BlockSpec((1, TM, dm), lambda b, t: (b, t, 0)),
                   pl.BlockSpec((1, TM, dm), lambda b, t: (b, t, 0)),
                   pl.BlockSpec((N_EXPERTS, TM), lambda b, t: (0, b * ntl + t))],
        out_shape=[jax.ShapeDtypeStruct((bsz, n, dm), F32),
                   jax.ShapeDtypeStruct((bsz, n, dm), BF16),
                   jax.ShapeDtypeStruct((N_EXPERTS, bsz * n), F32)],
        compiler_params=_cparams(("arbitrary", "arbitrary")),
        name="glu",
    )(y, xall, modtab, modtab, modtab, norm_g.reshape(1, 1, dm), glu_w.astype(BF16), glu_b[None, :], router_w.T)


def kernel(x, c, ctx, c_ctx, mod_w, mod_b, norm1_g, norm2_g, ar_w_in, ar_w_out, lru_conv_w, lru_conv_b, lru_lam, lru_r_w, lru_r_b, lru_i_w, lru_i_b, attn_q_g, attn_k_g, attn_lam_q1, attn_lam_k1, attn_lam_q2, attn_lam_k2, attn_subln_g, s5_a_re, s5_a_im, s5_log_step, s5_b_re, s5_b_im, s5_c_re, s5_c_im, s5_d, s5_glu_w, s5_glu_b, router_w, router_bias, exp_w_gate, exp_w_up, exp_w_down, sh_w_gate, sh_w_up, sh_w_down):
    bsz, n, dm = x.shape
    assert dm == D_MODEL and ctx.shape[1] == CTX_LEN == TM and n % TM == 0 and bsz < 8
    assert mod_w.shape[0] == DEPTH == 2
    s = CTX_LEN + n
    ntb = s // TM
    modtab = _modulation(c, c_ctx, mod_w, mod_b)

    gate, xa, q, k, v = _inproj(x, ctx, modtab, norm1_g[0], ar_w_in[0], attn_q_g[0], attn_k_g[0])
    lru_args = (lru_conv_w[0], lru_conv_b[0], lru_lam[0], lru_r_w[0], lru_r_b[0], lru_i_w[0], lru_i_b[0])
    hf = _lru(xa, *lru_args, reverse=False)
    hr = _lru(xa, *lru_args, reverse=True)
    lam_init = 0.8 - 0.6 * math.exp(-0.3 * 0)
    lam_vecs = jnp.stack([attn_lam_q1[0], attn_lam_k1[0], attn_lam_q2[0], attn_lam_k2[0]], axis=0)
    yb = _attention(q, k, v, lam_vecs, attn_subln_g[0], lam_init)
    x1, h2, lgt = _outproj(hf, hr, gate, yb, x, ctx, modtab, norm2_g[0], ar_w_out[0], router_w[0])
    row0 = lambda i: jnp.where(i % ntb == 0, bsz, i // ntb)
    xall = _moe(h2.reshape(bsz * s, dm), lgt, x1.reshape(bsz * s, dm), modtab, 0, row0, router_bias[0],
                exp_w_gate[0], exp_w_up[0], exp_w_down[0], sh_w_gate[0], sh_w_up[0], sh_w_down[0])
    xall = xall.reshape(bsz, s, dm)

    u = _prenorm(xall, modtab, 1, norm1_g[1])
    weights = _s5_weights(s5_a_re[0], s5_a_im[0], s5_log_step[0], s5_b_re[0], s5_b_im[0], s5_c_re[0],
                          s5_c_im[0], s5_d[0])
    y = _s5(u, weights)
    x1, h2, lgt = _glu(y, xall, modtab, 1, norm2_g[1], s5_glu_w[0], s5_glu_b[0], router_w[1])
    ntl = n // TM
    row1 = lambda i: i // ntl
    out = _moe(h2.reshape(bsz * n, dm), lgt, x1.reshape(bsz * n, dm), modtab, 1, row1, router_bias[1],
               exp_w_gate[1], exp_w_up[1], exp_w_down[1], sh_w_gate[1], sh_w_up[1], sh_w_down[1])
    return out.reshape(bsz, n, dm)
```

```python
import functools
import math

import jax
import jax.numpy as jnp
from jax import lax
from jax.experimental import pallas as pl
from jax.experimental.pallas import tpu as pltpu

F32, BF16, I32 = jnp.float32, jnp.bfloat16, jnp.int32
HIGHEST = lax.Precision.HIGHEST

D_MODEL = 1024
DEPTH = 2
GRID_W = 64
CTX_LEN = 256
EPS = 1e-6
LRU_WIDTH = 512
LRU_BLOCKS = 8
LRU_C = 8.0
ATT_HEADS = 4
ATT_DH = 64
ATT_DV = 128
ATT_QK_W = 512
ATT_WIDTH = 512
ROPE_BASE = 10000.0
EVEN_IN = 2 * LRU_WIDTH + 2 * ATT_QK_W + ATT_WIDTH
S5_GROUP = 16
S5_GROUPS = 64
S5_STATE = 64
S5_TC = 16
S5_GB = 4
N_EXPERTS = 64
TOP_K = 6
D_EXPERT = 256
ROUTED_SCALE = 2.5

TM = 256
SEG = 16
RMAX = 2560
TR = 512
ATT_TK = 768
VMEM_LIMIT = 56 * 1024 * 1024


def _cparams(sem):
    return pltpu.CompilerParams(dimension_semantics=sem, vmem_limit_bytes=VMEM_LIMIT)


def _norm_mod(x, g, shift, scale):
    y = x * lax.rsqrt(jnp.mean(x * x, axis=-1, keepdims=True) + EPS) * g
    return y * (1.0 + scale) + shift


def _mod_kernel(c_ref, w_ref, b_ref, o_ref):
    c = c_ref[...]
    s = c * jax.nn.sigmoid(c)
    o_ref[0] = jnp.dot(s, w_ref[0], precision=HIGHEST, preferred_element_type=F32) + b_ref[0]


def _modulation(c, c_ctx, mod_w, mod_b):
    bsz, dm = c.shape
    cc = jnp.concatenate([c, c_ctx[None, :], jnp.zeros((8 - bsz - 1, dm), F32)], axis=0)
    out = pl.pallas_call(
        _mod_kernel,
        grid=(DEPTH, 6),
        in_specs=[pl.BlockSpec((8, dm), lambda l, j: (0, 0)),
                  pl.BlockSpec((1, dm, dm), lambda l, j: (l, 0, j)),
                  pl.BlockSpec((1, 1, dm), lambda l, j: (l, 0, j))],
        out_specs=pl.BlockSpec((1, 8, dm), lambda l, j: (l, 0, j)),
        out_shape=jax.ShapeDtypeStruct((DEPTH, 8, 6 * dm), F32),
        compiler_params=_cparams(("arbitrary", "arbitrary")),
        name="modulation",
    )(cc, mod_w, mod_b.reshape(DEPTH, 1, 6 * dm))
    return out.reshape(DEPTH * 8 * 6, 1, dm)


def _mod_spec(layer, part, row_fn):
    return pl.BlockSpec((1, 1, D_MODEL), lambda *ids: ((layer * 8 + row_fn(*ids)) * 6 + part, 0, 0))


def _rope_tables(n):
    rows = n // GRID_W
    r, col = jnp.meshgrid(jnp.arange(rows), jnp.arange(GRID_W), indexing="ij")
    pos = jnp.stack([r.reshape(-1), col.reshape(-1)], axis=-1).astype(F32)
    n_freq = ATT_DH // 4
    inv_freq = ROPE_BASE ** (-jnp.arange(n_freq, dtype=F32) / n_freq)
    ang = pos[:, :, None] * inv_freq
    cos, sin = jnp.cos(ang), jnp.sin(ang)
    zero = jnp.zeros_like(sin)
    cos64 = jnp.stack([cos, cos], axis=2).reshape(n, ATT_DH)
    sin_lo = jnp.stack([zero, sin], axis=2).reshape(n, ATT_DH)
    sin_hi = jnp.stack([-sin, zero], axis=2).reshape(n, ATT_DH)

    def full(tab, ctx_val):
        tab = jnp.concatenate([jnp.full((CTX_LEN, ATT_DH), ctx_val, F32), tab], axis=0)
        return jnp.concatenate([tab, tab], axis=1)

    return full(cos64, 1.0), full(sin_lo, 0.0), full(sin_hi, 0.0)


def _qk_post(t, gain, ones_bd, cos, sin_lo, sin_hi):
    ss = jnp.dot((t * t).astype(BF16), ones_bd, preferred_element_type=F32) * (1.0 / ATT_DH)
    tn = t * lax.rsqrt(ss + EPS) * gain
    w = tn.shape[1]
    return tn * cos + pltpu.roll(tn, 16, 1) * sin_lo + pltpu.roll(tn, w - 16, 1) * sin_hi


def _inproj_kernel(x_ref, c_ref, sh_ref, sc_ref, g_ref, w_ref, qg_ref, kg_ref, ones_ref,
                   cos_ref, slo_ref, shi_ref, gate_ref, xa_ref, q_ref, k_ref, v_ref):
    t = pl.program_id(1)
    x = jnp.where(t == 0, c_ref[0], x_ref[0])
    h = _norm_mod(x, g_ref[0], sh_ref[0], sc_ref[0])
    z = jnp.dot(h.astype(BF16), w_ref[...], preferred_element_type=F32)
    lw, qw = LRU_WIDTH, ATT_QK_W
    gate_ref[0] = z[:, 0:lw].astype(BF16)
    xa_ref[0] = z[:, lw:2 * lw].astype(BF16)
    tile4 = lambda a: jnp.concatenate([a, a, a, a], axis=1)
    cos, slo, shi = tile4(cos_ref[...]), tile4(slo_ref[...]), tile4(shi_ref[...])
    ones_bd = ones_ref[...]
    q = _qk_post(z[:, 2 * lw:2 * lw + qw], qg_ref[...], ones_bd, cos, slo, shi)
    q_ref[0] = (q * (ATT_DH ** -0.5 * math.log2(math.e))).astype(BF16)
    k = _qk_post(z[:, 2 * lw + qw:2 * lw + 2 * qw], kg_ref[...], ones_bd, cos, slo, shi)
    k_ref[0] = k.astype(BF16)
    v_ref[0] = z[:, 2 * lw + 2 * qw:].astype(BF16)


def _inproj(x, ctx, modtab, norm_g, w_in, q_g, k_g):
    bsz, n, dm = x.shape
    ntb = (CTX_LEN + n) // TM
    s = CTX_LEN + n
    row = lambda b, t: jnp.where(t == 0, bsz, b)
    cos, slo, shi = _rope_tables(n)
    ones_bd = jnp.kron(jnp.eye(ATT_QK_W // ATT_DH, dtype=F32), jnp.ones((ATT_DH, ATT_DH), F32)).astype(BF16)
    tile_g = lambda g: jnp.tile(g, ATT_QK_W // ATT_DH)[None, :]
    tab_spec = pl.BlockSpec((TM, 2 * ATT_DH), lambda b, t: (t, 0))
    out_spec = pl.BlockSpec((1, TM, LRU_WIDTH), lambda b, t: (b, t, 0))
    out_sds = jax.ShapeDtypeStruct((bsz, s, LRU_WIDTH), BF16)
    return pl.pallas_call(
        _inproj_kernel,
        grid=(bsz, ntb),
        in_specs=[pl.BlockSpec((1, TM, dm), lambda b, t: (b, jnp.maximum(t - 1, 0), 0)),
                  pl.BlockSpec((1, TM, dm), lambda b, t: (b, 0, 0)),
                  _mod_spec(0, 0, row), _mod_spec(0, 1, row),
                  pl.BlockSpec((1, 1, dm), lambda b, t: (0, 0, 0)),
                  pl.BlockSpec((dm, EVEN_IN), lambda b, t: (0, 0)),
                  pl.BlockSpec((1, ATT_QK_W), lambda b, t: (0, 0)),
                  pl.BlockSpec((1, ATT_QK_W), lambda b, t: (0, 0)),
                  pl.BlockSpec((ATT_QK_W, ATT_QK_W), lambda b, t: (0, 0)),
                  tab_spec, tab_spec, tab_spec],
        out_specs=[out_spec] * 5,
        out_shape=[out_sds] * 5,
        compiler_params=_cparams(("arbitrary", "arbitrary")),
        name="inproj",
    )(x, ctx, modtab, modtab, norm_g.reshape(1, 1, dm), w_in.astype(BF16), tile_g(q_g), tile_g(k_g), ones_bd,
      cos, slo, shi)


def _lru_kernel(xa_ref, xp_ref, xn_ref, cw_ref, cb_ref, lam_ref, w_ref, bias_ref, h_ref,
                a_scr, b_scr, c_scr, *, ntb, reverse):
    s = pl.program_id(1)
    ti = jnp.where(s == 0, 0, ntb - s) if reverse else s
    lw = LRU_WIDTH
    x = xa_ref[0].astype(F32)
    row = lax.broadcasted_iota(I32, (TM, lw), 0)
    has_prev = jnp.where(ti > 1, 1.0, 0.0)
    has_next = jnp.where((ti > 0) & (ti < ntb - 1), 1.0, 0.0)
    prev = xp_ref[0].astype(F32) * has_prev
    nxt = xn_ref[0].astype(F32) * has_next
    xm1 = jnp.where(row == 0, prev[7:8], pltpu.roll(x, 1, 0))
    xm2 = jnp.where(row == 0, prev[6:7], jnp.where(row == 1, prev[7:8], pltpu.roll(x, 2, 0)))
    xp1 = jnp.where(row == TM - 1, nxt[0:1], pltpu.roll(x, TM - 1, 0))
    cw = cw_ref[...]
    xc = cw[0:1] * xm2 + cw[1:2] * xm1 + cw[2:3] * x + cw[3:4] * xp1 + cb_ref[...]

    z = jnp.dot(xc.astype(BF16), w_ref[...], preferred_element_type=F32) + bias_ref[...]
    r = jax.nn.sigmoid(z[:, :lw])
    ig = jax.nn.sigmoid(z[:, lw:])
    neg_lam = -lam_ref[...]
    softplus = jnp.maximum(neg_lam, 0.0) + jnp.log1p(jnp.exp(-jnp.abs(neg_lam)))
    log_a = (-LRU_C) * r * softplus
    a = jnp.exp(log_a)
    b = jnp.sqrt(-jnp.tanh(log_a) * (a * a + 1.0)) * (ig * xc)

    r8 = row & 7
    for sft in (1, 2, 4):
        if reverse:
            a_s, b_s, m = pltpu.roll(a, TM - sft, 0), pltpu.roll(b, TM - sft, 0), r8 < 8 - sft
        else:
            a_s, b_s, m = pltpu.roll(a, sft, 0), pltpu.roll(b, sft, 0), r8 >= sft
        b = jnp.where(m, a * b_s + b, b)
        a = jnp.where(m, a * a_s, a)
    a_scr[...] = a
    b_scr[...] = b

    @pl.when(s == 0)
    def _():
        c_scr[...] = jnp.zeros_like(c_scr)

    ng = TM // 8

    def body(j, carry):
        g = (ng - 1 - j) if reverse else j
        off = pl.multiple_of(g * 8, 8)
        h = a_scr[pl.ds(off, 8), :] * carry + b_scr[pl.ds(off, 8), :]
        h_ref[0, pl.ds(off, 8), :] = h
        last = h[0:1] if reverse else h[7:8]
        return jnp.broadcast_to(last, (8, lw))

    c_scr[...] = lax.fori_loop(0, ng, body, c_scr[...], unroll=4)


def _lru(xa, conv_w, conv_b, lam, r_w, r_b, i_w, i_b, reverse):
    bsz, s, lw = xa.shape
    ntb = s // TM
    d = 1 if reverse else 0
    bd = lambda w: jax.scipy.linalg.block_diag(*[w[i] for i in range(LRU_BLOCKS)])
    w = jnp.concatenate([bd(r_w[d]), bd(i_w[d])], axis=1).astype(BF16)
    bias = jnp.concatenate([r_b[d], i_b[d]])[None, :]
    cw = jnp.concatenate([conv_w, jnp.zeros((4, lw), F32)], axis=0)
    if reverse:
        tile = lambda t: jnp.where(t == 0, 0, ntb - t)
    else:
        tile = lambda t: t
    nb8 = s // 8
    return pl.pallas_call(
        functools.partial(_lru_kernel, ntb=ntb, reverse=reverse),
        grid=(bsz, ntb),
        in_specs=[pl.BlockSpec((1, TM, lw), lambda b, t: (b, tile(t), 0)),
                  pl.BlockSpec((1, 8, lw), lambda b, t: (b, jnp.maximum(tile(t) * (TM // 8) - 1, 0), 0)),
                  pl.BlockSpec((1, 8, lw), lambda b, t: (b, jnp.minimum((tile(t) + 1) * (TM // 8), nb8 - 1), 0)),
                  pl.BlockSpec((8, lw), lambda b, t: (0, 0)),
                  pl.BlockSpec((1, lw), lambda b, t: (0, 0)),
                  pl.BlockSpec((1, lw), lambda b, t: (0, 0)),
                  pl.BlockSpec((lw, 2 * lw), lambda b, t: (0, 0)),
                  pl.BlockSpec((1, 2 * lw), lambda b, t: (0, 0))],
        out_specs=pl.BlockSpec((1, TM, lw), lambda b, t: (b, tile(t), 0)),
        out_shape=jax.ShapeDtypeStruct((bsz, s, lw), F32),
        scratch_shapes=[pltpu.VMEM((TM, lw), F32), pltpu.VMEM((TM, lw), F32), pltpu.VMEM((8, lw), F32)],
        compiler_params=_cparams(("arbitrary", "arbitrary")),
        name="lru_rev" if reverse else "lru_fwd",
    )(xa, xa, xa, cw, conv_b[None, :], lam[d][None, :], w, bias)


def _attn_kernel(lamv_ref, q_ref, k_ref, v_ref, sg_ref, o_ref, m_scr, l_scr, acc_scr, al_scr, s_scr, p_scr, *,
                 nkv, lam_init):
    t = pl.program_id(2)
    q = q_ref[0]
    dh = ATT_DH
    dv = ATT_DV
    m_scr[...] = jnp.full_like(m_scr, -jnp.inf)
    l_scr[...] = jnp.zeros_like(l_scr)
    acc_scr[...] = jnp.zeros_like(acc_scr)

    def scores(buf, off, size):
        kc = k_ref[0, pl.ds(off, size), :]
        for mi in range(2):
            s_scr[buf, mi, :, 0:size] = lax.dot_general(
                q[:, mi * dh:(mi + 1) * dh], kc[:, mi * dh:(mi + 1) * dh], (((1,), (1,)), ((), ())),
                preferred_element_type=F32)

    def softmax(buf, size):
        nlb = size // 128
        groups = [(slice(r * 16, (r + 1) * 16), mi) for r in range(TM // 16) for mi in range(2)]
        for rows, mi in groups:
            sc = s_scr[buf, mi, rows, 0:size]
            mx = functools.reduce(jnp.maximum, [sc[:, i * 128:(i + 1) * 128] for i in range(nlb)])
            m_old = m_scr[mi, rows, :]
            m_new = jnp.maximum(m_old, jnp.max(mx, axis=1, keepdims=True))
            m_scr[mi, rows, :] = m_new
            al_scr[mi, rows, :] = jnp.exp2(m_old - m_new)
        for rows, mi in groups:
            m_new = m_scr[mi, rows, :]
            ps = [jnp.exp2(s_scr[buf, mi, rows, i * 128:(i + 1) * 128] - m_new) for i in range(nlb)]
            l_scr[mi, rows, :] = al_scr[mi, rows, :] * l_scr[mi, rows, :] + functools.reduce(jnp.add, ps)
            p_scr[buf, mi, rows, 0:size] = jnp.concatenate(ps, axis=1).astype(BF16)

    def values(buf, off, size):
        vc = v_ref[0, pl.ds(off, size), :]
        pv = jnp.dot(p_scr[buf, :, :, 0:size].reshape(2 * TM, size), vc, preferred_element_type=F32)
        acc_scr[...] = al_scr[...] * acc_scr[...] + pv.reshape(2, TM, dv)

    @pl.when(t == 0)
    def _():
        scores(0, 0, CTX_LEN)
        softmax(0, CTX_LEN)
        values(0, 0, CTX_LEN)

    @pl.when(t > 0)
    def _():
        tk = ATT_TK
        nchunk = nkv // tk
        scores(0, 0, tk)
        for j in range(nchunk):
            if j + 1 < nchunk:
                scores((j + 1) % 2, (j + 1) * tk, tk)
            softmax(j % 2, tk)
            values(j % 2, j * tk, tk)

    lv = lamv_ref[...]
    lam = (jnp.exp(jnp.sum(lv[0:1] * lv[1:2], axis=1, keepdims=True))
           - jnp.exp(jnp.sum(lv[2:3] * lv[3:4], axis=1, keepdims=True)) + lam_init)
    l0 = jnp.sum(l_scr[0], axis=1, keepdims=True)
    l1 = jnp.sum(l_scr[1], axis=1, keepdims=True)
    o = acc_scr[0] / l0 - lam * (acc_scr[1] / l1)
    y = o * lax.rsqrt(jnp.mean(o * o, axis=-1, keepdims=True) + EPS) * sg_ref[...] * (1.0 - lam_init)
    o_ref[0] = y.astype(BF16)


def _attention(q, k, v, lam_vecs, subln_g, lam_init):
    bsz, s, _ = q.shape
    ntb = s // TM
    assert s % ATT_TK == 0
    dv = ATT_DV
    lamv = jnp.concatenate([jnp.pad(lam_vecs, ((0, 0), (0, dv - ATT_DH))), jnp.zeros((4, dv), F32)], axis=0)
    return pl.pallas_call(
        functools.partial(_attn_kernel, nkv=s, lam_init=lam_init),
        grid=(bsz, ATT_HEADS, ntb),
        in_specs=[pl.BlockSpec((8, dv), lambda b, h, t: (0, 0)),
                  pl.BlockSpec((1, TM, dv), lambda b, h, t: (b, t, h)),
                  pl.BlockSpec((1, s, dv), lambda b, h, t: (b, 0, h)),
                  pl.BlockSpec((1, s, dv), lambda b, h, t: (b, 0, h)),
                  pl.BlockSpec((1, dv), lambda b, h, t: (0, 0))],
        out_specs=pl.BlockSpec((1, TM, dv), lambda b, h, t: (b, t, h)),
        out_shape=jax.ShapeDtypeStruct((bsz, s, ATT_WIDTH), BF16),
        scratch_shapes=[pltpu.VMEM((2, TM, dv), F32)] * 4
        + [pltpu.VMEM((2, 2, TM, ATT_TK), F32), pltpu.VMEM((2, 2, TM, ATT_TK), BF16)],
        compiler_params=_cparams(("arbitrary", "arbitrary", "arbitrary")),
        name="diff_attention",
    )(lamv, q, k, v, subln_g[None, :])


def _post_mixer(y, x, g1, sh2, sc2, n2g, rwt):
    x1 = x + g1 * y
    h2 = _norm_mod(x1, n2g, sh2, sc2)
    logits = lax.dot_general(rwt, h2, (((1,), (1,)), ((), ())), precision=HIGHEST, preferred_element_type=F32)
    return x1, h2.astype(BF16), logits


def _outproj_kernel(hf_ref, hr_ref, gate_ref, yb_ref, x_ref, c_ref, g1_ref, sh_ref, sc_ref, n2g_ref, w_ref,
                    rwt_ref, x1_ref, h2_ref, lg_ref):
    t = pl.program_id(1)
    x = jnp.where(t == 0, c_ref[0], x_ref[0])
    ya = ((hf_ref[0] + hr_ref[0]) * jax.nn.gelu(gate_ref[0].astype(F32))).astype(BF16)
    lw = LRU_WIDTH
    y = (jnp.dot(ya, w_ref[0:lw, :], preferred_element_type=F32)
         + jnp.dot(yb_ref[0], w_ref[lw:, :], preferred_element_type=F32))
    x1_ref[0], h2_ref[0], lg_ref[...] = _post_mixer(y, x, g1_ref[0], sh_ref[0], sc_ref[0], n2g_ref[0],
                                                    rwt_ref[...])


def _outproj(hf, hr, gate, yb, x, ctx, modtab, norm_g, w_out, router_w):
    bsz, s, lw = hf.shape
    dm = D_MODEL
    ntb = s // TM
    row = lambda b, t: jnp.where(t == 0, bsz, b)
    half = pl.BlockSpec((1, TM, lw), lambda b, t: (b, t, 0))
    return pl.pallas_call(
        _outproj_kernel,
        grid=(bsz, ntb),
        in_specs=[half, half, half, half,
                  pl.BlockSpec((1, TM, dm), lambda b, t: (b, jnp.maximum(t - 1, 0), 0)),
                  pl.BlockSpec((1, TM, dm), lambda b, t: (b, 0, 0)),
                  _mod_spec(0, 2, row), _mod_spec(0, 3, row), _mod_spec(0, 4, row),
                  pl.BlockSpec((1, 1, dm), lambda b, t: (0, 0, 0)),
                  pl.BlockSpec((2 * lw, dm), lambda b, t: (0, 0)),
                  pl.BlockSpec((N_EXPERTS, dm), lambda b, t: (0, 0))],
        out_specs=[pl.BlockSpec((1, TM, dm), lambda b, t: (b, t, 0)),
                   pl.BlockSpec((1, TM, dm), lambda b, t: (b, t, 0)),
                   pl.BlockSpec((N_EXPERTS, TM), lambda b, t: (0, b * ntb + t))],
        out_shape=[jax.ShapeDtypeStruct((bsz, s, dm), F32),
                   jax.ShapeDtypeStruct((bsz, s, dm), BF16),
                   jax.ShapeDtypeStruct((N_EXPERTS, bsz * s), F32)],
        compiler_params=_cparams(("arbitrary", "arbitrary")),
        name="outproj",
    )(hf, hr, gate, yb, x, ctx, modtab, modtab, modtab, norm_g.reshape(1, 1, dm), w_out.astype(BF16),
      router_w.T)


def _route_kernel(lg_ref, bias_ref, tri_ref, low_ref, pos_ref, gate_ref, cnt_ref):
    ne = N_EXPERTS
    aff = jax.nn.sigmoid(lg_ref[...])
    work = aff + bias_ref[:, 0:1]
    eidx = lax.broadcasted_iota(I32, (ne, TM), 0)
    sels = []
    for _ in range(TOP_K):
        mx = jnp.max(work, axis=0, keepdims=True)
        am = jnp.min(jnp.where(work == mx, eidx, ne), axis=0, keepdims=True)
        sk = eidx == am
        sels.append(sk)
        work = jnp.where(sk, -jnp.inf, work)
    sel = sels[0]
    for sk in sels[1:]:
        sel = sel | sk
    self = jnp.where(sel, 1.0, 0.0)
    s_sel = aff * self
    gates = s_sel / jnp.sum(s_sel, axis=0, keepdims=True) * ROUTED_SCALE
    rank = jnp.dot(self.astype(BF16), tri_ref[...], preferred_element_type=F32)
    cnt = jnp.sum(self, axis=1, keepdims=True)
    cnt_seg = jnp.floor((cnt + (SEG - 1)) * (1.0 / SEG)) * SEG
    cnt_b = jnp.broadcast_to(cnt_seg, (ne, 128))
    seg_off = jnp.dot(low_ref[...], cnt_b.astype(BF16), preferred_element_type=F32)
    lpos = seg_off[:, 0:1] + rank
    pos_rows, gate_rows = [], []
    for sk in sels:
        pos_rows.append(jnp.sum(jnp.where(sk, lpos, 0.0), axis=0, keepdims=True))
        gate_rows.append(jnp.sum(jnp.where(sk, gates, 0.0), axis=0, keepdims=True))
    for _ in range(8 - TOP_K):
        pos_rows.append(jnp.full((1, TM), -1.0, F32))
        gate_rows.append(jnp.zeros((1, TM), F32))
    pos_ref[...] = jnp.concatenate(pos_rows, axis=0).astype(I32)
    gate_ref[...] = jnp.concatenate(gate_rows, axis=0)
    cnt_ref[0] = cnt_b.astype(I32)


def _route(lgt, router_bias):
    ne, t = lgt.shape
    nt = t // TM
    tri = jnp.triu(jnp.ones((TM, TM), F32), k=1).astype(BF16)
    low = jnp.tril(jnp.ones((ne, ne), F32), k=-1).astype(BF16)
    bias = jnp.broadcast_to(router_bias[:, None], (ne, 128))
    pos, gate, cnt = pl.pallas_call(
        _route_kernel,
        grid=(nt,),
        in_specs=[pl.BlockSpec((ne, TM), lambda i: (0, i)),
                  pl.BlockSpec((ne, 128), lambda i: (0, 0)),
                  pl.BlockSpec((TM, TM), lambda i: (0, 0)),
                  pl.BlockSpec((ne, ne), lambda i: (0, 0))],
        out_specs=[pl.BlockSpec((8, TM), lambda i: (0, i)),
                   pl.BlockSpec((8, TM), lambda i: (0, i)),
                   pl.BlockSpec((1, ne, 128), lambda i: (i, 0, 0))],
        out_shape=[jax.ShapeDtypeStruct((8, t), I32),
                   jax.ShapeDtypeStruct((8, t), F32),
                   jax.ShapeDtypeStruct((nt, ne, 128), I32)],
        compiler_params=_cparams(("arbitrary",)),
        name="route",
    )(lgt, bias, tri, low)
    return pos, gate, cnt[:, :, 0]


def _moe_layout(cnt_seg, nblk_max):
    tot = jnp.sum(cnt_seg, axis=0)
    region = (tot + TR - 1) // TR * TR
    region_end = jnp.cumsum(region)
    goff = (region_end - region)[None, :] + jnp.cumsum(cnt_seg, axis=0) - cnt_seg
    packed = ((goff // SEG) << 5) | (cnt_seg // SEG)
    nblk = region_end[-1] // TR
    blk = jnp.arange(nblk_max, dtype=I32)
    blk_e = jnp.sum((region_end[None, :] // TR <= blk[:, None]).astype(I32), axis=1)
    blk_e = jnp.minimum(blk_e, N_EXPERTS - 1).astype(I32)
    tile_info = jnp.sum(cnt_seg, axis=1) | (jnp.any(cnt_seg >= 64, axis=1).astype(I32) << 16)
    tails = (((region_end - region + tot) // SEG) << 5) | ((region - tot) // SEG)
    tails = jnp.concatenate([tails, nblk[None]]).astype(I32)
    return packed.reshape(-1).astype(I32), tile_info.astype(I32), tails, blk_e, nblk.astype(I32).reshape(1)


def _segment_copy(stage, hbm, sem, to_sorted, lo, go, size):
    lo, go = pl.multiple_of(lo, SEG), pl.multiple_of(go, SEG)
    a, b = stage.at[pl.ds(lo, size)], hbm.at[pl.ds(go, size)]
    return pltpu.make_async_copy(a, b, sem) if to_sorted else pltpu.make_async_copy(b, a, sem)


def _segment_unpack(pk):
    return (pk & 31) * SEG, (pk >> 5) * SEG


def _segment_starts(pk_ref, tile, stage, hbm, sem, to_sorted, valid=None):
    loff = [0]

    def step(e):
        cnt, goff = _segment_unpack(pk_ref[tile * N_EXPERTS + e])
        done = (cnt >> 6) << 6
        for size in (32, 16):
            bit = (cnt & size) != 0
            if valid is not None:
                bit = bit & valid

            @pl.when(bit)
            def _(lo=loff[0] + done, go=goff + done, size=size):
                _segment_copy(stage, hbm, sem, to_sorted, lo, go, size).start()

            done = done + jnp.where(bit, size, 0)
        loff[0] = loff[0] + cnt

    return [functools.partial(step, e) for e in range(N_EXPERTS)]


def _interleave(*step_lists):
    total = max(len(steps) for steps in step_lists)
    done = [0] * len(step_lists)
    for t in range(1, total + 1):
        for k, steps in enumerate(step_lists):
            upto = len(steps) * t // total
            for step in steps[done[k]:upto]:
                step()
            done[k] = upto


def _segment_starts_long(pk_ref, tile, info, stage, hbm, sem, to_sorted):
    @pl.when((info >> 16) != 0)
    def _():
        def expert(e, loff):
            cnt, goff = _segment_unpack(pk_ref[tile * N_EXPERTS + e])

            def chunk(j, carry):
                _segment_copy(stage, hbm, sem, to_sorted, loff + j * 64, goff + j * 64, 64).start()
                return carry

            lax.fori_loop(0, cnt >> 6, chunk, 0)
            return loff + cnt

        lax.fori_loop(0, N_EXPERTS, expert, 0)


def _segment_wait(total, stage, hbm, sem, to_sorted):
    size = 1 << (RMAX.bit_length() - 1)
    while size >= SEG:
        @pl.when((total & size) != 0)
        def _(size=size):
            _segment_copy(stage, hbm, sem, to_sorted, 0, 0, size).wait()

        size //= 2


def _fill_slot_matrix(dst_ref, pos, weight_rows=None):
    ch = 64
    riota = lax.broadcasted_iota(I32, (ch, TM), 0)

    def step(c):
        local = pos - c * ch
        out = jnp.zeros((ch, TM), F32)
        for k in range(TOP_K):
            w = 1.0 if weight_rows is None else weight_rows[k:k + 1]
            out = jnp.where(riota == local[k:k + 1], w, out)
        dst_ref[c * ch:(c + 1) * ch, :] = out.astype(BF16)

    return [functools.partial(step, c) for c in range(RMAX // ch)]


def _zero_fill(tail_ref, zeros, xs_ref, sem, fn):
    def expert(e, carry):
        pk = tail_ref[e]
        off = (pk >> 5) * SEG

        def chunk(j, c):
            dst = xs_ref.at[pl.ds(pl.multiple_of(off + j * SEG, SEG), SEG)]
            fn(pltpu.make_async_copy(zeros.at[pl.ds(0, SEG)], dst, sem))
            return c

        return lax.fori_loop(0, pk & 31, chunk, carry)

    lax.fori_loop(0, N_EXPERTS, expert, 0)

    def block(j, carry):
        fn(pltpu.make_async_copy(zeros, xs_ref.at[pl.ds(pl.multiple_of(j * TR, TR), TR)], sem))
        return carry

    lax.fori_loop(tail_ref[N_EXPERTS], xs_ref.shape[0] // TR, block, 0)


def _dispatch_kernel(pk_ref, tot_ref, tail_ref, pos_ref, h_ref, xs_ref, stage, zeros, slots, sem, zsem):
    i = pl.program_id(0)
    slot = i % 2

    @pl.when(i == 0)
    def _():
        zeros[...] = jnp.zeros_like(zeros)
        _zero_fill(tail_ref, zeros, xs_ref, zsem, lambda c: c.start())

    for step in _fill_slot_matrix(slots, pos_ref[...]):
        step()
    stage[slot] = jnp.dot(slots[...], h_ref[...], preferred_element_type=F32).astype(BF16)
    prev_rows = jnp.where(i > 0, tot_ref[jnp.maximum(i - 1, 0)] & 0xFFFF, 0)
    _segment_wait(prev_rows, stage.at[1 - slot], xs_ref, sem.at[1 - slot], True)
    this = (stage.at[slot], xs_ref, sem.at[slot], True)
    for step in _segment_starts(pk_ref, i, *this):
        step()
    _segment_starts_long(pk_ref, i, tot_ref[i], *this)

    @pl.when(i == pl.num_programs(0) - 1)
    def _():
        _segment_wait(tot_ref[i] & 0xFFFF, *this)
        _zero_fill(tail_ref, zeros, xs_ref, zsem, lambda c: c.wait())


def _dispatch(packed, tile_rows, tails, pos, h2, nrows):
    t, dm = h2.shape
    nt = t // TM
    return pl.pallas_call(
        _dispatch_kernel,
        grid_spec=pltpu.PrefetchScalarGridSpec(
            num_scalar_prefetch=3,
            grid=(nt,),
            in_specs=[pl.BlockSpec((8, TM), lambda i, pk, tot, tail: (0, i)),
                      pl.BlockSpec((TM, dm), lambda i, pk, tot, tail: (i, 0))],
            out_specs=pl.BlockSpec(memory_space=pl.ANY),
            scratch_shapes=[pltpu.VMEM((2, RMAX, dm), BF16), pltpu.VMEM((TR, dm), BF16),
                            pltpu.VMEM((RMAX, TM), BF16),
                            pltpu.SemaphoreType.DMA((2,)), pltpu.SemaphoreType.DMA]),
        out_shape=jax.ShapeDtypeStruct((nrows, dm), BF16),
        compiler_params=_cparams(("arbitrary",)),
        name="moe_dispatch",
    )(packed, tile_rows, tails, pos, h2)


def _expert_kernel(be_ref, nb_ref, x_ref, wg_ref, wu_ref, wd_ref, y_ref, wg_s, wu_s, wd_s):
    j = pl.program_id(0)
    changed = be_ref[j] != be_ref[jnp.maximum(j - 1, 0)]

    @pl.when((j == 0) | changed)
    def _():
        wg_s[...] = wg_ref[0].astype(BF16)
        wu_s[...] = wu_ref[0].astype(BF16)
        wd_s[...] = wd_ref[0].astype(BF16)

    @pl.when(j < nb_ref[0])
    def _():
        x = x_ref[...]
        g = jnp.dot(x, wg_s[...], preferred_element_type=F32)
        u = jnp.dot(x, wu_s[...], preferred_element_type=F32)
        a = (g * jax.nn.sigmoid(g) * u).astype(BF16)
        y_ref[...] = jnp.dot(a, wd_s[...], preferred_element_type=F32).astype(BF16)


def _experts(blk_e, nblk, xs, w_gate, w_up, w_down):
    nrows, dm = xs.shape
    nblk_max = nrows // TR
    de = D_EXPERT
    row_blk = lambda j, be, nb: (jnp.minimum(j, nb[0] - 1), 0)
    return pl.pallas_call(
        _expert_kernel,
        grid_spec=pltpu.PrefetchScalarGridSpec(
            num_scalar_prefetch=2,
            grid=(nblk_max,),
            in_specs=[pl.BlockSpec((TR, dm), row_blk),
                      pl.BlockSpec((1, dm, de), lambda j, be, nb: (be[j], 0, 0)),
                      pl.BlockSpec((1, dm, de), lambda j, be, nb: (be[j], 0, 0)),
                      pl.BlockSpec((1, de, dm), lambda j, be, nb: (be[j], 0, 0))],
            out_specs=pl.BlockSpec((TR, dm), row_blk),
            scratch_shapes=[pltpu.VMEM((dm, de), BF16), pltpu.VMEM((dm, de), BF16), pltpu.VMEM((de, dm), BF16)]),
        out_shape=jax.ShapeDtypeStruct((nrows, dm), BF16),
        input_output_aliases={2: 0},
        compiler_params=_cparams(("arbitrary",)),
        name="moe_experts",
    )(blk_e, nblk, xs, w_gate, w_up, w_down)


def _combine_kernel(pk_ref, tot_ref, pos_ref, gate_ref, h_ref, x1_ref, g2_ref, wsg_ref, wsu_ref, wsd_ref, ys_ref,
                    o_ref, stage, gates, sem):
    i = pl.program_id(0)
    slot = i % 2
    last = pl.num_programs(0) - 1

    @pl.when(i == 0)
    def _():
        stage[...] = jnp.zeros_like(stage)
        for step in _segment_starts(pk_ref, 0, stage.at[0], ys_ref, sem.at[0], False):
            step()
        _segment_starts_long(pk_ref, 0, tot_ref[0], stage.at[0], ys_ref, sem.at[0], False)

    nxt = jnp.minimum(i + 1, last)
    _interleave(_fill_slot_matrix(gates, pos_ref[...], gate_ref[...]),
                _segment_starts(pk_ref, nxt, stage.at[1 - slot], ys_ref, sem.at[1 - slot], False))
    h = h_ref[...]
    g = jnp.dot(h, wsg_ref[...], preferred_element_type=F32)
    u = jnp.dot(h, wsu_ref[...], preferred_element_type=F32)
    shared = jnp.dot((g * jax.nn.sigmoid(g) * u).astype(BF16), wsd_ref[...], preferred_element_type=F32)
    _segment_wait(tot_ref[i] & 0xFFFF, stage.at[slot], ys_ref, sem.at[slot], False)
    routed = lax.dot_general(gates[...], stage[slot], (((0,), (0,)), ((), ())), preferred_element_type=F32)
    o_ref[...] = x1_ref[...] + g2_ref[0] * (routed + shared)
    _segment_starts_long(pk_ref, nxt, tot_ref[nxt], stage.at[1 - slot], ys_ref, sem.at[1 - slot], False)

    @pl.when(i == last)
    def _():
        _segment_wait(tot_ref[i] & 0xFFFF, stage.at[1 - slot], ys_ref, sem.at[1 - slot], False)


def _combine(packed, tile_rows, pos, gate, h2, x1, modtab, layer, row_fn, ys, ws_gate, ws_up, ws_down):
    t, dm = h2.shape
    nt = t // TM
    de = D_EXPERT
    const = lambda shape: pl.BlockSpec(shape, lambda i, pk, tot: (0,) * len(shape))
    return pl.pallas_call(
        _combine_kernel,
        grid_spec=pltpu.PrefetchScalarGridSpec(
            num_scalar_prefetch=2,
            grid=(nt,),
            in_specs=[pl.BlockSpec((8, TM), lambda i, pk, tot: (0, i)),
                      pl.BlockSpec((8, TM), lambda i, pk, tot: (0, i)),
                      pl.BlockSpec((TM, dm), lambda i, pk, tot: (i, 0)),
                      pl.BlockSpec((TM, dm), lambda i, pk, tot: (i, 0)),
                      pl.BlockSpec((1, 1, dm), lambda i, pk, tot: ((layer * 8 + row_fn(i)) * 6 + 5, 0, 0)),
                      const((dm, de)), const((dm, de)), const((de, dm)),
                      pl.BlockSpec(memory_space=pl.ANY)],
            out_specs=pl.BlockSpec((TM, dm), lambda i, pk, tot: (i, 0)),
            scratch_shapes=[pltpu.VMEM((2, RMAX, dm), BF16), pltpu.VMEM((RMAX, TM), BF16),
                            pltpu.SemaphoreType.DMA((2,))]),
        out_shape=jax.ShapeDtypeStruct((t, dm), F32),
        compiler_params=_cparams(("arbitrary",)),
        name="moe_combine",
    )(packed, tile_rows, pos, gate, h2, x1, modtab, ws_gate.astype(BF16), ws_up.astype(BF16),
      ws_down.astype(BF16), ys)


def _moe(h2, lgt, x1, modtab, layer, row_fn, router_bias, w_gate, w_up, w_down, ws_gate, ws_up, ws_down):
    t = h2.shape[0]
    nt = t // TM
    max_rows = t * TOP_K + nt * N_EXPERTS * (SEG - 1) + N_EXPERTS * (TR - 1)
    nblk_max = -(-max_rows // TR)
    pos, gate, cnt_seg = _route(lgt, router_bias)
    packed, tile_rows, tails, blk_e, nblk = _moe_layout(cnt_seg, nblk_max)
    xs = _dispatch(packed, tile_rows, tails, pos, h2, nblk_max * TR)
    ys = _experts(blk_e, nblk, xs, w_gate, w_up, w_down)
    return _combine(packed, tile_rows, pos, gate, h2, x1, modtab, layer, row_fn, ys, ws_gate, ws_up, ws_down)


S5_LANE_VREGS = D_MODEL // 128


def _segment_transpose(p):
    row = lax.broadcasted_iota(I32, (8, 128), 0)
    seg = lax.broadcasted_iota(I32, (8, 128), 1) // S5_GROUP
    for s in (4, 2, 1):
        m_up = ((row & s) == 0) & ((seg & s) != 0)
        m_dn = ((row & s) != 0) & ((seg & s) == 0)

        def swap(x):
            if s == 4:
                return jnp.where(m_up | m_dn, pltpu.roll(pltpu.roll(x, 4, 0), 4 * S5_GROUP, 1), x)
            up = pltpu.roll(pltpu.roll(x, 8 - s, 0), S5_GROUP * s, 1)
            dn = pltpu.roll(pltpu.roll(x, s, 0), 128 - S5_GROUP * s, 1)
            return jnp.where(m_up, up, jnp.where(m_dn, dn, x))

        p = [[swap(x) for x in half] for half in p]
    return [[p[1 - h][j - 1 + 2 * h] if (j % 2) != h else p[h][j] for j in range(S5_LANE_VREGS)] for h in range(2)]


def _s5_pack_kernel(x_ref, mod_ref, g_ref, xt_ref, u_scr, slab):
    t = pl.program_id(0)
    bsz = x_ref.shape[0]
    nchunk = TM // S5_TC
    for b in range(bsz):
        shift = jnp.where(t == 0, mod_ref[bsz, 0:1, :], mod_ref[b, 0:1, :])
        scale = jnp.where(t == 0, mod_ref[bsz, 1:2, :], mod_ref[b, 1:2, :])
        u_scr[b] = _norm_mod(x_ref[b], g_ref[0], shift, scale)

    def chunk(c, carry):
        r0 = pl.multiple_of(c * S5_TC, S5_TC)
        for b in range(bsz):
            p = [[u_scr[b, pl.ds(r0 + 8 * h, 8), 128 * j:128 * j + 128] for j in range(S5_LANE_VREGS)]
                 for h in range(2)]
            q = _segment_transpose(p)
            s0 = pl.multiple_of((c * bsz + b) * S5_TC, S5_TC)
            for h in range(2):
                for j in range(S5_LANE_VREGS):
                    slab[j, pl.ds(s0 + 8 * h, 8), :] = q[h][j]
        return carry

    lax.fori_loop(0, nchunk, chunk, 0)
    for j in range(S5_LANE_VREGS):
        for gl in range(S5_TC):
            g = (j // 2) * S5_TC + gl
            rows = slab[j, pl.ds(gl, nchunk * bsz, stride=S5_TC), :]
            xt_ref[g, :, 128 * (j % 2):128 * (j % 2) + 128] = rows.astype(BF16)


def _s5_pack(xall, modtab, layer, norm_g):
    bsz, s, dm = xall.shape
    ntb = s // TM
    rows = TM // S5_TC * bsz
    return pl.pallas_call(
        _s5_pack_kernel,
        grid=(ntb,),
        in_specs=[pl.BlockSpec((bsz, TM, dm), lambda t: (0, t, 0)),
                  pl.BlockSpec((8, 6, dm), lambda t: (layer, 0, 0)),
                  pl.BlockSpec((1, 1, dm), lambda t: (0, 0, 0))],
        out_specs=pl.BlockSpec((S5_GROUPS, rows, S5_TC * S5_GROUP), lambda t: (0, t, 0)),
        out_shape=jax.ShapeDtypeStruct((S5_GROUPS, ntb * rows, S5_TC * S5_GROUP), BF16),
        scratch_shapes=[pltpu.VMEM((bsz, TM, dm), F32), pltpu.VMEM((S5_LANE_VREGS, TM * bsz, 128), F32)],
        compiler_params=_cparams(("arbitrary",)),
        name="s5_pack",
    )(xall, modtab.reshape(DEPTH * 8, 6, dm), norm_g.reshape(1, 1, dm))


def _cmul(x, y):
    return x[0] * y[0] - x[1] * y[1], x[0] * y[1] + x[1] * y[0]


def _s5_weights(a_re, a_im, log_step, b_re, b_im, c_re, c_im, d_skip):
    tc, g, p, ch = S5_TC, S5_GROUPS, S5_STATE, S5_GROUP
    lam = (jnp.minimum(a_re, -1e-4), a_im)
    step = jnp.exp(log_step)
    mag = jnp.exp(lam[0] * step)
    lam_bar = (mag * jnp.cos(lam[1] * step), mag * jnp.sin(lam[1] * step))
    inv = 1.0 / (lam[0] * lam[0] + lam[1] * lam[1])
    coef = _cmul((lam_bar[0] - 1.0, lam_bar[1]), (lam[0] * inv, -lam[1] * inv))
    b_bar = _cmul((coef[0][..., None], coef[1][..., None]), (b_re, b_im))
    pw = [(jnp.ones_like(mag), jnp.zeros_like(mag))]
    for _ in range(tc):
        pw.append(_cmul(pw[-1], lam_bar))
    pw = (jnp.stack([q[0] for q in pw], axis=1), jnp.stack([q[1] for q in pw], axis=1))
    at = lambda d, idx: (pw[0][d, idx], pw[1][d, idx])
    cp = _cmul((c_re[:, None], c_im[:, None]), (pw[0][:, :tc, :, None, :], pw[1][:, :tc, :, None, :]))
    kern = (jnp.einsum("dkgop,dgpi->dkgoi", cp[0], b_bar[0], precision=HIGHEST)
            - jnp.einsum("dkgop,dgpi->dkgoi", cp[1], b_bar[1], precision=HIGHEST))
    s_idx = jnp.arange(tc)[:, None]
    t_idx = jnp.arange(tc)[None, :]

    def toeplitz(kd, tau):
        return jnp.where((tau >= 0)[:, :, None, None, None], kd[jnp.clip(tau, 0, tc - 1)], 0.0)

    m = toeplitz(kern[0], t_idx - s_idx) + toeplitz(kern[1], s_idx - t_idx)
    m = jnp.transpose(m, (2, 0, 4, 1, 3))
    eye_t = jnp.eye(tc, dtype=F32)[None, :, None, :, None]
    eye_c = jnp.eye(ch, dtype=F32)[None, None, :, None, :]
    m = m + eye_t * eye_c * d_skip.reshape(g, 1, ch, 1, 1)
    m = m.reshape(g, tc * ch, tc * ch)
    steps = jnp.arange(tc)
    lift = lambda z: (z[0][..., None], z[1][..., None])
    e_f = _cmul(lift(at(0, tc - 1 - steps)), (b_bar[0][0][None], b_bar[1][0][None]))
    e_r = _cmul(lift(at(1, steps)), (b_bar[0][1][None], b_bar[1][1][None]))
    w_in = jnp.stack([e_f[0], e_r[0], e_f[1], e_r[1]], axis=0)
    w_in = jnp.transpose(w_in, (2, 1, 4, 0, 3)).reshape(g, tc * ch, 4 * p)
    mid = lambda z: (z[0][:, :, None, :], z[1][:, :, None, :])
    g_f = _cmul((c_re[0][None], c_im[0][None]), mid(at(0, 1 + steps)))
    g_r = _cmul((c_re[1][None], c_im[1][None]), mid(at(1, tc - steps)))
    w_re = jnp.stack([g_f[0], g_r[0]], axis=0)
    w_im = -jnp.stack([g_f[1], g_r[1]], axis=0)
    to_rows = lambda w: jnp.transpose(w, (2, 0, 4, 1, 3)).reshape(g, 2 * p, tc * ch)
    a1 = (jnp.concatenate([pw[0][0, tc], pw[0][1, tc]], axis=-1),
          jnp.concatenate([pw[1][0, tc], pw[1][1, tc]], axis=-1))
    a2 = _cmul(a1, a1)
    second = _s5_second_rows(8)
    par = tuple(jnp.where(second[None], a2[k][:, None, :], a1[k][:, None, :]) for k in range(2))
    one = tuple(jnp.broadcast_to(a1[k][:, None, :], (g, 8, 2 * p)) for k in range(2))
    a_rows = jnp.concatenate([jnp.concatenate(par, axis=-1), jnp.concatenate(one, axis=-1)], axis=1)
    return m.astype(BF16), w_in.astype(BF16), to_rows(w_re).astype(BF16), to_rows(w_im).astype(BF16), a_rows


def _s5_second_rows(nrows):
    row = lax.broadcasted_iota(I32, (nrows, 2 * S5_STATE), 0)
    lane = lax.broadcasted_iota(I32, (nrows, 2 * S5_STATE), 1)
    return (lane < S5_STATE) != ((row & 7) < 4)


def _s5_kernel(x_ref, m_ref, win_ref, wre_ref, wim_ref, a_ref, y_ref, ure_scr, uim_scr, sre_scr, sim_scr, *,
               nblock, nctx):
    p2 = 2 * S5_STATE
    nrows = nblock * 8
    second = _s5_second_rows(nrows)
    fwd = lax.broadcasted_iota(I32, (nrows, p2), 1) < S5_STATE
    for g in range(S5_GB):
        v = jnp.dot(x_ref[g], win_ref[g], preferred_element_type=F32)
        vre, vim = v[:, 0:p2], v[:, p2:2 * p2]
        a_re, a_im = a_ref[g, 8:9, 0:p2], a_ref[g, 8:9, p2:2 * p2]
        fre = jnp.where(fwd, pltpu.roll(vre, 4, 0), pltpu.roll(vre, nrows - 4, 0))
        fim = jnp.where(fwd, pltpu.roll(vim, 4, 0), pltpu.roll(vim, nrows - 4, 0))
        ure_scr[g] = vre + jnp.where(second, a_re * fre - a_im * fim, 0.0)
        uim_scr[g] = vim + jnp.where(second, a_re * fim + a_im * fre, 0.0)
    second8 = second[0:8]
    fwd8 = fwd[0:8]
    ap_re = [a_ref[g, 0:8, 0:p2] for g in range(S5_GB)]
    ap_im = [a_ref[g, 0:8, p2:2 * p2] for g in range(S5_GB)]

    def body(s, carry):
        jr = jnp.where(s < nctx, nctx - 1 - s, nblock - 1 + nctx - s)
        of = pl.multiple_of(s * 8, 8)
        orv = pl.multiple_of(jr * 8, 8)
        new = []
        for g in range(S5_GB):
            cre, cim = carry[g]
            ure = jnp.where(fwd8, ure_scr[g, pl.ds(of, 8), :], ure_scr[g, pl.ds(orv, 8), :])
            uim = jnp.where(fwd8, uim_scr[g, pl.ds(of, 8), :], uim_scr[g, pl.ds(orv, 8), :])
            zre = ap_re[g] * cre - ap_im[g] * cim + ure
            zim = ap_re[g] * cim + ap_im[g] * cre + uim
            rre, rim = pltpu.roll(zre, 4, 0), pltpu.roll(zim, 4, 0)
            ere, eim = jnp.where(second8, rre, cre), jnp.where(second8, rim, cim)
            sre_scr[g, pl.ds(of, 8), 0:S5_STATE] = ere[:, 0:S5_STATE]
            sre_scr[g, pl.ds(orv, 8), S5_STATE:p2] = ere[:, S5_STATE:p2]
            sim_scr[g, pl.ds(of, 8), 0:S5_STATE] = eim[:, 0:S5_STATE]
            sim_scr[g, pl.ds(orv, 8), S5_STATE:p2] = eim[:, S5_STATE:p2]
            new.append((jnp.where(second8, zre, rre), jnp.where(second8, zim, rim)))
        return tuple(new)

    zero = jnp.zeros((8, p2), F32)
    lax.fori_loop(0, nblock, body, tuple((zero, zero) for _ in range(S5_GB)))
    for g in range(S5_GB):
        y = (jnp.dot(x_ref[g], m_ref[g], preferred_element_type=F32)
             + jnp.dot(sre_scr[g].astype(BF16), wre_ref[g], preferred_element_type=F32)
             + jnp.dot(sim_scr[g].astype(BF16), wim_ref[g], preferred_element_type=F32))
        y_ref[g] = y.astype(BF16)


def _s5(xt, bsz, weights):
    g, rows, lanes = xt.shape
    assert 2 * bsz == 8
    nchunk = rows // bsz
    nctx = CTX_LEN // S5_TC
    assert nchunk % 2 == 0 and nctx % 2 == 0
    m, w_in, w_re, w_im, a_rows = weights
    p2 = 2 * S5_STATE
    gb = S5_GB
    wspec = lambda r, c: pl.BlockSpec((gb, r, c), lambda i: (i, 0, 0))
    return pl.pallas_call(
        functools.partial(_s5_kernel, nblock=nchunk // 2, nctx=nctx // 2),
        grid=(g // gb,),
        in_specs=[wspec(rows, lanes), wspec(lanes, lanes), wspec(lanes, 2 * p2), wspec(p2, lanes),
                  wspec(p2, lanes), wspec(16, 2 * p2)],
        out_specs=wspec(rows, lanes),
        out_shape=jax.ShapeDtypeStruct((g, rows, lanes), BF16),
        scratch_shapes=[pltpu.VMEM((gb, rows, p2), F32)] * 4,
        compiler_params=_cparams(("arbitrary",)),
        name="s5",
    )(xt, m, w_in, w_re, w_im, a_rows)


GLU_SAMPLES = 2


def _glu_kernel(y_ref, x_ref, mod_ref, n2g_ref, w_ref, b_ref, rwt_ref, x1_ref, h2_ref, lg_ref, slab, y_scr, *,
                bsz):
    dm = D_MODEL
    half = pl.program_id(1)
    nchunk = TM // S5_TC

    @pl.when(half == 0)
    def _():
        for j in range(S5_LANE_VREGS):
            for gl in range(S5_TC):
                g = (j // 2) * S5_TC + gl
                rows = y_ref[g, :, 128 * (j % 2):128 * (j % 2) + 128].astype(F32)
                slab[j, pl.ds(gl, nchunk * bsz, stride=S5_TC), :] = rows

    for k in range(GLU_SAMPLES):
        b = half * GLU_SAMPLES + k

        def chunk(c, carry):
            s0 = pl.multiple_of((c * bsz + b) * S5_TC, S5_TC)
            p = [[slab[j, pl.ds(s0 + 8 * h, 8), :] for j in range(S5_LANE_VREGS)] for h in range(2)]
            q = _segment_transpose(p)
            r0 = pl.multiple_of(c * S5_TC, S5_TC)
            for h in range(2):
                for j in range(S5_LANE_VREGS):
                    y_scr[pl.ds(r0 + 8 * h, 8), 128 * j:128 * j + 128] = q[h][j]
            return carry

        lax.fori_loop(0, nchunk, chunk, 0)
        z = jax.nn.gelu(y_scr[...]).astype(BF16)
        zz = jnp.dot(z, w_ref[...], preferred_element_type=F32) + b_ref[...]
        glu = zz[:, :dm] * jax.nn.sigmoid(zz[:, dm:])
        x1_ref[k], h2_ref[k], lg_ref[k] = _post_mixer(glu, x_ref[k], mod_ref[b, 2:3, :], mod_ref[b, 3:4, :],
                                                      mod_ref[b, 4:5, :], n2g_ref[0], rwt_ref[...])


def _glu(y, xall, modtab, layer, norm_g, glu_w, glu_b, router_w):
    bsz, s, dm = xall.shape
    n = s - CTX_LEN
    ntl = n // TM
    assert bsz == 2 * GLU_SAMPLES
    gs = GLU_SAMPLES
    rows = TM // S5_TC * bsz
    ctx_tiles = CTX_LEN // TM
    tok = pl.BlockSpec((gs, TM, dm), lambda t, h: (h, t, 0))
    x1, h2, lg = pl.pallas_call(
        functools.partial(_glu_kernel, bsz=bsz),
        grid=(ntl, bsz // gs),
        in_specs=[pl.BlockSpec((S5_GROUPS, rows, S5_TC * S5_GROUP), lambda t, h: (0, t + ctx_tiles, 0)),
                  pl.BlockSpec((gs, TM, dm), lambda t, h: (h, t + ctx_tiles, 0)),
                  pl.BlockSpec((8, 6, dm), lambda t, h: (layer, 0, 0)),
                  pl.BlockSpec((1, 1, dm), lambda t, h: (0, 0, 0)),
                  pl.BlockSpec((dm, 2 * dm), lambda t, h: (0, 0)),
                  pl.BlockSpec((1, 2 * dm), lambda t, h: (0, 0)),
                  pl.BlockSpec((N_EXPERTS, dm), lambda t, h: (0, 0))],
        out_specs=[tok, tok, pl.BlockSpec((gs, N_EXPERTS, TM), lambda t, h: (h, 0, t))],
        out_shape=[jax.ShapeDtypeStruct((bsz, n, dm), F32),
                   jax.ShapeDtypeStruct((bsz, n, dm), BF16),
                   jax.ShapeDtypeStruct((bsz, N_EXPERTS, n), F32)],
        scratch_shapes=[pltpu.VMEM((S5_LANE_VREGS, TM * bsz, 128), F32), pltpu.VMEM((TM, dm), F32)],
        compiler_params=_cparams(("arbitrary", "arbitrary")),
        name="glu",
    )(y, xall, modtab.reshape(DEPTH * 8, 6, dm), norm_g.reshape(1, 1, dm), glu_w.astype(BF16), glu_b[None, :],
      router_w.T)
    return x1, h2, jnp.transpose(lg, (1, 0, 2)).reshape(N_EXPERTS, bsz * n)


def kernel(x, c, ctx, c_ctx, mod_w, mod_b, norm1_g, norm2_g, ar_w_in, ar_w_out, lru_conv_w, lru_conv_b, lru_lam, lru_r_w, lru_r_b, lru_i_w, lru_i_b, attn_q_g, attn_k_g, attn_lam_q1, attn_lam_k1, attn_lam_q2, attn_lam_k2, attn_subln_g, s5_a_re, s5_a_im, s5_log_step, s5_b_re, s5_b_im, s5_c_re, s5_c_im, s5_d, s5_glu_w, s5_glu_b, router_w, router_bias, exp_w_gate, exp_w_up, exp_w_down, sh_w_gate, sh_w_up, sh_w_down):
    bsz, n, dm = x.shape
    assert dm == D_MODEL and ctx.shape[1] == CTX_LEN == TM and n % TM == 0 and bsz < 8
    assert mod_w.shape[0] == DEPTH == 2
    s = CTX_LEN + n
    ntb = s // TM
    modtab = _modulation(c, c_ctx, mod_w, mod_b)

    gate, xa, q, k, v = _inproj(x, ctx, modtab, norm1_g[0], ar_w_in[0], attn_q_g[0], attn_k_g[0])
    lru_args = (lru_conv_w[0], lru_conv_b[0], lru_lam[0], lru_r_w[0], lru_r_b[0], lru_i_w[0], lru_i_b[0])
    hf = _lru(xa, *lru_args, reverse=False)
    hr = _lru(xa, *lru_args, reverse=True)
    lam_init = 0.8 - 0.6 * math.exp(-0.3 * 0)
    lam_vecs = jnp.stack([attn_lam_q1[0], attn_lam_k1[0], attn_lam_q2[0], attn_lam_k2[0]], axis=0)
    yb = _attention(q, k, v, lam_vecs, attn_subln_g[0], lam_init)
    x1, h2, lgt = _outproj(hf, hr, gate, yb, x, ctx, modtab, norm2_g[0], ar_w_out[0], router_w[0])
    row0 = lambda i: jnp.where(i % ntb == 0, bsz, i // ntb)
    xall = _moe(h2.reshape(bsz * s, dm), lgt, x1.reshape(bsz * s, dm), modtab, 0, row0, router_bias[0],
                exp_w_gate[0], exp_w_up[0], exp_w_down[0], sh_w_gate[0], sh_w_up[0], sh_w_down[0])
    xall = xall.reshape(bsz, s, dm)

    xt = _s5_pack(xall, modtab, 1, norm1_g[1])
    weights = _s5_weights(s5_a_re[0], s5_a_im[0], s5_log_step[0], s5_b_re[0], s5_b_im[0], s5_c_re[0],
                          s5_c_im[0], s5_d[0])
    y = _s5(xt, bsz, weights)
    x1, h2, lgt = _glu(y, xall, modtab, 1, norm2_g[1], s5_glu_w[0], s5_glu_b[0], router_w[1])
    ntl = n // TM
    row1 = lambda i: i // ntl
    out = _moe(h2.reshape(bsz * n, dm), lgt, x1.reshape(bsz * n, dm), modtab, 1, row1, router_bias[1],
               exp_w_gate[1], exp_w_up[1], exp_w_down[1], sh_w_gate[1], sh_w_up[1], sh_w_down[1])
    return out.reshape(bsz, n, dm)
```

```python
import functools
import math

import jax
import jax.numpy as jnp
from jax import lax
from jax.experimental import pallas as pl
from jax.experimental.pallas import tpu as pltpu

F32, BF16, I32 = jnp.float32, jnp.bfloat16, jnp.int32
HIGHEST = lax.Precision.HIGHEST

D_MODEL = 1024
DEPTH = 2
GRID_W = 64
CTX_LEN = 256
EPS = 1e-6
LRU_WIDTH = 512
LRU_BLOCKS = 8
LRU_C = 8.0
ATT_HEADS = 4
ATT_DH = 64
ATT_DV = 128
ATT_QK_W = 512
ATT_WIDTH = 512
ROPE_BASE = 10000.0
EVEN_IN = 2 * LRU_WIDTH + 2 * ATT_QK_W + ATT_WIDTH
S5_GROUP = 16
S5_GROUPS = 64
S5_STATE = 64
S5_TC = 16
S5_GB = 4
N_EXPERTS = 64
TOP_K = 6
D_EXPERT = 256
ROUTED_SCALE = 2.5

TM = 256
SEG = 16
SLOT_CHUNK = 64
RMAX = -(-(TM * TOP_K + N_EXPERTS * (SEG - 1)) // SLOT_CHUNK) * SLOT_CHUNK
TR = 512
ATT_TK = 768
VMEM_LIMIT = 56 * 1024 * 1024


def _cparams(sem):
    return pltpu.CompilerParams(dimension_semantics=sem, vmem_limit_bytes=VMEM_LIMIT)


def _norm_mod(x, g, shift, scale):
    y = x * lax.rsqrt(jnp.mean(x * x, axis=-1, keepdims=True) + EPS) * g
    return y * (1.0 + scale) + shift


def _mod_kernel(c_ref, w_ref, b_ref, o_ref):
    c = c_ref[...]
    s = c * jax.nn.sigmoid(c)
    o_ref[0] = jnp.dot(s, w_ref[0], precision=HIGHEST, preferred_element_type=F32) + b_ref[0]


def _modulation(c, c_ctx, mod_w, mod_b):
    bsz, dm = c.shape
    cc = jnp.concatenate([c, c_ctx[None, :], jnp.zeros((8 - bsz - 1, dm), F32)], axis=0)
    out = pl.pallas_call(
        _mod_kernel,
        grid=(DEPTH, 6),
        in_specs=[pl.BlockSpec((8, dm), lambda l, j: (0, 0)),
                  pl.BlockSpec((1, dm, dm), lambda l, j: (l, 0, j)),
                  pl.BlockSpec((1, 1, dm), lambda l, j: (l, 0, j))],
        out_specs=pl.BlockSpec((1, 8, dm), lambda l, j: (l, 0, j)),
        out_shape=jax.ShapeDtypeStruct((DEPTH, 8, 6 * dm), F32),
        compiler_params=_cparams(("arbitrary", "arbitrary")),
        name="modulation",
    )(cc, mod_w, mod_b.reshape(DEPTH, 1, 6 * dm))
    return out.reshape(DEPTH * 8 * 6, 1, dm)


def _mod_spec(layer, part, row_fn):
    return pl.BlockSpec((1, 1, D_MODEL), lambda *ids: ((layer * 8 + row_fn(*ids)) * 6 + part, 0, 0))


def _rope_tables(n):
    rows = n // GRID_W
    r, col = jnp.meshgrid(jnp.arange(rows), jnp.arange(GRID_W), indexing="ij")
    pos = jnp.stack([r.reshape(-1), col.reshape(-1)], axis=-1).astype(F32)
    n_freq = ATT_DH // 4
    inv_freq = ROPE_BASE ** (-jnp.arange(n_freq, dtype=F32) / n_freq)
    ang = pos[:, :, None] * inv_freq
    cos, sin = jnp.cos(ang), jnp.sin(ang)
    zero = jnp.zeros_like(sin)
    cos64 = jnp.stack([cos, cos], axis=2).reshape(n, ATT_DH)
    sin_lo = jnp.stack([zero, sin], axis=2).reshape(n, ATT_DH)
    sin_hi = jnp.stack([-sin, zero], axis=2).reshape(n, ATT_DH)

    def full(tab, ctx_val):
        tab = jnp.concatenate([jnp.full((CTX_LEN, ATT_DH), ctx_val, F32), tab], axis=0)
        return jnp.concatenate([tab, tab], axis=1)

    return full(cos64, 1.0), full(sin_lo, 0.0), full(sin_hi, 0.0)


def _qk_post(t, gain, ones_bd, cos, sin_lo, sin_hi):
    ss = jnp.dot((t * t).astype(BF16), ones_bd, preferred_element_type=F32) * (1.0 / ATT_DH)
    tn = t * lax.rsqrt(ss + EPS) * gain
    w = tn.shape[1]
    return tn * cos + pltpu.roll(tn, 16, 1) * sin_lo + pltpu.roll(tn, w - 16, 1) * sin_hi


def _inproj_kernel(x_ref, c_ref, sh_ref, sc_ref, g_ref, w_ref, qg_ref, kg_ref, ones_ref,
                   cos_ref, slo_ref, shi_ref, gate_ref, xa_ref, q_ref, k_ref, v_ref):
    t = pl.program_id(1)
    x = jnp.where(t == 0, c_ref[0], x_ref[0])
    h = _norm_mod(x, g_ref[0], sh_ref[0], sc_ref[0])
    z = jnp.dot(h.astype(BF16), w_ref[...], preferred_element_type=F32)
    lw, qw = LRU_WIDTH, ATT_QK_W
    gate_ref[0] = z[:, 0:lw].astype(BF16)
    xa_ref[0] = z[:, lw:2 * lw].astype(BF16)
    tile4 = lambda a: jnp.concatenate([a, a, a, a], axis=1)
    cos, slo, shi = tile4(cos_ref[...]), tile4(slo_ref[...]), tile4(shi_ref[...])
    ones_bd = ones_ref[...]
    q = _qk_post(z[:, 2 * lw:2 * lw + qw], qg_ref[...], ones_bd, cos, slo, shi)
    q_ref[0] = (q * (ATT_DH ** -0.5 * math.log2(math.e))).astype(BF16)
    k = _qk_post(z[:, 2 * lw + qw:2 * lw + 2 * qw], kg_ref[...], ones_bd, cos, slo, shi)
    k_ref[0] = k.astype(BF16)
    v_ref[0] = z[:, 2 * lw + 2 * qw:].astype(BF16)


def _inproj(x, ctx, modtab, norm_g, w_in, q_g, k_g):
    bsz, n, dm = x.shape
    ntb = (CTX_LEN + n) // TM
    s = CTX_LEN + n
    row = lambda b, t: jnp.where(t == 0, bsz, b)
    cos, slo, shi = _rope_tables(n)
    ones_bd = jnp.kron(jnp.eye(ATT_QK_W // ATT_DH, dtype=F32), jnp.ones((ATT_DH, ATT_DH), F32)).astype(BF16)
    tile_g = lambda g: jnp.tile(g, ATT_QK_W // ATT_DH)[None, :]
    tab_spec = pl.BlockSpec((TM, 2 * ATT_DH), lambda b, t: (t, 0))
    out_spec = pl.BlockSpec((1, TM, LRU_WIDTH), lambda b, t: (b, t, 0))
    out_sds = jax.ShapeDtypeStruct((bsz, s, LRU_WIDTH), BF16)
    return pl.pallas_call(
        _inproj_kernel,
        grid=(bsz, ntb),
        in_specs=[pl.BlockSpec((1, TM, dm), lambda b, t: (b, jnp.maximum(t - 1, 0), 0)),
                  pl.BlockSpec((1, TM, dm), lambda b, t: (b, 0, 0)),
                  _mod_spec(0, 0, row), _mod_spec(0, 1, row),
                  pl.BlockSpec((1, 1, dm), lambda b, t: (0, 0, 0)),
                  pl.BlockSpec((dm, EVEN_IN), lambda b, t: (0, 0)),
                  pl.BlockSpec((1, ATT_QK_W), lambda b, t: (0, 0)),
                  pl.BlockSpec((1, ATT_QK_W), lambda b, t: (0, 0)),
                  pl.BlockSpec((ATT_QK_W, ATT_QK_W), lambda b, t: (0, 0)),
                  tab_spec, tab_spec, tab_spec],
        out_specs=[out_spec] * 5,
        out_shape=[out_sds] * 5,
        compiler_params=_cparams(("arbitrary", "arbitrary")),
        name="inproj",
    )(x, ctx, modtab, modtab, norm_g.reshape(1, 1, dm), w_in.astype(BF16), tile_g(q_g), tile_g(k_g), ones_bd,
      cos, slo, shi)


def _lru_kernel(xa_ref, xp_ref, xn_ref, cw_ref, cb_ref, lam_ref, w_ref, bias_ref, h_ref,
                a_scr, b_scr, c_scr, *, ntb, reverse):
    s = pl.program_id(1)
    ti = jnp.where(s == 0, 0, ntb - s) if reverse else s
    lw = LRU_WIDTH
    x = xa_ref[0].astype(F32)
    row = lax.broadcasted_iota(I32, (TM, lw), 0)
    has_prev = jnp.where(ti > 1, 1.0, 0.0)
    has_next = jnp.where((ti > 0) & (ti < ntb - 1), 1.0, 0.0)
    prev = xp_ref[0].astype(F32) * has_prev
    nxt = xn_ref[0].astype(F32) * has_next
    xm1 = jnp.where(row == 0, prev[7:8], pltpu.roll(x, 1, 0))
    xm2 = jnp.where(row == 0, prev[6:7], jnp.where(row == 1, prev[7:8], pltpu.roll(x, 2, 0)))
    xp1 = jnp.where(row == TM - 1, nxt[0:1], pltpu.roll(x, TM - 1, 0))
    cw = cw_ref[...]
    xc = cw[0:1] * xm2 + cw[1:2] * xm1 + cw[2:3] * x + cw[3:4] * xp1 + cb_ref[...]

    z = jnp.dot(xc.astype(BF16), w_ref[...], preferred_element_type=F32) + bias_ref[...]
    r = jax.nn.sigmoid(z[:, :lw])
    ig = jax.nn.sigmoid(z[:, lw:])
    neg_lam = -lam_ref[...]
    softplus = jnp.maximum(neg_lam, 0.0) + jnp.log1p(jnp.exp(-jnp.abs(neg_lam)))
    log_a = (-LRU_C) * r * softplus
    a = jnp.exp(log_a)
    b = jnp.sqrt(-jnp.tanh(log_a) * (a * a + 1.0)) * (ig * xc)

    r8 = row & 7
    for sft in (1, 2, 4):
        if reverse:
            a_s, b_s, m = pltpu.roll(a, TM - sft, 0), pltpu.roll(b, TM - sft, 0), r8 < 8 - sft
        else:
            a_s, b_s, m = pltpu.roll(a, sft, 0), pltpu.roll(b, sft, 0), r8 >= sft
        b = jnp.where(m, a * b_s + b, b)
        a = jnp.where(m, a * a_s, a)
    a_scr[...] = a
    b_scr[...] = b

    @pl.when(s == 0)
    def _():
        c_scr[...] = jnp.zeros_like(c_scr)

    ng = TM // 8

    def body(j, carry):
        g = (ng - 1 - j) if reverse else j
        off = pl.multiple_of(g * 8, 8)
        h = a_scr[pl.ds(off, 8), :] * carry + b_scr[pl.ds(off, 8), :]
        h_ref[0, pl.ds(off, 8), :] = h
        last = h[0:1] if reverse else h[7:8]
        return jnp.broadcast_to(last, (8, lw))

    c_scr[...] = lax.fori_loop(0, ng, body, c_scr[...], unroll=4)


def _lru(xa, conv_w, conv_b, lam, r_w, r_b, i_w, i_b, reverse):
    bsz, s, lw = xa.shape
    ntb = s // TM
    d = 1 if reverse else 0
    bd = lambda w: jax.scipy.linalg.block_diag(*[w[i] for i in range(LRU_BLOCKS)])
    w = jnp.concatenate([bd(r_w[d]), bd(i_w[d])], axis=1).astype(BF16)
    bias = jnp.concatenate([r_b[d], i_b[d]])[None, :]
    cw = jnp.concatenate([conv_w, jnp.zeros((4, lw), F32)], axis=0)
    if reverse:
        tile = lambda t: jnp.where(t == 0, 0, ntb - t)
    else:
        tile = lambda t: t
    nb8 = s // 8
    return pl.pallas_call(
        functools.partial(_lru_kernel, ntb=ntb, reverse=reverse),
        grid=(bsz, ntb),
        in_specs=[pl.BlockSpec((1, TM, lw), lambda b, t: (b, tile(t), 0)),
                  pl.BlockSpec((1, 8, lw), lambda b, t: (b, jnp.maximum(tile(t) * (TM // 8) - 1, 0), 0)),
                  pl.BlockSpec((1, 8, lw), lambda b, t: (b, jnp.minimum((tile(t) + 1) * (TM // 8), nb8 - 1), 0)),
                  pl.BlockSpec((8, lw), lambda b, t: (0, 0)),
                  pl.BlockSpec((1, lw), lambda b, t: (0, 0)),
                  pl.BlockSpec((1, lw), lambda b, t: (0, 0)),
                  pl.BlockSpec((lw, 2 * lw), lambda b, t: (0, 0)),
                  pl.BlockSpec((1, 2 * lw), lambda b, t: (0, 0))],
        out_specs=pl.BlockSpec((1, TM, lw), lambda b, t: (b, tile(t), 0)),
        out_shape=jax.ShapeDtypeStruct((bsz, s, lw), F32),
        scratch_shapes=[pltpu.VMEM((TM, lw), F32), pltpu.VMEM((TM, lw), F32), pltpu.VMEM((8, lw), F32)],
        compiler_params=_cparams(("arbitrary", "arbitrary")),
        name="lru_rev" if reverse else "lru_fwd",
    )(xa, xa, xa, cw, conv_b[None, :], lam[d][None, :], w, bias)


def _attn_kernel(lamv_ref, q_ref, k_ref, v_ref, sg_ref, o_ref, m_scr, l_scr, acc_scr, al_scr, s_scr, p_scr, *,
                 nkv, lam_init):
    t = pl.program_id(2)
    q = q_ref[0]
    dh = ATT_DH
    dv = ATT_DV
    m_scr[...] = jnp.full_like(m_scr, -jnp.inf)
    l_scr[...] = jnp.zeros_like(l_scr)
    acc_scr[...] = jnp.zeros_like(acc_scr)

    def scores(buf, off, size):
        kc = k_ref[0, pl.ds(off, size), :]
        for mi in range(2):
            s_scr[buf, mi, :, 0:size] = lax.dot_general(
                q[:, mi * dh:(mi + 1) * dh], kc[:, mi * dh:(mi + 1) * dh], (((1,), (1,)), ((), ())),
                preferred_element_type=F32)

    def softmax(buf, size):
        nlb = size // 128
        groups = [(slice(r * 16, (r + 1) * 16), mi) for r in range(TM // 16) for mi in range(2)]
        for rows, mi in groups:
            sc = s_scr[buf, mi, rows, 0:size]
            mx = functools.reduce(jnp.maximum, [sc[:, i * 128:(i + 1) * 128] for i in range(nlb)])
            m_old = m_scr[mi, rows, :]
            m_new = jnp.maximum(m_old, jnp.max(mx, axis=1, keepdims=True))
            m_scr[mi, rows, :] = m_new
            al_scr[mi, rows, :] = jnp.exp2(m_old - m_new)
        for rows, mi in groups:
            m_new = m_scr[mi, rows, :]
            ps = [jnp.exp2(s_scr[buf, mi, rows, i * 128:(i + 1) * 128] - m_new) for i in range(nlb)]
            l_scr[mi, rows, :] = al_scr[mi, rows, :] * l_scr[mi, rows, :] + functools.reduce(jnp.add, ps)
            p_scr[buf, mi, rows, 0:size] = jnp.concatenate(ps, axis=1).astype(BF16)

    def values(buf, off, size):
        vc = v_ref[0, pl.ds(off, size), :]
        pv = jnp.dot(p_scr[buf, :, :, 0:size].reshape(2 * TM, size), vc, preferred_element_type=F32)
        acc_scr[...] = al_scr[...] * acc_scr[...] + pv.reshape(2, TM, dv)

    @pl.when(t == 0)
    def _():
        scores(0, 0, CTX_LEN)
        softmax(0, CTX_LEN)
        values(0, 0, CTX_LEN)

    @pl.when(t > 0)
    def _():
        tk = ATT_TK
        nchunk = nkv // tk
        scores(0, 0, tk)
        for j in range(nchunk):
            if j + 1 < nchunk:
                scores((j + 1) % 2, (j + 1) * tk, tk)
            softmax(j % 2, tk)
            values(j % 2, j * tk, tk)

    lv = lamv_ref[...]
    lam = (jnp.exp(jnp.sum(lv[0:1] * lv[1:2], axis=1, keepdims=True))
           - jnp.exp(jnp.sum(lv[2:3] * lv[3:4], axis=1, keepdims=True)) + lam_init)
    l0 = jnp.sum(l_scr[0], axis=1, keepdims=True)
    l1 = jnp.sum(l_scr[1], axis=1, keepdims=True)
    o = acc_scr[0] / l0 - lam * (acc_scr[1] / l1)
    y = o * lax.rsqrt(jnp.mean(o * o, axis=-1, keepdims=True) + EPS) * sg_ref[...] * (1.0 - lam_init)
    o_ref[0] = y.astype(BF16)


def _attention(q, k, v, lam_vecs, subln_g, lam_init):
    bsz, s, _ = q.shape
    ntb = s // TM
    assert s % ATT_TK == 0
    dv = ATT_DV
    lamv = jnp.concatenate([jnp.pad(lam_vecs, ((0, 0), (0, dv - ATT_DH))), jnp.zeros((4, dv), F32)], axis=0)
    return pl.pallas_call(
        functools.partial(_attn_kernel, nkv=s, lam_init=lam_init),
        grid=(bsz, ATT_HEADS, ntb),
        in_specs=[pl.BlockSpec((8, dv), lambda b, h, t: (0, 0)),
                  pl.BlockSpec((1, TM, dv), lambda b, h, t: (b, t, h)),
                  pl.BlockSpec((1, s, dv), lambda b, h, t: (b, 0, h)),
                  pl.BlockSpec((1, s, dv), lambda b, h, t: (b, 0, h)),
                  pl.BlockSpec((1, dv), lambda b, h, t: (0, 0))],
        out_specs=pl.BlockSpec((1, TM, dv), lambda b, h, t: (b, t, h)),
        out_shape=jax.ShapeDtypeStruct((bsz, s, ATT_WIDTH), BF16),
        scratch_shapes=[pltpu.VMEM((2, TM, dv), F32)] * 4
        + [pltpu.VMEM((2, 2, TM, ATT_TK), F32), pltpu.VMEM((2, 2, TM, ATT_TK), BF16)],
        compiler_params=_cparams(("arbitrary", "arbitrary", "arbitrary")),
        name="diff_attention",
    )(lamv, q, k, v, subln_g[None, :])


def _post_mixer(y, x, g1, sh2, sc2, n2g, rwt):
    x1 = x + g1 * y
    h2 = _norm_mod(x1, n2g, sh2, sc2)
    logits = lax.dot_general(rwt, h2, (((1,), (1,)), ((), ())), precision=HIGHEST, preferred_element_type=F32)
    return x1, h2.astype(BF16), logits


def _outproj_kernel(hf_ref, hr_ref, gate_ref, yb_ref, x_ref, c_ref, g1_ref, sh_ref, sc_ref, n2g_ref, w_ref,
                    rwt_ref, x1_ref, h2_ref, lg_ref):
    t = pl.program_id(1)
    x = jnp.where(t == 0, c_ref[0], x_ref[0])
    ya = ((hf_ref[0] + hr_ref[0]) * jax.nn.gelu(gate_ref[0].astype(F32))).astype(BF16)
    lw = LRU_WIDTH
    y = (jnp.dot(ya, w_ref[0:lw, :], preferred_element_type=F32)
         + jnp.dot(yb_ref[0], w_ref[lw:, :], preferred_element_type=F32))
    x1_ref[0], h2_ref[0], lg_ref[...] = _post_mixer(y, x, g1_ref[0], sh_ref[0], sc_ref[0], n2g_ref[0],
                                                    rwt_ref[...])


def _outproj(hf, hr, gate, yb, x, ctx, modtab, norm_g, w_out, router_w):
    bsz, s, lw = hf.shape
    dm = D_MODEL
    ntb = s // TM
    row = lambda b, t: jnp.where(t == 0, bsz, b)
    half = pl.BlockSpec((1, TM, lw), lambda b, t: (b, t, 0))
    return pl.pallas_call(
        _outproj_kernel,
        grid=(bsz, ntb),
        in_specs=[half, half, half, half,
                  pl.BlockSpec((1, TM, dm), lambda b, t: (b, jnp.maximum(t - 1, 0), 0)),
                  pl.BlockSpec((1, TM, dm), lambda b, t: (b, 0, 0)),
                  _mod_spec(0, 2, row), _mod_spec(0, 3, row), _mod_spec(0, 4, row),
                  pl.BlockSpec((1, 1, dm), lambda b, t: (0, 0, 0)),
                  pl.BlockSpec((2 * lw, dm), lambda b, t: (0, 0)),
                  pl.BlockSpec((N_EXPERTS, dm), lambda b, t: (0, 0))],
        out_specs=[pl.BlockSpec((1, TM, dm), lambda b, t: (b, t, 0)),
                   pl.BlockSpec((1, TM, dm), lambda b, t: (b, t, 0)),
                   pl.BlockSpec((N_EXPERTS, TM), lambda b, t: (0, b * ntb + t))],
        out_shape=[jax.ShapeDtypeStruct((bsz, s, dm), F32),
                   jax.ShapeDtypeStruct((bsz, s, dm), BF16),
                   jax.ShapeDtypeStruct((N_EXPERTS, bsz * s), F32)],
        compiler_params=_cparams(("arbitrary", "arbitrary")),
        name="outproj",
    )(hf, hr, gate, yb, x, ctx, modtab, modtab, modtab, norm_g.reshape(1, 1, dm), w_out.astype(BF16),
      router_w.T)


def _route_kernel(lg_ref, bias_ref, tri_ref, low_ref, pos_ref, gate_ref, cnt_ref):
    ne = N_EXPERTS
    aff = jax.nn.sigmoid(lg_ref[...])
    work = aff + bias_ref[:, 0:1]
    eidx = lax.broadcasted_iota(I32, (ne, TM), 0)
    sels = []
    for _ in range(TOP_K):
        mx = jnp.max(work, axis=0, keepdims=True)
        am = jnp.min(jnp.where(work == mx, eidx, ne), axis=0, keepdims=True)
        sk = eidx == am
        sels.append(sk)
        work = jnp.where(sk, -jnp.inf, work)
    sel = sels[0]
    for sk in sels[1:]:
        sel = sel | sk
    self = jnp.where(sel, 1.0, 0.0)
    s_sel = aff * self
    gates = s_sel / jnp.sum(s_sel, axis=0, keepdims=True) * ROUTED_SCALE
    rank = jnp.dot(self.astype(BF16), tri_ref[...], preferred_element_type=F32)
    cnt = jnp.sum(self, axis=1, keepdims=True)
    cnt_seg = jnp.floor((cnt + (SEG - 1)) * (1.0 / SEG)) * SEG
    cnt_b = jnp.broadcast_to(cnt_seg, (ne, 128))
    seg_off = jnp.dot(low_ref[...], cnt_b.astype(BF16), preferred_element_type=F32)
    lpos = seg_off[:, 0:1] + rank
    pos_rows, gate_rows = [], []
    for sk in sels:
        pos_rows.append(jnp.sum(jnp.where(sk, lpos, 0.0), axis=0, keepdims=True))
        gate_rows.append(jnp.sum(jnp.where(sk, gates, 0.0), axis=0, keepdims=True))
    for _ in range(8 - TOP_K):
        pos_rows.append(jnp.full((1, TM), -1.0, F32))
        gate_rows.append(jnp.zeros((1, TM), F32))
    pos_ref[...] = jnp.concatenate(pos_rows, axis=0).astype(I32)
    gate_ref[...] = jnp.concatenate(gate_rows, axis=0)
    cnt_ref[0] = cnt_b.astype(I32)


def _route(lgt, router_bias):
    ne, t = lgt.shape
    nt = t // TM
    tri = jnp.triu(jnp.ones((TM, TM), F32), k=1).astype(BF16)
    low = jnp.tril(jnp.ones((ne, ne), F32), k=-1).astype(BF16)
    bias = jnp.broadcast_to(router_bias[:, None], (ne, 128))
    pos, gate, cnt = pl.pallas_call(
        _route_kernel,
        grid=(nt,),
        in_specs=[pl.BlockSpec((ne, TM), lambda i: (0, i)),
                  pl.BlockSpec((ne, 128), lambda i: (0, 0)),
                  pl.BlockSpec((TM, TM), lambda i: (0, 0)),
                  pl.BlockSpec((ne, ne), lambda i: (0, 0))],
        out_specs=[pl.BlockSpec((8, TM), lambda i: (0, i)),
                   pl.BlockSpec((8, TM), lambda i: (0, i)),
                   pl.BlockSpec((1, ne, 128), lambda i: (i, 0, 0))],
        out_shape=[jax.ShapeDtypeStruct((8, t), I32),
                   jax.ShapeDtypeStruct((8, t), F32),
                   jax.ShapeDtypeStruct((nt, ne, 128), I32)],
        compiler_params=_cparams(("arbitrary",)),
        name="route",
    )(lgt, bias, tri, low)
    return pos, gate, cnt[:, :, 0]


def _moe_layout(cnt_seg, nblk_max):
    tot = jnp.sum(cnt_seg, axis=0)
    region = (tot + TR - 1) // TR * TR
    region_end = jnp.cumsum(region)
    goff = (region_end - region)[None, :] + jnp.cumsum(cnt_seg, axis=0) - cnt_seg
    packed = ((goff // SEG) << 5) | (cnt_seg // SEG)
    nblk = region_end[-1] // TR
    blk = jnp.arange(nblk_max, dtype=I32)
    blk_e = jnp.sum((region_end[None, :] // TR <= blk[:, None]).astype(I32), axis=1)
    blk_e = jnp.minimum(blk_e, N_EXPERTS - 1).astype(I32)
    tile_info = jnp.sum(cnt_seg, axis=1) | (jnp.any(cnt_seg >= 64, axis=1).astype(I32) << 16)
    tails = (((region_end - region + tot) // SEG) << 5) | ((region - tot) // SEG)
    tails = jnp.concatenate([tails, nblk[None]]).astype(I32)
    return packed.reshape(-1).astype(I32), tile_info.astype(I32), tails, blk_e, nblk.astype(I32).reshape(1)


def _segment_copy(stage, hbm, sem, to_sorted, lo, go, size):
    lo, go = pl.multiple_of(lo, SEG), pl.multiple_of(go, SEG)
    a, b = stage.at[pl.ds(lo, size)], hbm.at[pl.ds(go, size)]
    return pltpu.make_async_copy(a, b, sem) if to_sorted else pltpu.make_async_copy(b, a, sem)


def _segment_unpack(pk):
    return (pk & 31) * SEG, (pk >> 5) * SEG


def _segment_starts(pk_ref, tile, stage, hbm, sem, to_sorted, valid=None):
    loff = [0]

    def step(e):
        cnt, goff = _segment_unpack(pk_ref[tile * N_EXPERTS + e])
        done = (cnt >> 6) << 6
        for size in (32, 16):
            bit = (cnt & size) != 0
            if valid is not None:
                bit = bit & valid

            @pl.when(bit)
            def _(lo=loff[0] + done, go=goff + done, size=size):
                _segment_copy(stage, hbm, sem, to_sorted, lo, go, size).start()

            done = done + jnp.where(bit, size, 0)
        loff[0] = loff[0] + cnt

    return [functools.partial(step, e) for e in range(N_EXPERTS)]


def _interleave(*step_lists):
    total = max(len(steps) for steps in step_lists)
    done = [0] * len(step_lists)
    for t in range(1, total + 1):
        for k, steps in enumerate(step_lists):
            upto = len(steps) * t // total
            for step in steps[done[k]:upto]:
                step()
            done[k] = upto


def _segment_starts_long(pk_ref, tile, info, stage, hbm, sem, to_sorted):
    @pl.when((info >> 16) != 0)
    def _():
        def expert(e, loff):
            cnt, goff = _segment_unpack(pk_ref[tile * N_EXPERTS + e])

            def chunk(j, carry):
                _segment_copy(stage, hbm, sem, to_sorted, loff + j * 64, goff + j * 64, 64).start()
                return carry

            lax.fori_loop(0, cnt >> 6, chunk, 0)
            return loff + cnt

        lax.fori_loop(0, N_EXPERTS, expert, 0)


def _segment_wait(total, stage, hbm, sem, to_sorted):
    size = 1 << (RMAX.bit_length() - 1)
    while size >= SEG:
        @pl.when((total & size) != 0)
        def _(size=size):
            _segment_copy(stage, hbm, sem, to_sorted, 0, 0, size).wait()

        size //= 2


def _fill_slot_matrix(dst_ref, pos, weight_rows=None):
    ch = SLOT_CHUNK
    riota = lax.broadcasted_iota(I32, (ch, TM), 0)

    def step(c):
        local = pos - c * ch
        out = jnp.zeros((ch, TM), F32)
        for k in range(TOP_K):
            w = 1.0 if weight_rows is None else weight_rows[k:k + 1]
            out = jnp.where(riota == local[k:k + 1], w, out)
        dst_ref[c * ch:(c + 1) * ch, :] = out.astype(BF16)

    return [functools.partial(step, c) for c in range(RMAX // ch)]


def _zero_fill(tail_ref, zeros, xs_ref, sem, fn):
    def expert(e, carry):
        pk = tail_ref[e]
        off = (pk >> 5) * SEG

        def chunk(j, c):
            dst = xs_ref.at[pl.ds(pl.multiple_of(off + j * SEG, SEG), SEG)]
            fn(pltpu.make_async_copy(zeros.at[pl.ds(0, SEG)], dst, sem))
            return c

        return lax.fori_loop(0, pk & 31, chunk, carry)

    lax.fori_loop(0, N_EXPERTS, expert, 0)

    def block(j, carry):
        fn(pltpu.make_async_copy(zeros, xs_ref.at[pl.ds(pl.multiple_of(j * TR, TR), TR)], sem))
        return carry

    lax.fori_loop(tail_ref[N_EXPERTS], xs_ref.shape[0] // TR, block, 0)


def _dispatch_kernel(pk_ref, tot_ref, tail_ref, pos_ref, h_ref, xs_ref, stage, zeros, slots, sem, zsem):
    i = pl.program_id(0)
    slot = i % 2

    @pl.when(i == 0)
    def _():
        zeros[...] = jnp.zeros_like(zeros)
        _zero_fill(tail_ref, zeros, xs_ref, zsem, lambda c: c.start())

    for step in _fill_slot_matrix(slots, pos_ref[...]):
        step()
    stage[slot] = jnp.dot(slots[...], h_ref[...], preferred_element_type=F32).astype(BF16)
    prev_rows = jnp.where(i > 0, tot_ref[jnp.maximum(i - 1, 0)] & 0xFFFF, 0)
    _segment_wait(prev_rows, stage.at[1 - slot], xs_ref, sem.at[1 - slot], True)
    this = (stage.at[slot], xs_ref, sem.at[slot], True)
    for step in _segment_starts(pk_ref, i, *this):
        step()
    _segment_starts_long(pk_ref, i, tot_ref[i], *this)

    @pl.when(i == pl.num_programs(0) - 1)
    def _():
        _segment_wait(tot_ref[i] & 0xFFFF, *this)
        _zero_fill(tail_ref, zeros, xs_ref, zsem, lambda c: c.wait())


def _dispatch(packed, tile_rows, tails, pos, h2, nrows):
    t, dm = h2.shape
    nt = t // TM
    return pl.pallas_call(
        _dispatch_kernel,
        grid_spec=pltpu.PrefetchScalarGridSpec(
            num_scalar_prefetch=3,
            grid=(nt,),
            in_specs=[pl.BlockSpec((8, TM), lambda i, pk, tot, tail: (0, i)),
                      pl.BlockSpec((TM, dm), lambda i, pk, tot, tail: (i, 0))],
            out_specs=pl.BlockSpec(memory_space=pl.ANY),
            scratch_shapes=[pltpu.VMEM((2, RMAX, dm), BF16), pltpu.VMEM((TR, dm), BF16),
                            pltpu.VMEM((RMAX, TM), BF16),
                            pltpu.SemaphoreType.DMA((2,)), pltpu.SemaphoreType.DMA]),
        out_shape=jax.ShapeDtypeStruct((nrows, dm), BF16),
        compiler_params=_cparams(("arbitrary",)),
        name="moe_dispatch",
    )(packed, tile_rows, tails, pos, h2)


def _expert_kernel(be_ref, nb_ref, x_ref, wg_ref, wu_ref, wd_ref, y_ref, wg_s, wu_s, wd_s):
    j = pl.program_id(0)
    changed = be_ref[j] != be_ref[jnp.maximum(j - 1, 0)]

    @pl.when((j == 0) | changed)
    def _():
        wg_s[...] = wg_ref[0].astype(BF16)
        wu_s[...] = wu_ref[0].astype(BF16)
        wd_s[...] = wd_ref[0].astype(BF16)

    @pl.when(j < nb_ref[0])
    def _():
        x = x_ref[...]
        g = jnp.dot(x, wg_s[...], preferred_element_type=F32)
        u = jnp.dot(x, wu_s[...], preferred_element_type=F32)
        a = (g * jax.nn.sigmoid(g) * u).astype(BF16)
        y_ref[...] = jnp.dot(a, wd_s[...], preferred_element_type=F32).astype(BF16)


def _experts(blk_e, nblk, xs, layer, w_gate, w_up, w_down):
    nrows, dm = xs.shape
    nblk_max = nrows // TR
    de = D_EXPERT
    row_blk = lambda j, be, nb: (jnp.minimum(j, nb[0] - 1), 0)
    return pl.pallas_call(
        _expert_kernel,
        grid_spec=pltpu.PrefetchScalarGridSpec(
            num_scalar_prefetch=2,
            grid=(nblk_max,),
            in_specs=[pl.BlockSpec((TR, dm), row_blk),
                      pl.BlockSpec((None, 1, dm, de), lambda j, be, nb: (layer, be[j], 0, 0)),
                      pl.BlockSpec((None, 1, dm, de), lambda j, be, nb: (layer, be[j], 0, 0)),
                      pl.BlockSpec((None, 1, de, dm), lambda j, be, nb: (layer, be[j], 0, 0))],
            out_specs=pl.BlockSpec((TR, dm), row_blk),
            scratch_shapes=[pltpu.VMEM((dm, de), BF16), pltpu.VMEM((dm, de), BF16), pltpu.VMEM((de, dm), BF16)]),
        out_shape=jax.ShapeDtypeStruct((nrows, dm), BF16),
        input_output_aliases={2: 0},
        compiler_params=_cparams(("arbitrary",)),
        name="moe_experts",
    )(blk_e, nblk, xs, w_gate, w_up, w_down)


def _combine_kernel(pk_ref, tot_ref, pos_ref, gate_ref, h_ref, x1_ref, g2_ref, wsg_ref, wsu_ref, wsd_ref, ys_ref,
                    o_ref, stage, gates, sem):
    i = pl.program_id(0)
    slot = i % 2
    last = pl.num_programs(0) - 1

    @pl.when(i == 0)
    def _():
        stage[...] = jnp.zeros_like(stage)
        for step in _segment_starts(pk_ref, 0, stage.at[0], ys_ref, sem.at[0], False):
            step()
        _segment_starts_long(pk_ref, 0, tot_ref[0], stage.at[0], ys_ref, sem.at[0], False)

    nxt = jnp.minimum(i + 1, last)
    _interleave(_fill_slot_matrix(gates, pos_ref[...], gate_ref[...]),
                _segment_starts(pk_ref, nxt, stage.at[1 - slot], ys_ref, sem.at[1 - slot], False))
    h = h_ref[...]
    g = jnp.dot(h, wsg_ref[...], preferred_element_type=F32)
    u = jnp.dot(h, wsu_ref[...], preferred_element_type=F32)
    shared = jnp.dot((g * jax.nn.sigmoid(g) * u).astype(BF16), wsd_ref[...], preferred_element_type=F32)
    _segment_wait(tot_ref[i] & 0xFFFF, stage.at[slot], ys_ref, sem.at[slot], False)
    routed = lax.dot_general(gates[...], stage[slot], (((0,), (0,)), ((), ())), preferred_element_type=F32)
    o_ref[...] = x1_ref[...] + g2_ref[0] * (routed + shared)
    _segment_starts_long(pk_ref, nxt, tot_ref[nxt], stage.at[1 - slot], ys_ref, sem.at[1 - slot], False)

    @pl.when(i == last)
    def _():
        _segment_wait(tot_ref[i] & 0xFFFF, stage.at[1 - slot], ys_ref, sem.at[1 - slot], False)


def _combine(packed, tile_rows, pos, gate, h2, x1, modtab, layer, row_fn, ys, ws_gate, ws_up, ws_down):
    t, dm = h2.shape
    nt = t // TM
    de = D_EXPERT
    const = lambda shape: pl.BlockSpec(shape, lambda i, pk, tot: (0,) * len(shape))
    return pl.pallas_call(
        _combine_kernel,
        grid_spec=pltpu.PrefetchScalarGridSpec(
            num_scalar_prefetch=2,
            grid=(nt,),
            in_specs=[pl.BlockSpec((8, TM), lambda i, pk, tot: (0, i)),
                      pl.BlockSpec((8, TM), lambda i, pk, tot: (0, i)),
                      pl.BlockSpec((TM, dm), lambda i, pk, tot: (i, 0)),
                      pl.BlockSpec((TM, dm), lambda i, pk, tot: (i, 0)),
                      pl.BlockSpec((1, 1, dm), lambda i, pk, tot: ((layer * 8 + row_fn(i)) * 6 + 5, 0, 0)),
                      const((dm, de)), const((dm, de)), const((de, dm)),
                      pl.BlockSpec(memory_space=pl.ANY)],
            out_specs=pl.BlockSpec((TM, dm), lambda i, pk, tot: (i, 0)),
            scratch_shapes=[pltpu.VMEM((2, RMAX, dm), BF16), pltpu.VMEM((RMAX, TM), BF16),
                            pltpu.SemaphoreType.DMA((2,))]),
        out_shape=jax.ShapeDtypeStruct((t, dm), F32),
        compiler_params=_cparams(("arbitrary",)),
        name="moe_combine",
    )(packed, tile_rows, pos, gate, h2, x1, modtab, ws_gate.astype(BF16), ws_up.astype(BF16),
      ws_down.astype(BF16), ys)


def _moe(h2, lgt, x1, modtab, layer, row_fn, router_bias, w_gate, w_up, w_down, ws_gate, ws_up, ws_down):
    t = h2.shape[0]
    nt = t // TM
    max_rows = t * TOP_K + nt * N_EXPERTS * (SEG - 1) + N_EXPERTS * (TR - 1)
    nblk_max = -(-max_rows // TR)
    pos, gate, cnt_seg = _route(lgt, router_bias)
    packed, tile_rows, tails, blk_e, nblk = _moe_layout(cnt_seg, nblk_max)
    xs = _dispatch(packed, tile_rows, tails, pos, h2, nblk_max * TR)
    ys = _experts(blk_e, nblk, xs, layer, w_gate, w_up, w_down)
    return _combine(packed, tile_rows, pos, gate, h2, x1, modtab, layer, row_fn, ys, ws_gate, ws_up, ws_down)


S5_LANE_VREGS = D_MODEL // 128


def _segment_transpose(p):
    row = lax.broadcasted_iota(I32, (8, 128), 0)
    seg = lax.broadcasted_iota(I32, (8, 128), 1) // S5_GROUP
    for s in (4, 2, 1):
        m_up = ((row & s) == 0) & ((seg & s) != 0)
        m_dn = ((row & s) != 0) & ((seg & s) == 0)

        def swap(x):
            if s == 4:
                return jnp.where(m_up | m_dn, pltpu.roll(pltpu.roll(x, 4, 0), 4 * S5_GROUP, 1), x)
            up = pltpu.roll(pltpu.roll(x, 8 - s, 0), S5_GROUP * s, 1)
            dn = pltpu.roll(pltpu.roll(x, s, 0), 128 - S5_GROUP * s, 1)
            return jnp.where(m_up, up, jnp.where(m_dn, dn, x))

        p = [[swap(x) for x in half] for half in p]
    return [[p[1 - h][j - 1 + 2 * h] if (j % 2) != h else p[h][j] for j in range(S5_LANE_VREGS)] for h in range(2)]


def _s5_pack_kernel(x_ref, mod_ref, g_ref, xt_ref, u_scr, slab):
    t = pl.program_id(0)
    bsz = x_ref.shape[0]
    nchunk = TM // S5_TC
    for b in range(bsz):
        shift = jnp.where(t == 0, mod_ref[bsz, 0:1, :], mod_ref[b, 0:1, :])
        scale = jnp.where(t == 0, mod_ref[bsz, 1:2, :], mod_ref[b, 1:2, :])
        u_scr[b] = _norm_mod(x_ref[b], g_ref[0], shift, scale)

    def chunk(c, carry):
        r0 = pl.multiple_of(c * S5_TC, S5_TC)
        for b in range(bsz):
            p = [[u_scr[b, pl.ds(r0 + 8 * h, 8), 128 * j:128 * j + 128] for j in range(S5_LANE_VREGS)]
                 for h in range(2)]
            q = _segment_transpose(p)
            s0 = pl.multiple_of((c * bsz + b) * S5_TC, S5_TC)
            for h in range(2):
                for j in range(S5_LANE_VREGS):
                    slab[j, pl.ds(s0 + 8 * h, 8), :] = q[h][j]
        return carry

    lax.fori_loop(0, nchunk, chunk, 0)
    for j in range(S5_LANE_VREGS):
        for gl in range(S5_TC):
            g = (j // 2) * S5_TC + gl
            rows = slab[j, pl.ds(gl, nchunk * bsz, stride=S5_TC), :]
            xt_ref[g, :, 128 * (j % 2):128 * (j % 2) + 128] = rows.astype(BF16)


def _s5_pack(xall, modtab, layer, norm_g):
    bsz, s, dm = xall.shape
    ntb = s // TM
    rows = TM // S5_TC * bsz
    return pl.pallas_call(
        _s5_pack_kernel,
        grid=(ntb,),
        in_specs=[pl.BlockSpec((bsz, TM, dm), lambda t: (0, t, 0)),
                  pl.BlockSpec((8, 6, dm), lambda t: (layer, 0, 0)),
                  pl.BlockSpec((1, 1, dm), lambda t: (0, 0, 0))],
        out_specs=pl.BlockSpec((S5_GROUPS, rows, S5_TC * S5_GROUP), lambda t: (0, t, 0)),
        out_shape=jax.ShapeDtypeStruct((S5_GROUPS, ntb * rows, S5_TC * S5_GROUP), BF16),
        scratch_shapes=[pltpu.VMEM((bsz, TM, dm), F32), pltpu.VMEM((S5_LANE_VREGS, TM * bsz, 128), F32)],
        compiler_params=_cparams(("arbitrary",)),
        name="s5_pack",
    )(xall, modtab.reshape(DEPTH * 8, 6, dm), norm_g.reshape(1, 1, dm))


def _cmul(x, y):
    return x[0] * y[0] - x[1] * y[1], x[0] * y[1] + x[1] * y[0]


def _s5_weights(a_re, a_im, log_step, b_re, b_im, c_re, c_im, d_skip):
    tc, g, p, ch = S5_TC, S5_GROUPS, S5_STATE, S5_GROUP
    lam = (jnp.minimum(a_re, -1e-4), a_im)
    step = jnp.exp(log_step)
    mag = jnp.exp(lam[0] * step)
    lam_bar = (mag * jnp.cos(lam[1] * step), mag * jnp.sin(lam[1] * step))
    inv = 1.0 / (lam[0] * lam[0] + lam[1] * lam[1])
    coef = _cmul((lam_bar[0] - 1.0, lam_bar[1]), (lam[0] * inv, -lam[1] * inv))
    b_bar = _cmul((coef[0][..., None], coef[1][..., None]), (b_re, b_im))
    pw = [(jnp.ones_like(mag), jnp.zeros_like(mag))]
    for _ in range(tc):
        pw.append(_cmul(pw[-1], lam_bar))
    pw = (jnp.stack([q[0] for q in pw], axis=1), jnp.stack([q[1] for q in pw], axis=1))
    at = lambda d, idx: (pw[0][d, idx], pw[1][d, idx])
    lead = lambda z: jnp.moveaxis(z, -1, 0)
    cp = _cmul((lead(c_re)[:, :, :, None, :], lead(c_im)[:, :, :, None, :]),
               (jnp.transpose(pw[0][:, :tc], (3, 0, 2, 1))[..., None],
                jnp.transpose(pw[1][:, :tc], (3, 0, 2, 1))[..., None]))
    cp = tuple(z.reshape(p, 2, g, 1, tc * ch) for z in cp)
    bb = tuple(jnp.transpose(z, (2, 0, 1, 3))[..., None] for z in b_bar)
    kern = jnp.sum(bb[0] * cp[0] - bb[1] * cp[1], axis=0)
    rev = jnp.flip(kern[1].reshape(g, ch, tc, ch), axis=2).reshape(g, ch, tc * ch)
    zeros = lambda width: jnp.zeros((g, ch, width), F32)
    rows = []
    for s_ in range(tc):
        fwd_row = jnp.concatenate([zeros(s_ * ch), kern[0][..., :(tc - s_) * ch]], axis=-1)
        rev_row = jnp.concatenate([rev[..., (tc - 1 - s_) * ch:], zeros((tc - 1 - s_) * ch)], axis=-1)
        rows.append(fwd_row + rev_row)
    m = jnp.stack(rows, axis=1).reshape(g, tc * ch, tc * ch)
    skip = jnp.tile(d_skip.reshape(g, 1, ch), (1, tc, 1)).reshape(g, tc * ch)
    m = m + jnp.eye(tc * ch, dtype=F32)[None] * skip[:, :, None]
    steps = jnp.arange(tc)
    lift = lambda z: (z[0][..., None], z[1][..., None])
    e_f = _cmul(lift(at(0, tc - 1 - steps)), (b_bar[0][0][None], b_bar[1][0][None]))
    e_r = _cmul(lift(at(1, steps)), (b_bar[0][1][None], b_bar[1][1][None]))
    w_in = jnp.stack([e_f[0], e_r[0], e_f[1], e_r[1]], axis=0)
    w_in = jnp.transpose(w_in, (2, 1, 4, 0, 3)).reshape(g, tc * ch, 4 * p)
    mid = lambda z: (z[0][:, :, None, :], z[1][:, :, None, :])
    g_f = _cmul((c_re[0][None], c_im[0][None]), mid(at(0, 1 + steps)))
    g_r = _cmul((c_re[1][None], c_im[1][None]), mid(at(1, tc - steps)))
    w_re = jnp.stack([g_f[0], g_r[0]], axis=0)
    w_im = -jnp.stack([g_f[1], g_r[1]], axis=0)
    to_rows = lambda w: jnp.transpose(w, (2, 0, 4, 1, 3)).reshape(g, 2 * p, tc * ch)
    a1 = (jnp.concatenate([pw[0][0, tc], pw[0][1, tc]], axis=-1),
          jnp.concatenate([pw[1][0, tc], pw[1][1, tc]], axis=-1))
    a2 = _cmul(a1, a1)
    second = _s5_second_rows(8)
    par = tuple(jnp.where(second[None], a2[k][:, None, :], a1[k][:, None, :]) for k in range(2))
    one = tuple(jnp.broadcast_to(a1[k][:, None, :], (g, 8, 2 * p)) for k in range(2))
    a_rows = jnp.concatenate([jnp.concatenate(par, axis=-1), jnp.concatenate(one, axis=-1)], axis=1)
    return m.astype(BF16), w_in.astype(BF16), to_rows(w_re).astype(BF16), to_rows(w_im).astype(BF16), a_rows


def _s5_second_rows(nrows):
    row = lax.broadcasted_iota(I32, (nrows, 2 * S5_STATE), 0)
    lane = lax.broadcasted_iota(I32, (nrows, 2 * S5_STATE), 1)
    return (lane < S5_STATE) != ((row & 7) < 4)


def _s5_kernel(x_ref, m_ref, win_ref, wre_ref, wim_ref, a_ref, y_ref, ure_scr, uim_scr, sre_scr, sim_scr, *,
               nblock, nctx):
    p2 = 2 * S5_STATE
    nrows = nblock * 8
    second = _s5_second_rows(nrows)
    fwd = lax.broadcasted_iota(I32, (nrows, p2), 1) < S5_STATE
    for g in range(S5_GB):
        v = jnp.dot(x_ref[g], win_ref[g], preferred_element_type=F32)
        vre, vim = v[:, 0:p2], v[:, p2:2 * p2]
        a_re, a_im = a_ref[g, 8:9, 0:p2], a_ref[g, 8:9, p2:2 * p2]
        fre = jnp.where(fwd, pltpu.roll(vre, 4, 0), pltpu.roll(vre, nrows - 4, 0))
        fim = jnp.where(fwd, pltpu.roll(vim, 4, 0), pltpu.roll(vim, nrows - 4, 0))
        ure_scr[g] = vre + jnp.where(second, a_re * fre - a_im * fim, 0.0)
        uim_scr[g] = vim + jnp.where(second, a_re * fim + a_im * fre, 0.0)
    second8 = second[0:8]
    fwd8 = fwd[0:8]
    ap_re = [a_ref[g, 0:8, 0:p2] for g in range(S5_GB)]
    ap_im = [a_ref[g, 0:8, p2:2 * p2] for g in range(S5_GB)]

    def body(s, carry):
        jr = jnp.where(s < nctx, nctx - 1 - s, nblock - 1 + nctx - s)
        of = pl.multiple_of(s * 8, 8)
        orv = pl.multiple_of(jr * 8, 8)
        new = []
        for g in range(S5_GB):
            cre, cim = carry[g]
            ure = jnp.where(fwd8, ure_scr[g, pl.ds(of, 8), :], ure_scr[g, pl.ds(orv, 8), :])
            uim = jnp.where(fwd8, uim_scr[g, pl.ds(of, 8), :], uim_scr[g, pl.ds(orv, 8), :])
            zre = ap_re[g] * cre - ap_im[g] * cim + ure
            zim = ap_re[g] * cim + ap_im[g] * cre + uim
            rre, rim = pltpu.roll(zre, 4, 0), pltpu.roll(zim, 4, 0)
            ere, eim = jnp.where(second8, rre, cre), jnp.where(second8, rim, cim)
            sre_scr[g, pl.ds(of, 8), 0:S5_STATE] = ere[:, 0:S5_STATE]
            sre_scr[g, pl.ds(orv, 8), S5_STATE:p2] = ere[:, S5_STATE:p2]
            sim_scr[g, pl.ds(of, 8), 0:S5_STATE] = eim[:, 0:S5_STATE]
            sim_scr[g, pl.ds(orv, 8), S5_STATE:p2] = eim[:, S5_STATE:p2]
            new.append((jnp.where(second8, zre, rre), jnp.where(second8, zim, rim)))
        return tuple(new)

    zero = jnp.zeros((8, p2), F32)
    lax.fori_loop(0, nblock, body, tuple((zero, zero) for _ in range(S5_GB)))
    for g in range(S5_GB):
        y = (jnp.dot(x_ref[g], m_ref[g], preferred_element_type=F32)
             + jnp.dot(sre_scr[g].astype(BF16), wre_ref[g], preferred_element_type=F32)
             + jnp.dot(sim_scr[g].astype(BF16), wim_ref[g], preferred_element_type=F32))
        y_ref[g] = y.astype(BF16)


def _s5(xt, bsz, weights):
    g, rows, lanes = xt.shape
    assert 2 * bsz == 8
    nchunk = rows // bsz
    nctx = CTX_LEN // S5_TC
    assert nchunk % 2 == 0 and nctx % 2 == 0
    m, w_in, w_re, w_im, a_rows = weights
    p2 = 2 * S5_STATE
    gb = S5_GB
    wspec = lambda r, c: pl.BlockSpec((gb, r, c), lambda i: (i, 0, 0))
    return pl.pallas_call(
        functools.partial(_s5_kernel, nblock=nchunk // 2, nctx=nctx // 2),
        grid=(g // gb,),
        in_specs=[wspec(rows, lanes), wspec(lanes, lanes), wspec(lanes, 2 * p2), wspec(p2, lanes),
                  wspec(p2, lanes), wspec(16, 2 * p2)],
        out_specs=wspec(rows, lanes),
        out_shape=jax.ShapeDtypeStruct((g, rows, lanes), BF16),
        scratch_shapes=[pltpu.VMEM((gb, rows, p2), F32)] * 4,
        compiler_params=_cparams(("arbitrary",)),
        name="s5",
    )(xt, m, w_in, w_re, w_im, a_rows)


GLU_SAMPLES = 2


def _glu_kernel(y_ref, x_ref, mod_ref, n2g_ref, w_ref, b_ref, rwt_ref, x1_ref, h2_ref, lg_ref, slab, y_scr, *,
                bsz):
    dm = D_MODEL
    half = pl.program_id(1)
    nchunk = TM // S5_TC

    @pl.when(half == 0)
    def _():
        for j in range(S5_LANE_VREGS):
            for gl in range(S5_TC):
                g = (j // 2) * S5_TC + gl
                rows = y_ref[g, :, 128 * (j % 2):128 * (j % 2) + 128].astype(F32)
                slab[j, pl.ds(gl, nchunk * bsz, stride=S5_TC), :] = rows

    for k in range(GLU_SAMPLES):
        b = half * GLU_SAMPLES + k

        def chunk(c, carry):
            s0 = pl.multiple_of((c * bsz + b) * S5_TC, S5_TC)
            p = [[slab[j, pl.ds(s0 + 8 * h, 8), :] for j in range(S5_LANE_VREGS)] for h in range(2)]
            q = _segment_transpose(p)
            r0 = pl.multiple_of(c * S5_TC, S5_TC)
            for h in range(2):
                for j in range(S5_LANE_VREGS):
                    y_scr[pl.ds(r0 + 8 * h, 8), 128 * j:128 * j + 128] = q[h][j]
            return carry

        lax.fori_loop(0, nchunk, chunk, 0)
        z = jax.nn.gelu(y_scr[...]).astype(BF16)
        zz = jnp.dot(z, w_ref[...], preferred_element_type=F32) + b_ref[...]
        glu = zz[:, :dm] * jax.nn.sigmoid(zz[:, dm:])
        x1_ref[k], h2_ref[k], lg_ref[k] = _post_mixer(glu, x_ref[k], mod_ref[b, 2:3, :], mod_ref[b, 3:4, :],
                                                      mod_ref[b, 4:5, :], n2g_ref[0], rwt_ref[...])


def _glu(y, xall, modtab, layer, norm_g, glu_w, glu_b, router_w):
    bsz, s, dm = xall.shape
    n = s - CTX_LEN
    ntl = n // TM
    assert bsz == 2 * GLU_SAMPLES
    gs = GLU_SAMPLES
    rows = TM // S5_TC * bsz
    ctx_tiles = CTX_LEN // TM
    tok = pl.BlockSpec((gs, TM, dm), lambda t, h: (h, t, 0))
    x1, h2, lg = pl.pallas_call(
        functools.partial(_glu_kernel, bsz=bsz),
        grid=(ntl, bsz // gs),
        in_specs=[pl.BlockSpec((S5_GROUPS, rows, S5_TC * S5_GROUP), lambda t, h: (0, t + ctx_tiles, 0)),
                  pl.BlockSpec((gs, TM, dm), lambda t, h: (h, t + ctx_tiles, 0)),
                  pl.BlockSpec((8, 6, dm), lambda t, h: (layer, 0, 0)),
                  pl.BlockSpec((1, 1, dm), lambda t, h: (0, 0, 0)),
                  pl.BlockSpec((dm, 2 * dm), lambda t, h: (0, 0)),
                  pl.BlockSpec((1, 2 * dm), lambda t, h: (0, 0)),
                  pl.BlockSpec((N_EXPERTS, dm), lambda t, h: (0, 0))],
        out_specs=[tok, tok, pl.BlockSpec((gs, N_EXPERTS, TM), lambda t, h: (h, 0, t))],
        out_shape=[jax.ShapeDtypeStruct((bsz, n, dm), F32),
                   jax.ShapeDtypeStruct((bsz, n, dm), BF16),
                   jax.ShapeDtypeStruct((bsz, N_EXPERTS, n), F32)],
        scratch_shapes=[pltpu.VMEM((S5_LANE_VREGS, TM * bsz, 128), F32), pltpu.VMEM((TM, dm), F32)],
        compiler_params=_cparams(("arbitrary", "arbitrary")),
        name="glu",
    )(y, xall, modtab.reshape(DEPTH * 8, 6, dm), norm_g.reshape(1, 1, dm), glu_w.astype(BF16), glu_b[None, :],
      router_w.T)
    return x1, h2, jnp.transpose(lg, (1, 0, 2)).reshape(N_EXPERTS, bsz * n)


def kernel(x, c, ctx, c_ctx, mod_w, mod_b, norm1_g, norm2_g, ar_w_in, ar_w_out, lru_conv_w, lru_conv_b, lru_lam, lru_r_w, lru_r_b, lru_i_w, lru_i_b, attn_q_g, attn_k_g, attn_lam_q1, attn_lam_k1, attn_lam_q2, attn_lam_k2, attn_subln_g, s5_a_re, s5_a_im, s5_log_step, s5_b_re, s5_b_im, s5_c_re, s5_c_im, s5_d, s5_glu_w, s5_glu_b, router_w, router_bias, exp_w_gate, exp_w_up, exp_w_down, sh_w_gate, sh_w_up, sh_w_down):
    bsz, n, dm = x.shape
    assert dm == D_MODEL and ctx.shape[1] == CTX_LEN == TM and n % TM == 0 and bsz < 8
    assert mod_w.shape[0] == DEPTH == 2
    s = CTX_LEN + n
    ntb = s // TM
    modtab = _modulation(c, c_ctx, mod_w, mod_b)

    gate, xa, q, k, v = _inproj(x, ctx, modtab, norm1_g[0], ar_w_in[0], attn_q_g[0], attn_k_g[0])
    lru_args = (lru_conv_w[0], lru_conv_b[0], lru_lam[0], lru_r_w[0], lru_r_b[0], lru_i_w[0], lru_i_b[0])
    hf = _lru(xa, *lru_args, reverse=False)
    hr = _lru(xa, *lru_args, reverse=True)
    lam_init = 0.8 - 0.6 * math.exp(-0.3 * 0)
    lam_vecs = jnp.stack([attn_lam_q1[0], attn_lam_k1[0], attn_lam_q2[0], attn_lam_k2[0]], axis=0)
    yb = _attention(q, k, v, lam_vecs, attn_subln_g[0], lam_init)
    x1, h2, lgt = _outproj(hf, hr, gate, yb, x, ctx, modtab, norm2_g[0], ar_w_out[0], router_w[0])
    row0 = lambda i: jnp.where(i % ntb == 0, bsz, i // ntb)
    xall = _moe(h2.reshape(bsz * s, dm), lgt, x1.reshape(bsz * s, dm), modtab, 0, row0, router_bias[0],
                exp_w_gate, exp_w_up, exp_w_down, sh_w_gate[0], sh_w_up[0], sh_w_down[0])
    xall = xall.reshape(bsz, s, dm)

    xt = _s5_pack(xall, modtab, 1, norm1_g[1])
    weights = _s5_weights(s5_a_re[0], s5_a_im[0], s5_log_step[0], s5_b_re[0], s5_b_im[0], s5_c_re[0],
                          s5_c_im[0], s5_d[0])
    y = _s5(xt, bsz, weights)
    x1, h2, lgt = _glu(y, xall, modtab, 1, norm2_g[1], s5_glu_w[0], s5_glu_b[0], router_w[1])
    ntl = n // TM
    row1 = lambda i: i // ntl
    out = _moe(h2.reshape(bsz * n, dm), lgt, x1.reshape(bsz * n, dm), modtab, 1, row1, router_bias[1],
               exp_w_gate, exp_w_up, exp_w_down, sh_w_gate[1], sh_w_up[1], sh_w_down[1])
    return out.reshape(bsz, n, dm)
```

```python
import functools
import math

import jax
import jax.numpy as jnp
from jax import lax
from jax.experimental import pallas as pl
from jax.experimental.pallas import tpu as pltpu

F32, BF16, I32 = jnp.float32, jnp.bfloat16, jnp.int32
HIGHEST = lax.Precision.HIGHEST

D_MODEL = 1024
DEPTH = 2
GRID_W = 64
CTX_LEN = 256
EPS = 1e-6
LRU_WIDTH = 512
LRU_BLOCKS = 8
LRU_C = 8.0
ATT_HEADS = 4
ATT_DH = 64
ATT_DV = 128
ATT_QK_W = 512
ATT_WIDTH = 512
ROPE_BASE = 10000.0
EVEN_IN = 2 * LRU_WIDTH + 2 * ATT_QK_W + ATT_WIDTH
S5_GROUP = 16
S5_GROUPS = 64
S5_STATE = 64
S5_TC = 16
S5_GB = 4
N_EXPERTS = 64
TOP_K = 6
D_EXPERT = 256
ROUTED_SCALE = 2.5

TM = 256
SEG = 16
SLOT_CHUNK = 64
RMAX = -(-(TM * TOP_K + N_EXPERTS * (SEG - 1)) // SLOT_CHUNK) * SLOT_CHUNK
TR = 1024
TAIL_BITS = (TR // SEG - 1).bit_length()
ATT_TK = 1408
VMEM_LIMIT = 56 * 1024 * 1024


def _cparams(sem):
    return pltpu.CompilerParams(dimension_semantics=sem, vmem_limit_bytes=VMEM_LIMIT)


def _norm_mod(x, g, shift, scale):
    y = x * lax.rsqrt(jnp.mean(x * x, axis=-1, keepdims=True) + EPS) * g
    return y * (1.0 + scale) + shift


def _mod_kernel(c_ref, w_ref, b_ref, o_ref):
    c = c_ref[...]
    s = c * jax.nn.sigmoid(c)
    o_ref[0] = jnp.dot(s, w_ref[0], precision=HIGHEST, preferred_element_type=F32) + b_ref[0]


def _modulation(c, c_ctx, mod_w, mod_b):
    bsz, dm = c.shape
    cc = jnp.concatenate([c, c_ctx[None, :], jnp.zeros((8 - bsz - 1, dm), F32)], axis=0)
    out = pl.pallas_call(
        _mod_kernel,
        grid=(DEPTH, 6),
        in_specs=[pl.BlockSpec((8, dm), lambda l, j: (0, 0)),
                  pl.BlockSpec((1, dm, dm), lambda l, j: (l, 0, j)),
                  pl.BlockSpec((1, 1, dm), lambda l, j: (l, 0, j))],
        out_specs=pl.BlockSpec((1, 8, dm), lambda l, j: (l, 0, j)),
        out_shape=jax.ShapeDtypeStruct((DEPTH, 8, 6 * dm), F32),
        compiler_params=_cparams(("arbitrary", "arbitrary")),
        name="modulation",
    )(cc, mod_w, mod_b.reshape(DEPTH, 1, 6 * dm))
    return out.reshape(DEPTH * 8 * 6, 1, dm)


def _mod_spec(layer, part, row_fn):
    return pl.BlockSpec((1, 1, D_MODEL), lambda *ids: ((layer * 8 + row_fn(*ids)) * 6 + part, 0, 0))


def _rope_tables(n):
    rows = n // GRID_W
    r, col = jnp.meshgrid(jnp.arange(rows), jnp.arange(GRID_W), indexing="ij")
    pos = jnp.stack([r.reshape(-1), col.reshape(-1)], axis=-1).astype(F32)
    n_freq = ATT_DH // 4
    inv_freq = ROPE_BASE ** (-jnp.arange(n_freq, dtype=F32) / n_freq)
    ang = pos[:, :, None] * inv_freq
    cos, sin = jnp.cos(ang), jnp.sin(ang)
    zero = jnp.zeros_like(sin)
    cos64 = jnp.stack([cos, cos], axis=2).reshape(n, ATT_DH)
    sin_lo = jnp.stack([zero, sin], axis=2).reshape(n, ATT_DH)
    sin_hi = jnp.stack([-sin, zero], axis=2).reshape(n, ATT_DH)

    def full(tab, ctx_val):
        tab = jnp.concatenate([jnp.full((CTX_LEN, ATT_DH), ctx_val, F32), tab], axis=0)
        return jnp.concatenate([tab, tab], axis=1)

    return full(cos64, 1.0), full(sin_lo, 0.0), full(sin_hi, 0.0)


def _qk_post(t, gain, ones_bd, cos, sin_lo, sin_hi):
    ss = jnp.dot((t * t).astype(BF16), ones_bd, preferred_element_type=F32) * (1.0 / ATT_DH)
    tn = t * lax.rsqrt(ss + EPS) * gain
    w = tn.shape[1]
    return tn * cos + pltpu.roll(tn, 16, 1) * sin_lo + pltpu.roll(tn, w - 16, 1) * sin_hi


def _inproj_kernel(x_ref, c_ref, sh_ref, sc_ref, g_ref, w_ref, qg_ref, kg_ref, ones_ref,
                   cos_ref, slo_ref, shi_ref, gate_ref, xa_ref, q_ref, k_ref, v_ref):
    t = pl.program_id(1)
    x = jnp.where(t == 0, c_ref[0], x_ref[0])
    h = _norm_mod(x, g_ref[0], sh_ref[0], sc_ref[0])
    z = jnp.dot(h.astype(BF16), w_ref[...], preferred_element_type=F32)
    lw, qw = LRU_WIDTH, ATT_QK_W
    gate_ref[0] = z[:, 0:lw].astype(BF16)
    xa_ref[0] = z[:, lw:2 * lw].astype(BF16)
    tile4 = lambda a: jnp.concatenate([a, a, a, a], axis=1)
    cos, slo, shi = tile4(cos_ref[...]), tile4(slo_ref[...]), tile4(shi_ref[...])
    ones_bd = ones_ref[...]
    q = _qk_post(z[:, 2 * lw:2 * lw + qw], qg_ref[...], ones_bd, cos, slo, shi)
    q_ref[0] = (q * (ATT_DH ** -0.5 * math.log2(math.e))).astype(BF16)
    k = _qk_post(z[:, 2 * lw + qw:2 * lw + 2 * qw], kg_ref[...], ones_bd, cos, slo, shi)
    k_ref[0] = k.astype(BF16)
    v_ref[0] = z[:, 2 * lw + 2 * qw:].astype(BF16)


def _inproj(x, ctx, modtab, norm_g, w_in, q_g, k_g):
    bsz, n, dm = x.shape
    ntb = (CTX_LEN + n) // TM
    s = CTX_LEN + n
    row = lambda b, t: jnp.where(t == 0, bsz, b)
    cos, slo, shi = _rope_tables(n)
    ones_bd = jnp.kron(jnp.eye(ATT_QK_W // ATT_DH, dtype=F32), jnp.ones((ATT_DH, ATT_DH), F32)).astype(BF16)
    tile_g = lambda g: jnp.tile(g, ATT_QK_W // ATT_DH)[None, :]
    tab_spec = pl.BlockSpec((TM, 2 * ATT_DH), lambda b, t: (t, 0))
    out_spec = pl.BlockSpec((1, TM, LRU_WIDTH), lambda b, t: (b, t, 0))
    out_sds = jax.ShapeDtypeStruct((bsz, s, LRU_WIDTH), BF16)
    return pl.pallas_call(
        _inproj_kernel,
        grid=(bsz, ntb),
        in_specs=[pl.BlockSpec((1, TM, dm), lambda b, t: (b, jnp.maximum(t - 1, 0), 0)),
                  pl.BlockSpec((1, TM, dm), lambda b, t: (b, 0, 0)),
                  _mod_spec(0, 0, row), _mod_spec(0, 1, row),
                  pl.BlockSpec((1, 1, dm), lambda b, t: (0, 0, 0)),
                  pl.BlockSpec((dm, EVEN_IN), lambda b, t: (0, 0)),
                  pl.BlockSpec((1, ATT_QK_W), lambda b, t: (0, 0)),
                  pl.BlockSpec((1, ATT_QK_W), lambda b, t: (0, 0)),
                  pl.BlockSpec((ATT_QK_W, ATT_QK_W), lambda b, t: (0, 0)),
                  tab_spec, tab_spec, tab_spec],
        out_specs=[out_spec] * 5,
        out_shape=[out_sds] * 5,
        compiler_params=_cparams(("arbitrary", "arbitrary")),
        name="inproj",
    )(x, ctx, modtab, modtab, norm_g.reshape(1, 1, dm), w_in.astype(BF16), tile_g(q_g), tile_g(k_g), ones_bd,
      cos, slo, shi)


def _lru_kernel(xa_ref, xp_ref, xn_ref, cw_ref, cb_ref, lam_ref, w_ref, bias_ref, h_ref,
                a_scr, b_scr, c_scr, *, ntb, reverse):
    s = pl.program_id(1)
    ti = jnp.where(s == 0, 0, ntb - s) if reverse else s
    lw = LRU_WIDTH
    x = xa_ref[0].astype(F32)
    row = lax.broadcasted_iota(I32, (TM, lw), 0)
    has_prev = jnp.where(ti > 1, 1.0, 0.0)
    has_next = jnp.where((ti > 0) & (ti < ntb - 1), 1.0, 0.0)
    prev = xp_ref[0].astype(F32) * has_prev
    nxt = xn_ref[0].astype(F32) * has_next
    xm1 = jnp.where(row == 0, prev[7:8], pltpu.roll(x, 1, 0))
    xm2 = jnp.where(row == 0, prev[6:7], jnp.where(row == 1, prev[7:8], pltpu.roll(x, 2, 0)))
    xp1 = jnp.where(row == TM - 1, nxt[0:1], pltpu.roll(x, TM - 1, 0))
    cw = cw_ref[...]
    xc = cw[0:1] * xm2 + cw[1:2] * xm1 + cw[2:3] * x + cw[3:4] * xp1 + cb_ref[...]

    z = jnp.dot(xc.astype(BF16), w_ref[...], preferred_element_type=F32) + bias_ref[...]
    r = jax.nn.sigmoid(z[:, :lw])
    ig = jax.nn.sigmoid(z[:, lw:])
    neg_lam = -lam_ref[...]
    softplus = jnp.maximum(neg_lam, 0.0) + jnp.log1p(jnp.exp(-jnp.abs(neg_lam)))
    log_a = (-LRU_C) * r * softplus
    a = jnp.exp(log_a)
    b = jnp.sqrt(-jnp.tanh(log_a) * (a * a + 1.0)) * (ig * xc)

    r8 = row & 7
    for sft in (1, 2, 4):
        if reverse:
            a_s, b_s, m = pltpu.roll(a, TM - sft, 0), pltpu.roll(b, TM - sft, 0), r8 < 8 - sft
        else:
            a_s, b_s, m = pltpu.roll(a, sft, 0), pltpu.roll(b, sft, 0), r8 >= sft
        b = jnp.where(m, a * b_s + b, b)
        a = jnp.where(m, a * a_s, a)
    a_scr[...] = a
    b_scr[...] = b

    @pl.when(s == 0)
    def _():
        c_scr[...] = jnp.zeros_like(c_scr)

    ng = TM // 8

    def body(j, carry):
        g = (ng - 1 - j) if reverse else j
        off = pl.multiple_of(g * 8, 8)
        h = a_scr[pl.ds(off, 8), :] * carry + b_scr[pl.ds(off, 8), :]
        h_ref[0, pl.ds(off, 8), :] = h
        last = h[0:1] if reverse else h[7:8]
        return jnp.broadcast_to(last, (8, lw))

    c_scr[...] = lax.fori_loop(0, ng, body, c_scr[...], unroll=4)


def _lru(xa, conv_w, conv_b, lam, r_w, r_b, i_w, i_b, reverse):
    bsz, s, lw = xa.shape
    ntb = s // TM
    d = 1 if reverse else 0
    bd = lambda w: jax.scipy.linalg.block_diag(*[w[i] for i in range(LRU_BLOCKS)])
    w = jnp.concatenate([bd(r_w[d]), bd(i_w[d])], axis=1).astype(BF16)
    bias = jnp.concatenate([r_b[d], i_b[d]])[None, :]
    cw = jnp.concatenate([conv_w, jnp.zeros((4, lw), F32)], axis=0)
    if reverse:
        tile = lambda t: jnp.where(t == 0, 0, ntb - t)
    else:
        tile = lambda t: t
    nb8 = s // 8
    return pl.pallas_call(
        functools.partial(_lru_kernel, ntb=ntb, reverse=reverse),
        grid=(bsz, ntb),
        in_specs=[pl.BlockSpec((1, TM, lw), lambda b, t: (b, tile(t), 0)),
                  pl.BlockSpec((1, 8, lw), lambda b, t: (b, jnp.maximum(tile(t) * (TM // 8) - 1, 0), 0)),
                  pl.BlockSpec((1, 8, lw), lambda b, t: (b, jnp.minimum((tile(t) + 1) * (TM // 8), nb8 - 1), 0)),
                  pl.BlockSpec((8, lw), lambda b, t: (0, 0)),
                  pl.BlockSpec((1, lw), lambda b, t: (0, 0)),
                  pl.BlockSpec((1, lw), lambda b, t: (0, 0)),
                  pl.BlockSpec((lw, 2 * lw), lambda b, t: (0, 0)),
                  pl.BlockSpec((1, 2 * lw), lambda b, t: (0, 0))],
        out_specs=pl.BlockSpec((1, TM, lw), lambda b, t: (b, tile(t), 0)),
        out_shape=jax.ShapeDtypeStruct((bsz, s, lw), F32),
        scratch_shapes=[pltpu.VMEM((TM, lw), F32), pltpu.VMEM((TM, lw), F32), pltpu.VMEM((8, lw), F32)],
        compiler_params=_cparams(("arbitrary", "arbitrary")),
        name="lru_rev" if reverse else "lru_fwd",
    )(xa, xa, xa, cw, conv_b[None, :], lam[d][None, :], w, bias)


def _attn_kernel(lamv_ref, q_ref, k_ref, v_ref, sg_ref, o_ref, m_scr, l_scr, acc_scr, al_scr, s_scr, p_scr, *,
                 nkv, lam_init):
    t = pl.program_id(2)
    q = q_ref[0]
    dh = ATT_DH
    dv = ATT_DV
    m_scr[...] = jnp.full_like(m_scr, -jnp.inf)
    l_scr[...] = jnp.zeros_like(l_scr)
    acc_scr[...] = jnp.zeros_like(acc_scr)

    def scores(buf, off, size):
        kc = k_ref[0, pl.ds(off, size), :]
        for mi in range(2):
            s_scr[buf, mi, :, 0:size] = lax.dot_general(
                q[:, mi * dh:(mi + 1) * dh], kc[:, mi * dh:(mi + 1) * dh], (((1,), (1,)), ((), ())),
                preferred_element_type=F32)

    def softmax(buf, size):
        nlb = size // 128
        groups = [(slice(r * 16, (r + 1) * 16), mi) for r in range(TM // 16) for mi in range(2)]
        for rows, mi in groups:
            sc = s_scr[buf, mi, rows, 0:size]
            mx = functools.reduce(jnp.maximum, [sc[:, i * 128:(i + 1) * 128] for i in range(nlb)])
            m_old = m_scr[mi, rows, :]
            m_new = jnp.maximum(m_old, jnp.max(mx, axis=1, keepdims=True))
            m_scr[mi, rows, :] = m_new
            al_scr[mi, rows, :] = jnp.exp2(m_old - m_new)
        for rows, mi in groups:
            m_new = m_scr[mi, rows, :]
            ps = [jnp.exp2(s_scr[buf, mi, rows, i * 128:(i + 1) * 128] - m_new) for i in range(nlb)]
            l_scr[mi, rows, :] = al_scr[mi, rows, :] * l_scr[mi, rows, :] + functools.reduce(jnp.add, ps)
            p_scr[buf, mi, rows, 0:size] = jnp.concatenate(ps, axis=1).astype(BF16)

    def values(buf, off, size):
        vc = v_ref[0, pl.ds(off, size), :]
        pv = jnp.dot(p_scr[buf, :, :, 0:size].reshape(2 * TM, size), vc, preferred_element_type=F32)
        acc_scr[...] = al_scr[...] * acc_scr[...] + pv.reshape(2, TM, dv)

    @pl.when(t == 0)
    def _():
        scores(0, 0, CTX_LEN)
        softmax(0, CTX_LEN)
        values(0, 0, CTX_LEN)

    @pl.when(t > 0)
    def _():
        tk = ATT_TK
        nchunk = nkv // tk
        scores(0, 0, tk)
        for j in range(nchunk):
            if j + 1 < nchunk:
                scores((j + 1) % 2, (j + 1) * tk, tk)
            softmax(j % 2, tk)
            values(j % 2, j * tk, tk)

    lv = lamv_ref[...]
    lam = (jnp.exp(jnp.sum(lv[0:1] * lv[1:2], axis=1, keepdims=True))
           - jnp.exp(jnp.sum(lv[2:3] * lv[3:4], axis=1, keepdims=True)) + lam_init)
    l0 = jnp.sum(l_scr[0], axis=1, keepdims=True)
    l1 = jnp.sum(l_scr[1], axis=1, keepdims=True)
    o = acc_scr[0] / l0 - lam * (acc_scr[1] / l1)
    y = o * lax.rsqrt(jnp.mean(o * o, axis=-1, keepdims=True) + EPS) * sg_ref[...] * (1.0 - lam_init)
    o_ref[0] = y.astype(BF16)


def _attention(q, k, v, lam_vecs, subln_g, lam_init):
    bsz, s, _ = q.shape
    ntb = s // TM
    assert s % ATT_TK == 0
    dv = ATT_DV
    lamv = jnp.concatenate([jnp.pad(lam_vecs, ((0, 0), (0, dv - ATT_DH))), jnp.zeros((4, dv), F32)], axis=0)
    return pl.pallas_call(
        functools.partial(_attn_kernel, nkv=s, lam_init=lam_init),
        grid=(bsz, ATT_HEADS, ntb),
        in_specs=[pl.BlockSpec((8, dv), lambda b, h, t: (0, 0)),
                  pl.BlockSpec((1, TM, dv), lambda b, h, t: (b, t, h)),
                  pl.BlockSpec((1, s, dv), lambda b, h, t: (b, 0, h)),
                  pl.BlockSpec((1, s, dv), lambda b, h, t: (b, 0, h)),
                  pl.BlockSpec((1, dv), lambda b, h, t: (0, 0))],
        out_specs=pl.BlockSpec((1, TM, dv), lambda b, h, t: (b, t, h)),
        out_shape=jax.ShapeDtypeStruct((bsz, s, ATT_WIDTH), BF16),
        scratch_shapes=[pltpu.VMEM((2, TM, dv), F32)] * 4
        + [pltpu.VMEM((2, 2, TM, ATT_TK), F32), pltpu.VMEM((2, 2, TM, ATT_TK), BF16)],
        compiler_params=_cparams(("arbitrary", "arbitrary", "arbitrary")),
        name="diff_attention",
    )(lamv, q, k, v, subln_g[None, :])


def _post_mixer(y, x, g1, sh2, sc2, n2g, rwt):
    x1 = x + g1 * y
    h2 = _norm_mod(x1, n2g, sh2, sc2)
    logits = lax.dot_general(rwt, h2, (((1,), (1,)), ((), ())), precision=HIGHEST, preferred_element_type=F32)
    return x1, h2.astype(BF16), logits


def _outproj_kernel(hf_ref, hr_ref, gate_ref, yb_ref, x_ref, c_ref, g1_ref, sh_ref, sc_ref, n2g_ref, w_ref,
                    rwt_ref, x1_ref, h2_ref, lg_ref):
    t = pl.program_id(1)
    x = jnp.where(t == 0, c_ref[0], x_ref[0])
    ya = ((hf_ref[0] + hr_ref[0]) * jax.nn.gelu(gate_ref[0].astype(F32))).astype(BF16)
    lw = LRU_WIDTH
    y = (jnp.dot(ya, w_ref[0:lw, :], preferred_element_type=F32)
         + jnp.dot(yb_ref[0], w_ref[lw:, :], preferred_element_type=F32))
    x1_ref[0], h2_ref[0], lg_ref[...] = _post_mixer(y, x, g1_ref[0], sh_ref[0], sc_ref[0], n2g_ref[0],
                                                    rwt_ref[...])


def _outproj(hf, hr, gate, yb, x, ctx, modtab, norm_g, w_out, router_w):
    bsz, s, lw = hf.shape
    dm = D_MODEL
    ntb = s // TM
    row = lambda b, t: jnp.where(t == 0, bsz, b)
    half = pl.BlockSpec((1, TM, lw), lambda b, t: (b, t, 0))
    return pl.pallas_call(
        _outproj_kernel,
        grid=(bsz, ntb),
        in_specs=[half, half, half, half,
                  pl.BlockSpec((1, TM, dm), lambda b, t: (b, jnp.maximum(t - 1, 0), 0)),
                  pl.BlockSpec((1, TM, dm), lambda b, t: (b, 0, 0)),
                  _mod_spec(0, 2, row), _mod_spec(0, 3, row), _mod_spec(0, 4, row),
                  pl.BlockSpec((1, 1, dm), lambda b, t: (0, 0, 0)),
                  pl.BlockSpec((2 * lw, dm), lambda b, t: (0, 0)),
                  pl.BlockSpec((N_EXPERTS, dm), lambda b, t: (0, 0))],
        out_specs=[pl.BlockSpec((1, TM, dm), lambda b, t: (b, t, 0)),
                   pl.BlockSpec((1, TM, dm), lambda b, t: (b, t, 0)),
                   pl.BlockSpec((N_EXPERTS, TM), lambda b, t: (0, b * ntb + t))],
        out_shape=[jax.ShapeDtypeStruct((bsz, s, dm), F32),
                   jax.ShapeDtypeStruct((bsz, s, dm), BF16),
                   jax.ShapeDtypeStruct((N_EXPERTS, bsz * s), F32)],
        compiler_params=_cparams(("arbitrary", "arbitrary")),
        name="outproj",
    )(hf, hr, gate, yb, x, ctx, modtab, modtab, modtab, norm_g.reshape(1, 1, dm), w_out.astype(BF16),
      router_w.T)


def _route_kernel(lg_ref, bias_ref, tri_ref, low_ref, pos_ref, gate_ref, cnt_ref):
    ne = N_EXPERTS
    aff = jax.nn.sigmoid(lg_ref[...])
    work = aff + bias_ref[:, 0:1]
    eidx = lax.broadcasted_iota(I32, (ne, TM), 0)
    sels = []
    for _ in range(TOP_K):
        mx = jnp.max(work, axis=0, keepdims=True)
        am = jnp.min(jnp.where(work == mx, eidx, ne), axis=0, keepdims=True)
        sk = eidx == am
        sels.append(sk)
        work = jnp.where(sk, -jnp.inf, work)
    sel = sels[0]
    for sk in sels[1:]:
        sel = sel | sk
    self = jnp.where(sel, 1.0, 0.0)
    s_sel = aff * self
    gates = s_sel / jnp.sum(s_sel, axis=0, keepdims=True) * ROUTED_SCALE
    rank = jnp.dot(self.astype(BF16), tri_ref[...], preferred_element_type=F32)
    cnt = jnp.sum(self, axis=1, keepdims=True)
    cnt_seg = jnp.floor((cnt + (SEG - 1)) * (1.0 / SEG)) * SEG
    cnt_b = jnp.broadcast_to(cnt_seg, (ne, 128))
    seg_off = jnp.dot(low_ref[...], cnt_b.astype(BF16), preferred_element_type=F32)
    lpos = seg_off[:, 0:1] + rank
    pos_rows, gate_rows = [], []
    for sk in sels:
        pos_rows.append(jnp.sum(jnp.where(sk, lpos, 0.0), axis=0, keepdims=True))
        gate_rows.append(jnp.sum(jnp.where(sk, gates, 0.0), axis=0, keepdims=True))
    for _ in range(8 - TOP_K):
        pos_rows.append(jnp.full((1, TM), -1.0, F32))
        gate_rows.append(jnp.zeros((1, TM), F32))
    pos_ref[...] = jnp.concatenate(pos_rows, axis=0).astype(I32)
    gate_ref[...] = jnp.concatenate(gate_rows, axis=0)
    cnt_ref[0] = cnt_b.astype(I32)


def _route(lgt, router_bias):
    ne, t = lgt.shape
    nt = t // TM
    tri = jnp.triu(jnp.ones((TM, TM), F32), k=1).astype(BF16)
    low = jnp.tril(jnp.ones((ne, ne), F32), k=-1).astype(BF16)
    bias = jnp.broadcast_to(router_bias[:, None], (ne, 128))
    pos, gate, cnt = pl.pallas_call(
        _route_kernel,
        grid=(nt,),
        in_specs=[pl.BlockSpec((ne, TM), lambda i: (0, i)),
                  pl.BlockSpec((ne, 128), lambda i: (0, 0)),
                  pl.BlockSpec((TM, TM), lambda i: (0, 0)),
                  pl.BlockSpec((ne, ne), lambda i: (0, 0))],
        out_specs=[pl.BlockSpec((8, TM), lambda i: (0, i)),
                   pl.BlockSpec((8, TM), lambda i: (0, i)),
                   pl.BlockSpec((1, ne, 128), lambda i: (i, 0, 0))],
        out_shape=[jax.ShapeDtypeStruct((8, t), I32),
                   jax.ShapeDtypeStruct((8, t), F32),
                   jax.ShapeDtypeStruct((nt, ne, 128), I32)],
        compiler_params=_cparams(("arbitrary",)),
        name="route",
    )(lgt, bias, tri, low)
    return pos, gate, cnt[:, :, 0]


def _moe_layout(cnt_seg, nblk_max):
    tot = jnp.sum(cnt_seg, axis=0)
    region = (tot + TR - 1) // TR * TR
    region_end = jnp.cumsum(region)
    goff = (region_end - region)[None, :] + jnp.cumsum(cnt_seg, axis=0) - cnt_seg
    packed = ((goff // SEG) << 5) | (cnt_seg // SEG)
    nblk = region_end[-1] // TR
    blk = jnp.arange(nblk_max, dtype=I32)
    blk_e = jnp.sum((region_end[None, :] // TR <= blk[:, None]).astype(I32), axis=1)
    blk_e = jnp.minimum(blk_e, N_EXPERTS - 1).astype(I32)
    tile_info = jnp.sum(cnt_seg, axis=1) | (jnp.any(cnt_seg >= 64, axis=1).astype(I32) << 16)
    tails = (((region_end - region + tot) // SEG) << TAIL_BITS) | ((region - tot) // SEG)
    tails = jnp.concatenate([tails, nblk[None]]).astype(I32)
    return packed.reshape(-1).astype(I32), tile_info.astype(I32), tails, blk_e, nblk.astype(I32).reshape(1)


def _segment_copy(stage, hbm, sem, to_sorted, lo, go, size):
    lo, go = pl.multiple_of(lo, SEG), pl.multiple_of(go, SEG)
    a, b = stage.at[pl.ds(lo, size)], hbm.at[pl.ds(go, size)]
    return pltpu.make_async_copy(a, b, sem) if to_sorted else pltpu.make_async_copy(b, a, sem)


def _segment_unpack(pk):
    return (pk & 31) * SEG, (pk >> 5) * SEG


def _segment_starts(pk_ref, tile, stage, hbm, sem, to_sorted, valid=None):
    loff = [0]

    def step(e):
        cnt, goff = _segment_unpack(pk_ref[tile * N_EXPERTS + e])
        done = (cnt >> 6) << 6
        for size in (32, 16):
            bit = (cnt & size) != 0
            if valid is not None:
                bit = bit & valid

            @pl.when(bit)
            def _(lo=loff[0] + done, go=goff + done, size=size):
                _segment_copy(stage, hbm, sem, to_sorted, lo, go, size).start()

            done = done + jnp.where(bit, size, 0)
        loff[0] = loff[0] + cnt

    return [functools.partial(step, e) for e in range(N_EXPERTS)]


def _interleave(*step_lists):
    total = max(len(steps) for steps in step_lists)
    done = [0] * len(step_lists)
    for t in range(1, total + 1):
        for k, steps in enumerate(step_lists):
            upto = len(steps) * t // total
            for step in steps[done[k]:upto]:
                step()
            done[k] = upto


def _segment_starts_long(pk_ref, tile, info, stage, hbm, sem, to_sorted):
    @pl.when((info >> 16) != 0)
    def _():
        def expert(e, loff):
            cnt, goff = _segment_unpack(pk_ref[tile * N_EXPERTS + e])

            def chunk(j, carry):
                _segment_copy(stage, hbm, sem, to_sorted, loff + j * 64, goff + j * 64, 64).start()
                return carry

            lax.fori_loop(0, cnt >> 6, chunk, 0)
            return loff + cnt

        lax.fori_loop(0, N_EXPERTS, expert, 0)


def _segment_wait(total, stage, hbm, sem, to_sorted):
    size = 1 << (RMAX.bit_length() - 1)
    while size >= SEG:
        @pl.when((total & size) != 0)
        def _(size=size):
            _segment_copy(stage, hbm, sem, to_sorted, 0, 0, size).wait()

        size //= 2


def _fill_slot_matrix(dst_ref, pos, weight_rows=None):
    ch = SLOT_CHUNK
    riota = lax.broadcasted_iota(I32, (ch, TM), 0)

    def step(c):
        local = pos - c * ch
        out = jnp.zeros((ch, TM), F32)
        for k in range(TOP_K):
            w = 1.0 if weight_rows is None else weight_rows[k:k + 1]
            out = jnp.where(riota == local[k:k + 1], w, out)
        dst_ref[c * ch:(c + 1) * ch, :] = out.astype(BF16)

    return [functools.partial(step, c) for c in range(RMAX // ch)]


def _zero_fill(tail_ref, zeros, xs_ref, sem, fn):
    def expert(e, carry):
        pk = tail_ref[e]
        off = (pk >> TAIL_BITS) * SEG

        def chunk(j, c):
            dst = xs_ref.at[pl.ds(pl.multiple_of(off + j * SEG, SEG), SEG)]
            fn(pltpu.make_async_copy(zeros.at[pl.ds(0, SEG)], dst, sem))
            return c

        return lax.fori_loop(0, pk & ((1 << TAIL_BITS) - 1), chunk, carry)

    lax.fori_loop(0, N_EXPERTS, expert, 0)

    def block(j, carry):
        fn(pltpu.make_async_copy(zeros, xs_ref.at[pl.ds(pl.multiple_of(j * TR, TR), TR)], sem))
        return carry

    lax.fori_loop(tail_ref[N_EXPERTS], xs_ref.shape[0] // TR, block, 0)


def _dispatch_kernel(pk_ref, tot_ref, tail_ref, pos_ref, h_ref, xs_ref, stage, zeros, slots, sem, zsem):
    i = pl.program_id(0)
    slot = i % 2

    @pl.when(i == 0)
    def _():
        zeros[...] = jnp.zeros_like(zeros)
        _zero_fill(tail_ref, zeros, xs_ref, zsem, lambda c: c.start())

    for step in _fill_slot_matrix(slots, pos_ref[...]):
        step()
    stage[slot] = jnp.dot(slots[...], h_ref[...], preferred_element_type=F32).astype(BF16)
    prev_rows = jnp.where(i > 0, tot_ref[jnp.maximum(i - 1, 0)] & 0xFFFF, 0)
    _segment_wait(prev_rows, stage.at[1 - slot], xs_ref, sem.at[1 - slot], True)
    this = (stage.at[slot], xs_ref, sem.at[slot], True)
    for step in _segment_starts(pk_ref, i, *this):
        step()
    _segment_starts_long(pk_ref, i, tot_ref[i], *this)

    @pl.when(i == pl.num_programs(0) - 1)
    def _():
        _segment_wait(tot_ref[i] & 0xFFFF, *this)
        _zero_fill(tail_ref, zeros, xs_ref, zsem, lambda c: c.wait())


def _dispatch(packed, tile_rows, tails, pos, h2, nrows):
    t, dm = h2.shape
    nt = t // TM
    return pl.pallas_call(
        _dispatch_kernel,
        grid_spec=pltpu.PrefetchScalarGridSpec(
            num_scalar_prefetch=3,
            grid=(nt,),
            in_specs=[pl.BlockSpec((8, TM), lambda i, pk, tot, tail: (0, i)),
                      pl.BlockSpec((TM, dm), lambda i, pk, tot, tail: (i, 0))],
            out_specs=pl.BlockSpec(memory_space=pl.ANY),
            scratch_shapes=[pltpu.VMEM((2, RMAX, dm), BF16), pltpu.VMEM((TR, dm), BF16),
                            pltpu.VMEM((RMAX, TM), BF16),
                            pltpu.SemaphoreType.DMA((2,)), pltpu.SemaphoreType.DMA]),
        out_shape=jax.ShapeDtypeStruct((nrows, dm), BF16),
        compiler_params=_cparams(("arbitrary",)),
        name="moe_dispatch",
    )(packed, tile_rows, tails, pos, h2)


def _expert_kernel(be_ref, nb_ref, x_ref, wg_ref, wu_ref, wd_ref, y_ref, wg_s, wu_s, wd_s):
    j = pl.program_id(0)
    changed = be_ref[j] != be_ref[jnp.maximum(j - 1, 0)]

    @pl.when((j == 0) | changed)
    def _():
        wg_s[...] = wg_ref[0].astype(BF16)
        wu_s[...] = wu_ref[0].astype(BF16)
        wd_s[...] = wd_ref[0].astype(BF16)

    @pl.when(j < nb_ref[0])
    def _():
        x = x_ref[...]
        g = jnp.dot(x, wg_s[...], preferred_element_type=F32)
        u = jnp.dot(x, wu_s[...], preferred_element_type=F32)
        a = (g * jax.nn.sigmoid(g) * u).astype(BF16)
        y_ref[...] = jnp.dot(a, wd_s[...], preferred_element_type=F32).astype(BF16)


def _experts(blk_e, nblk, xs, layer, w_gate, w_up, w_down):
    nrows, dm = xs.shape
    nblk_max = nrows // TR
    de = D_EXPERT
    row_blk = lambda j, be, nb: (jnp.minimum(j, nb[0] - 1), 0)
    return pl.pallas_call(
        _expert_kernel,
        grid_spec=pltpu.PrefetchScalarGridSpec(
            num_scalar_prefetch=2,
            grid=(nblk_max,),
            in_specs=[pl.BlockSpec((TR, dm), row_blk),
                      pl.BlockSpec((None, 1, dm, de), lambda j, be, nb: (layer, be[j], 0, 0)),
                      pl.BlockSpec((None, 1, dm, de), lambda j, be, nb: (layer, be[j], 0, 0)),
                      pl.BlockSpec((None, 1, de, dm), lambda j, be, nb: (layer, be[j], 0, 0))],
            out_specs=pl.BlockSpec((TR, dm), row_blk),
            scratch_shapes=[pltpu.VMEM((dm, de), BF16), pltpu.VMEM((dm, de), BF16), pltpu.VMEM((de, dm), BF16)]),
        out_shape=jax.ShapeDtypeStruct((nrows, dm), BF16),
        input_output_aliases={2: 0},
        compiler_params=_cparams(("arbitrary",)),
        name="moe_experts",
    )(blk_e, nblk, xs, w_gate, w_up, w_down)


def _combine_kernel(pk_ref, tot_ref, pos_ref, gate_ref, h_ref, x1_ref, g2_ref, wsg_ref, wsu_ref, wsd_ref, ys_ref,
                    o_ref, stage, gates, sem):
    i = pl.program_id(0)
    slot = i % 2
    last = pl.num_programs(0) - 1

    @pl.when(i == 0)
    def _():
        stage[...] = jnp.zeros_like(stage)
        for step in _segment_starts(pk_ref, 0, stage.at[0], ys_ref, sem.at[0], False):
            step()
        _segment_starts_long(pk_ref, 0, tot_ref[0], stage.at[0], ys_ref, sem.at[0], False)

    nxt = jnp.minimum(i + 1, last)
    _interleave(_fill_slot_matrix(gates, pos_ref[...], gate_ref[...]),
                _segment_starts(pk_ref, nxt, stage.at[1 - slot], ys_ref, sem.at[1 - slot], False))
    h = h_ref[...]
    g = jnp.dot(h, wsg_ref[...], preferred_element_type=F32)
    u = jnp.dot(h, wsu_ref[...], preferred_element_type=F32)
    shared = jnp.dot((g * jax.nn.sigmoid(g) * u).astype(BF16), wsd_ref[...], preferred_element_type=F32)
    _segment_wait(tot_ref[i] & 0xFFFF, stage.at[slot], ys_ref, sem.at[slot], False)
    routed = lax.dot_general(gates[...], stage[slot], (((0,), (0,)), ((), ())), preferred_element_type=F32)
    o_ref[...] = x1_ref[...] + g2_ref[0] * (routed + shared)
    _segment_starts_long(pk_ref, nxt, tot_ref[nxt], stage.at[1 - slot], ys_ref, sem.at[1 - slot], False)

    @pl.when(i == last)
    def _():
        _segment_wait(tot_ref[i] & 0xFFFF, stage.at[1 - slot], ys_ref, sem.at[1 - slot], False)


def _combine(packed, tile_rows, pos, gate, h2, x1, modtab, layer, row_fn, ys, ws_gate, ws_up, ws_down):
    t, dm = h2.shape
    nt = t // TM
    de = D_EXPERT
    const = lambda shape: pl.BlockSpec(shape, lambda i, pk, tot: (0,) * len(shape))
    return pl.pallas_call(
        _combine_kernel,
        grid_spec=pltpu.PrefetchScalarGridSpec(
            num_scalar_prefetch=2,
            grid=(nt,),
            in_specs=[pl.BlockSpec((8, TM), lambda i, pk, tot: (0, i)),
                      pl.BlockSpec((8, TM), lambda i, pk, tot: (0, i)),
                      pl.BlockSpec((TM, dm), lambda i, pk, tot: (i, 0)),
                      pl.BlockSpec((TM, dm), lambda i, pk, tot: (i, 0)),
                      pl.BlockSpec((1, 1, dm), lambda i, pk, tot: ((layer * 8 + row_fn(i)) * 6 + 5, 0, 0)),
                      const((dm, de)), const((dm, de)), const((de, dm)),
                      pl.BlockSpec(memory_space=pl.ANY)],
            out_specs=pl.BlockSpec((TM, dm), lambda i, pk, tot: (i, 0)),
            scratch_shapes=[pltpu.VMEM((2, RMAX, dm), BF16), pltpu.VMEM((RMAX, TM), BF16),
                            pltpu.SemaphoreType.DMA((2,))]),
        out_shape=jax.ShapeDtypeStruct((t, dm), F32),
        compiler_params=_cparams(("arbitrary",)),
        name="moe_combine",
    )(packed, tile_rows, pos, gate, h2, x1, modtab, ws_gate.astype(BF16), ws_up.astype(BF16),
      ws_down.astype(BF16), ys)


def _moe(h2, lgt, x1, modtab, layer, row_fn, router_bias, w_gate, w_up, w_down, ws_gate, ws_up, ws_down):
    t = h2.shape[0]
    nt = t // TM
    max_rows = t * TOP_K + nt * N_EXPERTS * (SEG - 1) + N_EXPERTS * (TR - 1)
    nblk_max = -(-max_rows // TR)
    pos, gate, cnt_seg = _route(lgt, router_bias)
    packed, tile_rows, tails, blk_e, nblk = _moe_layout(cnt_seg, nblk_max)
    xs = _dispatch(packed, tile_rows, tails, pos, h2, nblk_max * TR)
    ys = _experts(blk_e, nblk, xs, layer, w_gate, w_up, w_down)
    return _combine(packed, tile_rows, pos, gate, h2, x1, modtab, layer, row_fn, ys, ws_gate, ws_up, ws_down)


S5_LANE_VREGS = D_MODEL // 128


def _segment_transpose(p):
    row = lax.broadcasted_iota(I32, (8, 128), 0)
    seg = lax.broadcasted_iota(I32, (8, 128), 1) // S5_GROUP
    for s in (4, 2, 1):
        m_up = ((row & s) == 0) & ((seg & s) != 0)
        m_dn = ((row & s) != 0) & ((seg & s) == 0)

        def swap(x):
            if s == 4:
                return jnp.where(m_up | m_dn, pltpu.roll(pltpu.roll(x, 4, 0), 4 * S5_GROUP, 1), x)
            up = pltpu.roll(pltpu.roll(x, 8 - s, 0), S5_GROUP * s, 1)
            dn = pltpu.roll(pltpu.roll(x, s, 0), 128 - S5_GROUP * s, 1)
            return jnp.where(m_up, up, jnp.where(m_dn, dn, x))

        p = [[swap(x) for x in half] for half in p]
    return [[p[1 - h][j - 1 + 2 * h] if (j % 2) != h else p[h][j] for j in range(S5_LANE_VREGS)] for h in range(2)]


def _s5_pack_kernel(x_ref, mod_ref, g_ref, xt_ref, u_scr, slab):
    t = pl.program_id(0)
    bsz = x_ref.shape[0]
    nchunk = TM // S5_TC
    for b in range(bsz):
        shift = jnp.where(t == 0, mod_ref[bsz, 0:1, :], mod_ref[b, 0:1, :])
        scale = jnp.where(t == 0, mod_ref[bsz, 1:2, :], mod_ref[b, 1:2, :])
        u_scr[b] = _norm_mod(x_ref[b], g_ref[0], shift, scale)

    def chunk(c, carry):
        r0 = pl.multiple_of(c * S5_TC, S5_TC)
        for b in range(bsz):
            p = [[u_scr[b, pl.ds(r0 + 8 * h, 8), 128 * j:128 * j + 128] for j in range(S5_LANE_VREGS)]
                 for h in range(2)]
            q = _segment_transpose(p)
            s0 = pl.multiple_of((c * bsz + b) * S5_TC, S5_TC)
            for h in range(2):
                for j in range(S5_LANE_VREGS):
                    slab[j, pl.ds(s0 + 8 * h, 8), :] = q[h][j]
        return carry

    lax.fori_loop(0, nchunk, chunk, 0)
    for j in range(S5_LANE_VREGS):
        for gl in range(S5_TC):
            g = (j // 2) * S5_TC + gl
            rows = slab[j, pl.ds(gl, nchunk * bsz, stride=S5_TC), :]
            xt_ref[g, :, 128 * (j % 2):128 * (j % 2) + 128] = rows.astype(BF16)


def _s5_pack(xall, modtab, layer, norm_g):
    bsz, s, dm = xall.shape
    ntb = s // TM
    rows = TM // S5_TC * bsz
    return pl.pallas_call(
        _s5_pack_kernel,
        grid=(ntb,),
        in_specs=[pl.BlockSpec((bsz, TM, dm), lambda t: (0, t, 0)),
                  pl.BlockSpec((8, 6, dm), lambda t: (layer, 0, 0)),
                  pl.BlockSpec((1, 1, dm), lambda t: (0, 0, 0))],
        out_specs=pl.BlockSpec((S5_GROUPS, rows, S5_TC * S5_GROUP), lambda t: (0, t, 0)),
        out_shape=jax.ShapeDtypeStruct((S5_GROUPS, ntb * rows, S5_TC * S5_GROUP), BF16),
        scratch_shapes=[pltpu.VMEM((bsz, TM, dm), F32), pltpu.VMEM((S5_LANE_VREGS, TM * bsz, 128), F32)],
        compiler_params=_cparams(("arbitrary",)),
        name="s5_pack",
    )(xall, modtab.reshape(DEPTH * 8, 6, dm), norm_g.reshape(1, 1, dm))


def _cmul(x, y):
    return x[0] * y[0] - x[1] * y[1], x[0] * y[1] + x[1] * y[0]


def _s5_weights(a_re, a_im, log_step, b_re, b_im, c_re, c_im, d_skip):
    tc, g, p, ch = S5_TC, S5_GROUPS, S5_STATE, S5_GROUP
    lam = (jnp.minimum(a_re, -1e-4), a_im)
    step = jnp.exp(log_step)
    mag = jnp.exp(lam[0] * step)
    lam_bar = (mag * jnp.cos(lam[1] * step), mag * jnp.sin(lam[1] * step))
    inv = 1.0 / (lam[0] * lam[0] + lam[1] * lam[1])
    coef = _cmul((lam_bar[0] - 1.0, lam_bar[1]), (lam[0] * inv, -lam[1] * inv))
    b_bar = _cmul((coef[0][..., None], coef[1][..., None]), (b_re, b_im))
    pw = [(jnp.ones_like(mag), jnp.zeros_like(mag))]
    for _ in range(tc):
        pw.append(_cmul(pw[-1], lam_bar))
    pw = (jnp.stack([q[0] for q in pw], axis=1), jnp.stack([q[1] for q in pw], axis=1))
    at = lambda d, idx: (pw[0][d, idx], pw[1][d, idx])
    lead = lambda z: jnp.moveaxis(z, -1, 0)
    cp = _cmul((lead(c_re)[:, :, :, None, :], lead(c_im)[:, :, :, None, :]),
               (jnp.transpose(pw[0][:, :tc], (3, 0, 2, 1))[..., None],
                jnp.transpose(pw[1][:, :tc], (3, 0, 2, 1))[..., None]))
    cp = tuple(z.reshape(p, 2, g, 1, tc * ch) for z in cp)
    bb = tuple(jnp.transpose(z, (2, 0, 1, 3))[..., None] for z in b_bar)
    kern = jnp.sum(bb[0] * cp[0] - bb[1] * cp[1], axis=0)
    rev = jnp.flip(kern[1].reshape(g, ch, tc, ch), axis=2).reshape(g, ch, tc * ch)
    zeros = lambda width: jnp.zeros((g, ch, width), F32)
    rows = []
    for s_ in range(tc):
        fwd_row = jnp.concatenate([zeros(s_ * ch), kern[0][..., :(tc - s_) * ch]], axis=-1)
        rev_row = jnp.concatenate([rev[..., (tc - 1 - s_) * ch:], zeros((tc - 1 - s_) * ch)], axis=-1)
        rows.append(fwd_row + rev_row)
    m = jnp.stack(rows, axis=1).reshape(g, tc * ch, tc * ch)
    skip = jnp.tile(d_skip.reshape(g, 1, ch), (1, tc, 1)).reshape(g, tc * ch)
    m = m + jnp.eye(tc * ch, dtype=F32)[None] * skip[:, :, None]
    steps = jnp.arange(tc)
    lift = lambda z: (z[0][..., None], z[1][..., None])
    e_f = _cmul(lift(at(0, tc - 1 - steps)), (b_bar[0][0][None], b_bar[1][0][None]))
    e_r = _cmul(lift(at(1, steps)), (b_bar[0][1][None], b_bar[1][1][None]))
    w_in = jnp.stack([e_f[0], e_r[0], e_f[1], e_r[1]], axis=0)
    w_in = jnp.transpose(w_in, (2, 1, 4, 0, 3)).reshape(g, tc * ch, 4 * p)
    mid = lambda z: (z[0][:, :, None, :], z[1][:, :, None, :])
    g_f = _cmul((c_re[0][None], c_im[0][None]), mid(at(0, 1 + steps)))
    g_r = _cmul((c_re[1][None], c_im[1][None]), mid(at(1, tc - steps)))
    w_re = jnp.stack([g_f[0], g_r[0]], axis=0)
    w_im = -jnp.stack([g_f[1], g_r[1]], axis=0)
    to_rows = lambda w: jnp.transpose(w, (2, 0, 4, 1, 3)).reshape(g, 2 * p, tc * ch)
    a1 = (jnp.concatenate([pw[0][0, tc], pw[0][1, tc]], axis=-1),
          jnp.concatenate([pw[1][0, tc], pw[1][1, tc]], axis=-1))
    a2 = _cmul(a1, a1)
    second = _s5_second_rows(8)
    par = tuple(jnp.where(second[None], a2[k][:, None, :], a1[k][:, None, :]) for k in range(2))
    one = tuple(jnp.broadcast_to(a1[k][:, None, :], (g, 8, 2 * p)) for k in range(2))
    a_rows = jnp.concatenate([jnp.concatenate(par, axis=-1), jnp.concatenate(one, axis=-1)], axis=1)
    return m.astype(BF16), w_in.astype(BF16), to_rows(w_re).astype(BF16), to_rows(w_im).astype(BF16), a_rows


def _s5_second_rows(nrows):
    row = lax.broadcasted_iota(I32, (nrows, 2 * S5_STATE), 0)
    lane = lax.broadcasted_iota(I32, (nrows, 2 * S5_STATE), 1)
    return (lane < S5_STATE) != ((row & 7) < 4)


def _s5_kernel(x_ref, m_ref, win_ref, wre_ref, wim_ref, a_ref, y_ref, ure_scr, uim_scr, sre_scr, sim_scr, *,
               nblock, nctx):
    p2 = 2 * S5_STATE
    nrows = nblock * 8
    second = _s5_second_rows(nrows)
    fwd = lax.broadcasted_iota(I32, (nrows, p2), 1) < S5_STATE
    for g in range(S5_GB):
        v = jnp.dot(x_ref[g], win_ref[g], preferred_element_type=F32)
        vre, vim = v[:, 0:p2], v[:, p2:2 * p2]
        a_re, a_im = a_ref[g, 8:9, 0:p2], a_ref[g, 8:9, p2:2 * p2]
        fre = jnp.where(fwd, pltpu.roll(vre, 4, 0), pltpu.roll(vre, nrows - 4, 0))
        fim = jnp.where(fwd, pltpu.roll(vim, 4, 0), pltpu.roll(vim, nrows - 4, 0))
        ure_scr[g] = vre + jnp.where(second, a_re * fre - a_im * fim, 0.0)
        uim_scr[g] = vim + jnp.where(second, a_re * fim + a_im * fre, 0.0)
    second8 = second[0:8]
    fwd8 = fwd[0:8]
    ap_re = [a_ref[g, 0:8, 0:p2] for g in range(S5_GB)]
    ap_im = [a_ref[g, 0:8, p2:2 * p2] for g in range(S5_GB)]

    def body(s, carry):
        jr = jnp.where(s < nctx, nctx - 1 - s, nblock - 1 + nctx - s)
        of = pl.multiple_of(s * 8, 8)
        orv = pl.multiple_of(jr * 8, 8)
        new = []
        for g in range(S5_GB):
            cre, cim = carry[g]
            ure = jnp.where(fwd8, ure_scr[g, pl.ds(of, 8), :], ure_scr[g, pl.ds(orv, 8), :])
            uim = jnp.where(fwd8, uim_scr[g, pl.ds(of, 8), :], uim_scr[g, pl.ds(orv, 8), :])
            zre = ap_re[g] * cre - ap_im[g] * cim + ure
            zim = ap_re[g] * cim + ap_im[g] * cre + uim
            rre, rim = pltpu.roll(zre, 4, 0), pltpu.roll(zim, 4, 0)
            ere, eim = jnp.where(second8, rre, cre), jnp.where(second8, rim, cim)
            sre_scr[g, pl.ds(of, 8), 0:S5_STATE] = ere[:, 0:S5_STATE]
            sre_scr[g, pl.ds(orv, 8), S5_STATE:p2] = ere[:, S5_STATE:p2]
            sim_scr[g, pl.ds(of, 8), 0:S5_STATE] = eim[:, 0:S5_STATE]
            sim_scr[g, pl.ds(orv, 8), S5_STATE:p2] = eim[:, S5_STATE:p2]
            new.append((jnp.where(second8, zre, rre), jnp.where(second8, zim, rim)))
        return tuple(new)

    zero = jnp.zeros((8, p2), F32)
    lax.fori_loop(0, nblock, body, tuple((zero, zero) for _ in range(S5_GB)))
    for g in range(S5_GB):
        y = (jnp.dot(x_ref[g], m_ref[g], preferred_element_type=F32)
             + jnp.dot(sre_scr[g].astype(BF16), wre_ref[g], preferred_element_type=F32)
             + jnp.dot(sim_scr[g].astype(BF16), wim_ref[g], preferred_element_type=F32))
        y_ref[g] = y.astype(BF16)


def _s5(xt, bsz, weights):
    g, rows, lanes = xt.shape
    assert 2 * bsz == 8
    nchunk = rows // bsz
    nctx = CTX_LEN // S5_TC
    assert nchunk % 2 == 0 and nctx % 2 == 0
    m, w_in, w_re, w_im, a_rows = weights
    p2 = 2 * S5_STATE
    gb = S5_GB
    wspec = lambda r, c: pl.BlockSpec((gb, r, c), lambda i: (i, 0, 0))
    return pl.pallas_call(
        functools.partial(_s5_kernel, nblock=nchunk // 2, nctx=nctx // 2),
        grid=(g // gb,),
        in_specs=[wspec(rows, lanes), wspec(lanes, lanes), wspec(lanes, 2 * p2), wspec(p2, lanes),
                  wspec(p2, lanes), wspec(16, 2 * p2)],
        out_specs=wspec(rows, lanes),
        out_shape=jax.ShapeDtypeStruct((g, rows, lanes), BF16),
        scratch_shapes=[pltpu.VMEM((gb, rows, p2), F32)] * 4,
        compiler_params=_cparams(("arbitrary",)),
        name="s5",
    )(xt, m, w_in, w_re, w_im, a_rows)


GLU_SAMPLES = 2


def _glu_kernel(y_ref, x_ref, mod_ref, n2g_ref, w_ref, b_ref, rwt_ref, x1_ref, h2_ref, lg_ref, slab, y_scr, *,
                bsz):
    dm = D_MODEL
    half = pl.program_id(1)
    nchunk = TM // S5_TC

    @pl.when(half == 0)
    def _():
        for j in range(S5_LANE_VREGS):
            for gl in range(S5_TC):
                g = (j // 2) * S5_TC + gl
                rows = y_ref[g, :, 128 * (j % 2):128 * (j % 2) + 128].astype(F32)
                slab[j, pl.ds(gl, nchunk * bsz, stride=S5_TC), :] = rows

    for k in range(GLU_SAMPLES):
        b = half * GLU_SAMPLES + k

        def chunk(c, carry):
            s0 = pl.multiple_of((c * bsz + b) * S5_TC, S5_TC)
            p = [[slab[j, pl.ds(s0 + 8 * h, 8), :] for j in range(S5_LANE_VREGS)] for h in range(2)]
            q = _segment_transpose(p)
            r0 = pl.multiple_of(c * S5_TC, S5_TC)
            for h in range(2):
                for j in range(S5_LANE_VREGS):
                    y_scr[pl.ds(r0 + 8 * h, 8), 128 * j:128 * j + 128] = q[h][j]
            return carry

        lax.fori_loop(0, nchunk, chunk, 0)
        z = jax.nn.gelu(y_scr[...]).astype(BF16)
        zz = jnp.dot(z, w_ref[...], preferred_element_type=F32) + b_ref[...]
        glu = zz[:, :dm] * jax.nn.sigmoid(zz[:, dm:])
        x1_ref[k], h2_ref[k], lg_ref[k] = _post_mixer(glu, x_ref[k], mod_ref[b, 2:3, :], mod_ref[b, 3:4, :],
                                                      mod_ref[b, 4:5, :], n2g_ref[0], rwt_ref[...])


def _glu(y, xall, modtab, layer, norm_g, glu_w, glu_b, router_w):
    bsz, s, dm = xall.shape
    n = s - CTX_LEN
    ntl = n // TM
    assert bsz == 2 * GLU_SAMPLES
    gs = GLU_SAMPLES
    rows = TM // S5_TC * bsz
    ctx_tiles = CTX_LEN // TM
    tok = pl.BlockSpec((gs, TM, dm), lambda t, h: (h, t, 0))
    x1, h2, lg = pl.pallas_call(
        functools.partial(_glu_kernel, bsz=bsz),
        grid=(ntl, bsz // gs),
        in_specs=[pl.BlockSpec((S5_GROUPS, rows, S5_TC * S5_GROUP), lambda t, h: (0, t + ctx_tiles, 0)),
                  pl.BlockSpec((gs, TM, dm), lambda t, h: (h, t + ctx_tiles, 0)),
                  pl.BlockSpec((8, 6, dm), lambda t, h: (layer, 0, 0)),
                  pl.BlockSpec((1, 1, dm), lambda t, h: (0, 0, 0)),
                  pl.BlockSpec((dm, 2 * dm), lambda t, h: (0, 0)),
                  pl.BlockSpec((1, 2 * dm), lambda t, h: (0, 0)),
                  pl.BlockSpec((N_EXPERTS, dm), lambda t, h: (0, 0))],
        out_specs=[tok, tok, pl.BlockSpec((gs, N_EXPERTS, TM), lambda t, h: (h, 0, t))],
        out_shape=[jax.ShapeDtypeStruct((bsz, n, dm), F32),
                   jax.ShapeDtypeStruct((bsz, n, dm), BF16),
                   jax.ShapeDtypeStruct((bsz, N_EXPERTS, n), F32)],
        scratch_shapes=[pltpu.VMEM((S5_LANE_VREGS, TM * bsz, 128), F32), pltpu.VMEM((TM, dm), F32)],
        compiler_params=_cparams(("arbitrary", "arbitrary")),
        name="glu",
    )(y, xall, modtab.reshape(DEPTH * 8, 6, dm), norm_g.reshape(1, 1, dm), glu_w.astype(BF16), glu_b[None, :],
      router_w.T)
    return x1, h2, jnp.transpose(lg, (1, 0, 2)).reshape(N_EXPERTS, bsz * n)


def kernel(x, c, ctx, c_ctx, mod_w, mod_b, norm1_g, norm2_g, ar_w_in, ar_w_out, lru_conv_w, lru_conv_b, lru_lam, lru_r_w, lru_r_b, lru_i_w, lru_i_b, attn_q_g, attn_k_g, attn_lam_q1, attn_lam_k1, attn_lam_q2, attn_lam_k2, attn_subln_g, s5_a_re, s5_a_im, s5_log_step, s5_b_re, s5_b_im, s5_c_re, s5_c_im, s5_d, s5_glu_w, s5_glu_b, router_w, router_bias, exp_w_gate, exp_w_up, exp_w_down, sh_w_gate, sh_w_up, sh_w_down):
    bsz, n, dm = x.shape
    assert dm == D_MODEL and ctx.shape[1] == CTX_LEN == TM and n % TM == 0 and bsz < 8
    assert mod_w.shape[0] == DEPTH == 2
    s = CTX_LEN + n
    ntb = s // TM
    modtab = _modulation(c, c_ctx, mod_w, mod_b)

    gate, xa, q, k, v = _inproj(x, ctx, modtab, norm1_g[0], ar_w_in[0], attn_q_g[0], attn_k_g[0])
    lru_args = (lru_conv_w[0], lru_conv_b[0], lru_lam[0], lru_r_w[0], lru_r_b[0], lru_i_w[0], lru_i_b[0])
    hf = _lru(xa, *lru_args, reverse=False)
    hr = _lru(xa, *lru_args, reverse=True)
    lam_init = 0.8 - 0.6 * math.exp(-0.3 * 0)
    lam_vecs = jnp.stack([attn_lam_q1[0], attn_lam_k1[0], attn_lam_q2[0], attn_lam_k2[0]], axis=0)
    yb = _attention(q, k, v, lam_vecs, attn_subln_g[0], lam_init)
    x1, h2, lgt = _outproj(hf, hr, gate, yb, x, ctx, modtab, norm2_g[0], ar_w_out[0], router_w[0])
    row0 = lambda i: jnp.where(i % ntb == 0, bsz, i // ntb)
    xall = _moe(h2.reshape(bsz * s, dm), lgt, x1.reshape(bsz * s, dm), modtab, 0, row0, router_bias[0],
                exp_w_gate, exp_w_up, exp_w_down, sh_w_gate[0], sh_w_up[0], sh_w_down[0])
    xall = xall.reshape(bsz, s, dm)

    xt = _s5_pack(xall, modtab, 1, norm1_g[1])
    weights = _s5_weights(s5_a_re[0], s5_a_im[0], s5_log_step[0], s5_b_re[0], s5_b_im[0], s5_c_re[0],
                          s5_c_im[0], s5_d[0])
    y = _s5(xt, bsz, weights)
    x1, h2, lgt = _glu(y, xall, modtab, 1, norm2_g[1], s5_glu_w[0], s5_glu_b[0], router_w[1])
    ntl = n // TM
    row1 = lambda i: i // ntl
    out = _moe(h2.reshape(bsz * n, dm), lgt, x1.reshape(bsz * n, dm), modtab, 1, row1, router_bias[1],
               exp_w_gate, exp_w_up, exp_w_down, sh_w_gate[1], sh_w_up[1], sh_w_down[1])
    return out.reshape(bsz, n, dm)
```

```python
import functools
import math

import jax
import jax.numpy as jnp
from jax import lax
from jax.experimental import pallas as pl
from jax.experimental.pallas import tpu as pltpu

F32, BF16, I32 = jnp.float32, jnp.bfloat16, jnp.int32
HIGHEST = lax.Precision.HIGHEST

D_MODEL = 1024
DEPTH = 2
GRID_W = 64
CTX_LEN = 256
EPS = 1e-6
LRU_WIDTH = 512
LRU_BLOCKS = 8
LRU_C = 8.0
ATT_HEADS = 4
ATT_DH = 64
ATT_DV = 128
ATT_QK_W = 512
ATT_WIDTH = 512
ROPE_BASE = 10000.0
EVEN_IN = 2 * LRU_WIDTH + 2 * ATT_QK_W + ATT_WIDTH
S5_GROUP = 16
S5_GROUPS = 64
S5_STATE = 64
S5_TC = 16
S5_GB = 4
N_EXPERTS = 64
TOP_K = 6
D_EXPERT = 256
ROUTED_SCALE = 2.5

TM = 256
SEG = 16
SLOT_CHUNK = 64
RMAX = -(-(TM * TOP_K + N_EXPERTS * (SEG - 1)) // SLOT_CHUNK) * SLOT_CHUNK
TR = 1024
TAIL_BITS = (TR // SEG - 1).bit_length()
ATT_TK = 2816
VMEM_LIMIT = 56 * 1024 * 1024


def _cparams(sem):
    return pltpu.CompilerParams(dimension_semantics=sem, vmem_limit_bytes=VMEM_LIMIT)


def _row_halves():
    return [slice(0, TM // 2), slice(TM // 2, TM)]


def _norm_mod(x, g, shift, scale):
    y = x * lax.rsqrt(jnp.mean(x * x, axis=-1, keepdims=True) + EPS) * g
    return y * (1.0 + scale) + shift


def _mod_kernel(c_ref, w_ref, b_ref, o_ref):
    c = c_ref[...]
    s = c * jax.nn.sigmoid(c)
    o_ref[0] = jnp.dot(s, w_ref[0], precision=HIGHEST, preferred_element_type=F32) + b_ref[0]


def _modulation(c, c_ctx, mod_w, mod_b):
    bsz, dm = c.shape
    cc = jnp.concatenate([c, c_ctx[None, :], jnp.zeros((8 - bsz - 1, dm), F32)], axis=0)
    out = pl.pallas_call(
        _mod_kernel,
        grid=(DEPTH, 6),
        in_specs=[pl.BlockSpec((8, dm), lambda l, j: (0, 0)),
                  pl.BlockSpec((1, dm, dm), lambda l, j: (l, 0, j)),
                  pl.BlockSpec((1, 1, dm), lambda l, j: (l, 0, j))],
        out_specs=pl.BlockSpec((1, 8, dm), lambda l, j: (l, 0, j)),
        out_shape=jax.ShapeDtypeStruct((DEPTH, 8, 6 * dm), F32),
        compiler_params=_cparams(("arbitrary", "arbitrary")),
        name="modulation",
    )(cc, mod_w, mod_b.reshape(DEPTH, 1, 6 * dm))
    return out.reshape(DEPTH * 8 * 6, 1, dm)


def _mod_spec(layer, part, row_fn):
    return pl.BlockSpec((1, 1, D_MODEL), lambda *ids: ((layer * 8 + row_fn(*ids)) * 6 + part, 0, 0))


def _rope_tables(n):
    rows = n // GRID_W
    r, col = jnp.meshgrid(jnp.arange(rows), jnp.arange(GRID_W), indexing="ij")
    pos = jnp.stack([r.reshape(-1), col.reshape(-1)], axis=-1).astype(F32)
    n_freq = ATT_DH // 4
    inv_freq = ROPE_BASE ** (-jnp.arange(n_freq, dtype=F32) / n_freq)
    ang = pos[:, :, None] * inv_freq
    cos, sin = jnp.cos(ang), jnp.sin(ang)
    zero = jnp.zeros_like(sin)
    cos64 = jnp.stack([cos, cos], axis=2).reshape(n, ATT_DH)
    sin_lo = jnp.stack([zero, sin], axis=2).reshape(n, ATT_DH)
    sin_hi = jnp.stack([-sin, zero], axis=2).reshape(n, ATT_DH)

    def full(tab, ctx_val):
        tab = jnp.concatenate([jnp.full((CTX_LEN, ATT_DH), ctx_val, F32), tab], axis=0)
        return jnp.concatenate([tab, tab], axis=1)

    return full(cos64, 1.0), full(sin_lo, 0.0), full(sin_hi, 0.0)


def _qk_post(t, gain, ones_bd, cos, sin_lo, sin_hi):
    ss = jnp.dot((t * t).astype(BF16), ones_bd, preferred_element_type=F32) * (1.0 / ATT_DH)
    tn = t * lax.rsqrt(ss + EPS) * gain
    w = tn.shape[1]
    return tn * cos + pltpu.roll(tn, 16, 1) * sin_lo + pltpu.roll(tn, w - 16, 1) * sin_hi


def _inproj_kernel(x_ref, c_ref, sh_ref, sc_ref, g_ref, w_ref, qg_ref, kg_ref, ones_ref,
                   cos_ref, slo_ref, shi_ref, gate_ref, xa_ref, q_ref, k_ref, v_ref):
    t = pl.program_id(1)
    lw, qw = LRU_WIDTH, ATT_QK_W
    tile4 = lambda a: jnp.concatenate([a, a, a, a], axis=1)
    ones_bd = ones_ref[...]
    for rows in _row_halves():
        x = jnp.where(t == 0, c_ref[0, rows], x_ref[0, rows])
        h = _norm_mod(x, g_ref[0], sh_ref[0], sc_ref[0])
        z = jnp.dot(h.astype(BF16), w_ref[...], preferred_element_type=F32)
        gate_ref[0, rows] = z[:, 0:lw].astype(BF16)
        xa_ref[0, rows] = z[:, lw:2 * lw].astype(BF16)
        cos, slo, shi = tile4(cos_ref[rows]), tile4(slo_ref[rows]), tile4(shi_ref[rows])
        q = _qk_post(z[:, 2 * lw:2 * lw + qw], qg_ref[...], ones_bd, cos, slo, shi)
        q_ref[0, rows] = (q * (ATT_DH ** -0.5 * math.log2(math.e))).astype(BF16)
        k = _qk_post(z[:, 2 * lw + qw:2 * lw + 2 * qw], kg_ref[...], ones_bd, cos, slo, shi)
        k_ref[0, rows] = k.astype(BF16)
        v_ref[0, rows] = z[:, 2 * lw + 2 * qw:].astype(BF16)


def _inproj(x, ctx, modtab, norm_g, w_in, q_g, k_g):
    bsz, n, dm = x.shape
    ntb = (CTX_LEN + n) // TM
    s = CTX_LEN + n
    row = lambda b, t: jnp.where(t == 0, bsz, b)
    cos, slo, shi = _rope_tables(n)
    ones_bd = jnp.kron(jnp.eye(ATT_QK_W // ATT_DH, dtype=F32), jnp.ones((ATT_DH, ATT_DH), F32)).astype(BF16)
    tile_g = lambda g: jnp.tile(g, ATT_QK_W // ATT_DH)[None, :]
    tab_spec = pl.BlockSpec((TM, 2 * ATT_DH), lambda b, t: (t, 0))
    out_spec = pl.BlockSpec((1, TM, LRU_WIDTH), lambda b, t: (b, t, 0))
    out_sds = jax.ShapeDtypeStruct((bsz, s, LRU_WIDTH), BF16)
    return pl.pallas_call(
        _inproj_kernel,
        grid=(bsz, ntb),
        in_specs=[pl.BlockSpec((1, TM, dm), lambda b, t: (b, jnp.maximum(t - 1, 0), 0)),
                  pl.BlockSpec((1, TM, dm), lambda b, t: (b, 0, 0)),
                  _mod_spec(0, 0, row), _mod_spec(0, 1, row),
                  pl.BlockSpec((1, 1, dm), lambda b, t: (0, 0, 0)),
                  pl.BlockSpec((dm, EVEN_IN), lambda b, t: (0, 0)),
                  pl.BlockSpec((1, ATT_QK_W), lambda b, t: (0, 0)),
                  pl.BlockSpec((1, ATT_QK_W), lambda b, t: (0, 0)),
                  pl.BlockSpec((ATT_QK_W, ATT_QK_W), lambda b, t: (0, 0)),
                  tab_spec, tab_spec, tab_spec],
        out_specs=[out_spec] * 5,
        out_shape=[out_sds] * 5,
        compiler_params=_cparams(("arbitrary", "arbitrary")),
        name="inproj",
    )(x, ctx, modtab, modtab, norm_g.reshape(1, 1, dm), w_in.astype(BF16), tile_g(q_g), tile_g(k_g), ones_bd,
      cos, slo, shi)


def _lru_kernel(xa_ref, xp_ref, xn_ref, cw_ref, cb_ref, lam_ref, w_ref, bias_ref, h_ref,
                a_scr, b_scr, c_scr, *, ntb, reverse):
    s = pl.program_id(1)
    ti = jnp.where(s == 0, 0, ntb - s) if reverse else s
    lw = LRU_WIDTH
    x = xa_ref[0].astype(F32)
    row = lax.broadcasted_iota(I32, (TM, lw), 0)
    has_prev = jnp.where(ti > 1, 1.0, 0.0)
    has_next = jnp.where((ti > 0) & (ti < ntb - 1), 1.0, 0.0)
    prev = xp_ref[0].astype(F32) * has_prev
    nxt = xn_ref[0].astype(F32) * has_next
    xm1 = jnp.where(row == 0, prev[7:8], pltpu.roll(x, 1, 0))
    xm2 = jnp.where(row == 0, prev[6:7], jnp.where(row == 1, prev[7:8], pltpu.roll(x, 2, 0)))
    xp1 = jnp.where(row == TM - 1, nxt[0:1], pltpu.roll(x, TM - 1, 0))
    cw = cw_ref[...]
    xc = cw[0:1] * xm2 + cw[1:2] * xm1 + cw[2:3] * x + cw[3:4] * xp1 + cb_ref[...]

    z = jnp.dot(xc.astype(BF16), w_ref[...], preferred_element_type=F32) + bias_ref[...]
    r = jax.nn.sigmoid(z[:, :lw])
    ig = jax.nn.sigmoid(z[:, lw:])
    neg_lam = -lam_ref[...]
    softplus = jnp.maximum(neg_lam, 0.0) + jnp.log1p(jnp.exp(-jnp.abs(neg_lam)))
    log_a = (-LRU_C) * r * softplus
    a = jnp.exp(log_a)
    b = jnp.sqrt(-jnp.tanh(log_a) * (a * a + 1.0)) * (ig * xc)

    r8 = row & 7
    for sft in (1, 2, 4):
        if reverse:
            a_s, b_s, m = pltpu.roll(a, TM - sft, 0), pltpu.roll(b, TM - sft, 0), r8 < 8 - sft
        else:
            a_s, b_s, m = pltpu.roll(a, sft, 0), pltpu.roll(b, sft, 0), r8 >= sft
        b = jnp.where(m, a * b_s + b, b)
        a = jnp.where(m, a * a_s, a)
    a_scr[...] = a
    b_scr[...] = b

    @pl.when(s == 0)
    def _():
        c_scr[...] = jnp.zeros_like(c_scr)

    ng = TM // 8

    def body(j, carry):
        g = (ng - 1 - j) if reverse else j
        off = pl.multiple_of(g * 8, 8)
        h = a_scr[pl.ds(off, 8), :] * carry + b_scr[pl.ds(off, 8), :]
        h_ref[0, pl.ds(off, 8), :] = h
        last = h[0:1] if reverse else h[7:8]
        return jnp.broadcast_to(last, (8, lw))

    c_scr[...] = lax.fori_loop(0, ng, body, c_scr[...], unroll=4)


def _lru(xa, conv_w, conv_b, lam, r_w, r_b, i_w, i_b, reverse):
    bsz, s, lw = xa.shape
    ntb = s // TM
    d = 1 if reverse else 0
    bd = lambda w: jax.scipy.linalg.block_diag(*[w[i] for i in range(LRU_BLOCKS)])
    w = jnp.concatenate([bd(r_w[d]), bd(i_w[d])], axis=1).astype(BF16)
    bias = jnp.concatenate([r_b[d], i_b[d]])[None, :]
    cw = jnp.concatenate([conv_w, jnp.zeros((4, lw), F32)], axis=0)
    if reverse:
        tile = lambda t: jnp.where(t == 0, 0, ntb - t)
    else:
        tile = lambda t: t
    nb8 = s // 8
    return pl.pallas_call(
        functools.partial(_lru_kernel, ntb=ntb, reverse=reverse),
        grid=(bsz, ntb),
        in_specs=[pl.BlockSpec((1, TM, lw), lambda b, t: (b, tile(t), 0)),
                  pl.BlockSpec((1, 8, lw), lambda b, t: (b, jnp.maximum(tile(t) * (TM // 8) - 1, 0), 0)),
                  pl.BlockSpec((1, 8, lw), lambda b, t: (b, jnp.minimum((tile(t) + 1) * (TM // 8), nb8 - 1), 0)),
                  pl.BlockSpec((8, lw), lambda b, t: (0, 0)),
                  pl.BlockSpec((1, lw), lambda b, t: (0, 0)),
                  pl.BlockSpec((1, lw), lambda b, t: (0, 0)),
                  pl.BlockSpec((lw, 2 * lw), lambda b, t: (0, 0)),
                  pl.BlockSpec((1, 2 * lw), lambda b, t: (0, 0))],
        out_specs=pl.BlockSpec((1, TM, lw), lambda b, t: (b, tile(t), 0)),
        out_shape=jax.ShapeDtypeStruct((bsz, s, lw), F32),
        scratch_shapes=[pltpu.VMEM((TM, lw), F32), pltpu.VMEM((TM, lw), F32), pltpu.VMEM((8, lw), F32)],
        compiler_params=_cparams(("arbitrary", "arbitrary")),
        name="lru_rev" if reverse else "lru_fwd",
    )(xa, xa, xa, cw, conv_b[None, :], lam[d][None, :], w, bias)


def _attn_kernel(lamv_ref, q_ref, k_ref, v_ref, sg_ref, o_ref, m_scr, l_scr, acc_scr, al_scr, s_scr, p_scr, *,
                 nkv, lam_init):
    t = pl.program_id(2)
    q = q_ref[0]
    dh = ATT_DH
    dv = ATT_DV
    m_scr[...] = jnp.full_like(m_scr, -jnp.inf)
    l_scr[...] = jnp.zeros_like(l_scr)
    acc_scr[...] = jnp.zeros_like(acc_scr)

    def scores(buf, off, size):
        kc = k_ref[0, pl.ds(off, size), :]
        for mi in range(2):
            s_scr[buf, mi, :, 0:size] = lax.dot_general(
                q[:, mi * dh:(mi + 1) * dh], kc[:, mi * dh:(mi + 1) * dh], (((1,), (1,)), ((), ())),
                preferred_element_type=F32)

    def softmax(buf, size):
        nlb = size // 128
        groups = [(slice(r * 16, (r + 1) * 16), mi) for r in range(TM // 16) for mi in range(2)]
        for rows, mi in groups:
            sc = s_scr[buf, mi, rows, 0:size]
            mx = functools.reduce(jnp.maximum, [sc[:, i * 128:(i + 1) * 128] for i in range(nlb)])
            m_old = m_scr[mi, rows, :]
            m_new = jnp.maximum(m_old, jnp.max(mx, axis=1, keepdims=True))
            m_scr[mi, rows, :] = m_new
            al_scr[mi, rows, :] = jnp.exp2(m_old - m_new)
        for rows, mi in groups:
            m_new = m_scr[mi, rows, :]
            ps = [jnp.exp2(s_scr[buf, mi, rows, i * 128:(i + 1) * 128] - m_new) for i in range(nlb)]
            l_scr[mi, rows, :] = al_scr[mi, rows, :] * l_scr[mi, rows, :] + functools.reduce(jnp.add, ps)
            p_scr[buf, mi, rows, 0:size] = jnp.concatenate(ps, axis=1).astype(BF16)

    def values(buf, off, size):
        vc = v_ref[0, pl.ds(off, size), :]
        pv = jnp.dot(p_scr[buf, :, :, 0:size].reshape(2 * TM, size), vc, preferred_element_type=F32)
        acc_scr[...] = al_scr[...] * acc_scr[...] + pv.reshape(2, TM, dv)

    @pl.when(t == 0)
    def _():
        scores(0, 0, CTX_LEN)
        softmax(0, CTX_LEN)
        values(0, 0, CTX_LEN)

    @pl.when(t > 0)
    def _():
        tk = ATT_TK
        nchunk = nkv // tk
        scores(0, 0, tk)
        for j in range(nchunk):
            if j + 1 < nchunk:
                scores((j + 1) % 2, (j + 1) * tk, tk)
            softmax(j % 2, tk)
            values(j % 2, j * tk, tk)

    lv = lamv_ref[...]
    lam = (jnp.exp(jnp.sum(lv[0:1] * lv[1:2], axis=1, keepdims=True))
           - jnp.exp(jnp.sum(lv[2:3] * lv[3:4], axis=1, keepdims=True)) + lam_init)
    l0 = jnp.sum(l_scr[0], axis=1, keepdims=True)
    l1 = jnp.sum(l_scr[1], axis=1, keepdims=True)
    o = acc_scr[0] / l0 - lam * (acc_scr[1] / l1)
    y = o * lax.rsqrt(jnp.mean(o * o, axis=-1, keepdims=True) + EPS) * sg_ref[...] * (1.0 - lam_init)
    o_ref[0] = y.astype(BF16)


def _attention(q, k, v, lam_vecs, subln_g, lam_init):
    bsz, s, _ = q.shape
    ntb = s // TM
    assert s % ATT_TK == 0
    dv = ATT_DV
    lamv = jnp.concatenate([jnp.pad(lam_vecs, ((0, 0), (0, dv - ATT_DH))), jnp.zeros((4, dv), F32)], axis=0)
    return pl.pallas_call(
        functools.partial(_attn_kernel, nkv=s, lam_init=lam_init),
        grid=(bsz, ATT_HEADS, ntb),
        in_specs=[pl.BlockSpec((8, dv), lambda b, h, t: (0, 0)),
                  pl.BlockSpec((1, TM, dv), lambda b, h, t: (b, t, h)),
                  pl.BlockSpec((1, s, dv), lambda b, h, t: (b, 0, h)),
                  pl.BlockSpec((1, s, dv), lambda b, h, t: (b, 0, h)),
                  pl.BlockSpec((1, dv), lambda b, h, t: (0, 0))],
        out_specs=pl.BlockSpec((1, TM, dv), lambda b, h, t: (b, t, h)),
        out_shape=jax.ShapeDtypeStruct((bsz, s, ATT_WIDTH), BF16),
        scratch_shapes=[pltpu.VMEM((2, TM, dv), F32)] * 4
        + [pltpu.VMEM((2, 2, TM, ATT_TK), F32), pltpu.VMEM((2, 2, TM, ATT_TK), BF16)],
        compiler_params=_cparams(("arbitrary", "arbitrary", "arbitrary")),
        name="diff_attention",
    )(lamv, q, k, v, subln_g[None, :])


def _post_mixer(y, x, g1, sh2, sc2, n2g, rwt):
    x1 = x + g1 * y
    h2 = _norm_mod(x1, n2g, sh2, sc2)
    logits = lax.dot_general(rwt, h2, (((1,), (1,)), ((), ())), precision=HIGHEST, preferred_element_type=F32)
    return x1, h2.astype(BF16), logits


def _outproj_kernel(hf_ref, hr_ref, gate_ref, yb_ref, x_ref, c_ref, g1_ref, sh_ref, sc_ref, n2g_ref, w_ref,
                    rwt_ref, x1_ref, h2_ref, lg_ref):
    t = pl.program_id(1)
    lw = LRU_WIDTH
    for rows in _row_halves():
        x = jnp.where(t == 0, c_ref[0, rows], x_ref[0, rows])
        ya = ((hf_ref[0, rows] + hr_ref[0, rows]) * jax.nn.gelu(gate_ref[0, rows].astype(F32))).astype(BF16)
        y = (jnp.dot(ya, w_ref[0:lw, :], preferred_element_type=F32)
             + jnp.dot(yb_ref[0, rows], w_ref[lw:, :], preferred_element_type=F32))
        x1_ref[0, rows], h2_ref[0, rows], lg_ref[:, rows] = _post_mixer(
            y, x, g1_ref[0], sh_ref[0], sc_ref[0], n2g_ref[0], rwt_ref[...])


def _outproj(hf, hr, gate, yb, x, ctx, modtab, norm_g, w_out, router_w):
    bsz, s, lw = hf.shape
    dm = D_MODEL
    ntb = s // TM
    row = lambda b, t: jnp.where(t == 0, bsz, b)
    half = pl.BlockSpec((1, TM, lw), lambda b, t: (b, t, 0))
    return pl.pallas_call(
        _outproj_kernel,
        grid=(bsz, ntb),
        in_specs=[half, half, half, half,
                  pl.BlockSpec((1, TM, dm), lambda b, t: (b, jnp.maximum(t - 1, 0), 0)),
                  pl.BlockSpec((1, TM, dm), lambda b, t: (b, 0, 0)),
                  _mod_spec(0, 2, row), _mod_spec(0, 3, row), _mod_spec(0, 4, row),
                  pl.BlockSpec((1, 1, dm), lambda b, t: (0, 0, 0)),
                  pl.BlockSpec((2 * lw, dm), lambda b, t: (0, 0)),
                  pl.BlockSpec((N_EXPERTS, dm), lambda b, t: (0, 0))],
        out_specs=[pl.BlockSpec((1, TM, dm), lambda b, t: (b, t, 0)),
                   pl.BlockSpec((1, TM, dm), lambda b, t: (b, t, 0)),
                   pl.BlockSpec((N_EXPERTS, TM), lambda b, t: (0, b * ntb + t))],
        out_shape=[jax.ShapeDtypeStruct((bsz, s, dm), F32),
                   jax.ShapeDtypeStruct((bsz, s, dm), BF16),
                   jax.ShapeDtypeStruct((N_EXPERTS, bsz * s), F32)],
        compiler_params=_cparams(("arbitrary", "arbitrary")),
        name="outproj",
    )(hf, hr, gate, yb, x, ctx, modtab, modtab, modtab, norm_g.reshape(1, 1, dm), w_out.astype(BF16),
      router_w.T)


def _route_kernel(lg_ref, bias_ref, tri_ref, low_ref, pos_ref, gate_ref, cnt_ref):
    ne = N_EXPERTS
    aff = jax.nn.sigmoid(lg_ref[...])
    work = aff + bias_ref[:, 0:1]
    eidx = lax.broadcasted_iota(I32, (ne, TM), 0)
    sels = []
    for _ in range(TOP_K):
        mx = jnp.max(work, axis=0, keepdims=True)
        am = jnp.min(jnp.where(work == mx, eidx, ne), axis=0, keepdims=True)
        sk = eidx == am
        sels.append(sk)
        work = jnp.where(sk, -jnp.inf, work)
    sel = sels[0]
    for sk in sels[1:]:
        sel = sel | sk
    self = jnp.where(sel, 1.0, 0.0)
    s_sel = aff * self
    gates = s_sel / jnp.sum(s_sel, axis=0, keepdims=True) * ROUTED_SCALE
    rank = jnp.dot(self.astype(BF16), tri_ref[...], preferred_element_type=F32)
    cnt = jnp.sum(self, axis=1, keepdims=True)
    cnt_seg = jnp.floor((cnt + (SEG - 1)) * (1.0 / SEG)) * SEG
    cnt_b = jnp.broadcast_to(cnt_seg, (ne, 128))
    seg_off = jnp.dot(low_ref[...], cnt_b.astype(BF16), preferred_element_type=F32)
    lpos = seg_off[:, 0:1] + rank
    pos_rows, gate_rows = [], []
    for sk in sels:
        pos_rows.append(jnp.sum(jnp.where(sk, lpos, 0.0), axis=0, keepdims=True))
        gate_rows.append(jnp.sum(jnp.where(sk, gates, 0.0), axis=0, keepdims=True))
    for _ in range(8 - TOP_K):
        pos_rows.append(jnp.full((1, TM), -1.0, F32))
        gate_rows.append(jnp.zeros((1, TM), F32))
    pos_ref[...] = jnp.concatenate(pos_rows, axis=0).astype(I32)
    gate_ref[...] = jnp.concatenate(gate_rows, axis=0)
    cnt_ref[0] = cnt_b.astype(I32)


def _route(lgt, router_bias):
    ne, t = lgt.shape
    nt = t // TM
    tri = jnp.triu(jnp.ones((TM, TM), F32), k=1).astype(BF16)
    low = jnp.tril(jnp.ones((ne, ne), F32), k=-1).astype(BF16)
    bias = jnp.broadcast_to(router_bias[:, None], (ne, 128))
    pos, gate, cnt = pl.pallas_call(
        _route_kernel,
        grid=(nt,),
        in_specs=[pl.BlockSpec((ne, TM), lambda i: (0, i)),
                  pl.BlockSpec((ne, 128), lambda i: (0, 0)),
                  pl.BlockSpec((TM, TM), lambda i: (0, 0)),
                  pl.BlockSpec((ne, ne), lambda i: (0, 0))],
        out_specs=[pl.BlockSpec((8, TM), lambda i: (0, i)),
                   pl.BlockSpec((8, TM), lambda i: (0, i)),
                   pl.BlockSpec((1, ne, 128), lambda i: (i, 0, 0))],
        out_shape=[jax.ShapeDtypeStruct((8, t), I32),
                   jax.ShapeDtypeStruct((8, t), F32),
                   jax.ShapeDtypeStruct((nt, ne, 128), I32)],
        compiler_params=_cparams(("arbitrary",)),
        name="route",
    )(lgt, bias, tri, low)
    return pos, gate, cnt[:, :, 0]


def _moe_layout(cnt_seg, nblk_max):
    tot = jnp.sum(cnt_seg, axis=0)
    region = (tot + TR - 1) // TR * TR
    region_end = jnp.cumsum(region)
    goff = (region_end - region)[None, :] + jnp.cumsum(cnt_seg, axis=0) - cnt_seg
    packed = ((goff // SEG) << 5) | (cnt_seg // SEG)
    nblk = region_end[-1] // TR
    blk = jnp.arange(nblk_max, dtype=I32)
    blk_e = jnp.sum((region_end[None, :] // TR <= blk[:, None]).astype(I32), axis=1)
    blk_e = jnp.minimum(blk_e, N_EXPERTS - 1).astype(I32)
    tile_info = jnp.sum(cnt_seg, axis=1) | (jnp.any(cnt_seg >= 64, axis=1).astype(I32) << 16)
    tails = (((region_end - region + tot) // SEG) << TAIL_BITS) | ((region - tot) // SEG)
    tails = jnp.concatenate([tails, nblk[None]]).astype(I32)
    return packed.reshape(-1).astype(I32), tile_info.astype(I32), tails, blk_e, nblk.astype(I32).reshape(1)


def _segment_copy(stage, hbm, sem, to_sorted, lo, go, size):
    lo, go = pl.multiple_of(lo, SEG), pl.multiple_of(go, SEG)
    a, b = stage.at[pl.ds(lo, size)], hbm.at[pl.ds(go, size)]
    return pltpu.make_async_copy(a, b, sem) if to_sorted else pltpu.make_async_copy(b, a, sem)


def _segment_unpack(pk):
    return (pk & 31) * SEG, (pk >> 5) * SEG


def _segment_starts(pk_ref, tile, stage, hbm, sem, to_sorted, valid=None):
    loff = [0]

    def step(e):
        cnt, goff = _segment_unpack(pk_ref[tile * N_EXPERTS + e])
        done = (cnt >> 6) << 6
        for size in (32, 16):
            bit = (cnt & size) != 0
            if valid is not None:
                bit = bit & valid

            @pl.when(bit)
            def _(lo=loff[0] + done, go=goff + done, size=size):
                _segment_copy(stage, hbm, sem, to_sorted, lo, go, size).start()

            done = done + jnp.where(bit, size, 0)
        loff[0] = loff[0] + cnt

    return [functools.partial(step, e) for e in range(N_EXPERTS)]


def _interleave(*step_lists):
    total = max(len(steps) for steps in step_lists)
    done = [0] * len(step_lists)
    for t in range(1, total + 1):
        for k, steps in enumerate(step_lists):
            upto = len(steps) * t // total
            for step in steps[done[k]:upto]:
                step()
            done[k] = upto


def _segment_starts_long(pk_ref, tile, info, stage, hbm, sem, to_sorted):
    @pl.when((info >> 16) != 0)
    def _():
        def expert(e, loff):
            cnt, goff = _segment_unpack(pk_ref[tile * N_EXPERTS + e])

            def chunk(j, carry):
                _segment_copy(stage, hbm, sem, to_sorted, loff + j * 64, goff + j * 64, 64).start()
                return carry

            lax.fori_loop(0, cnt >> 6, chunk, 0)
            return loff + cnt

        lax.fori_loop(0, N_EXPERTS, expert, 0)


def _segment_wait(total, stage, hbm, sem, to_sorted):
    size = 1 << (RMAX.bit_length() - 1)
    while size >= SEG:
        @pl.when((total & size) != 0)
        def _(size=size):
            _segment_copy(stage, hbm, sem, to_sorted, 0, 0, size).wait()

        size //= 2


def _fill_slot_matrix(dst_ref, pos, weight_rows=None):
    ch = SLOT_CHUNK
    riota = lax.broadcasted_iota(I32, (ch, TM), 0)

    def step(c):
        local = pos - c * ch
        out = jnp.zeros((ch, TM), F32)
        for k in range(TOP_K):
            w = 1.0 if weight_rows is None else weight_rows[k:k + 1]
            out = jnp.where(riota == local[k:k + 1], w, out)
        dst_ref[c * ch:(c + 1) * ch, :] = out.astype(BF16)

    return [functools.partial(step, c) for c in range(RMAX // ch)]


def _zero_fill(tail_ref, zeros, xs_ref, sem, fn):
    def expert(e, carry):
        pk = tail_ref[e]
        off = (pk >> TAIL_BITS) * SEG

        def chunk(j, c):
            dst = xs_ref.at[pl.ds(pl.multiple_of(off + j * SEG, SEG), SEG)]
            fn(pltpu.make_async_copy(zeros.at[pl.ds(0, SEG)], dst, sem))
            return c

        return lax.fori_loop(0, pk & ((1 << TAIL_BITS) - 1), chunk, carry)

    lax.fori_loop(0, N_EXPERTS, expert, 0)

    def block(j, carry):
        fn(pltpu.make_async_copy(zeros, xs_ref.at[pl.ds(pl.multiple_of(j * TR, TR), TR)], sem))
        return carry

    lax.fori_loop(tail_ref[N_EXPERTS], xs_ref.shape[0] // TR, block, 0)


def _dispatch_kernel(pk_ref, tot_ref, tail_ref, pos_ref, h_ref, xs_ref, stage, zeros, slots, sem, zsem):
    i = pl.program_id(0)
    slot = i % 2

    @pl.when(i == 0)
    def _():
        zeros[...] = jnp.zeros_like(zeros)
        _zero_fill(tail_ref, zeros, xs_ref, zsem, lambda c: c.start())

    for step in _fill_slot_matrix(slots, pos_ref[...]):
        step()
    stage[slot] = jnp.dot(slots[...], h_ref[...], preferred_element_type=F32).astype(BF16)
    prev_rows = jnp.where(i > 0, tot_ref[jnp.maximum(i - 1, 0)] & 0xFFFF, 0)
    _segment_wait(prev_rows, stage.at[1 - slot], xs_ref, sem.at[1 - slot], True)
    this = (stage.at[slot], xs_ref, sem.at[slot], True)
    for step in _segment_starts(pk_ref, i, *this):
        step()
    _segment_starts_long(pk_ref, i, tot_ref[i], *this)

    @pl.when(i == pl.num_programs(0) - 1)
    def _():
        _segment_wait(tot_ref[i] & 0xFFFF, *this)
        _zero_fill(tail_ref, zeros, xs_ref, zsem, lambda c: c.wait())


def _dispatch(packed, tile_rows, tails, pos, h2, nrows):
    t, dm = h2.shape
    nt = t // TM
    return pl.pallas_call(
        _dispatch_kernel,
        grid_spec=pltpu.PrefetchScalarGridSpec(
            num_scalar_prefetch=3,
            grid=(nt,),
            in_specs=[pl.BlockSpec((8, TM), lambda i, pk, tot, tail: (0, i)),
                      pl.BlockSpec((TM, dm), lambda i, pk, tot, tail: (i, 0))],
            out_specs=pl.BlockSpec(memory_space=pl.ANY),
            scratch_shapes=[pltpu.VMEM((2, RMAX, dm), BF16), pltpu.VMEM((TR, dm), BF16),
                            pltpu.VMEM((RMAX, TM), BF16),
                            pltpu.SemaphoreType.DMA((2,)), pltpu.SemaphoreType.DMA]),
        out_shape=jax.ShapeDtypeStruct((nrows, dm), BF16),
        compiler_params=_cparams(("arbitrary",)),
        name="moe_dispatch",
    )(packed, tile_rows, tails, pos, h2)


def _expert_kernel(be_ref, nb_ref, x_ref, wg_ref, wu_ref, wd_ref, y_ref, wg_s, wu_s, wd_s):
    j = pl.program_id(0)
    changed = be_ref[j] != be_ref[jnp.maximum(j - 1, 0)]

    @pl.when((j == 0) | changed)
    def _():
        wg_s[...] = wg_ref[0].astype(BF16)
        wu_s[...] = wu_ref[0].astype(BF16)
        wd_s[...] = wd_ref[0].astype(BF16)

    @pl.when(j < nb_ref[0])
    def _():
        x = x_ref[...]
        g = jnp.dot(x, wg_s[...], preferred_element_type=F32)
        u = jnp.dot(x, wu_s[...], preferred_element_type=F32)
        a = (g * jax.nn.sigmoid(g) * u).astype(BF16)
        y_ref[...] = jnp.dot(a, wd_s[...], preferred_element_type=F32).astype(BF16)


def _experts(blk_e, nblk, xs, layer, w_gate, w_up, w_down):
    nrows, dm = xs.shape
    nblk_max = nrows // TR
    de = D_EXPERT
    row_blk = lambda j, be, nb: (jnp.minimum(j, nb[0] - 1), 0)
    return pl.pallas_call(
        _expert_kernel,
        grid_spec=pltpu.PrefetchScalarGridSpec(
            num_scalar_prefetch=2,
            grid=(nblk_max,),
            in_specs=[pl.BlockSpec((TR, dm), row_blk),
                      pl.BlockSpec((None, 1, dm, de), lambda j, be, nb: (layer, be[j], 0, 0)),
                      pl.BlockSpec((None, 1, dm, de), lambda j, be, nb: (layer, be[j], 0, 0)),
                      pl.BlockSpec((None, 1, de, dm), lambda j, be, nb: (layer, be[j], 0, 0))],
            out_specs=pl.BlockSpec((TR, dm), row_blk),
            scratch_shapes=[pltpu.VMEM((dm, de), BF16), pltpu.VMEM((dm, de), BF16), pltpu.VMEM((de, dm), BF16)]),
        out_shape=jax.ShapeDtypeStruct((nrows, dm), BF16),
        input_output_aliases={2: 0},
        compiler_params=_cparams(("arbitrary",)),
        name="moe_experts",
    )(blk_e, nblk, xs, w_gate, w_up, w_down)


def _combine_kernel(pk_ref, tot_ref, pos_ref, gate_ref, h_ref, x1_ref, g2_ref, wsg_ref, wsu_ref, wsd_ref, ys_ref,
                    o_ref, stage, gates, sem):
    i = pl.program_id(0)
    slot = i % 2
    last = pl.num_programs(0) - 1

    @pl.when(i == 0)
    def _():
        stage[...] = jnp.zeros_like(stage)
        for step in _segment_starts(pk_ref, 0, stage.at[0], ys_ref, sem.at[0], False):
            step()
        _segment_starts_long(pk_ref, 0, tot_ref[0], stage.at[0], ys_ref, sem.at[0], False)

    nxt = jnp.minimum(i + 1, last)
    _interleave(_fill_slot_matrix(gates, pos_ref[...], gate_ref[...]),
                _segment_starts(pk_ref, nxt, stage.at[1 - slot], ys_ref, sem.at[1 - slot], False))
    h = h_ref[...]
    g = jnp.dot(h, wsg_ref[...], preferred_element_type=F32)
    u = jnp.dot(h, wsu_ref[...], preferred_element_type=F32)
    shared = jnp.dot((g * jax.nn.sigmoid(g) * u).astype(BF16), wsd_ref[...], preferred_element_type=F32)
    _segment_wait(tot_ref[i] & 0xFFFF, stage.at[slot], ys_ref, sem.at[slot], False)
    routed = lax.dot_general(gates[...], stage[slot], (((0,), (0,)), ((), ())), preferred_element_type=F32)
    o_ref[...] = x1_ref[...] + g2_ref[0] * (routed + shared)
    _segment_starts_long(pk_ref, nxt, tot_ref[nxt], stage.at[1 - slot], ys_ref, sem.at[1 - slot], False)

    @pl.when(i == last)
    def _():
        _segment_wait(tot_ref[i] & 0xFFFF, stage.at[1 - slot], ys_ref, sem.at[1 - slot], False)


def _combine(packed, tile_rows, pos, gate, h2, x1, modtab, layer, row_fn, ys, ws_gate, ws_up, ws_down):
    t, dm = h2.shape
    nt = t // TM
    de = D_EXPERT
    const = lambda shape: pl.BlockSpec(shape, lambda i, pk, tot: (0,) * len(shape))
    return pl.pallas_call(
        _combine_kernel,
        grid_spec=pltpu.PrefetchScalarGridSpec(
            num_scalar_prefetch=2,
            grid=(nt,),
            in_specs=[pl.BlockSpec((8, TM), lambda i, pk, tot: (0, i)),
                      pl.BlockSpec((8, TM), lambda i, pk, tot: (0, i)),
                      pl.BlockSpec((TM, dm), lambda i, pk, tot: (i, 0)),
                      pl.BlockSpec((TM, dm), lambda i, pk, tot: (i, 0)),
                      pl.BlockSpec((1, 1, dm), lambda i, pk, tot: ((layer * 8 + row_fn(i)) * 6 + 5, 0, 0)),
                      const((dm, de)), const((dm, de)), const((de, dm)),
                      pl.BlockSpec(memory_space=pl.ANY)],
            out_specs=pl.BlockSpec((TM, dm), lambda i, pk, tot: (i, 0)),
            scratch_shapes=[pltpu.VMEM((2, RMAX, dm), BF16), pltpu.VMEM((RMAX, TM), BF16),
                            pltpu.SemaphoreType.DMA((2,))]),
        out_shape=jax.ShapeDtypeStruct((t, dm), F32),
        compiler_params=_cparams(("arbitrary",)),
        name="moe_combine",
    )(packed, tile_rows, pos, gate, h2, x1, modtab, ws_gate.astype(BF16), ws_up.astype(BF16),
      ws_down.astype(BF16), ys)


def _moe(h2, lgt, x1, modtab, layer, row_fn, router_bias, w_gate, w_up, w_down, ws_gate, ws_up, ws_down):
    t = h2.shape[0]
    nt = t // TM
    max_rows = t * TOP_K + nt * N_EXPERTS * (SEG - 1) + N_EXPERTS * (TR - 1)
    nblk_max = -(-max_rows // TR)
    pos, gate, cnt_seg = _route(lgt, router_bias)
    packed, tile_rows, tails, blk_e, nblk = _moe_layout(cnt_seg, nblk_max)
    xs = _dispatch(packed, tile_rows, tails, pos, h2, nblk_max * TR)
    ys = _experts(blk_e, nblk, xs, layer, w_gate, w_up, w_down)
    return _combine(packed, tile_rows, pos, gate, h2, x1, modtab, layer, row_fn, ys, ws_gate, ws_up, ws_down)


S5_LANE_VREGS = D_MODEL // 128


def _segment_transpose(p):
    row = lax.broadcasted_iota(I32, (8, 128), 0)
    seg = lax.broadcasted_iota(I32, (8, 128), 1) // S5_GROUP
    for s in (4, 2, 1):
        m_up = ((row & s) == 0) & ((seg & s) != 0)
        m_dn = ((row & s) != 0) & ((seg & s) == 0)

        def swap(x):
            if s == 4:
                return jnp.where(m_up | m_dn, pltpu.roll(pltpu.roll(x, 4, 0), 4 * S5_GROUP, 1), x)
            up = pltpu.roll(pltpu.roll(x, 8 - s, 0), S5_GROUP * s, 1)
            dn = pltpu.roll(pltpu.roll(x, s, 0), 128 - S5_GROUP * s, 1)
            return jnp.where(m_up, up, jnp.where(m_dn, dn, x))

        p = [[swap(x) for x in half] for half in p]
    return [[p[1 - h][j - 1 + 2 * h] if (j % 2) != h else p[h][j] for j in range(S5_LANE_VREGS)] for h in range(2)]


def _s5_pack_kernel(x_ref, mod_ref, g_ref, xt_ref, u_scr, slab):
    t = pl.program_id(0)
    bsz = x_ref.shape[0]
    nchunk = TM // S5_TC
    for b in range(bsz):
        shift = jnp.where(t == 0, mod_ref[bsz, 0:1, :], mod_ref[b, 0:1, :])
        scale = jnp.where(t == 0, mod_ref[bsz, 1:2, :], mod_ref[b, 1:2, :])
        u_scr[b] = _norm_mod(x_ref[b], g_ref[0], shift, scale)

    def chunk(c, carry):
        r0 = pl.multiple_of(c * S5_TC, S5_TC)
        for b in range(bsz):
            p = [[u_scr[b, pl.ds(r0 + 8 * h, 8), 128 * j:128 * j + 128] for j in range(S5_LANE_VREGS)]
                 for h in range(2)]
            q = _segment_transpose(p)
            s0 = pl.multiple_of((c * bsz + b) * S5_TC, S5_TC)
            for h in range(2):
                for j in range(S5_LANE_VREGS):
                    slab[j, pl.ds(s0 + 8 * h, 8), :] = q[h][j]
        return carry

    lax.fori_loop(0, nchunk, chunk, 0)
    for j in range(S5_LANE_VREGS):
        for gl in range(S5_TC):
            g = (j // 2) * S5_TC + gl
            rows = slab[j, pl.ds(gl, nchunk * bsz, stride=S5_TC), :]
            xt_ref[g, :, 128 * (j % 2):128 * (j % 2) + 128] = rows.astype(BF16)


def _s5_pack(xall, modtab, layer, norm_g):
    bsz, s, dm = xall.shape
    ntb = s // TM
    rows = TM // S5_TC * bsz
    return pl.pallas_call(
        _s5_pack_kernel,
        grid=(ntb,),
        in_specs=[pl.BlockSpec((bsz, TM, dm), lambda t: (0, t, 0)),
                  pl.BlockSpec((8, 6, dm), lambda t: (layer, 0, 0)),
                  pl.BlockSpec((1, 1, dm), lambda t: (0, 0, 0))],
        out_specs=pl.BlockSpec((S5_GROUPS, rows, S5_TC * S5_GROUP), lambda t: (0, t, 0)),
        out_shape=jax.ShapeDtypeStruct((S5_GROUPS, ntb * rows, S5_TC * S5_GROUP), BF16),
        scratch_shapes=[pltpu.VMEM((bsz, TM, dm), F32), pltpu.VMEM((S5_LANE_VREGS, TM * bsz, 128), F32)],
        compiler_params=_cparams(("arbitrary",)),
        name="s5_pack",
    )(xall, modtab.reshape(DEPTH * 8, 6, dm), norm_g.reshape(1, 1, dm))


def _cmul(x, y):
    return x[0] * y[0] - x[1] * y[1], x[0] * y[1] + x[1] * y[0]


def _s5_weights(a_re, a_im, log_step, b_re, b_im, c_re, c_im, d_skip):
    tc, g, p, ch = S5_TC, S5_GROUPS, S5_STATE, S5_GROUP
    lam = (jnp.minimum(a_re, -1e-4), a_im)
    step = jnp.exp(log_step)
    mag = jnp.exp(lam[0] * step)
    lam_bar = (mag * jnp.cos(lam[1] * step), mag * jnp.sin(lam[1] * step))
    inv = 1.0 / (lam[0] * lam[0] + lam[1] * lam[1])
    coef = _cmul((lam_bar[0] - 1.0, lam_bar[1]), (lam[0] * inv, -lam[1] * inv))
    b_bar = _cmul((coef[0][..., None], coef[1][..., None]), (b_re, b_im))
    pw = [(jnp.ones_like(mag), jnp.zeros_like(mag))]
    for _ in range(tc):
        pw.append(_cmul(pw[-1], lam_bar))
    pw = (jnp.stack([q[0] for q in pw], axis=1), jnp.stack([q[1] for q in pw], axis=1))
    at = lambda d, idx: (pw[0][d, idx], pw[1][d, idx])
    lead = lambda z: jnp.moveaxis(z, -1, 0)
    cp = _cmul((lead(c_re)[:, :, :, None, :], lead(c_im)[:, :, :, None, :]),
               (jnp.transpose(pw[0][:, :tc], (3, 0, 2, 1))[..., None],
                jnp.transpose(pw[1][:, :tc], (3, 0, 2, 1))[..., None]))
    cp = tuple(z.reshape(p, 2, g, 1, tc * ch) for z in cp)
    bb = tuple(jnp.transpose(z, (2, 0, 1, 3))[..., None] for z in b_bar)
    kern = jnp.sum(bb[0] * cp[0] - bb[1] * cp[1], axis=0)
    rev = jnp.flip(kern[1].reshape(g, ch, tc, ch), axis=2).reshape(g, ch, tc * ch)
    zeros = lambda width: jnp.zeros((g, ch, width), F32)
    rows = []
    for s_ in range(tc):
        fwd_row = jnp.concatenate([zeros(s_ * ch), kern[0][..., :(tc - s_) * ch]], axis=-1)
        rev_row = jnp.concatenate([rev[..., (tc - 1 - s_) * ch:], zeros((tc - 1 - s_) * ch)], axis=-1)
        rows.append(fwd_row + rev_row)
    m = jnp.stack(rows, axis=1).reshape(g, tc * ch, tc * ch)
    skip = jnp.tile(d_skip.reshape(g, 1, ch), (1, tc, 1)).reshape(g, tc * ch)
    m = m + jnp.eye(tc * ch, dtype=F32)[None] * skip[:, :, None]
    steps = jnp.arange(tc)
    lift = lambda z: (z[0][..., None], z[1][..., None])
    e_f = _cmul(lift(at(0, tc - 1 - steps)), (b_bar[0][0][None], b_bar[1][0][None]))
    e_r = _cmul(lift(at(1, steps)), (b_bar[0][1][None], b_bar[1][1][None]))
    w_in = jnp.stack([e_f[0], e_r[0], e_f[1], e_r[1]], axis=0)
    w_in = jnp.transpose(w_in, (2, 1, 4, 0, 3)).reshape(g, tc * ch, 4 * p)
    mid = lambda z: (z[0][:, :, None, :], z[1][:, :, None, :])
    g_f = _cmul((c_re[0][None], c_im[0][None]), mid(at(0, 1 + steps)))
    g_r = _cmul((c_re[1][None], c_im[1][None]), mid(at(1, tc - steps)))
    w_re = jnp.stack([g_f[0], g_r[0]], axis=0)
    w_im = -jnp.stack([g_f[1], g_r[1]], axis=0)
    to_rows = lambda w: jnp.transpose(w, (2, 0, 4, 1, 3)).reshape(g, 2 * p, tc * ch)
    a1 = (jnp.concatenate([pw[0][0, tc], pw[0][1, tc]], axis=-1),
          jnp.concatenate([pw[1][0, tc], pw[1][1, tc]], axis=-1))
    a2 = _cmul(a1, a1)
    second = _s5_second_rows(8)
    par = tuple(jnp.where(second[None], a2[k][:, None, :], a1[k][:, None, :]) for k in range(2))
    one = tuple(jnp.broadcast_to(a1[k][:, None, :], (g, 8, 2 * p)) for k in range(2))
    a_rows = jnp.concatenate([jnp.concatenate(par, axis=-1), jnp.concatenate(one, axis=-1)], axis=1)
    return m.astype(BF16), w_in.astype(BF16), to_rows(w_re).astype(BF16), to_rows(w_im).astype(BF16), a_rows


def _s5_second_rows(nrows):
    row = lax.broadcasted_iota(I32, (nrows, 2 * S5_STATE), 0)
    lane = lax.broadcasted_iota(I32, (nrows, 2 * S5_STATE), 1)
    return (lane < S5_STATE) != ((row & 7) < 4)


def _s5_kernel(x_ref, m_ref, win_ref, wre_ref, wim_ref, a_ref, y_ref, ure_scr, uim_scr, sre_scr, sim_scr, *,
               nblock, nctx):
    p2 = 2 * S5_STATE
    nrows = nblock * 8
    second = _s5_second_rows(nrows)
    fwd = lax.broadcasted_iota(I32, (nrows, p2), 1) < S5_STATE
    for g in range(S5_GB):
        v = jnp.dot(x_ref[g], win_ref[g], preferred_element_type=F32)
        vre, vim = v[:, 0:p2], v[:, p2:2 * p2]
        a_re, a_im = a_ref[g, 8:9, 0:p2], a_ref[g, 8:9, p2:2 * p2]
        fre = jnp.where(fwd, pltpu.roll(vre, 4, 0), pltpu.roll(vre, nrows - 4, 0))
        fim = jnp.where(fwd, pltpu.roll(vim, 4, 0), pltpu.roll(vim, nrows - 4, 0))
        ure_scr[g] = vre + jnp.where(second, a_re * fre - a_im * fim, 0.0)
        uim_scr[g] = vim + jnp.where(second, a_re * fim + a_im * fre, 0.0)
    second8 = second[0:8]
    fwd8 = fwd[0:8]
    ap_re = [a_ref[g, 0:8, 0:p2] for g in range(S5_GB)]
    ap_im = [a_ref[g, 0:8, p2:2 * p2] for g in range(S5_GB)]

    def body(s, carry):
        jr = jnp.where(s < nctx, nctx - 1 - s, nblock - 1 + nctx - s)
        of = pl.multiple_of(s * 8, 8)
        orv = pl.multiple_of(jr * 8, 8)
        new = []
        for g in range(S5_GB):
            cre, cim = carry[g]
            ure = jnp.where(fwd8, ure_scr[g, pl.ds(of, 8), :], ure_scr[g, pl.ds(orv, 8), :])
            uim = jnp.where(fwd8, uim_scr[g, pl.ds(of, 8), :], uim_scr[g, pl.ds(orv, 8), :])
            zre = ap_re[g] * cre - ap_im[g] * cim + ure
            zim = ap_re[g] * cim + ap_im[g] * cre + uim
            rre, rim = pltpu.roll(zre, 4, 0), pltpu.roll(zim, 4, 0)
            ere, eim = jnp.where(second8, rre, cre), jnp.where(second8, rim, cim)
            sre_scr[g, pl.ds(of, 8), 0:S5_STATE] = ere[:, 0:S5_STATE]
            sre_scr[g, pl.ds(orv, 8), S5_STATE:p2] = ere[:, S5_STATE:p2]
            sim_scr[g, pl.ds(of, 8), 0:S5_STATE] = eim[:, 0:S5_STATE]
            sim_scr[g, pl.ds(orv, 8), S5_STATE:p2] = eim[:, S5_STATE:p2]
            new.append((jnp.where(second8, zre, rre), jnp.where(second8, zim, rim)))
        return tuple(new)

    zero = jnp.zeros((8, p2), F32)
    lax.fori_loop(0, nblock, body, tuple((zero, zero) for _ in range(S5_GB)))
    for g in range(S5_GB):
        y = (jnp.dot(x_ref[g], m_ref[g], preferred_element_type=F32)
             + jnp.dot(sre_scr[g].astype(BF16), wre_ref[g], preferred_element_type=F32)
             + jnp.dot(sim_scr[g].astype(BF16), wim_ref[g], preferred_element_type=F32))
        y_ref[g] = y.astype(BF16)


def _s5(xt, bsz, weights):
    g, rows, lanes = xt.shape
    assert 2 * bsz == 8
    nchunk = rows // bsz
    nctx = CTX_LEN // S5_TC
    assert nchunk % 2 == 0 and nctx % 2 == 0
    m, w_in, w_re, w_im, a_rows = weights
    p2 = 2 * S5_STATE
    gb = S5_GB
    wspec = lambda r, c: pl.BlockSpec((gb, r, c), lambda i: (i, 0, 0))
    return pl.pallas_call(
        functools.partial(_s5_kernel, nblock=nchunk // 2, nctx=nctx // 2),
        grid=(g // gb,),
        in_specs=[wspec(rows, lanes), wspec(lanes, lanes), wspec(lanes, 2 * p2), wspec(p2, lanes),
                  wspec(p2, lanes), wspec(16, 2 * p2)],
        out_specs=wspec(rows, lanes),
        out_shape=jax.ShapeDtypeStruct((g, rows, lanes), BF16),
        scratch_shapes=[pltpu.VMEM((gb, rows, p2), F32)] * 4,
        compiler_params=_cparams(("arbitrary",)),
        name="s5",
    )(xt, m, w_in, w_re, w_im, a_rows)


GLU_SAMPLES = 2


def _glu_kernel(y_ref, x_ref, mod_ref, n2g_ref, w_ref, b_ref, rwt_ref, x1_ref, h2_ref, lg_ref, slab, y_scr, *,
                bsz):
    dm = D_MODEL
    half = pl.program_id(1)
    nchunk = TM // S5_TC

    @pl.when(half == 0)
    def _():
        for j in range(S5_LANE_VREGS):
            for gl in range(S5_TC):
                g = (j // 2) * S5_TC + gl
                rows = y_ref[g, :, 128 * (j % 2):128 * (j % 2) + 128].astype(F32)
                slab[j, pl.ds(gl, nchunk * bsz, stride=S5_TC), :] = rows

    for k in range(GLU_SAMPLES):
        b = half * GLU_SAMPLES + k

        def chunk(c, carry):
            s0 = pl.multiple_of((c * bsz + b) * S5_TC, S5_TC)
            p = [[slab[j, pl.ds(s0 + 8 * h, 8), :] for j in range(S5_LANE_VREGS)] for h in range(2)]
            q = _segment_transpose(p)
            r0 = pl.multiple_of(c * S5_TC, S5_TC)
            for h in range(2):
                for j in range(S5_LANE_VREGS):
                    y_scr[pl.ds(r0 + 8 * h, 8), 128 * j:128 * j + 128] = q[h][j]
            return carry

        lax.fori_loop(0, nchunk, chunk, 0)
        for rows in _row_halves():
            z = jax.nn.gelu(y_scr[rows]).astype(BF16)
            zz = jnp.dot(z, w_ref[...], preferred_element_type=F32) + b_ref[...]
            glu = zz[:, :dm] * jax.nn.sigmoid(zz[:, dm:])
            x1_ref[k, rows], h2_ref[k, rows], lg_ref[k, :, rows] = _post_mixer(
                glu, x_ref[k, rows], mod_ref[b, 2:3, :], mod_ref[b, 3:4, :], mod_ref[b, 4:5, :], n2g_ref[0],
                rwt_ref[...])


def _glu(y, xall, modtab, layer, norm_g, glu_w, glu_b, router_w):
    bsz, s, dm = xall.shape
    n = s - CTX_LEN
    ntl = n // TM
    assert bsz == 2 * GLU_SAMPLES
    gs = GLU_SAMPLES
    rows = TM // S5_TC * bsz
    ctx_tiles = CTX_LEN // TM
    tok = pl.BlockSpec((gs, TM, dm), lambda t, h: (h, t, 0))
    x1, h2, lg = pl.pallas_call(
        functools.partial(_glu_kernel, bsz=bsz),
        grid=(ntl, bsz // gs),
        in_specs=[pl.BlockSpec((S5_GROUPS, rows, S5_TC * S5_GROUP), lambda t, h: (0, t + ctx_tiles, 0)),
                  pl.BlockSpec((gs, TM, dm), lambda t, h: (h, t + ctx_tiles, 0)),
                  pl.BlockSpec((8, 6, dm), lambda t, h: (layer, 0, 0)),
                  pl.BlockSpec((1, 1, dm), lambda t, h: (0, 0, 0)),
                  pl.BlockSpec((dm, 2 * dm), lambda t, h: (0, 0)),
                  pl.BlockSpec((1, 2 * dm), lambda t, h: (0, 0)),
                  pl.BlockSpec((N_EXPERTS, dm), lambda t, h: (0, 0))],
        out_specs=[tok, tok, pl.BlockSpec((gs, N_EXPERTS, TM), lambda t, h: (h, 0, t))],
        out_shape=[jax.ShapeDtypeStruct((bsz, n, dm), F32),
                   jax.ShapeDtypeStruct((bsz, n, dm), BF16),
                   jax.ShapeDtypeStruct((bsz, N_EXPERTS, n), F32)],
        scratch_shapes=[pltpu.VMEM((S5_LANE_VREGS, TM * bsz, 128), F32), pltpu.VMEM((TM, dm), F32)],
        compiler_params=_cparams(("arbitrary", "arbitrary")),
        name="glu",
    )(y, xall, modtab.reshape(DEPTH * 8, 6, dm), norm_g.reshape(1, 1, dm), glu_w.astype(BF16), glu_b[None, :],
      router_w.T)
    return x1, h2, jnp.transpose(lg, (1, 0, 2)).reshape(N_EXPERTS, bsz * n)


def kernel(x, c, ctx, c_ctx, mod_w, mod_b, norm1_g, norm2_g, ar_w_in, ar_w_out, lru_conv_w, lru_conv_b, lru_lam, lru_r_w, lru_r_b, lru_i_w, lru_i_b, attn_q_g, attn_k_g, attn_lam_q1, attn_lam_k1, attn_lam_q2, attn_lam_k2, attn_subln_g, s5_a_re, s5_a_im, s5_log_step, s5_b_re, s5_b_im, s5_c_re, s5_c_im, s5_d, s5_glu_w, s5_glu_b, router_w, router_bias, exp_w_gate, exp_w_up, exp_w_down, sh_w_gate, sh_w_up, sh_w_down):
    bsz, n, dm = x.shape
    assert dm == D_MODEL and ctx.shape[1] == CTX_LEN == TM and n % TM == 0 and bsz < 8
    assert mod_w.shape[0] == DEPTH == 2
    s = CTX_LEN + n
    ntb = s // TM
    modtab = _modulation(c, c_ctx, mod_w, mod_b)

    gate, xa, q, k, v = _inproj(x, ctx, modtab, norm1_g[0], ar_w_in[0], attn_q_g[0], attn_k_g[0])
    lru_args = (lru_conv_w[0], lru_conv_b[0], lru_lam[0], lru_r_w[0], lru_r_b[0], lru_i_w[0], lru_i_b[0])
    hf = _lru(xa, *lru_args, reverse=False)
    hr = _lru(xa, *lru_args, reverse=True)
    lam_init = 0.8 - 0.6 * math.exp(-0.3 * 0)
    lam_vecs = jnp.stack([attn_lam_q1[0], attn_lam_k1[0], attn_lam_q2[0], attn_lam_k2[0]], axis=0)
    yb = _attention(q, k, v, lam_vecs, attn_subln_g[0], lam_init)
    x1, h2, lgt = _outproj(hf, hr, gate, yb, x, ctx, modtab, norm2_g[0], ar_w_out[0], router_w[0])
    row0 = lambda i: jnp.where(i % ntb == 0, bsz, i // ntb)
    xall = _moe(h2.reshape(bsz * s, dm), lgt, x1.reshape(bsz * s, dm), modtab, 0, row0, router_bias[0],
                exp_w_gate, exp_w_up, exp_w_down, sh_w_gate[0], sh_w_up[0], sh_w_down[0])
    xall = xall.reshape(bsz, s, dm)

    xt = _s5_pack(xall, modtab, 1, norm1_g[1])
    weights = _s5_weights(s5_a_re[0], s5_a_im[0], s5_log_step[0], s5_b_re[0], s5_b_im[0], s5_c_re[0],
                          s5_c_im[0], s5_d[0])
    y = _s5(xt, bsz, weights)
    x1, h2, lgt = _glu(y, xall, modtab, 1, norm2_g[1], s5_glu_w[0], s5_glu_b[0], router_w[1])
    ntl = n // TM
    row1 = lambda i: i // ntl
    out = _moe(h2.reshape(bsz * n, dm), lgt, x1.reshape(bsz * n, dm), modtab, 1, row1, router_bias[1],
               exp_w_gate, exp_w_up, exp_w_down, sh_w_gate[1], sh_w_up[1], sh_w_down[1])
    return out.reshape(bsz, n, dm)
```

```python
import functools
import math

import jax
import jax.numpy as jnp
from jax import lax
from jax.experimental import pallas as pl
from jax.experimental.pallas import tpu as pltpu

F32, BF16, I32 = jnp.float32, jnp.bfloat16, jnp.int32
HIGHEST = lax.Precision.HIGHEST

D_MODEL = 1024
DEPTH = 2
GRID_W = 64
CTX_LEN = 256
EPS = 1e-6
LRU_WIDTH = 512
LRU_BLOCKS = 8
LRU_C = 8.0
ATT_HEADS = 4
ATT_DH = 64
ATT_DV = 128
ATT_QK_W = 512
ATT_WIDTH = 512
ROPE_BASE = 10000.0
EVEN_IN = 2 * LRU_WIDTH + 2 * ATT_QK_W + ATT_WIDTH
S5_GROUP = 16
S5_GROUPS = 64
S5_STATE = 64
S5_TC = 16
S5_GB = 4
N_EXPERTS = 64
TOP_K = 6
D_EXPERT = 256
ROUTED_SCALE = 2.5

TM = 256
SEG = 16
SLOT_CHUNK = 64
RMAX = -(-(TM * TOP_K + N_EXPERTS * (SEG - 1)) // SLOT_CHUNK) * SLOT_CHUNK
TR = 1024
TAIL_BITS = (TR // SEG - 1).bit_length()
ATT_TK = 1408
VMEM_LIMIT = 56 * 1024 * 1024


def _cparams(sem):
    return pltpu.CompilerParams(dimension_semantics=sem, vmem_limit_bytes=VMEM_LIMIT)


def _row_halves():
    return [slice(0, TM // 2), slice(TM // 2, TM)]


def _norm_mod(x, g, shift, scale):
    y = x * lax.rsqrt(jnp.mean(x * x, axis=-1, keepdims=True) + EPS) * g
    return y * (1.0 + scale) + shift


def _mod_kernel(c_ref, w_ref, b_ref, o_ref):
    c = c_ref[...]
    s = c * jax.nn.sigmoid(c)
    o_ref[0] = jnp.dot(s, w_ref[0], precision=HIGHEST, preferred_element_type=F32) + b_ref[0]


def _modulation(c, c_ctx, mod_w, mod_b):
    bsz, dm = c.shape
    cc = jnp.concatenate([c, c_ctx[None, :], jnp.zeros((8 - bsz - 1, dm), F32)], axis=0)
    out = pl.pallas_call(
        _mod_kernel,
        grid=(DEPTH, 6),
        in_specs=[pl.BlockSpec((8, dm), lambda l, j: (0, 0)),
                  pl.BlockSpec((1, dm, dm), lambda l, j: (l, 0, j)),
                  pl.BlockSpec((1, 1, dm), lambda l, j: (l, 0, j))],
        out_specs=pl.BlockSpec((1, 8, dm), lambda l, j: (l, 0, j)),
        out_shape=jax.ShapeDtypeStruct((DEPTH, 8, 6 * dm), F32),
        compiler_params=_cparams(("arbitrary", "arbitrary")),
        name="modulation",
    )(cc, mod_w, mod_b.reshape(DEPTH, 1, 6 * dm))
    return out.reshape(DEPTH * 8 * 6, 1, dm)


def _mod_spec(layer, part, row_fn):
    return pl.BlockSpec((1, 1, D_MODEL), lambda *ids: ((layer * 8 + row_fn(*ids)) * 6 + part, 0, 0))


def _rope_tables(n):
    rows = n // GRID_W
    r, col = jnp.meshgrid(jnp.arange(rows), jnp.arange(GRID_W), indexing="ij")
    pos = jnp.stack([r.reshape(-1), col.reshape(-1)], axis=-1).astype(F32)
    n_freq = ATT_DH // 4
    inv_freq = ROPE_BASE ** (-jnp.arange(n_freq, dtype=F32) / n_freq)
    ang = pos[:, :, None] * inv_freq
    cos, sin = jnp.cos(ang), jnp.sin(ang)
    zero = jnp.zeros_like(sin)
    cos64 = jnp.stack([cos, cos], axis=2).reshape(n, ATT_DH)
    sin_lo = jnp.stack([zero, sin], axis=2).reshape(n, ATT_DH)
    sin_hi = jnp.stack([-sin, zero], axis=2).reshape(n, ATT_DH)

    def full(tab, ctx_val):
        tab = jnp.concatenate([jnp.full((CTX_LEN, ATT_DH), ctx_val, F32), tab], axis=0)
        return jnp.concatenate([tab, tab], axis=1)

    return full(cos64, 1.0), full(sin_lo, 0.0), full(sin_hi, 0.0)


def _qk_post(t, gain, ones_bd, cos, sin_lo, sin_hi):
    ss = jnp.dot((t * t).astype(BF16), ones_bd, preferred_element_type=F32) * (1.0 / ATT_DH)
    tn = t * lax.rsqrt(ss + EPS) * gain
    w = tn.shape[1]
    return tn * cos + pltpu.roll(tn, 16, 1) * sin_lo + pltpu.roll(tn, w - 16, 1) * sin_hi


def _inproj_kernel(x_ref, c_ref, sh_ref, sc_ref, g_ref, w_ref, qg_ref, kg_ref, ones_ref,
                   cos_ref, slo_ref, shi_ref, gate_ref, xa_ref, q_ref, k_ref, v_ref):
    t = pl.program_id(1)
    lw, qw = LRU_WIDTH, ATT_QK_W
    tile4 = lambda a: jnp.concatenate([a, a, a, a], axis=1)
    ones_bd = ones_ref[...]
    for rows in _row_halves():
        x = jnp.where(t == 0, c_ref[0, rows], x_ref[0, rows])
        h = _norm_mod(x, g_ref[0], sh_ref[0], sc_ref[0])
        z = jnp.dot(h.astype(BF16), w_ref[...], preferred_element_type=F32)
        gate_ref[0, rows] = z[:, 0:lw].astype(BF16)
        xa_ref[0, rows] = z[:, lw:2 * lw].astype(BF16)
        cos, slo, shi = tile4(cos_ref[rows]), tile4(slo_ref[rows]), tile4(shi_ref[rows])
        q = _qk_post(z[:, 2 * lw:2 * lw + qw], qg_ref[...], ones_bd, cos, slo, shi)
        q_ref[0, rows] = (q * (ATT_DH ** -0.5 * math.log2(math.e))).astype(BF16)
        k = _qk_post(z[:, 2 * lw + qw:2 * lw + 2 * qw], kg_ref[...], ones_bd, cos, slo, shi)
        k_ref[0, rows] = k.astype(BF16)
        v_ref[0, rows] = z[:, 2 * lw + 2 * qw:].astype(BF16)


def _inproj(x, ctx, modtab, norm_g, w_in, q_g, k_g):
    bsz, n, dm = x.shape
    ntb = (CTX_LEN + n) // TM
    s = CTX_LEN + n
    row = lambda b, t: jnp.where(t == 0, bsz, b)
    cos, slo, shi = _rope_tables(n)
    ones_bd = jnp.kron(jnp.eye(ATT_QK_W // ATT_DH, dtype=F32), jnp.ones((ATT_DH, ATT_DH), F32)).astype(BF16)
    tile_g = lambda g: jnp.tile(g, ATT_QK_W // ATT_DH)[None, :]
    tab_spec = pl.BlockSpec((TM, 2 * ATT_DH), lambda b, t: (t, 0))
    out_spec = pl.BlockSpec((1, TM, LRU_WIDTH), lambda b, t: (b, t, 0))
    out_sds = jax.ShapeDtypeStruct((bsz, s, LRU_WIDTH), BF16)
    return pl.pallas_call(
        _inproj_kernel,
        grid=(bsz, ntb),
        in_specs=[pl.BlockSpec((1, TM, dm), lambda b, t: (b, jnp.maximum(t - 1, 0), 0)),
                  pl.BlockSpec((1, TM, dm), lambda b, t: (b, 0, 0)),
                  _mod_spec(0, 0, row), _mod_spec(0, 1, row),
                  pl.BlockSpec((1, 1, dm), lambda b, t: (0, 0, 0)),
                  pl.BlockSpec((dm, EVEN_IN), lambda b, t: (0, 0)),
                  pl.BlockSpec((1, ATT_QK_W), lambda b, t: (0, 0)),
                  pl.BlockSpec((1, ATT_QK_W), lambda b, t: (0, 0)),
                  pl.BlockSpec((ATT_QK_W, ATT_QK_W), lambda b, t: (0, 0)),
                  tab_spec, tab_spec, tab_spec],
        out_specs=[out_spec] * 5,
        out_shape=[out_sds] * 5,
        compiler_params=_cparams(("arbitrary", "arbitrary")),
        name="inproj",
    )(x, ctx, modtab, modtab, norm_g.reshape(1, 1, dm), w_in.astype(BF16), tile_g(q_g), tile_g(k_g), ones_bd,
      cos, slo, shi)


def _lru_kernel(xa_ref, xp_ref, xn_ref, cw_ref, cb_ref, lam_ref, w_ref, bias_ref, h_ref,
                a_scr, b_scr, c_scr, *, ntb, reverse):
    s = pl.program_id(1)
    ti = jnp.where(s == 0, 0, ntb - s) if reverse else s
    lw = LRU_WIDTH
    x = xa_ref[0].astype(F32)
    row = lax.broadcasted_iota(I32, (TM, lw), 0)
    has_prev = jnp.where(ti > 1, 1.0, 0.0)
    has_next = jnp.where((ti > 0) & (ti < ntb - 1), 1.0, 0.0)
    prev = xp_ref[0].astype(F32) * has_prev
    nxt = xn_ref[0].astype(F32) * has_next
    xm1 = jnp.where(row == 0, prev[7:8], pltpu.roll(x, 1, 0))
    xm2 = jnp.where(row == 0, prev[6:7], jnp.where(row == 1, prev[7:8], pltpu.roll(x, 2, 0)))
    xp1 = jnp.where(row == TM - 1, nxt[0:1], pltpu.roll(x, TM - 1, 0))
    cw = cw_ref[...]
    xc = cw[0:1] * xm2 + cw[1:2] * xm1 + cw[2:3] * x + cw[3:4] * xp1 + cb_ref[...]

    z = jnp.dot(xc.astype(BF16), w_ref[...], preferred_element_type=F32) + bias_ref[...]
    r = jax.nn.sigmoid(z[:, :lw])
    ig = jax.nn.sigmoid(z[:, lw:])
    neg_lam = -lam_ref[...]
    softplus = jnp.maximum(neg_lam, 0.0) + jnp.log1p(jnp.exp(-jnp.abs(neg_lam)))
    log_a = (-LRU_C) * r * softplus
    a = jnp.exp(log_a)
    b = jnp.sqrt(-jnp.tanh(log_a) * (a * a + 1.0)) * (ig * xc)

    r8 = row & 7
    for sft in (1, 2, 4):
        if reverse:
            a_s, b_s, m = pltpu.roll(a, TM - sft, 0), pltpu.roll(b, TM - sft, 0), r8 < 8 - sft
        else:
            a_s, b_s, m = pltpu.roll(a, sft, 0), pltpu.roll(b, sft, 0), r8 >= sft
        b = jnp.where(m, a * b_s + b, b)
        a = jnp.where(m, a * a_s, a)
    a_scr[...] = a
    b_scr[...] = b

    @pl.when(s == 0)
    def _():
        c_scr[...] = jnp.zeros_like(c_scr)

    ng = TM // 8

    def body(j, carry):
        g = (ng - 1 - j) if reverse else j
        off = pl.multiple_of(g * 8, 8)
        h = a_scr[pl.ds(off, 8), :] * carry + b_scr[pl.ds(off, 8), :]
        h_ref[0, pl.ds(off, 8), :] = h
        last = h[0:1] if reverse else h[7:8]
        return jnp.broadcast_to(last, (8, lw))

    c_scr[...] = lax.fori_loop(0, ng, body, c_scr[...], unroll=4)


def _lru(xa, conv_w, conv_b, lam, r_w, r_b, i_w, i_b, reverse):
    bsz, s, lw = xa.shape
    ntb = s // TM
    d = 1 if reverse else 0
    bd = lambda w: jax.scipy.linalg.block_diag(*[w[i] for i in range(LRU_BLOCKS)])
    w = jnp.concatenate([bd(r_w[d]), bd(i_w[d])], axis=1).astype(BF16)
    bias = jnp.concatenate([r_b[d], i_b[d]])[None, :]
    cw = jnp.concatenate([conv_w, jnp.zeros((4, lw), F32)], axis=0)
    if reverse:
        tile = lambda t: jnp.where(t == 0, 0, ntb - t)
    else:
        tile = lambda t: t
    nb8 = s // 8
    return pl.pallas_call(
        functools.partial(_lru_kernel, ntb=ntb, reverse=reverse),
        grid=(bsz, ntb),
        in_specs=[pl.BlockSpec((1, TM, lw), lambda b, t: (b, tile(t), 0)),
                  pl.BlockSpec((1, 8, lw), lambda b, t: (b, jnp.maximum(tile(t) * (TM // 8) - 1, 0), 0)),
                  pl.BlockSpec((1, 8, lw), lambda b, t: (b, jnp.minimum((tile(t) + 1) * (TM // 8), nb8 - 1), 0)),
                  pl.BlockSpec((8, lw), lambda b, t: (0, 0)),
                  pl.BlockSpec((1, lw), lambda b, t: (0, 0)),
                  pl.BlockSpec((1, lw), lambda b, t: (0, 0)),
                  pl.BlockSpec((lw, 2 * lw), lambda b, t: (0, 0)),
                  pl.BlockSpec((1, 2 * lw), lambda b, t: (0, 0))],
        out_specs=pl.BlockSpec((1, TM, lw), lambda b, t: (b, tile(t), 0)),
        out_shape=jax.ShapeDtypeStruct((bsz, s, lw), F32),
        scratch_shapes=[pltpu.VMEM((TM, lw), F32), pltpu.VMEM((TM, lw), F32), pltpu.VMEM((8, lw), F32)],
        compiler_params=_cparams(("arbitrary", "arbitrary")),
        name="lru_rev" if reverse else "lru_fwd",
    )(xa, xa, xa, cw, conv_b[None, :], lam[d][None, :], w, bias)


def _attn_kernel(lamv_ref, q_ref, k_ref, v_ref, sg_ref, o_ref, m_scr, l_scr, acc_scr, al_scr, s_scr, p_scr, *,
                 nkv, lam_init):
    t = pl.program_id(2)
    q = q_ref[0]
    dh = ATT_DH
    dv = ATT_DV
    m_scr[...] = jnp.full_like(m_scr, -jnp.inf)
    l_scr[...] = jnp.zeros_like(l_scr)
    acc_scr[...] = jnp.zeros_like(acc_scr)

    def scores(buf, off, size):
        kc = k_ref[0, pl.ds(off, size), :]
        for mi in range(2):
            s_scr[buf, mi, :, 0:size] = lax.dot_general(
                q[:, mi * dh:(mi + 1) * dh], kc[:, mi * dh:(mi + 1) * dh], (((1,), (1,)), ((), ())),
                preferred_element_type=F32)

    def softmax(buf, size):
        nlb = size // 128
        groups = [(slice(r * 16, (r + 1) * 16), mi) for r in range(TM // 16) for mi in range(2)]
        for rows, mi in groups:
            sc = s_scr[buf, mi, rows, 0:size]
            mx = functools.reduce(jnp.maximum, [sc[:, i * 128:(i + 1) * 128] for i in range(nlb)])
            m_old = m_scr[mi, rows, :]
            m_new = jnp.maximum(m_old, jnp.max(mx, axis=1, keepdims=True))
            m_scr[mi, rows, :] = m_new
            al_scr[mi, rows, :] = jnp.exp2(m_old - m_new)
        for rows, mi in groups:
            m_new = m_scr[mi, rows, :]
            ps = [jnp.exp2(s_scr[buf, mi, rows, i * 128:(i + 1) * 128] - m_new) for i in range(nlb)]
            l_scr[mi, rows, :] = al_scr[mi, rows, :] * l_scr[mi, rows, :] + functools.reduce(jnp.add, ps)
            p_scr[buf, mi, rows, 0:size] = jnp.concatenate(ps, axis=1).astype(BF16)

    def values(buf, off, size):
        vc = v_ref[0, pl.ds(off, size), :]
        pv = jnp.dot(p_scr[buf, :, :, 0:size].reshape(2 * TM, size), vc, preferred_element_type=F32)
        acc_scr[...] = al_scr[...] * acc_scr[...] + pv.reshape(2, TM, dv)

    @pl.when(t == 0)
    def _():
        scores(0, 0, CTX_LEN)
        softmax(0, CTX_LEN)
        values(0, 0, CTX_LEN)

    @pl.when(t > 0)
    def _():
        tk = ATT_TK
        nchunk = nkv // tk
        scores(0, 0, tk)
        for j in range(nchunk):
            if j + 1 < nchunk:
                scores((j + 1) % 2, (j + 1) * tk, tk)
            softmax(j % 2, tk)
            values(j % 2, j * tk, tk)

    lv = lamv_ref[...]
    lam = (jnp.exp(jnp.sum(lv[0:1] * lv[1:2], axis=1, keepdims=True))
           - jnp.exp(jnp.sum(lv[2:3] * lv[3:4], axis=1, keepdims=True)) + lam_init)
    l0 = jnp.sum(l_scr[0], axis=1, keepdims=True)
    l1 = jnp.sum(l_scr[1], axis=1, keepdims=True)
    o = acc_scr[0] / l0 - lam * (acc_scr[1] / l1)
    y = o * lax.rsqrt(jnp.mean(o * o, axis=-1, keepdims=True) + EPS) * sg_ref[...] * (1.0 - lam_init)
    o_ref[0] = y.astype(BF16)


def _attention(q, k, v, lam_vecs, subln_g, lam_init):
    bsz, s, _ = q.shape
    ntb = s // TM
    assert s % ATT_TK == 0
    dv = ATT_DV
    lamv = jnp.concatenate([jnp.pad(lam_vecs, ((0, 0), (0, dv - ATT_DH))), jnp.zeros((4, dv), F32)], axis=0)
    return pl.pallas_call(
        functools.partial(_attn_kernel, nkv=s, lam_init=lam_init),
        grid=(bsz, ATT_HEADS, ntb),
        in_specs=[pl.BlockSpec((8, dv), lambda b, h, t: (0, 0)),
                  pl.BlockSpec((1, TM, dv), lambda b, h, t: (b, t, h)),
                  pl.BlockSpec((1, s, dv), lambda b, h, t: (b, 0, h)),
                  pl.BlockSpec((1, s, dv), lambda b, h, t: (b, 0, h)),
                  pl.BlockSpec((1, dv), lambda b, h, t: (0, 0))],
        out_specs=pl.BlockSpec((1, TM, dv), lambda b, h, t: (b, t, h)),
        out_shape=jax.ShapeDtypeStruct((bsz, s, ATT_WIDTH), BF16),
        scratch_shapes=[pltpu.VMEM((2, TM, dv), F32)] * 4
        + [pltpu.VMEM((2, 2, TM, ATT_TK), F32), pltpu.VMEM((2, 2, TM, ATT_TK), BF16)],
        compiler_params=_cparams(("arbitrary", "arbitrary", "arbitrary")),
        name="diff_attention",
    )(lamv, q, k, v, subln_g[None, :])


def _post_mixer(y, x, g1, sh2, sc2, n2g, rwt):
    x1 = x + g1 * y
    h2 = _norm_mod(x1, n2g, sh2, sc2)
    logits = lax.dot_general(rwt, h2, (((1,), (1,)), ((), ())), precision=HIGHEST, preferred_element_type=F32)
    return x1, h2.astype(BF16), logits


def _outproj_kernel(hf_ref, hr_ref, gate_ref, yb_ref, x_ref, c_ref, g1_ref, sh_ref, sc_ref, n2g_ref, w_ref,
                    rwt_ref, x1_ref, h2_ref, lg_ref):
    t = pl.program_id(1)
    lw = LRU_WIDTH
    for rows in _row_halves():
        x = jnp.where(t == 0, c_ref[0, rows], x_ref[0, rows])
        ya = ((hf_ref[0, rows] + hr_ref[0, rows]) * jax.nn.gelu(gate_ref[0, rows].astype(F32))).astype(BF16)
        y = (jnp.dot(ya, w_ref[0:lw, :], preferred_element_type=F32)
             + jnp.dot(yb_ref[0, rows], w_ref[lw:, :], preferred_element_type=F32))
        x1_ref[0, rows], h2_ref[0, rows], lg_ref[:, rows] = _post_mixer(
            y, x, g1_ref[0], sh_ref[0], sc_ref[0], n2g_ref[0], rwt_ref[...])


def _outproj(hf, hr, gate, yb, x, ctx, modtab, norm_g, w_out, router_w):
    bsz, s, lw = hf.shape
    dm = D_MODEL
    ntb = s // TM
    row = lambda b, t: jnp.where(t == 0, bsz, b)
    half = pl.BlockSpec((1, TM, lw), lambda b, t: (b, t, 0))
    return pl.pallas_call(
        _outproj_kernel,
        grid=(bsz, ntb),
        in_specs=[half, half, half, half,
                  pl.BlockSpec((1, TM, dm), lambda b, t: (b, jnp.maximum(t - 1, 0), 0)),
                  pl.BlockSpec((1, TM, dm), lambda b, t: (b, 0, 0)),
                  _mod_spec(0, 2, row), _mod_spec(0, 3, row), _mod_spec(0, 4, row),
                  pl.BlockSpec((1, 1, dm), lambda b, t: (0, 0, 0)),
                  pl.BlockSpec((2 * lw, dm), lambda b, t: (0, 0)),
                  pl.BlockSpec((N_EXPERTS, dm), lambda b, t: (0, 0))],
        out_specs=[pl.BlockSpec((1, TM, dm), lambda b, t: (b, t, 0)),
                   pl.BlockSpec((1, TM, dm), lambda b, t: (b, t, 0)),
                   pl.BlockSpec((N_EXPERTS, TM), lambda b, t: (0, b * ntb + t))],
        out_shape=[jax.ShapeDtypeStruct((bsz, s, dm), F32),
                   jax.ShapeDtypeStruct((bsz, s, dm), BF16),
                   jax.ShapeDtypeStruct((N_EXPERTS, bsz * s), F32)],
        compiler_params=_cparams(("arbitrary", "arbitrary")),
        name="outproj",
    )(hf, hr, gate, yb, x, ctx, modtab, modtab, modtab, norm_g.reshape(1, 1, dm), w_out.astype(BF16),
      router_w.T)


def _route_kernel(lg_ref, bias_ref, tri_ref, low_ref, pos_ref, gate_ref, cnt_ref):
    ne = N_EXPERTS
    aff = jax.nn.sigmoid(lg_ref[...])
    work = aff + bias_ref[:, 0:1]
    eidx = lax.broadcasted_iota(I32, (ne, TM), 0)
    sels = []
    for _ in range(TOP_K):
        mx = jnp.max(work, axis=0, keepdims=True)
        am = jnp.min(jnp.where(work == mx, eidx, ne), axis=0, keepdims=True)
        sk = eidx == am
        sels.append(sk)
        work = jnp.where(sk, -jnp.inf, work)
    sel = sels[0]
    for sk in sels[1:]:
        sel = sel | sk
    self = jnp.where(sel, 1.0, 0.0)
    s_sel = aff * self
    gates = s_sel / jnp.sum(s_sel, axis=0, keepdims=True) * ROUTED_SCALE
    rank = jnp.dot(self.astype(BF16), tri_ref[...], preferred_element_type=F32)
    cnt = jnp.sum(self, axis=1, keepdims=True)
    cnt_seg = jnp.floor((cnt + (SEG - 1)) * (1.0 / SEG)) * SEG
    cnt_b = jnp.broadcast_to(cnt_seg, (ne, 128))
    seg_off = jnp.dot(low_ref[...], cnt_b.astype(BF16), preferred_element_type=F32)
    lpos = seg_off[:, 0:1] + rank
    pos_rows, gate_rows = [], []
    for sk in sels:
        pos_rows.append(jnp.sum(jnp.where(sk, lpos, 0.0), axis=0, keepdims=True))
        gate_rows.append(jnp.sum(jnp.where(sk, gates, 0.0), axis=0, keepdims=True))
    for _ in range(8 - TOP_K):
        pos_rows.append(jnp.full((1, TM), -1.0, F32))
        gate_rows.append(jnp.zeros((1, TM), F32))
    pos_ref[...] = jnp.concatenate(pos_rows, axis=0).astype(I32)
    gate_ref[...] = jnp.concatenate(gate_rows, axis=0)
    cnt_ref[0] = cnt_b.astype(I32)


def _route(lgt, router_bias):
    ne, t = lgt.shape
    nt = t // TM
    tri = jnp.triu(jnp.ones((TM, TM), F32), k=1).astype(BF16)
    low = jnp.tril(jnp.ones((ne, ne), F32), k=-1).astype(BF16)
    bias = jnp.broadcast_to(router_bias[:, None], (ne, 128))
    pos, gate, cnt = pl.pallas_call(
        _route_kernel,
        grid=(nt,),
        in_specs=[pl.BlockSpec((ne, TM), lambda i: (0, i)),
                  pl.BlockSpec((ne, 128), lambda i: (0, 0)),
                  pl.BlockSpec((TM, TM), lambda i: (0, 0)),
                  pl.BlockSpec((ne, ne), lambda i: (0, 0))],
        out_specs=[pl.BlockSpec((8, TM), lambda i: (0, i)),
                   pl.BlockSpec((8, TM), lambda i: (0, i)),
                   pl.BlockSpec((1, ne, 128), lambda i: (i, 0, 0))],
        out_shape=[jax.ShapeDtypeStruct((8, t), I32),
                   jax.ShapeDtypeStruct((8, t), F32),
                   jax.ShapeDtypeStruct((nt, ne, 128), I32)],
        compiler_params=_cparams(("arbitrary",)),
        name="route",
    )(lgt, bias, tri, low)
    return pos, gate, cnt[:, :, 0]


def _moe_layout(cnt_seg, nblk_max):
    tot = jnp.sum(cnt_seg, axis=0)
    region = (tot + TR - 1) // TR * TR
    region_end = jnp.cumsum(region)
    goff = (region_end - region)[None, :] + jnp.cumsum(cnt_seg, axis=0) - cnt_seg
    packed = ((goff // SEG) << 5) | (cnt_seg // SEG)
    nblk = region_end[-1] // TR
    blk = jnp.arange(nblk_max, dtype=I32)
    blk_e = jnp.sum((region_end[None, :] // TR <= blk[:, None]).astype(I32), axis=1)
    blk_e = jnp.minimum(blk_e, N_EXPERTS - 1).astype(I32)
    tile_info = jnp.sum(cnt_seg, axis=1) | (jnp.any(cnt_seg >= 64, axis=1).astype(I32) << 16)
    tails = (((region_end - region + tot) // SEG) << TAIL_BITS) | ((region - tot) // SEG)
    tails = jnp.concatenate([tails, nblk[None]]).astype(I32)
    return packed.reshape(-1).astype(I32), tile_info.astype(I32), tails, blk_e, nblk.astype(I32).reshape(1)


def _segment_copy(stage, hbm, sem, to_sorted, lo, go, size):
    lo, go = pl.multiple_of(lo, SEG), pl.multiple_of(go, SEG)
    a, b = stage.at[pl.ds(lo, size)], hbm.at[pl.ds(go, size)]
    return pltpu.make_async_copy(a, b, sem) if to_sorted else pltpu.make_async_copy(b, a, sem)


def _segment_unpack(pk):
    return (pk & 31) * SEG, (pk >> 5) * SEG


def _segment_starts(pk_ref, tile, stage, hbm, sem, to_sorted, valid=None):
    loff = [0]

    def step(e):
        cnt, goff = _segment_unpack(pk_ref[tile * N_EXPERTS + e])
        done = (cnt >> 6) << 6
        for size in (32, 16):
            bit = (cnt & size) != 0
            if valid is not None:
                bit = bit & valid

            @pl.when(bit)
            def _(lo=loff[0] + done, go=goff + done, size=size):
                _segment_copy(stage, hbm, sem, to_sorted, lo, go, size).start()

            done = done + jnp.where(bit, size, 0)
        loff[0] = loff[0] + cnt

    return [functools.partial(step, e) for e in range(N_EXPERTS)]


def _interleave(*step_lists):
    total = max(len(steps) for steps in step_lists)
    done = [0] * len(step_lists)
    for t in range(1, total + 1):
        for k, steps in enumerate(step_lists):
            upto = len(steps) * t // total
            for step in steps[done[k]:upto]:
                step()
            done[k] = upto


def _segment_starts_long(pk_ref, tile, info, stage, hbm, sem, to_sorted):
    @pl.when((info >> 16) != 0)
    def _():
        def expert(e, loff):
            cnt, goff = _segment_unpack(pk_ref[tile * N_EXPERTS + e])

            def chunk(j, carry):
                _segment_copy(stage, hbm, sem, to_sorted, loff + j * 64, goff + j * 64, 64).start()
                return carry

            lax.fori_loop(0, cnt >> 6, chunk, 0)
            return loff + cnt

        lax.fori_loop(0, N_EXPERTS, expert, 0)


def _segment_wait(total, stage, hbm, sem, to_sorted):
    size = 1 << (RMAX.bit_length() - 1)
    while size >= SEG:
        @pl.when((total & size) != 0)
        def _(size=size):
            _segment_copy(stage, hbm, sem, to_sorted, 0, 0, size).wait()

        size //= 2


def _fill_slot_matrix(dst_ref, pos, weight_rows=None):
    ch = SLOT_CHUNK
    riota = lax.broadcasted_iota(I32, (ch, TM), 0)

    def step(c):
        local = pos - c * ch
        out = jnp.zeros((ch, TM), F32)
        for k in range(TOP_K):
            w = 1.0 if weight_rows is None else weight_rows[k:k + 1]
            out = jnp.where(riota == local[k:k + 1], w, out)
        dst_ref[c * ch:(c + 1) * ch, :] = out.astype(BF16)

    return [functools.partial(step, c) for c in range(RMAX // ch)]


def _zero_fill(tail_ref, zeros, xs_ref, sem, fn):
    def expert(e, carry):
        pk = tail_ref[e]
        off = (pk >> TAIL_BITS) * SEG

        def chunk(j, c):
            dst = xs_ref.at[pl.ds(pl.multiple_of(off + j * SEG, SEG), SEG)]
            fn(pltpu.make_async_copy(zeros.at[pl.ds(0, SEG)], dst, sem))
            return c

        return lax.fori_loop(0, pk & ((1 << TAIL_BITS) - 1), chunk, carry)

    lax.fori_loop(0, N_EXPERTS, expert, 0)

    def block(j, carry):
        fn(pltpu.make_async_copy(zeros, xs_ref.at[pl.ds(pl.multiple_of(j * TR, TR), TR)], sem))
        return carry

    lax.fori_loop(tail_ref[N_EXPERTS], xs_ref.shape[0] // TR, block, 0)


def _dispatch_kernel(pk_ref, tot_ref, tail_ref, pos_ref, h_ref, xs_ref, stage, zeros, slots, sem, zsem):
    i = pl.program_id(0)
    slot = i % 2

    @pl.when(i == 0)
    def _():
        zeros[...] = jnp.zeros_like(zeros)
        _zero_fill(tail_ref, zeros, xs_ref, zsem, lambda c: c.start())

    for step in _fill_slot_matrix(slots, pos_ref[...]):
        step()
    stage[slot] = jnp.dot(slots[...], h_ref[...], preferred_element_type=F32).astype(BF16)
    prev_rows = jnp.where(i > 0, tot_ref[jnp.maximum(i - 1, 0)] & 0xFFFF, 0)
    _segment_wait(prev_rows, stage.at[1 - slot], xs_ref, sem.at[1 - slot], True)
    this = (stage.at[slot], xs_ref, sem.at[slot], True)
    for step in _segment_starts(pk_ref, i, *this):
        step()
    _segment_starts_long(pk_ref, i, tot_ref[i], *this)

    @pl.when(i == pl.num_programs(0) - 1)
    def _():
        _segment_wait(tot_ref[i] & 0xFFFF, *this)
        _zero_fill(tail_ref, zeros, xs_ref, zsem, lambda c: c.wait())


def _dispatch(packed, tile_rows, tails, pos, h2, nrows):
    t, dm = h2.shape
    nt = t // TM
    return pl.pallas_call(
        _dispatch_kernel,
        grid_spec=pltpu.PrefetchScalarGridSpec(
            num_scalar_prefetch=3,
            grid=(nt,),
            in_specs=[pl.BlockSpec((8, TM), lambda i, pk, tot, tail: (0, i)),
                      pl.BlockSpec((TM, dm), lambda i, pk, tot, tail: (i, 0))],
            out_specs=pl.BlockSpec(memory_space=pl.ANY),
            scratch_shapes=[pltpu.VMEM((2, RMAX, dm), BF16), pltpu.VMEM((TR, dm), BF16),
                            pltpu.VMEM((RMAX, TM), BF16),
                            pltpu.SemaphoreType.DMA((2,)), pltpu.SemaphoreType.DMA]),
        out_shape=jax.ShapeDtypeStruct((nrows, dm), BF16),
        compiler_params=_cparams(("arbitrary",)),
        name="moe_dispatch",
    )(packed, tile_rows, tails, pos, h2)


def _expert_kernel(be_ref, nb_ref, x_ref, wg_ref, wu_ref, wd_ref, y_ref, wg_s, wu_s, wd_s):
    j = pl.program_id(0)
    changed = be_ref[j] != be_ref[jnp.maximum(j - 1, 0)]

    @pl.when((j == 0) | changed)
    def _():
        wg_s[...] = wg_ref[0].astype(BF16)
        wu_s[...] = wu_ref[0].astype(BF16)
        wd_s[...] = wd_ref[0].astype(BF16)

    @pl.when(j < nb_ref[0])
    def _():
        x = x_ref[...]
        g = jnp.dot(x, wg_s[...], preferred_element_type=F32)
        u = jnp.dot(x, wu_s[...], preferred_element_type=F32)
        a = (g * jax.nn.sigmoid(g) * u).astype(BF16)
        y_ref[...] = jnp.dot(a, wd_s[...], preferred_element_type=F32).astype(BF16)


def _experts(blk_e, nblk, xs, layer, w_gate, w_up, w_down):
    nrows, dm = xs.shape
    nblk_max = nrows // TR
    de = D_EXPERT
    row_blk = lambda j, be, nb: (jnp.minimum(j, nb[0] - 1), 0)
    return pl.pallas_call(
        _expert_kernel,
        grid_spec=pltpu.PrefetchScalarGridSpec(
            num_scalar_prefetch=2,
            grid=(nblk_max,),
            in_specs=[pl.BlockSpec((TR, dm), row_blk),
                      pl.BlockSpec((None, 1, dm, de), lambda j, be, nb: (layer, be[j], 0, 0)),
                      pl.BlockSpec((None, 1, dm, de), lambda j, be, nb: (layer, be[j], 0, 0)),
                      pl.BlockSpec((None, 1, de, dm), lambda j, be, nb: (layer, be[j], 0, 0))],
            out_specs=pl.BlockSpec((TR, dm), row_blk),
            scratch_shapes=[pltpu.VMEM((dm, de), BF16), pltpu.VMEM((dm, de), BF16), pltpu.VMEM((de, dm), BF16)]),
        out_shape=jax.ShapeDtypeStruct((nrows, dm), BF16),
        input_output_aliases={2: 0},
        compiler_params=_cparams(("arbitrary",)),
        name="moe_experts",
    )(blk_e, nblk, xs, w_gate, w_up, w_down)


def _combine_kernel(pk_ref, tot_ref, pos_ref, gate_ref, h_ref, x1_ref, g2_ref, wsg_ref, wsu_ref, wsd_ref, ys_ref,
                    o_ref, stage, gates, sem):
    i = pl.program_id(0)
    slot = i % 2
    last = pl.num_programs(0) - 1

    @pl.when(i == 0)
    def _():
        stage[...] = jnp.zeros_like(stage)
        for step in _segment_starts(pk_ref, 0, stage.at[0], ys_ref, sem.at[0], False):
            step()
        _segment_starts_long(pk_ref, 0, tot_ref[0], stage.at[0], ys_ref, sem.at[0], False)

    nxt = jnp.minimum(i + 1, last)
    _interleave(_fill_slot_matrix(gates, pos_ref[...], gate_ref[...]),
                _segment_starts(pk_ref, nxt, stage.at[1 - slot], ys_ref, sem.at[1 - slot], False))
    h = h_ref[...]
    g = jnp.dot(h, wsg_ref[...], preferred_element_type=F32)
    u = jnp.dot(h, wsu_ref[...], preferred_element_type=F32)
    shared = jnp.dot((g * jax.nn.sigmoid(g) * u).astype(BF16), wsd_ref[...], preferred_element_type=F32)
    _segment_wait(tot_ref[i] & 0xFFFF, stage.at[slot], ys_ref, sem.at[slot], False)
    routed = lax.dot_general(gates[...], stage[slot], (((0,), (0,)), ((), ())), preferred_element_type=F32)
    o_ref[...] = x1_ref[...] + g2_ref[0] * (routed + shared)
    _segment_starts_long(pk_ref, nxt, tot_ref[nxt], stage.at[1 - slot], ys_ref, sem.at[1 - slot], False)

    @pl.when(i == last)
    def _():
        _segment_wait(tot_ref[i] & 0xFFFF, stage.at[1 - slot], ys_ref, sem.at[1 - slot], False)


def _combine(packed, tile_rows, pos, gate, h2, x1, modtab, layer, row_fn, ys, ws_gate, ws_up, ws_down):
    t, dm = h2.shape
    nt = t // TM
    de = D_EXPERT
    const = lambda shape: pl.BlockSpec(shape, lambda i, pk, tot: (0,) * len(shape))
    return pl.pallas_call(
        _combine_kernel,
        grid_spec=pltpu.PrefetchScalarGridSpec(
            num_scalar_prefetch=2,
            grid=(nt,),
            in_specs=[pl.BlockSpec((8, TM), lambda i, pk, tot: (0, i)),
                      pl.BlockSpec((8, TM), lambda i, pk, tot: (0, i)),
                      pl.BlockSpec((TM, dm), lambda i, pk, tot: (i, 0)),
                      pl.BlockSpec((TM, dm), lambda i, pk, tot: (i, 0)),
                      pl.BlockSpec((1, 1, dm), lambda i, pk, tot: ((layer * 8 + row_fn(i)) * 6 + 5, 0, 0)),
                      const((dm, de)), const((dm, de)), const((de, dm)),
                      pl.BlockSpec(memory_space=pl.ANY)],
            out_specs=pl.BlockSpec((TM, dm), lambda i, pk, tot: (i, 0)),
            scratch_shapes=[pltpu.VMEM((2, RMAX, dm), BF16), pltpu.VMEM((RMAX, TM), BF16),
                            pltpu.SemaphoreType.DMA((2,))]),
        out_shape=jax.ShapeDtypeStruct((t, dm), F32),
        compiler_params=_cparams(("arbitrary",)),
        name="moe_combine",
    )(packed, tile_rows, pos, gate, h2, x1, modtab, ws_gate.astype(BF16), ws_up.astype(BF16),
      ws_down.astype(BF16), ys)


def _moe(h2, lgt, x1, modtab, layer, row_fn, router_bias, w_gate, w_up, w_down, ws_gate, ws_up, ws_down):
    t = h2.shape[0]
    nt = t // TM
    max_rows = t * TOP_K + nt * N_EXPERTS * (SEG - 1) + N_EXPERTS * (TR - 1)
    nblk_max = -(-max_rows // TR)
    pos, gate, cnt_seg = _route(lgt, router_bias)
    packed, tile_rows, tails, blk_e, nblk = _moe_layout(cnt_seg, nblk_max)
    xs = _dispatch(packed, tile_rows, tails, pos, h2, nblk_max * TR)
    ys = _experts(blk_e, nblk, xs, layer, w_gate, w_up, w_down)
    return _combine(packed, tile_rows, pos, gate, h2, x1, modtab, layer, row_fn, ys, ws_gate, ws_up, ws_down)


S5_LANE_VREGS = D_MODEL // 128


def _segment_transpose(p):
    row = lax.broadcasted_iota(I32, (8, 128), 0)
    seg = lax.broadcasted_iota(I32, (8, 128), 1) // S5_GROUP
    for s in (4, 2, 1):
        m_up = ((row & s) == 0) & ((seg & s) != 0)
        m_dn = ((row & s) != 0) & ((seg & s) == 0)

        def swap(x):
            if s == 4:
                return jnp.where(m_up | m_dn, pltpu.roll(pltpu.roll(x, 4, 0), 4 * S5_GROUP, 1), x)
            up = pltpu.roll(pltpu.roll(x, 8 - s, 0), S5_GROUP * s, 1)
            dn = pltpu.roll(pltpu.roll(x, s, 0), 128 - S5_GROUP * s, 1)
            return jnp.where(m_up, up, jnp.where(m_dn, dn, x))

        p = [[swap(x) for x in half] for half in p]
    return [[p[1 - h][j - 1 + 2 * h] if (j % 2) != h else p[h][j] for j in range(S5_LANE_VREGS)] for h in range(2)]


def _s5_pack_kernel(x_ref, mod_ref, g_ref, xt_ref, u_scr, slab):
    t = pl.program_id(0)
    bsz = x_ref.shape[0]
    nchunk = TM // S5_TC
    for b in range(bsz):
        shift = jnp.where(t == 0, mod_ref[bsz, 0:1, :], mod_ref[b, 0:1, :])
        scale = jnp.where(t == 0, mod_ref[bsz, 1:2, :], mod_ref[b, 1:2, :])
        u_scr[b] = _norm_mod(x_ref[b], g_ref[0], shift, scale)

    def chunk(c, carry):
        r0 = pl.multiple_of(c * S5_TC, S5_TC)
        for b in range(bsz):
            p = [[u_scr[b, pl.ds(r0 + 8 * h, 8), 128 * j:128 * j + 128] for j in range(S5_LANE_VREGS)]
                 for h in range(2)]
            q = _segment_transpose(p)
            s0 = pl.multiple_of((c * bsz + b) * S5_TC, S5_TC)
            for h in range(2):
                for j in range(S5_LANE_VREGS):
                    slab[j, pl.ds(s0 + 8 * h, 8), :] = q[h][j]
        return carry

    lax.fori_loop(0, nchunk, chunk, 0)
    for j in range(S5_LANE_VREGS):
        for gl in range(S5_TC):
            g = (j // 2) * S5_TC + gl
            rows = slab[j, pl.ds(gl, nchunk * bsz, stride=S5_TC), :]
            xt_ref[g, :, 128 * (j % 2):128 * (j % 2) + 128] = rows.astype(BF16)


def _s5_pack(xall, modtab, layer, norm_g):
    bsz, s, dm = xall.shape
    ntb = s // TM
    rows = TM // S5_TC * bsz
    return pl.pallas_call(
        _s5_pack_kernel,
        grid=(ntb,),
        in_specs=[pl.BlockSpec((bsz, TM, dm), lambda t: (0, t, 0)),
                  pl.BlockSpec((8, 6, dm), lambda t: (layer, 0, 0)),
                  pl.BlockSpec((1, 1, dm), lambda t: (0, 0, 0))],
        out_specs=pl.BlockSpec((S5_GROUPS, rows, S5_TC * S5_GROUP), lambda t: (0, t, 0)),
        out_shape=jax.ShapeDtypeStruct((S5_GROUPS, ntb * rows, S5_TC * S5_GROUP), BF16),
        scratch_shapes=[pltpu.VMEM((bsz, TM, dm), F32), pltpu.VMEM((S5_LANE_VREGS, TM * bsz, 128), F32)],
        compiler_params=_cparams(("arbitrary",)),
        name="s5_pack",
    )(xall, modtab.reshape(DEPTH * 8, 6, dm), norm_g.reshape(1, 1, dm))


def _cmul(x, y):
    return x[0] * y[0] - x[1] * y[1], x[0] * y[1] + x[1] * y[0]


def _s5_weights(a_re, a_im, log_step, b_re, b_im, c_re, c_im, d_skip):
    tc, g, p, ch = S5_TC, S5_GROUPS, S5_STATE, S5_GROUP
    lam = (jnp.minimum(a_re, -1e-4), a_im)
    step = jnp.exp(log_step)
    mag = jnp.exp(lam[0] * step)
    lam_bar = (mag * jnp.cos(lam[1] * step), mag * jnp.sin(lam[1] * step))
    inv = 1.0 / (lam[0] * lam[0] + lam[1] * lam[1])
    coef = _cmul((lam_bar[0] - 1.0, lam_bar[1]), (lam[0] * inv, -lam[1] * inv))
    b_bar = _cmul((coef[0][..., None], coef[1][..., None]), (b_re, b_im))
    pw = [(jnp.ones_like(mag), jnp.zeros_like(mag))]
    for _ in range(tc):
        pw.append(_cmul(pw[-1], lam_bar))
    pw = (jnp.stack([q[0] for q in pw], axis=1), jnp.stack([q[1] for q in pw], axis=1))
    at = lambda d, idx: (pw[0][d, idx], pw[1][d, idx])
    lead = lambda z: jnp.moveaxis(z, -1, 0)
    cp = _cmul((lead(c_re)[:, :, :, None, :], lead(c_im)[:, :, :, None, :]),
               (jnp.transpose(pw[0][:, :tc], (3, 0, 2, 1))[..., None],
                jnp.transpose(pw[1][:, :tc], (3, 0, 2, 1))[..., None]))
    cp = tuple(z.reshape(p, 2, g, 1, tc * ch) for z in cp)
    bb = tuple(jnp.transpose(z, (2, 0, 1, 3))[..., None] for z in b_bar)
    kern = jnp.sum(bb[0] * cp[0] - bb[1] * cp[1], axis=0)
    rev = jnp.flip(kern[1].reshape(g, ch, tc, ch), axis=2).reshape(g, ch, tc * ch)
    zeros = lambda width: jnp.zeros((g, ch, width), F32)
    rows = []
    for s_ in range(tc):
        fwd_row = jnp.concatenate([zeros(s_ * ch), kern[0][..., :(tc - s_) * ch]], axis=-1)
        rev_row = jnp.concatenate([rev[..., (tc - 1 - s_) * ch:], zeros((tc - 1 - s_) * ch)], axis=-1)
        rows.append(fwd_row + rev_row)
    m = jnp.stack(rows, axis=1).reshape(g, tc * ch, tc * ch)
    skip = jnp.tile(d_skip.reshape(g, 1, ch), (1, tc, 1)).reshape(g, tc * ch)
    m = m + jnp.eye(tc * ch, dtype=F32)[None] * skip[:, :, None]
    steps = jnp.arange(tc)
    lift = lambda z: (z[0][..., None], z[1][..., None])
    e_f = _cmul(lift(at(0, tc - 1 - steps)), (b_bar[0][0][None], b_bar[1][0][None]))
    e_r = _cmul(lift(at(1, steps)), (b_bar[0][1][None], b_bar[1][1][None]))
    w_in = jnp.stack([e_f[0], e_r[0], e_f[1], e_r[1]], axis=0)
    w_in = jnp.transpose(w_in, (2, 1, 4, 0, 3)).reshape(g, tc * ch, 4 * p)
    mid = lambda z: (z[0][:, :, None, :], z[1][:, :, None, :])
    g_f = _cmul((c_re[0][None], c_im[0][None]), mid(at(0, 1 + steps)))
    g_r = _cmul((c_re[1][None], c_im[1][None]), mid(at(1, tc - steps)))
    w_re = jnp.stack([g_f[0], g_r[0]], axis=0)
    w_im = -jnp.stack([g_f[1], g_r[1]], axis=0)
    to_rows = lambda w: jnp.transpose(w, (2, 0, 4, 1, 3)).reshape(g, 2 * p, tc * ch)
    a1 = (jnp.concatenate([pw[0][0, tc], pw[0][1, tc]], axis=-1),
          jnp.concatenate([pw[1][0, tc], pw[1][1, tc]], axis=-1))
    a2 = _cmul(a1, a1)
    second = _s5_second_rows(8)
    par = tuple(jnp.where(second[None], a2[k][:, None, :], a1[k][:, None, :]) for k in range(2))
    one = tuple(jnp.broadcast_to(a1[k][:, None, :], (g, 8, 2 * p)) for k in range(2))
    a_rows = jnp.concatenate([jnp.concatenate(par, axis=-1), jnp.concatenate(one, axis=-1)], axis=1)
    return m.astype(BF16), w_in.astype(BF16), to_rows(w_re).astype(BF16), to_rows(w_im).astype(BF16), a_rows


def _s5_second_rows(nrows):
    row = lax.broadcasted_iota(I32, (nrows, 2 * S5_STATE), 0)
    lane = lax.broadcasted_iota(I32, (nrows, 2 * S5_STATE), 1)
    return (lane < S5_STATE) != ((row & 7) < 4)


def _s5_kernel(x_ref, m_ref, win_ref, wre_ref, wim_ref, a_ref, y_ref, ure_scr, uim_scr, sre_scr, sim_scr, *,
               nblock, nctx):
    p2 = 2 * S5_STATE
    nrows = nblock * 8
    second = _s5_second_rows(nrows)
    fwd = lax.broadcasted_iota(I32, (nrows, p2), 1) < S5_STATE
    for g in range(S5_GB):
        v = jnp.dot(x_ref[g], win_ref[g], preferred_element_type=F32)
        vre, vim = v[:, 0:p2], v[:, p2:2 * p2]
        a_re, a_im = a_ref[g, 8:9, 0:p2], a_ref[g, 8:9, p2:2 * p2]
        fre = jnp.where(fwd, pltpu.roll(vre, 4, 0), pltpu.roll(vre, nrows - 4, 0))
        fim = jnp.where(fwd, pltpu.roll(vim, 4, 0), pltpu.roll(vim, nrows - 4, 0))
        ure_scr[g] = vre + jnp.where(second, a_re * fre - a_im * fim, 0.0)
        uim_scr[g] = vim + jnp.where(second, a_re * fim + a_im * fre, 0.0)
    second8 = second[0:8]
    fwd8 = fwd[0:8]
    ap_re = [a_ref[g, 0:8, 0:p2] for g in range(S5_GB)]
    ap_im = [a_ref[g, 0:8, p2:2 * p2] for g in range(S5_GB)]

    def body(s, carry):
        jr = jnp.where(s < nctx, nctx - 1 - s, nblock - 1 + nctx - s)
        of = pl.multiple_of(s * 8, 8)
        orv = pl.multiple_of(jr * 8, 8)
        new = []
        for g in range(S5_GB):
            cre, cim = carry[g]
            ure = jnp.where(fwd8, ure_scr[g, pl.ds(of, 8), :], ure_scr[g, pl.ds(orv, 8), :])
            uim = jnp.where(fwd8, uim_scr[g, pl.ds(of, 8), :], uim_scr[g, pl.ds(orv, 8), :])
            zre = ap_re[g] * cre - ap_im[g] * cim + ure
            zim = ap_re[g] * cim + ap_im[g] * cre + uim
            rre, rim = pltpu.roll(zre, 4, 0), pltpu.roll(zim, 4, 0)
            ere, eim = jnp.where(second8, rre, cre), jnp.where(second8, rim, cim)
            sre_scr[g, pl.ds(of, 8), 0:S5_STATE] = ere[:, 0:S5_STATE]
            sre_scr[g, pl.ds(orv, 8), S5_STATE:p2] = ere[:, S5_STATE:p2]
            sim_scr[g, pl.ds(of, 8), 0:S5_STATE] = eim[:, 0:S5_STATE]
            sim_scr[g, pl.ds(orv, 8), S5_STATE:p2] = eim[:, S5_STATE:p2]
            new.append((jnp.where(second8, zre, rre), jnp.where(second8, zim, rim)))
        return tuple(new)

    zero = jnp.zeros((8, p2), F32)
    lax.fori_loop(0, nblock, body, tuple((zero, zero) for _ in range(S5_GB)))
    for g in range(S5_GB):
        y = (jnp.dot(x_ref[g], m_ref[g], preferred_element_type=F32)
             + jnp.dot(sre_scr[g].astype(BF16), wre_ref[g], preferred_element_type=F32)
             + jnp.dot(sim_scr[g].astype(BF16), wim_ref[g], preferred_element_type=F32))
        y_ref[g] = y.astype(BF16)


def _s5(xt, bsz, weights):
    g, rows, lanes = xt.shape
    assert 2 * bsz == 8
    nchunk = rows // bsz
    nctx = CTX_LEN // S5_TC
    assert nchunk % 2 == 0 and nctx % 2 == 0
    m, w_in, w_re, w_im, a_rows = weights
    p2 = 2 * S5_STATE
    gb = S5_GB
    wspec = lambda r, c: pl.BlockSpec((gb, r, c), lambda i: (i, 0, 0))
    return pl.pallas_call(
        functools.partial(_s5_kernel, nblock=nchunk // 2, nctx=nctx // 2),
        grid=(g // gb,),
        in_specs=[wspec(rows, lanes), wspec(lanes, lanes), wspec(lanes, 2 * p2), wspec(p2, lanes),
                  wspec(p2, lanes), wspec(16, 2 * p2)],
        out_specs=wspec(rows, lanes),
        out_shape=jax.ShapeDtypeStruct((g, rows, lanes), BF16),
        scratch_shapes=[pltpu.VMEM((gb, rows, p2), F32)] * 4,
        compiler_params=_cparams(("arbitrary",)),
        name="s5",
    )(xt, m, w_in, w_re, w_im, a_rows)


GLU_SAMPLES = 2


def _glu_kernel(y_ref, x_ref, mod_ref, n2g_ref, w_ref, b_ref, rwt_ref, x1_ref, h2_ref, lg_ref, slab, y_scr, *,
                bsz):
    dm = D_MODEL
    half = pl.program_id(1)
    nchunk = TM // S5_TC

    @pl.when(half == 0)
    def _():
        for j in range(S5_LANE_VREGS):
            for gl in range(S5_TC):
                g = (j // 2) * S5_TC + gl
                rows = y_ref[g, :, 128 * (j % 2):128 * (j % 2) + 128].astype(F32)
                slab[j, pl.ds(gl, nchunk * bsz, stride=S5_TC), :] = rows

    for k in range(GLU_SAMPLES):
        b = half * GLU_SAMPLES + k

        def chunk(c, carry):
            s0 = pl.multiple_of((c * bsz + b) * S5_TC, S5_TC)
            p = [[slab[j, pl.ds(s0 + 8 * h, 8), :] for j in range(S5_LANE_VREGS)] for h in range(2)]
            q = _segment_transpose(p)
            r0 = pl.multiple_of(c * S5_TC, S5_TC)
            for h in range(2):
                for j in range(S5_LANE_VREGS):
                    y_scr[pl.ds(r0 + 8 * h, 8), 128 * j:128 * j + 128] = q[h][j]
            return carry

        lax.fori_loop(0, nchunk, chunk, 0)
        for rows in _row_halves():
            z = jax.nn.gelu(y_scr[rows]).astype(BF16)
            zz = jnp.dot(z, w_ref[...], preferred_element_type=F32) + b_ref[...]
            glu = zz[:, :dm] * jax.nn.sigmoid(zz[:, dm:])
            x1_ref[k, rows], h2_ref[k, rows], lg_ref[k, :, rows] = _post_mixer(
                glu, x_ref[k, rows], mod_ref[b, 2:3, :], mod_ref[b, 3:4, :], mod_ref[b, 4:5, :], n2g_ref[0],
                rwt_ref[...])


def _glu(y, xall, modtab, layer, norm_g, glu_w, glu_b, router_w):
    bsz, s, dm = xall.shape
    n = s - CTX_LEN
    ntl = n // TM
    assert bsz == 2 * GLU_SAMPLES
    gs = GLU_SAMPLES
    rows = TM // S5_TC * bsz
    ctx_tiles = CTX_LEN // TM
    tok = pl.BlockSpec((gs, TM, dm), lambda t, h: (h, t, 0))
    x1, h2, lg = pl.pallas_call(
        functools.partial(_glu_kernel, bsz=bsz),
        grid=(ntl, bsz // gs),
        in_specs=[pl.BlockSpec((S5_GROUPS, rows, S5_TC * S5_GROUP), lambda t, h: (0, t + ctx_tiles, 0)),
                  pl.BlockSpec((gs, TM, dm), lambda t, h: (h, t + ctx_tiles, 0)),
                  pl.BlockSpec((8, 6, dm), lambda t, h: (layer, 0, 0)),
                  pl.BlockSpec((1, 1, dm), lambda t, h: (0, 0, 0)),
                  pl.BlockSpec((dm, 2 * dm), lambda t, h: (0, 0)),
                  pl.BlockSpec((1, 2 * dm), lambda t, h: (0, 0)),
                  pl.BlockSpec((N_EXPERTS, dm), lambda t, h: (0, 0))],
        out_specs=[tok, tok, pl.BlockSpec((gs, N_EXPERTS, TM), lambda t, h: (h, 0, t))],
        out_shape=[jax.ShapeDtypeStruct((bsz, n, dm), F32),
                   jax.ShapeDtypeStruct((bsz, n, dm), BF16),
                   jax.ShapeDtypeStruct((bsz, N_EXPERTS, n), F32)],
        scratch_shapes=[pltpu.VMEM((S5_LANE_VREGS, TM * bsz, 128), F32), pltpu.VMEM((TM, dm), F32)],
        compiler_params=_cparams(("arbitrary", "arbitrary")),
        name="glu",
    )(y, xall, modtab.reshape(DEPTH * 8, 6, dm), norm_g.reshape(1, 1, dm), glu_w.astype(BF16), glu_b[None, :],
      router_w.T)
    return x1, h2, jnp.transpose(lg, (1, 0, 2)).reshape(N_EXPERTS, bsz * n)


def kernel(x, c, ctx, c_ctx, mod_w, mod_b, norm1_g, norm2_g, ar_w_in, ar_w_out, lru_conv_w, lru_conv_b, lru_lam, lru_r_w, lru_r_b, lru_i_w, lru_i_b, attn_q_g, attn_k_g, attn_lam_q1, attn_lam_k1, attn_lam_q2, attn_lam_k2, attn_subln_g, s5_a_re, s5_a_im, s5_log_step, s5_b_re, s5_b_im, s5_c_re, s5_c_im, s5_d, s5_glu_w, s5_glu_b, router_w, router_bias, exp_w_gate, exp_w_up, exp_w_down, sh_w_gate, sh_w_up, sh_w_down):
    bsz, n, dm = x.shape
    assert dm == D_MODEL and ctx.shape[1] == CTX_LEN == TM and n % TM == 0 and bsz < 8
    assert mod_w.shape[0] == DEPTH == 2
    s = CTX_LEN + n
    ntb = s // TM
    modtab = _modulation(c, c_ctx, mod_w, mod_b)

    gate, xa, q, k, v = _inproj(x, ctx, modtab, norm1_g[0], ar_w_in[0], attn_q_g[0], attn_k_g[0])
    lru_args = (lru_conv_w[0], lru_conv_b[0], lru_lam[0], lru_r_w[0], lru_r_b[0], lru_i_w[0], lru_i_b[0])
    hf = _lru(xa, *lru_args, reverse=False)
    hr = _lru(xa, *lru_args, reverse=True)
    lam_init = 0.8 - 0.6 * math.exp(-0.3 * 0)
    lam_vecs = jnp.stack([attn_lam_q1[0], attn_lam_k1[0], attn_lam_q2[0], attn_lam_k2[0]], axis=0)
    yb = _attention(q, k, v, lam_vecs, attn_subln_g[0], lam_init)
    x1, h2, lgt = _outproj(hf, hr, gate, yb, x, ctx, modtab, norm2_g[0], ar_w_out[0], router_w[0])
    row0 = lambda i: jnp.where(i % ntb == 0, bsz, i // ntb)
    xall = _moe(h2.reshape(bsz * s, dm), lgt, x1.reshape(bsz * s, dm), modtab, 0, row0, router_bias[0],
                exp_w_gate, exp_w_up, exp_w_down, sh_w_gate[0], sh_w_up[0], sh_w_down[0])
    xall = xall.reshape(bsz, s, dm)

    xt = _s5_pack(xall, modtab, 1, norm1_g[1])
    weights = _s5_weights(s5_a_re[0], s5_a_im[0], s5_log_step[0], s5_b_re[0], s5_b_im[0], s5_c_re[0],
                          s5_c_im[0], s5_d[0])
    y = _s5(xt, bsz, weights)
    x1, h2, lgt = _glu(y, xall, modtab, 1, norm2_g[1], s5_glu_w[0], s5_glu_b[0], router_w[1])
    ntl = n // TM
    row1 = lambda i: i // ntl
    out = _moe(h2.reshape(bsz * n, dm), lgt, x1.reshape(bsz * n, dm), modtab, 1, row1, router_bias[1],
               exp_w_gate, exp_w_up, exp_w_down, sh_w_gate[1], sh_w_up[1], sh_w_down[1])
    return out.reshape(bsz, n, dm)
```

```python
import functools
import math

import jax
import jax.numpy as jnp
from jax import lax
from jax.experimental import pallas as pl
from jax.experimental.pallas import tpu as pltpu

F32, BF16, I32 = jnp.float32, jnp.bfloat16, jnp.int32
HIGHEST = lax.Precision.HIGHEST

D_MODEL = 1024
DEPTH = 2
GRID_W = 64
CTX_LEN = 256
EPS = 1e-6
LRU_WIDTH = 512
LRU_BLOCKS = 8
LRU_C = 8.0
ATT_HEADS = 4
ATT_DH = 64
ATT_DV = 128
ATT_QK_W = 512
ATT_WIDTH = 512
ROPE_BASE = 10000.0
EVEN_IN = 2 * LRU_WIDTH + 2 * ATT_QK_W + ATT_WIDTH
S5_GROUP = 16
S5_GROUPS = 64
S5_STATE = 64
S5_TC = 16
S5_GB = 4
N_EXPERTS = 64
TOP_K = 6
D_EXPERT = 256
ROUTED_SCALE = 2.5

TM = 256
SEG = 16
SLOT_CHUNK = 64
RMAX = -(-(TM * TOP_K + N_EXPERTS * (SEG - 1)) // SLOT_CHUNK) * SLOT_CHUNK
TR = 1024
TAIL_BITS = (TR // SEG - 1).bit_length()
ATT_TK = 1408
VMEM_LIMIT = 56 * 1024 * 1024


def _cparams(sem):
    return pltpu.CompilerParams(dimension_semantics=sem, vmem_limit_bytes=VMEM_LIMIT)


def _row_halves():
    return [slice(0, TM // 2), slice(TM // 2, TM)]


def _norm_mod(x, g, shift, scale):
    y = x * lax.rsqrt(jnp.mean(x * x, axis=-1, keepdims=True) + EPS) * g
    return y * (1.0 + scale) + shift


def _mod_kernel(c_ref, w_ref, b_ref, o_ref):
    c = c_ref[...]
    s = c * jax.nn.sigmoid(c)
    o_ref[0] = jnp.dot(s, w_ref[0], precision=HIGHEST, preferred_element_type=F32) + b_ref[0]


def _modulation(c, c_ctx, mod_w, mod_b):
    bsz, dm = c.shape
    cc = jnp.concatenate([c, c_ctx[None, :], jnp.zeros((8 - bsz - 1, dm), F32)], axis=0)
    out = pl.pallas_call(
        _mod_kernel,
        grid=(DEPTH, 6),
        in_specs=[pl.BlockSpec((8, dm), lambda l, j: (0, 0)),
                  pl.BlockSpec((1, dm, dm), lambda l, j: (l, 0, j)),
                  pl.BlockSpec((1, 1, dm), lambda l, j: (l, 0, j))],
        out_specs=pl.BlockSpec((1, 8, dm), lambda l, j: (l, 0, j)),
        out_shape=jax.ShapeDtypeStruct((DEPTH, 8, 6 * dm), F32),
        compiler_params=_cparams(("arbitrary", "arbitrary")),
        name="modulation",
    )(cc, mod_w, mod_b.reshape(DEPTH, 1, 6 * dm))
    return out.reshape(DEPTH * 8 * 6, 1, dm)


def _mod_spec(layer, part, row_fn):
    return pl.BlockSpec((1, 1, D_MODEL), lambda *ids: ((layer * 8 + row_fn(*ids)) * 6 + part, 0, 0))


def _rope_tables(n):
    rows = n // GRID_W
    r, col = jnp.meshgrid(jnp.arange(rows), jnp.arange(GRID_W), indexing="ij")
    pos = jnp.stack([r.reshape(-1), col.reshape(-1)], axis=-1).astype(F32)
    n_freq = ATT_DH // 4
    inv_freq = ROPE_BASE ** (-jnp.arange(n_freq, dtype=F32) / n_freq)
    ang = pos[:, :, None] * inv_freq
    cos, sin = jnp.cos(ang), jnp.sin(ang)
    zero = jnp.zeros_like(sin)
    cos64 = jnp.stack([cos, cos], axis=2).reshape(n, ATT_DH)
    sin_lo = jnp.stack([zero, sin], axis=2).reshape(n, ATT_DH)
    sin_hi = jnp.stack([-sin, zero], axis=2).reshape(n, ATT_DH)

    def full(tab, ctx_val):
        tab = jnp.concatenate([jnp.full((CTX_LEN, ATT_DH), ctx_val, F32), tab], axis=0)
        return jnp.concatenate([tab, tab], axis=1)

    return full(cos64, 1.0), full(sin_lo, 0.0), full(sin_hi, 0.0)


def _qk_post(t, gain, ones_bd, cos, sin_lo, sin_hi):
    ss = jnp.dot((t * t).astype(BF16), ones_bd, preferred_element_type=F32) * (1.0 / ATT_DH)
    tn = t * lax.rsqrt(ss + EPS) * gain
    w = tn.shape[1]
    return tn * cos + pltpu.roll(tn, 16, 1) * sin_lo + pltpu.roll(tn, w - 16, 1) * sin_hi


def _inproj_kernel(x_ref, c_ref, sh_ref, sc_ref, g_ref, w_ref, qg_ref, kg_ref, ones_ref,
                   cos_ref, slo_ref, shi_ref, gate_ref, xa_ref, q_ref, k_ref, v_ref):
    t = pl.program_id(1)
    lw, qw = LRU_WIDTH, ATT_QK_W
    tile4 = lambda a: jnp.concatenate([a, a, a, a], axis=1)
    ones_bd = ones_ref[...]
    for rows in _row_halves():
        x = jnp.where(t == 0, c_ref[0, rows], x_ref[0, rows])
        h = _norm_mod(x, g_ref[0], sh_ref[0], sc_ref[0])
        z = jnp.dot(h.astype(BF16), w_ref[...], preferred_element_type=F32)
        gate_ref[0, rows] = z[:, 0:lw].astype(BF16)
        xa_ref[0, rows] = z[:, lw:2 * lw].astype(BF16)
        cos, slo, shi = tile4(cos_ref[rows]), tile4(slo_ref[rows]), tile4(shi_ref[rows])
        q = _qk_post(z[:, 2 * lw:2 * lw + qw], qg_ref[...], ones_bd, cos, slo, shi)
        q_ref[0, rows] = (q * (ATT_DH ** -0.5 * math.log2(math.e))).astype(BF16)
        k = _qk_post(z[:, 2 * lw + qw:2 * lw + 2 * qw], kg_ref[...], ones_bd, cos, slo, shi)
        k_ref[0, rows] = k.astype(BF16)
        v_ref[0, rows] = z[:, 2 * lw + 2 * qw:].astype(BF16)


def _inproj(x, ctx, modtab, norm_g, w_in, q_g, k_g):
    bsz, n, dm = x.shape
    ntb = (CTX_LEN + n) // TM
    s = CTX_LEN + n
    row = lambda b, t: jnp.where(t == 0, bsz, b)
    cos, slo, shi = _rope_tables(n)
    ones_bd = jnp.kron(jnp.eye(ATT_QK_W // ATT_DH, dtype=F32), jnp.ones((ATT_DH, ATT_DH), F32)).astype(BF16)
    tile_g = lambda g: jnp.tile(g, ATT_QK_W // ATT_DH)[None, :]
    tab_spec = pl.BlockSpec((TM, 2 * ATT_DH), lambda b, t: (t, 0))
    out_spec = pl.BlockSpec((1, TM, LRU_WIDTH), lambda b, t: (b, t, 0))
    out_sds = jax.ShapeDtypeStruct((bsz, s, LRU_WIDTH), BF16)
    return pl.pallas_call(
        _inproj_kernel,
        grid=(bsz, ntb),
        in_specs=[pl.BlockSpec((1, TM, dm), lambda b, t: (b, jnp.maximum(t - 1, 0), 0)),
                  pl.BlockSpec((1, TM, dm), lambda b, t: (b, 0, 0)),
                  _mod_spec(0, 0, row), _mod_spec(0, 1, row),
                  pl.BlockSpec((1, 1, dm), lambda b, t: (0, 0, 0)),
                  pl.BlockSpec((dm, EVEN_IN), lambda b, t: (0, 0)),
                  pl.BlockSpec((1, ATT_QK_W), lambda b, t: (0, 0)),
                  pl.BlockSpec((1, ATT_QK_W), lambda b, t: (0, 0)),
                  pl.BlockSpec((ATT_QK_W, ATT_QK_W), lambda b, t: (0, 0)),
                  tab_spec, tab_spec, tab_spec],
        out_specs=[out_spec] * 5,
        out_shape=[out_sds] * 5,
        compiler_params=_cparams(("arbitrary", "arbitrary")),
        name="inproj",
    )(x, ctx, modtab, modtab, norm_g.reshape(1, 1, dm), w_in.astype(BF16), tile_g(q_g), tile_g(k_g), ones_bd,
      cos, slo, shi)


def _lru_kernel(xa_ref, xp_ref, xn_ref, cw_ref, cb_ref, lam_ref, w_ref, bias_ref, h_ref,
                a_scr, b_scr, c_scr, *, ntb, reverse):
    s = pl.program_id(1)
    ti = jnp.where(s == 0, 0, ntb - s) if reverse else s
    lw = LRU_WIDTH
    x = xa_ref[0].astype(F32)
    row = lax.broadcasted_iota(I32, (TM, lw), 0)
    has_prev = jnp.where(ti > 1, 1.0, 0.0)
    has_next = jnp.where((ti > 0) & (ti < ntb - 1), 1.0, 0.0)
    prev = xp_ref[0].astype(F32) * has_prev
    nxt = xn_ref[0].astype(F32) * has_next
    xm1 = jnp.where(row == 0, prev[7:8], pltpu.roll(x, 1, 0))
    xm2 = jnp.where(row == 0, prev[6:7], jnp.where(row == 1, prev[7:8], pltpu.roll(x, 2, 0)))
    xp1 = jnp.where(row == TM - 1, nxt[0:1], pltpu.roll(x, TM - 1, 0))
    cw = cw_ref[...]
    xc = cw[0:1] * xm2 + cw[1:2] * xm1 + cw[2:3] * x + cw[3:4] * xp1 + cb_ref[...]

    z = jnp.dot(xc.astype(BF16), w_ref[...], preferred_element_type=F32) + bias_ref[...]
    r = jax.nn.sigmoid(z[:, :lw])
    ig = jax.nn.sigmoid(z[:, lw:])
    neg_lam = -lam_ref[...]
    softplus = jnp.maximum(neg_lam, 0.0) + jnp.log1p(jnp.exp(-jnp.abs(neg_lam)))
    log_a = (-LRU_C) * r * softplus
    a = jnp.exp(log_a)
    b = jnp.sqrt(-jnp.tanh(log_a) * (a * a + 1.0)) * (ig * xc)

    r8 = row & 7
    for sft in (1, 2, 4):
        if reverse:
            a_s, b_s, m = pltpu.roll(a, TM - sft, 0), pltpu.roll(b, TM - sft, 0), r8 < 8 - sft
        else:
            a_s, b_s, m = pltpu.roll(a, sft, 0), pltpu.roll(b, sft, 0), r8 >= sft
        b = jnp.where(m, a * b_s + b, b)
        a = jnp.where(m, a * a_s, a)
    a_scr[...] = a
    b_scr[...] = b

    @pl.when(s == 0)
    def _():
        c_scr[...] = jnp.zeros_like(c_scr)

    ng = TM // 8

    def body(j, carry):
        g = (ng - 1 - j) if reverse else j
        off = pl.multiple_of(g * 8, 8)
        h = a_scr[pl.ds(off, 8), :] * carry + b_scr[pl.ds(off, 8), :]
        h_ref[0, pl.ds(off, 8), :] = h
        last = h[0:1] if reverse else h[7:8]
        return jnp.broadcast_to(last, (8, lw))

    c_scr[...] = lax.fori_loop(0, ng, body, c_scr[...], unroll=4)


def _lru(xa, conv_w, conv_b, lam, r_w, r_b, i_w, i_b, reverse):
    bsz, s, lw = xa.shape
    ntb = s // TM
    d = 1 if reverse else 0
    bd = lambda w: jax.scipy.linalg.block_diag(*[w[i] for i in range(LRU_BLOCKS)])
    w = jnp.concatenate([bd(r_w[d]), bd(i_w[d])], axis=1).astype(BF16)
    bias = jnp.concatenate([r_b[d], i_b[d]])[None, :]
    cw = jnp.concatenate([conv_w, jnp.zeros((4, lw), F32)], axis=0)
    if reverse:
        tile = lambda t: jnp.where(t == 0, 0, ntb - t)
    else:
        tile = lambda t: t
    nb8 = s // 8
    return pl.pallas_call(
        functools.partial(_lru_kernel, ntb=ntb, reverse=reverse),
        grid=(bsz, ntb),
        in_specs=[pl.BlockSpec((1, TM, lw), lambda b, t: (b, tile(t), 0)),
                  pl.BlockSpec((1, 8, lw), lambda b, t: (b, jnp.maximum(tile(t) * (TM // 8) - 1, 0), 0)),
                  pl.BlockSpec((1, 8, lw), lambda b, t: (b, jnp.minimum((tile(t) + 1) * (TM // 8), nb8 - 1), 0)),
                  pl.BlockSpec((8, lw), lambda b, t: (0, 0)),
                  pl.BlockSpec((1, lw), lambda b, t: (0, 0)),
                  pl.BlockSpec((1, lw), lambda b, t: (0, 0)),
                  pl.BlockSpec((lw, 2 * lw), lambda b, t: (0, 0)),
                  pl.BlockSpec((1, 2 * lw), lambda b, t: (0, 0))],
        out_specs=pl.BlockSpec((1, TM, lw), lambda b, t: (b, tile(t), 0)),
        out_shape=jax.ShapeDtypeStruct((bsz, s, lw), F32),
        scratch_shapes=[pltpu.VMEM((TM, lw), F32), pltpu.VMEM((TM, lw), F32), pltpu.VMEM((8, lw), F32)],
        compiler_params=_cparams(("arbitrary", "arbitrary")),
        name="lru_rev" if reverse else "lru_fwd",
    )(xa, xa, xa, cw, conv_b[None, :], lam[d][None, :], w, bias)


def _attn_kernel(lamv_ref, q_ref, k_ref, v_ref, sg_ref, o_ref, m_scr, l_scr, acc_scr, al_scr, s_scr, p_scr, *,
                 nkv, lam_init):
    t = pl.program_id(2)
    q = q_ref[0]
    dh = ATT_DH
    dv = ATT_DV
    m_scr[...] = jnp.full_like(m_scr, -jnp.inf)
    l_scr[...] = jnp.zeros_like(l_scr)
    acc_scr[...] = jnp.zeros_like(acc_scr)

    def scores(buf, off, size):
        kc = k_ref[0, pl.ds(off, size), :]
        for mi in range(2):
            s_scr[buf, mi, :, 0:size] = lax.dot_general(
                q[:, mi * dh:(mi + 1) * dh], kc[:, mi * dh:(mi + 1) * dh], (((1,), (1,)), ((), ())),
                preferred_element_type=F32)

    def softmax(buf, size):
        nlb = size // 128
        groups = [(slice(r * 16, (r + 1) * 16), mi) for r in range(TM // 16) for mi in range(2)]
        for rows, mi in groups:
            sc = s_scr[buf, mi, rows, 0:size]
            mx = functools.reduce(jnp.maximum, [sc[:, i * 128:(i + 1) * 128] for i in range(nlb)])
            m_old = m_scr[mi, rows, :]
            m_new = jnp.maximum(m_old, jnp.max(mx, axis=1, keepdims=True))
            m_scr[mi, rows, :] = m_new
            al_scr[mi, rows, :] = jnp.exp2(m_old - m_new)
        for rows, mi in groups:
            m_new = m_scr[mi, rows, :]
            ps = [jnp.exp2(s_scr[buf, mi, rows, i * 128:(i + 1) * 128] - m_new) for i in range(nlb)]
            l_scr[mi, rows, :] = al_scr[mi, rows, :] * l_scr[mi, rows, :] + functools.reduce(jnp.add, ps)
            p_scr[buf, mi, rows, 0:size] = jnp.concatenate(ps, axis=1).astype(BF16)

    def values(buf, off, size):
        vc = v_ref[0, pl.ds(off, size), :]
        pv = jnp.dot(p_scr[buf, :, :, 0:size].reshape(2 * TM, size), vc, preferred_element_type=F32)
        acc_scr[...] = al_scr[...] * acc_scr[...] + pv.reshape(2, TM, dv)

    @pl.when(t == 0)
    def _():
        scores(0, 0, CTX_LEN)
        softmax(0, CTX_LEN)
        values(0, 0, CTX_LEN)

    @pl.when(t > 0)
    def _():
        tk = ATT_TK
        nchunk = nkv // tk
        scores(0, 0, tk)
        for j in range(nchunk):
            if j + 1 < nchunk:
                scores((j + 1) % 2, (j + 1) * tk, tk)
            softmax(j % 2, tk)
            values(j % 2, j * tk, tk)

    lv = lamv_ref[...]
    lam = (jnp.exp(jnp.sum(lv[0:1] * lv[1:2], axis=1, keepdims=True))
           - jnp.exp(jnp.sum(lv[2:3] * lv[3:4], axis=1, keepdims=True)) + lam_init)
    l0 = jnp.sum(l_scr[0], axis=1, keepdims=True)
    l1 = jnp.sum(l_scr[1], axis=1, keepdims=True)
    o = acc_scr[0] / l0 - lam * (acc_scr[1] / l1)
    y = o * lax.rsqrt(jnp.mean(o * o, axis=-1, keepdims=True) + EPS) * sg_ref[...] * (1.0 - lam_init)
    o_ref[0] = y.astype(BF16)


def _attention(q, k, v, lam_vecs, subln_g, lam_init):
    bsz, s, _ = q.shape
    ntb = s // TM
    assert s % ATT_TK == 0
    dv = ATT_DV
    lamv = jnp.concatenate([jnp.pad(lam_vecs, ((0, 0), (0, dv - ATT_DH))), jnp.zeros((4, dv), F32)], axis=0)
    return pl.pallas_call(
        functools.partial(_attn_kernel, nkv=s, lam_init=lam_init),
        grid=(bsz, ATT_HEADS, ntb),
        in_specs=[pl.BlockSpec((8, dv), lambda b, h, t: (0, 0)),
                  pl.BlockSpec((1, TM, dv), lambda b, h, t: (b, t, h)),
                  pl.BlockSpec((1, s, dv), lambda b, h, t: (b, 0, h)),
                  pl.BlockSpec((1, s, dv), lambda b, h, t: (b, 0, h)),
                  pl.BlockSpec((1, dv), lambda b, h, t: (0, 0))],
        out_specs=pl.BlockSpec((1, TM, dv), lambda b, h, t: (b, t, h)),
        out_shape=jax.ShapeDtypeStruct((bsz, s, ATT_WIDTH), BF16),
        scratch_shapes=[pltpu.VMEM((2, TM, dv), F32)] * 4
        + [pltpu.VMEM((2, 2, TM, ATT_TK), F32), pltpu.VMEM((2, 2, TM, ATT_TK), BF16)],
        compiler_params=_cparams(("arbitrary", "arbitrary", "arbitrary")),
        name="diff_attention",
    )(lamv, q, k, v, subln_g[None, :])


def _post_mixer(y, x, g1, sh2, sc2, n2g, rwt):
    x1 = x + g1 * y
    h2 = _norm_mod(x1, n2g, sh2, sc2)
    logits = lax.dot_general(rwt, h2, (((1,), (1,)), ((), ())), precision=HIGHEST, preferred_element_type=F32)
    return x1, h2.astype(BF16), logits


def _outproj_kernel(hf_ref, hr_ref, gate_ref, yb_ref, x_ref, c_ref, g1_ref, sh_ref, sc_ref, n2g_ref, w_ref,
                    rwt_ref, x1_ref, h2_ref, lg_ref):
    t = pl.program_id(1)
    lw = LRU_WIDTH
    for rows in _row_halves():
        x = jnp.where(t == 0, c_ref[0, rows], x_ref[0, rows])
        ya = ((hf_ref[0, rows] + hr_ref[0, rows]) * jax.nn.gelu(gate_ref[0, rows].astype(F32))).astype(BF16)
        y = (jnp.dot(ya, w_ref[0:lw, :], preferred_element_type=F32)
             + jnp.dot(yb_ref[0, rows], w_ref[lw:, :], preferred_element_type=F32))
        x1_ref[0, rows], h2_ref[0, rows], lg_ref[:, rows] = _post_mixer(
            y, x, g1_ref[0], sh_ref[0], sc_ref[0], n2g_ref[0], rwt_ref[...])


def _outproj(hf, hr, gate, yb, x, ctx, modtab, norm_g, w_out, router_w):
    bsz, s, lw = hf.shape
    dm = D_MODEL
    ntb = s // TM
    row = lambda b, t: jnp.where(t == 0, bsz, b)
    half = pl.BlockSpec((1, TM, lw), lambda b, t: (b, t, 0))
    return pl.pallas_call(
        _outproj_kernel,
        grid=(bsz, ntb),
        in_specs=[half, half, half, half,
                  pl.BlockSpec((1, TM, dm), lambda b, t: (b, jnp.maximum(t - 1, 0), 0)),
                  pl.BlockSpec((1, TM, dm), lambda b, t: (b, 0, 0)),
                  _mod_spec(0, 2, row), _mod_spec(0, 3, row), _mod_spec(0, 4, row),
                  pl.BlockSpec((1, 1, dm), lambda b, t: (0, 0, 0)),
                  pl.BlockSpec((2 * lw, dm), lambda b, t: (0, 0)),
                  pl.BlockSpec((N_EXPERTS, dm), lambda b, t: (0, 0))],
        out_specs=[pl.BlockSpec((1, TM, dm), lambda b, t: (b, t, 0)),
                   pl.BlockSpec((1, TM, dm), lambda b, t: (b, t, 0)),
                   pl.BlockSpec((N_EXPERTS, TM), lambda b, t: (0, b * ntb + t))],
        out_shape=[jax.ShapeDtypeStruct((bsz, s, dm), F32),
                   jax.ShapeDtypeStruct((bsz, s, dm), BF16),
                   jax.ShapeDtypeStruct((N_EXPERTS, bsz * s), F32)],
        compiler_params=_cparams(("arbitrary", "arbitrary")),
        name="outproj",
    )(hf, hr, gate, yb, x, ctx, modtab, modtab, modtab, norm_g.reshape(1, 1, dm), w_out.astype(BF16),
      router_w.T)


def _route_kernel(lg_ref, bias_ref, tri_ref, low_ref, pos_ref, gate_ref, cnt_ref):
    ne = N_EXPERTS
    aff = jax.nn.sigmoid(lg_ref[...])
    work = aff + bias_ref[:, 0:1]
    eidx = lax.broadcasted_iota(I32, (ne, TM), 0)
    sels = []
    for _ in range(TOP_K):
        mx = jnp.max(work, axis=0, keepdims=True)
        am = jnp.min(jnp.where(work == mx, eidx, ne), axis=0, keepdims=True)
        sk = eidx == am
        sels.append(sk)
        work = jnp.where(sk, -jnp.inf, work)
    sel = sels[0]
    for sk in sels[1:]:
        sel = sel | sk
    self = jnp.where(sel, 1.0, 0.0)
    s_sel = aff * self
    gates = s_sel / jnp.sum(s_sel, axis=0, keepdims=True) * ROUTED_SCALE
    rank = jnp.dot(self.astype(BF16), tri_ref[...], preferred_element_type=F32)
    cnt = jnp.sum(self, axis=1, keepdims=True)
    cnt_seg = jnp.floor((cnt + (SEG - 1)) * (1.0 / SEG)) * SEG
    cnt_b = jnp.broadcast_to(cnt_seg, (ne, 128))
    seg_off = jnp.dot(low_ref[...], cnt_b.astype(BF16), preferred_element_type=F32)
    lpos = seg_off[:, 0:1] + rank
    pos_rows, gate_rows = [], []
    for sk in sels:
        pos_rows.append(jnp.sum(jnp.where(sk, lpos, 0.0), axis=0, keepdims=True))
        gate_rows.append(jnp.sum(jnp.where(sk, gates, 0.0), axis=0, keepdims=True))
    for _ in range(8 - TOP_K):
        pos_rows.append(jnp.full((1, TM), -1.0, F32))
        gate_rows.append(jnp.zeros((1, TM), F32))
    pos_ref[...] = jnp.concatenate(pos_rows, axis=0).astype(I32)
    gate_ref[...] = jnp.concatenate(gate_rows, axis=0)
    cnt_ref[0] = cnt_b.astype(I32)


def _route(lgt, router_bias):
    ne, t = lgt.shape
    nt = t // TM
    tri = jnp.triu(jnp.ones((TM, TM), F32), k=1).astype(BF16)
    low = jnp.tril(jnp.ones((ne, ne), F32), k=-1).astype(BF16)
    bias = jnp.broadcast_to(router_bias[:, None], (ne, 128))
    pos, gate, cnt = pl.pallas_call(
        _route_kernel,
        grid=(nt,),
        in_specs=[pl.BlockSpec((ne, TM), lambda i: (0, i)),
                  pl.BlockSpec((ne, 128), lambda i: (0, 0)),
                  pl.BlockSpec((TM, TM), lambda i: (0, 0)),
                  pl.BlockSpec((ne, ne), lambda i: (0, 0))],
        out_specs=[pl.BlockSpec((8, TM), lambda i: (0, i)),
                   pl.BlockSpec((8, TM), lambda i: (0, i)),
                   pl.BlockSpec((1, ne, 128), lambda i: (i, 0, 0))],
        out_shape=[jax.ShapeDtypeStruct((8, t), I32),
                   jax.ShapeDtypeStruct((8, t), F32),
                   jax.ShapeDtypeStruct((nt, ne, 128), I32)],
        compiler_params=_cparams(("arbitrary",)),
        name="route",
    )(lgt, bias, tri, low)
    return pos, gate, cnt[:, :, 0]


def _moe_layout(cnt_seg, nblk_max):
    tot = jnp.sum(cnt_seg, axis=0)
    region = (tot + TR - 1) // TR * TR
    region_end = jnp.cumsum(region)
    goff = (region_end - region)[None, :] + jnp.cumsum(cnt_seg, axis=0) - cnt_seg
    packed = ((goff // SEG) << 5) | (cnt_seg // SEG)
    nblk = region_end[-1] // TR
    blk = jnp.arange(nblk_max, dtype=I32)
    blk_e = jnp.sum((region_end[None, :] // TR <= blk[:, None]).astype(I32), axis=1)
    blk_e = jnp.minimum(blk_e, N_EXPERTS - 1).astype(I32)
    tile_info = jnp.sum(cnt_seg, axis=1) | (jnp.any(cnt_seg >= 64, axis=1).astype(I32) << 16)
    tails = (((region_end - region + tot) // SEG) << TAIL_BITS) | ((region - tot) // SEG)
    tails = jnp.concatenate([tails, nblk[None]]).astype(I32)
    return packed.reshape(-1).astype(I32), tile_info.astype(I32), tails, blk_e, nblk.astype(I32).reshape(1)


def _segment_copy(stage, hbm, sem, to_sorted, lo, go, size):
    lo, go = pl.multiple_of(lo, SEG), pl.multiple_of(go, SEG)
    a, b = stage.at[pl.ds(lo, size)], hbm.at[pl.ds(go, size)]
    return pltpu.make_async_copy(a, b, sem) if to_sorted else pltpu.make_async_copy(b, a, sem)


def _segment_unpack(pk):
    return (pk & 31) * SEG, (pk >> 5) * SEG


def _segment_starts(pk_ref, tile, stage, hbm, sem, to_sorted, valid=None):
    loff = [0]

    def step(e):
        cnt, goff = _segment_unpack(pk_ref[tile * N_EXPERTS + e])
        done = (cnt >> 6) << 6
        for size in (32, 16):
            bit = (cnt & size) != 0
            if valid is not None:
                bit = bit & valid

            @pl.when(bit)
            def _(lo=loff[0] + done, go=goff + done, size=size):
                _segment_copy(stage, hbm, sem, to_sorted, lo, go, size).start()

            done = done + jnp.where(bit, size, 0)
        loff[0] = loff[0] + cnt

    return [functools.partial(step, e) for e in range(N_EXPERTS)]


def _interleave(*step_lists):
    total = max(len(steps) for steps in step_lists)
    done = [0] * len(step_lists)
    for t in range(1, total + 1):
        for k, steps in enumerate(step_lists):
            upto = len(steps) * t // total
            for step in steps[done[k]:upto]:
                step()
            done[k] = upto


def _segment_starts_long(pk_ref, tile, info, stage, hbm, sem, to_sorted):
    @pl.when((info >> 16) != 0)
    def _():
        def expert(e, loff):
            cnt, goff = _segment_unpack(pk_ref[tile * N_EXPERTS + e])

            def chunk(j, carry):
                _segment_copy(stage, hbm, sem, to_sorted, loff + j * 64, goff + j * 64, 64).start()
                return carry

            lax.fori_loop(0, cnt >> 6, chunk, 0)
            return loff + cnt

        lax.fori_loop(0, N_EXPERTS, expert, 0)


def _segment_wait(total, stage, hbm, sem, to_sorted):
    size = 1 << (RMAX.bit_length() - 1)
    while size >= SEG:
        @pl.when((total & size) != 0)
        def _(size=size):
            _segment_copy(stage, hbm, sem, to_sorted, 0, 0, size).wait()

        size //= 2


def _fill_slot_matrix(dst_ref, pos, weight_rows=None):
    ch = SLOT_CHUNK
    riota = lax.broadcasted_iota(I32, (ch, TM), 0)

    def step(c):
        local = pos - c * ch
        out = jnp.zeros((ch, TM), F32)
        for k in range(TOP_K):
            w = 1.0 if weight_rows is None else weight_rows[k:k + 1]
            out = jnp.where(riota == local[k:k + 1], w, out)
        dst_ref[c * ch:(c + 1) * ch, :] = out.astype(BF16)

    return [functools.partial(step, c) for c in range(RMAX // ch)]


def _zero_fill(tail_ref, zeros, xs_ref, sem, fn):
    def expert(e, carry):
        pk = tail_ref[e]
        off = (pk >> TAIL_BITS) * SEG

        def chunk(j, c):
            dst = xs_ref.at[pl.ds(pl.multiple_of(off + j * SEG, SEG), SEG)]
            fn(pltpu.make_async_copy(zeros.at[pl.ds(0, SEG)], dst, sem))
            return c

        return lax.fori_loop(0, pk & ((1 << TAIL_BITS) - 1), chunk, carry)

    lax.fori_loop(0, N_EXPERTS, expert, 0)

    def block(j, carry):
        fn(pltpu.make_async_copy(zeros, xs_ref.at[pl.ds(pl.multiple_of(j * TR, TR), TR)], sem))
        return carry

    lax.fori_loop(tail_ref[N_EXPERTS], xs_ref.shape[0] // TR, block, 0)


def _dispatch_kernel(pk_ref, tot_ref, tail_ref, pos_ref, h_ref, xs_ref, stage, zeros, slots, sem, zsem):
    i = pl.program_id(0)
    slot = i % 2

    @pl.when(i == 0)
    def _():
        zeros[...] = jnp.zeros_like(zeros)
        _zero_fill(tail_ref, zeros, xs_ref, zsem, lambda c: c.start())

    for step in _fill_slot_matrix(slots, pos_ref[...]):
        step()
    stage[slot] = jnp.dot(slots[...], h_ref[...], preferred_element_type=F32).astype(BF16)
    prev_rows = jnp.where(i > 0, tot_ref[jnp.maximum(i - 1, 0)] & 0xFFFF, 0)
    _segment_wait(prev_rows, stage.at[1 - slot], xs_ref, sem.at[1 - slot], True)
    this = (stage.at[slot], xs_ref, sem.at[slot], True)
    for step in _segment_starts(pk_ref, i, *this):
        step()
    _segment_starts_long(pk_ref, i, tot_ref[i], *this)

    @pl.when(i == pl.num_programs(0) - 1)
    def _():
        _segment_wait(tot_ref[i] & 0xFFFF, *this)
        _zero_fill(tail_ref, zeros, xs_ref, zsem, lambda c: c.wait())


def _dispatch(packed, tile_rows, tails, pos, h2, nrows):
    t, dm = h2.shape
    nt = t // TM
    return pl.pallas_call(
        _dispatch_kernel,
        grid_spec=pltpu.PrefetchScalarGridSpec(
            num_scalar_prefetch=3,
            grid=(nt,),
            in_specs=[pl.BlockSpec((8, TM), lambda i, pk, tot, tail: (0, i)),
                      pl.BlockSpec((TM, dm), lambda i, pk, tot, tail: (i, 0))],
            out_specs=pl.BlockSpec(memory_space=pl.ANY),
            scratch_shapes=[pltpu.VMEM((2, RMAX, dm), BF16), pltpu.VMEM((TR, dm), BF16),
                            pltpu.VMEM((RMAX, TM), BF16),
                            pltpu.SemaphoreType.DMA((2,)), pltpu.SemaphoreType.DMA]),
        out_shape=jax.ShapeDtypeStruct((nrows, dm), BF16),
        compiler_params=_cparams(("arbitrary",)),
        name="moe_dispatch",
    )(packed, tile_rows, tails, pos, h2)


def _expert_kernel(be_ref, nb_ref, x_ref, wg_ref, wu_ref, wd_ref, y_ref, wg_s, wu_s, wd_s):
    j = pl.program_id(0)
    changed = be_ref[j] != be_ref[jnp.maximum(j - 1, 0)]

    @pl.when((j == 0) | changed)
    def _():
        wg_s[...] = wg_ref[0].astype(BF16)
        wu_s[...] = wu_ref[0].astype(BF16)
        wd_s[...] = wd_ref[0].astype(BF16)

    @pl.when(j < nb_ref[0])
    def _():
        x = x_ref[...]
        g = jnp.dot(x, wg_s[...], preferred_element_type=F32)
        u = jnp.dot(x, wu_s[...], preferred_element_type=F32)
        a = (g * jax.nn.sigmoid(g) * u).astype(BF16)
        y_ref[...] = jnp.dot(a, wd_s[...], preferred_element_type=F32).astype(BF16)


def _experts(blk_e, nblk, xs, layer, w_gate, w_up, w_down):
    nrows, dm = xs.shape
    nblk_max = nrows // TR
    de = D_EXPERT
    row_blk = lambda j, be, nb: (jnp.minimum(j, nb[0] - 1), 0)
    return pl.pallas_call(
        _expert_kernel,
        grid_spec=pltpu.PrefetchScalarGridSpec(
            num_scalar_prefetch=2,
            grid=(nblk_max,),
            in_specs=[pl.BlockSpec((TR, dm), row_blk),
                      pl.BlockSpec((None, 1, dm, de), lambda j, be, nb: (layer, be[j], 0, 0)),
                      pl.BlockSpec((None, 1, dm, de), lambda j, be, nb: (layer, be[j], 0, 0)),
                      pl.BlockSpec((None, 1, de, dm), lambda j, be, nb: (layer, be[j], 0, 0))],
            out_specs=pl.BlockSpec((TR, dm), row_blk),
            scratch_shapes=[pltpu.VMEM((dm, de), BF16), pltpu.VMEM((dm, de), BF16), pltpu.VMEM((de, dm), BF16)]),
        out_shape=jax.ShapeDtypeStruct((nrows, dm), BF16),
        input_output_aliases={2: 0},
        compiler_params=_cparams(("arbitrary",)),
        name="moe_experts",
    )(blk_e, nblk, xs, w_gate, w_up, w_down)


def _combine_kernel(pk_ref, tot_ref, pos_ref, gate_ref, h_ref, x1_ref, g2_ref, wsg_ref, wsu_ref, wsd_ref, ys_ref,
                    o_ref, stage, gates, sem):
    i = pl.program_id(0)
    slot = i % 2
    last = pl.num_programs(0) - 1

    @pl.when(i == 0)
    def _():
        stage[...] = jnp.zeros_like(stage)
        for step in _segment_starts(pk_ref, 0, stage.at[0], ys_ref, sem.at[0], False):
            step()
        _segment_starts_long(pk_ref, 0, tot_ref[0], stage.at[0], ys_ref, sem.at[0], False)

    nxt = jnp.minimum(i + 1, last)
    _interleave(_fill_slot_matrix(gates, pos_ref[...], gate_ref[...]),
                _segment_starts(pk_ref, nxt, stage.at[1 - slot], ys_ref, sem.at[1 - slot], False))
    h = h_ref[...]
    g = jnp.dot(h, wsg_ref[...], preferred_element_type=F32)
    u = jnp.dot(h, wsu_ref[...], preferred_element_type=F32)
    shared = jnp.dot((g * jax.nn.sigmoid(g) * u).astype(BF16), wsd_ref[...], preferred_element_type=F32)
    _segment_wait(tot_ref[i] & 0xFFFF, stage.at[slot], ys_ref, sem.at[slot], False)
    routed = lax.dot_general(gates[...], stage[slot], (((0,), (0,)), ((), ())), preferred_element_type=F32)
    o_ref[...] = x1_ref[...] + g2_ref[0] * (routed + shared)
    _segment_starts_long(pk_ref, nxt, tot_ref[nxt], stage.at[1 - slot], ys_ref, sem.at[1 - slot], False)

    @pl.when(i == last)
    def _():
        _segment_wait(tot_ref[i] & 0xFFFF, stage.at[1 - slot], ys_ref, sem.at[1 - slot], False)


def _combine(packed, tile_rows, pos, gate, h2, x1, modtab, layer, row_fn, ys, ws_gate, ws_up, ws_down):
    t, dm = h2.shape
    nt = t // TM
    de = D_EXPERT
    const = lambda shape: pl.BlockSpec(shape, lambda i, pk, tot: (0,) * len(shape))
    return pl.pallas_call(
        _combine_kernel,
        grid_spec=pltpu.PrefetchScalarGridSpec(
            num_scalar_prefetch=2,
            grid=(nt,),
            in_specs=[pl.BlockSpec((8, TM), lambda i, pk, tot: (0, i)),
                      pl.BlockSpec((8, TM), lambda i, pk, tot: (0, i)),
                      pl.BlockSpec((TM, dm), lambda i, pk, tot: (i, 0)),
                      pl.BlockSpec((TM, dm), lambda i, pk, tot: (i, 0)),
                      pl.BlockSpec((1, 1, dm), lambda i, pk, tot: ((layer * 8 + row_fn(i)) * 6 + 5, 0, 0)),
                      const((dm, de)), const((dm, de)), const((de, dm)),
                      pl.BlockSpec(memory_space=pl.ANY)],
            out_specs=pl.BlockSpec((TM, dm), lambda i, pk, tot: (i, 0)),
            scratch_shapes=[pltpu.VMEM((2, RMAX, dm), BF16), pltpu.VMEM((RMAX, TM), BF16),
                            pltpu.SemaphoreType.DMA((2,))]),
        out_shape=jax.ShapeDtypeStruct((t, dm), F32),
        compiler_params=_cparams(("arbitrary",)),
        name="moe_combine",
    )(packed, tile_rows, pos, gate, h2, x1, modtab, ws_gate.astype(BF16), ws_up.astype(BF16),
      ws_down.astype(BF16), ys)


def _moe(h2, lgt, x1, modtab, layer, row_fn, router_bias, w_gate, w_up, w_down, ws_gate, ws_up, ws_down):
    t = h2.shape[0]
    nt = t // TM
    max_rows = t * TOP_K + nt * N_EXPERTS * (SEG - 1) + N_EXPERTS * (TR - 1)
    nblk_max = -(-max_rows // TR)
    pos, gate, cnt_seg = _route(lgt, router_bias)
    packed, tile_rows, tails, blk_e, nblk = _moe_layout(cnt_seg, nblk_max)
    xs = _dispatch(packed, tile_rows, tails, pos, h2, nblk_max * TR)
    ys = _experts(blk_e, nblk, xs, layer, w_gate, w_up, w_down)
    return _combine(packed, tile_rows, pos, gate, h2, x1, modtab, layer, row_fn, ys, ws_gate, ws_up, ws_down)


S5_LANE_VREGS = D_MODEL // 128


def _segment_transpose(p):
    row = lax.broadcasted_iota(I32, (8, 128), 0)
    seg = lax.broadcasted_iota(I32, (8, 128), 1) // S5_GROUP
    for s in (4, 2, 1):
        m_up = ((row & s) == 0) & ((seg & s) != 0)
        m_dn = ((row & s) != 0) & ((seg & s) == 0)

        def swap(x):
            if s == 4:
                return jnp.where(m_up | m_dn, pltpu.roll(pltpu.roll(x, 4, 0), 4 * S5_GROUP, 1), x)
            up = pltpu.roll(pltpu.roll(x, 8 - s, 0), S5_GROUP * s, 1)
            dn = pltpu.roll(pltpu.roll(x, s, 0), 128 - S5_GROUP * s, 1)
            return jnp.where(m_up, up, jnp.where(m_dn, dn, x))

        p = [[swap(x) for x in half] for half in p]
    return [[p[1 - h][j - 1 + 2 * h] if (j % 2) != h else p[h][j] for j in range(S5_LANE_VREGS)] for h in range(2)]


def _s5_pack_kernel(x_ref, mod_ref, g_ref, xt_ref, u_scr, slab):
    t = pl.program_id(0)
    bsz = x_ref.shape[0]
    nchunk = TM // S5_TC
    for b in range(bsz):
        shift = jnp.where(t == 0, mod_ref[bsz, 0:1, :], mod_ref[b, 0:1, :])
        scale = jnp.where(t == 0, mod_ref[bsz, 1:2, :], mod_ref[b, 1:2, :])
        u_scr[b] = _norm_mod(x_ref[b], g_ref[0], shift, scale)

    def chunk(c, carry):
        r0 = pl.multiple_of(c * S5_TC, S5_TC)
        for b in range(bsz):
            p = [[u_scr[b, pl.ds(r0 + 8 * h, 8), 128 * j:128 * j + 128] for j in range(S5_LANE_VREGS)]
                 for h in range(2)]
            q = _segment_transpose(p)
            s0 = pl.multiple_of((c * bsz + b) * S5_TC, S5_TC)
            for h in range(2):
                for j in range(S5_LANE_VREGS):
                    slab[j, pl.ds(s0 + 8 * h, 8), :] = q[h][j]
        return carry

    lax.fori_loop(0, nchunk, chunk, 0, unroll=2)
    for j in range(S5_LANE_VREGS):
        for gl in range(S5_TC):
            g = (j // 2) * S5_TC + gl
            rows = slab[j, pl.ds(gl, nchunk * bsz, stride=S5_TC), :]
            xt_ref[g, :, 128 * (j % 2):128 * (j % 2) + 128] = rows.astype(BF16)


def _s5_pack(xall, modtab, layer, norm_g):
    bsz, s, dm = xall.shape
    ntb = s // TM
    rows = TM // S5_TC * bsz
    return pl.pallas_call(
        _s5_pack_kernel,
        grid=(ntb,),
        in_specs=[pl.BlockSpec((bsz, TM, dm), lambda t: (0, t, 0)),
                  pl.BlockSpec((8, 6, dm), lambda t: (layer, 0, 0)),
                  pl.BlockSpec((1, 1, dm), lambda t: (0, 0, 0))],
        out_specs=pl.BlockSpec((S5_GROUPS, rows, S5_TC * S5_GROUP), lambda t: (0, t, 0)),
        out_shape=jax.ShapeDtypeStruct((S5_GROUPS, ntb * rows, S5_TC * S5_GROUP), BF16),
        scratch_shapes=[pltpu.VMEM((bsz, TM, dm), F32), pltpu.VMEM((S5_LANE_VREGS, TM * bsz, 128), F32)],
        compiler_params=_cparams(("arbitrary",)),
        name="s5_pack",
    )(xall, modtab.reshape(DEPTH * 8, 6, dm), norm_g.reshape(1, 1, dm))


def _cmul(x, y):
    return x[0] * y[0] - x[1] * y[1], x[0] * y[1] + x[1] * y[0]


def _s5_weights(a_re, a_im, log_step, b_re, b_im, c_re, c_im, d_skip):
    tc, g, p, ch = S5_TC, S5_GROUPS, S5_STATE, S5_GROUP
    lam = (jnp.minimum(a_re, -1e-4), a_im)
    step = jnp.exp(log_step)
    mag = jnp.exp(lam[0] * step)
    lam_bar = (mag * jnp.cos(lam[1] * step), mag * jnp.sin(lam[1] * step))
    inv = 1.0 / (lam[0] * lam[0] + lam[1] * lam[1])
    coef = _cmul((lam_bar[0] - 1.0, lam_bar[1]), (lam[0] * inv, -lam[1] * inv))
    b_bar = _cmul((coef[0][..., None], coef[1][..., None]), (b_re, b_im))
    pw = [(jnp.ones_like(mag), jnp.zeros_like(mag))]
    for _ in range(tc):
        pw.append(_cmul(pw[-1], lam_bar))
    pw = (jnp.stack([q[0] for q in pw], axis=1), jnp.stack([q[1] for q in pw], axis=1))
    at = lambda d, idx: (pw[0][d, idx], pw[1][d, idx])
    lead = lambda z: jnp.moveaxis(z, -1, 0)
    cp = _cmul((lead(c_re)[:, :, :, None, :], lead(c_im)[:, :, :, None, :]),
               (jnp.transpose(pw[0][:, :tc], (3, 0, 2, 1))[..., None],
                jnp.transpose(pw[1][:, :tc], (3, 0, 2, 1))[..., None]))
    cp = tuple(z.reshape(p, 2, g, 1, tc * ch) for z in cp)
    bb = tuple(jnp.transpose(z, (2, 0, 1, 3))[..., None] for z in b_bar)
    kern = jnp.sum(bb[0] * cp[0] - bb[1] * cp[1], axis=0)
    rev = jnp.flip(kern[1].reshape(g, ch, tc, ch), axis=2).reshape(g, ch, tc * ch)
    zeros = lambda width: jnp.zeros((g, ch, width), F32)
    rows = []
    for s_ in range(tc):
        fwd_row = jnp.concatenate([zeros(s_ * ch), kern[0][..., :(tc - s_) * ch]], axis=-1)
        rev_row = jnp.concatenate([rev[..., (tc - 1 - s_) * ch:], zeros((tc - 1 - s_) * ch)], axis=-1)
        rows.append(fwd_row + rev_row)
    m = jnp.stack(rows, axis=1).reshape(g, tc * ch, tc * ch)
    skip = jnp.tile(d_skip.reshape(g, 1, ch), (1, tc, 1)).reshape(g, tc * ch)
    m = m + jnp.eye(tc * ch, dtype=F32)[None] * skip[:, :, None]
    steps = jnp.arange(tc)
    lift = lambda z: (z[0][..., None], z[1][..., None])
    e_f = _cmul(lift(at(0, tc - 1 - steps)), (b_bar[0][0][None], b_bar[1][0][None]))
    e_r = _cmul(lift(at(1, steps)), (b_bar[0][1][None], b_bar[1][1][None]))
    w_in = jnp.stack([e_f[0], e_r[0], e_f[1], e_r[1]], axis=0)
    w_in = jnp.transpose(w_in, (2, 1, 4, 0, 3)).reshape(g, tc * ch, 4 * p)
    mid = lambda z: (z[0][:, :, None, :], z[1][:, :, None, :])
    g_f = _cmul((c_re[0][None], c_im[0][None]), mid(at(0, 1 + steps)))
    g_r = _cmul((c_re[1][None], c_im[1][None]), mid(at(1, tc - steps)))
    w_re = jnp.stack([g_f[0], g_r[0]], axis=0)
    w_im = -jnp.stack([g_f[1], g_r[1]], axis=0)
    to_rows = lambda w: jnp.transpose(w, (2, 0, 4, 1, 3)).reshape(g, 2 * p, tc * ch)
    a1 = (jnp.concatenate([pw[0][0, tc], pw[0][1, tc]], axis=-1),
          jnp.concatenate([pw[1][0, tc], pw[1][1, tc]], axis=-1))
    a2 = _cmul(a1, a1)
    second = _s5_second_rows(8)
    par = tuple(jnp.where(second[None], a2[k][:, None, :], a1[k][:, None, :]) for k in range(2))
    one = tuple(jnp.broadcast_to(a1[k][:, None, :], (g, 8, 2 * p)) for k in range(2))
    a_rows = jnp.concatenate([jnp.concatenate(par, axis=-1), jnp.concatenate(one, axis=-1)], axis=1)
    return m.astype(BF16), w_in.astype(BF16), to_rows(w_re).astype(BF16), to_rows(w_im).astype(BF16), a_rows


def _s5_second_rows(nrows):
    row = lax.broadcasted_iota(I32, (nrows, 2 * S5_STATE), 0)
    lane = lax.broadcasted_iota(I32, (nrows, 2 * S5_STATE), 1)
    return (lane < S5_STATE) != ((row & 7) < 4)


def _s5_kernel(x_ref, m_ref, win_ref, wre_ref, wim_ref, a_ref, y_ref, ure_scr, uim_scr, sre_scr, sim_scr, *,
               nblock, nctx):
    p2 = 2 * S5_STATE
    nrows = nblock * 8
    second = _s5_second_rows(nrows)
    fwd = lax.broadcasted_iota(I32, (nrows, p2), 1) < S5_STATE
    for g in range(S5_GB):
        v = jnp.dot(x_ref[g], win_ref[g], preferred_element_type=F32)
        vre, vim = v[:, 0:p2], v[:, p2:2 * p2]
        a_re, a_im = a_ref[g, 8:9, 0:p2], a_ref[g, 8:9, p2:2 * p2]
        fre = jnp.where(fwd, pltpu.roll(vre, 4, 0), pltpu.roll(vre, nrows - 4, 0))
        fim = jnp.where(fwd, pltpu.roll(vim, 4, 0), pltpu.roll(vim, nrows - 4, 0))
        ure_scr[g] = vre + jnp.where(second, a_re * fre - a_im * fim, 0.0)
        uim_scr[g] = vim + jnp.where(second, a_re * fim + a_im * fre, 0.0)
    second8 = second[0:8]
    fwd8 = fwd[0:8]
    ap_re = [a_ref[g, 0:8, 0:p2] for g in range(S5_GB)]
    ap_im = [a_ref[g, 0:8, p2:2 * p2] for g in range(S5_GB)]

    def body(s, carry):
        jr = jnp.where(s < nctx, nctx - 1 - s, nblock - 1 + nctx - s)
        of = pl.multiple_of(s * 8, 8)
        orv = pl.multiple_of(jr * 8, 8)
        new = []
        for g in range(S5_GB):
            cre, cim = carry[g]
            ure = jnp.where(fwd8, ure_scr[g, pl.ds(of, 8), :], ure_scr[g, pl.ds(orv, 8), :])
            uim = jnp.where(fwd8, uim_scr[g, pl.ds(of, 8), :], uim_scr[g, pl.ds(orv, 8), :])
            zre = ap_re[g] * cre - ap_im[g] * cim + ure
            zim = ap_re[g] * cim + ap_im[g] * cre + uim
            rre, rim = pltpu.roll(zre, 4, 0), pltpu.roll(zim, 4, 0)
            ere, eim = jnp.where(second8, rre, cre), jnp.where(second8, rim, cim)
            sre_scr[g, pl.ds(of, 8), 0:S5_STATE] = ere[:, 0:S5_STATE]
            sre_scr[g, pl.ds(orv, 8), S5_STATE:p2] = ere[:, S5_STATE:p2]
            sim_scr[g, pl.ds(of, 8), 0:S5_STATE] = eim[:, 0:S5_STATE]
            sim_scr[g, pl.ds(orv, 8), S5_STATE:p2] = eim[:, S5_STATE:p2]
            new.append((jnp.where(second8, zre, rre), jnp.where(second8, zim, rim)))
        return tuple(new)

    zero = jnp.zeros((8, p2), F32)
    lax.fori_loop(0, nblock, body, tuple((zero, zero) for _ in range(S5_GB)))
    for g in range(S5_GB):
        y = (jnp.dot(x_ref[g], m_ref[g], preferred_element_type=F32)
             + jnp.dot(sre_scr[g].astype(BF16), wre_ref[g], preferred_element_type=F32)
             + jnp.dot(sim_scr[g].astype(BF16), wim_ref[g], preferred_element_type=F32))
        y_ref[g] = y.astype(BF16)


def _s5(xt, bsz, weights):
    g, rows, lanes = xt.shape
    assert 2 * bsz == 8
    nchunk = rows // bsz
    nctx = CTX_LEN // S5_TC
    assert nchunk % 2 == 0 and nctx % 2 == 0
    m, w_in, w_re, w_im, a_rows = weights
    p2 = 2 * S5_STATE
    gb = S5_GB
    wspec = lambda r, c: pl.BlockSpec((gb, r, c), lambda i: (i, 0, 0))
    return pl.pallas_call(
        functools.partial(_s5_kernel, nblock=nchunk // 2, nctx=nctx // 2),
        grid=(g // gb,),
        in_specs=[wspec(rows, lanes), wspec(lanes, lanes), wspec(lanes, 2 * p2), wspec(p2, lanes),
                  wspec(p2, lanes), wspec(16, 2 * p2)],
        out_specs=wspec(rows, lanes),
        out_shape=jax.ShapeDtypeStruct((g, rows, lanes), BF16),
        scratch_shapes=[pltpu.VMEM((gb, rows, p2), F32)] * 4,
        compiler_params=_cparams(("arbitrary",)),
        name="s5",
    )(xt, m, w_in, w_re, w_im, a_rows)


GLU_SAMPLES = 2


def _glu_kernel(y_ref, x_ref, mod_ref, n2g_ref, w_ref, b_ref, rwt_ref, x1_ref, h2_ref, lg_ref, slab, y_scr, *,
                bsz):
    dm = D_MODEL
    half = pl.program_id(1)
    nchunk = TM // S5_TC

    @pl.when(half == 0)
    def _():
        for j in range(S5_LANE_VREGS):
            for gl in range(S5_TC):
                g = (j // 2) * S5_TC + gl
                rows = y_ref[g, :, 128 * (j % 2):128 * (j % 2) + 128].astype(F32)
                slab[j, pl.ds(gl, nchunk * bsz, stride=S5_TC), :] = rows

    for k in range(GLU_SAMPLES):
        b = half * GLU_SAMPLES + k

        def chunk(c, carry):
            s0 = pl.multiple_of((c * bsz + b) * S5_TC, S5_TC)
            p = [[slab[j, pl.ds(s0 + 8 * h, 8), :] for j in range(S5_LANE_VREGS)] for h in range(2)]
            q = _segment_transpose(p)
            r0 = pl.multiple_of(c * S5_TC, S5_TC)
            for h in range(2):
                for j in range(S5_LANE_VREGS):
                    y_scr[pl.ds(r0 + 8 * h, 8), 128 * j:128 * j + 128] = q[h][j]
            return carry

        lax.fori_loop(0, nchunk, chunk, 0, unroll=4)
        for rows in _row_halves():
            z = jax.nn.gelu(y_scr[rows]).astype(BF16)
            zz = jnp.dot(z, w_ref[...], preferred_element_type=F32) + b_ref[...]
            glu = zz[:, :dm] * jax.nn.sigmoid(zz[:, dm:])
            x1_ref[k, rows], h2_ref[k, rows], lg_ref[k, :, rows] = _post_mixer(
                glu, x_ref[k, rows], mod_ref[b, 2:3, :], mod_ref[b, 3:4, :], mod_ref[b, 4:5, :], n2g_ref[0],
                rwt_ref[...])


def _glu(y, xall, modtab, layer, norm_g, glu_w, glu_b, router_w):
    bsz, s, dm = xall.shape
    n = s - CTX_LEN
    ntl = n // TM
    assert bsz == 2 * GLU_SAMPLES
    gs = GLU_SAMPLES
    rows = TM // S5_TC * bsz
    ctx_tiles = CTX_LEN // TM
    tok = pl.BlockSpec((gs, TM, dm), lambda t, h: (h, t, 0))
    x1, h2, lg = pl.pallas_call(
        functools.partial(_glu_kernel, bsz=bsz),
        grid=(ntl, bsz // gs),
        in_specs=[pl.BlockSpec((S5_GROUPS, rows, S5_TC * S5_GROUP), lambda t, h: (0, t + ctx_tiles, 0)),
                  pl.BlockSpec((gs, TM, dm), lambda t, h: (h, t + ctx_tiles, 0)),
                  pl.BlockSpec((8, 6, dm), lambda t, h: (layer, 0, 0)),
                  pl.BlockSpec((1, 1, dm), lambda t, h: (0, 0, 0)),
                  pl.BlockSpec((dm, 2 * dm), lambda t, h: (0, 0)),
                  pl.BlockSpec((1, 2 * dm), lambda t, h: (0, 0)),
                  pl.BlockSpec((N_EXPERTS, dm), lambda t, h: (0, 0))],
        out_specs=[tok, tok, pl.BlockSpec((gs, N_EXPERTS, TM), lambda t, h: (h, 0, t))],
        out_shape=[jax.ShapeDtypeStruct((bsz, n, dm), F32),
                   jax.ShapeDtypeStruct((bsz, n, dm), BF16),
                   jax.ShapeDtypeStruct((bsz, N_EXPERTS, n), F32)],
        scratch_shapes=[pltpu.VMEM((S5_LANE_VREGS, TM * bsz, 128), F32), pltpu.VMEM((TM, dm), F32)],
        compiler_params=_cparams(("arbitrary", "arbitrary")),
        name="glu",
    )(y, xall, modtab.reshape(DEPTH * 8, 6, dm), norm_g.reshape(1, 1, dm), glu_w.astype(BF16), glu_b[None, :],
      router_w.T)
    return x1, h2, jnp.transpose(lg, (1, 0, 2)).reshape(N_EXPERTS, bsz * n)


def kernel(x, c, ctx, c_ctx, mod_w, mod_b, norm1_g, norm2_g, ar_w_in, ar_w_out, lru_conv_w, lru_conv_b, lru_lam, lru_r_w, lru_r_b, lru_i_w, lru_i_b, attn_q_g, attn_k_g, attn_lam_q1, attn_lam_k1, attn_lam_q2, attn_lam_k2, attn_subln_g, s5_a_re, s5_a_im, s5_log_step, s5_b_re, s5_b_im, s5_c_re, s5_c_im, s5_d, s5_glu_w, s5_glu_b, router_w, router_bias, exp_w_gate, exp_w_up, exp_w_down, sh_w_gate, sh_w_up, sh_w_down):
    bsz, n, dm = x.shape
    assert dm == D_MODEL and ctx.shape[1] == CTX_LEN == TM and n % TM == 0 and bsz < 8
    assert mod_w.shape[0] == DEPTH == 2
    s = CTX_LEN + n
    ntb = s // TM
    modtab = _modulation(c, c_ctx, mod_w, mod_b)

    gate, xa, q, k, v = _inproj(x, ctx, modtab, norm1_g[0], ar_w_in[0], attn_q_g[0], attn_k_g[0])
    lru_args = (lru_conv_w[0], lru_conv_b[0], lru_lam[0], lru_r_w[0], lru_r_b[0], lru_i_w[0], lru_i_b[0])
    hf = _lru(xa, *lru_args, reverse=False)
    hr = _lru(xa, *lru_args, reverse=True)
    lam_init = 0.8 - 0.6 * math.exp(-0.3 * 0)
    lam_vecs = jnp.stack([attn_lam_q1[0], attn_lam_k1[0], attn_lam_q2[0], attn_lam_k2[0]], axis=0)
    yb = _attention(q, k, v, lam_vecs, attn_subln_g[0], lam_init)
    x1, h2, lgt = _outproj(hf, hr, gate, yb, x, ctx, modtab, norm2_g[0], ar_w_out[0], router_w[0])
    row0 = lambda i: jnp.where(i % ntb == 0, bsz, i // ntb)
    xall = _moe(h2.reshape(bsz * s, dm), lgt, x1.reshape(bsz * s, dm), modtab, 0, row0, router_bias[0],
                exp_w_gate, exp_w_up, exp_w_down, sh_w_gate[0], sh_w_up[0], sh_w_down[0])
    xall = xall.reshape(bsz, s, dm)

    xt = _s5_pack(xall, modtab, 1, norm1_g[1])
    weights = _s5_weights(s5_a_re[0], s5_a_im[0], s5_log_step[0], s5_b_re[0], s5_b_im[0], s5_c_re[0],
                          s5_c_im[0], s5_d[0])
    y = _s5(xt, bsz, weights)
    x1, h2, lgt = _glu(y, xall, modtab, 1, norm2_g[1], s5_glu_w[0], s5_glu_b[0], router_w[1])
    ntl = n // TM
    row1 = lambda i: i // ntl
    out = _moe(h2.reshape(bsz * n, dm), lgt, x1.reshape(bsz * n, dm), modtab, 1, row1, router_bias[1],
               exp_w_gate, exp_w_up, exp_w_down, sh_w_gate[1], sh_w_up[1], sh_w_down[1])
    return out.reshape(bsz, n, dm)
```

```python
import functools
import math

import jax
import jax.numpy as jnp
from jax import lax
from jax.experimental import pallas as pl
from jax.experimental.pallas import tpu as pltpu

F32, BF16, I32 = jnp.float32, jnp.bfloat16, jnp.int32
HIGHEST = lax.Precision.HIGHEST

D_MODEL = 1024
DEPTH = 2
GRID_W = 64
CTX_LEN = 256
EPS = 1e-6
LRU_WIDTH = 512
LRU_BLOCKS = 8
LRU_C = 8.0
ATT_HEADS = 4
ATT_DH = 64
ATT_DV = 128
ATT_QK_W = 512
ATT_WIDTH = 512
ROPE_BASE = 10000.0
EVEN_IN = 2 * LRU_WIDTH + 2 * ATT_QK_W + ATT_WIDTH
S5_GROUP = 16
S5_GROUPS = 64
S5_STATE = 64
S5_TC = 16
S5_GB = 4
N_EXPERTS = 64
TOP_K = 6
D_EXPERT = 256
ROUTED_SCALE = 2.5

TM = 256
SEG = 16
SLOT_CHUNK = 64
RMAX = -(-(TM * TOP_K + N_EXPERTS * (SEG - 1)) // SLOT_CHUNK) * SLOT_CHUNK
TR = 1024
TAIL_BITS = (TR // SEG - 1).bit_length()
ATT_TK = 1408
VMEM_LIMIT = 56 * 1024 * 1024


def _cparams(sem):
    return pltpu.CompilerParams(dimension_semantics=sem, vmem_limit_bytes=VMEM_LIMIT)


def _row_halves():
    return [slice(0, TM // 2), slice(TM // 2, TM)]


def _norm_mod(x, g, shift, scale):
    y = x * lax.rsqrt(jnp.mean(x * x, axis=-1, keepdims=True) + EPS) * g
    return y * (1.0 + scale) + shift


def _mod_kernel(c_ref, w_ref, b_ref, o_ref):
    c = c_ref[...]
    s = c * jax.nn.sigmoid(c)
    o_ref[0] = jnp.dot(s, w_ref[0], precision=HIGHEST, preferred_element_type=F32) + b_ref[0]


def _modulation(c, c_ctx, mod_w, mod_b):
    bsz, dm = c.shape
    cc = jnp.concatenate([c, c_ctx[None, :], jnp.zeros((8 - bsz - 1, dm), F32)], axis=0)
    out = pl.pallas_call(
        _mod_kernel,
        grid=(DEPTH, 6),
        in_specs=[pl.BlockSpec((8, dm), lambda l, j: (0, 0)),
                  pl.BlockSpec((1, dm, dm), lambda l, j: (l, 0, j)),
                  pl.BlockSpec((1, 1, dm), lambda l, j: (l, 0, j))],
        out_specs=pl.BlockSpec((1, 8, dm), lambda l, j: (l, 0, j)),
        out_shape=jax.ShapeDtypeStruct((DEPTH, 8, 6 * dm), F32),
        compiler_params=_cparams(("arbitrary", "arbitrary")),
        name="modulation",
    )(cc, mod_w, mod_b.reshape(DEPTH, 1, 6 * dm))
    return out.reshape(DEPTH * 8 * 6, 1, dm)


def _mod_spec(layer, part, row_fn):
    return pl.BlockSpec((1, 1, D_MODEL), lambda *ids: ((layer * 8 + row_fn(*ids)) * 6 + part, 0, 0))


def _rope_tables(n):
    rows = n // GRID_W
    r, col = jnp.meshgrid(jnp.arange(rows), jnp.arange(GRID_W), indexing="ij")
    pos = jnp.stack([r.reshape(-1), col.reshape(-1)], axis=-1).astype(F32)
    n_freq = ATT_DH // 4
    inv_freq = ROPE_BASE ** (-jnp.arange(n_freq, dtype=F32) / n_freq)
    ang = pos[:, :, None] * inv_freq
    cos, sin = jnp.cos(ang), jnp.sin(ang)
    zero = jnp.zeros_like(sin)
    cos64 = jnp.stack([cos, cos], axis=2).reshape(n, ATT_DH)
    sin_lo = jnp.stack([zero, sin], axis=2).reshape(n, ATT_DH)
    sin_hi = jnp.stack([-sin, zero], axis=2).reshape(n, ATT_DH)

    def full(tab, ctx_val):
        tab = jnp.concatenate([jnp.full((CTX_LEN, ATT_DH), ctx_val, F32), tab], axis=0)
        return jnp.concatenate([tab, tab], axis=1)

    return full(cos64, 1.0), full(sin_lo, 0.0), full(sin_hi, 0.0)


def _qk_post(t, gain, ones_bd, cos, sin_lo, sin_hi):
    ss = jnp.dot((t * t).astype(BF16), ones_bd, preferred_element_type=F32) * (1.0 / ATT_DH)
    tn = t * lax.rsqrt(ss + EPS) * gain
    w = tn.shape[1]
    return tn * cos + pltpu.roll(tn, 16, 1) * sin_lo + pltpu.roll(tn, w - 16, 1) * sin_hi


def _inproj_kernel(x_ref, c_ref, sh_ref, sc_ref, g_ref, w_ref, qg_ref, kg_ref, ones_ref,
                   cos_ref, slo_ref, shi_ref, gate_ref, xa_ref, q_ref, k_ref, v_ref):
    t = pl.program_id(1)
    lw, qw = LRU_WIDTH, ATT_QK_W
    tile4 = lambda a: jnp.concatenate([a, a, a, a], axis=1)
    ones_bd = ones_ref[...]
    for rows in _row_halves():
        x = jnp.where(t == 0, c_ref[0, rows], x_ref[0, rows])
        h = _norm_mod(x, g_ref[0], sh_ref[0], sc_ref[0])
        z = jnp.dot(h.astype(BF16), w_ref[...], preferred_element_type=F32)
        gate_ref[0, rows] = z[:, 0:lw].astype(BF16)
        xa_ref[0, rows] = z[:, lw:2 * lw].astype(BF16)
        cos, slo, shi = tile4(cos_ref[rows]), tile4(slo_ref[rows]), tile4(shi_ref[rows])
        q = _qk_post(z[:, 2 * lw:2 * lw + qw], qg_ref[...], ones_bd, cos, slo, shi)
        q_ref[0, rows] = (q * (ATT_DH ** -0.5 * math.log2(math.e))).astype(BF16)
        k = _qk_post(z[:, 2 * lw + qw:2 * lw + 2 * qw], kg_ref[...], ones_bd, cos, slo, shi)
        k_ref[0, rows] = k.astype(BF16)
        v_ref[0, rows] = z[:, 2 * lw + 2 * qw:].astype(BF16)


def _inproj(x, ctx, modtab, norm_g, w_in, q_g, k_g):
    bsz, n, dm = x.shape
    ntb = (CTX_LEN + n) // TM
    s = CTX_LEN + n
    row = lambda b, t: jnp.where(t == 0, bsz, b)
    cos, slo, shi = _rope_tables(n)
    ones_bd = jnp.kron(jnp.eye(ATT_QK_W // ATT_DH, dtype=F32), jnp.ones((ATT_DH, ATT_DH), F32)).astype(BF16)
    tile_g = lambda g: jnp.tile(g, ATT_QK_W // ATT_DH)[None, :]
    tab_spec = pl.BlockSpec((TM, 2 * ATT_DH), lambda b, t: (t, 0))
    out_spec = pl.BlockSpec((1, TM, LRU_WIDTH), lambda b, t: (b, t, 0))
    out_sds = jax.ShapeDtypeStruct((bsz, s, LRU_WIDTH), BF16)
    return pl.pallas_call(
        _inproj_kernel,
        grid=(bsz, ntb),
        in_specs=[pl.BlockSpec((1, TM, dm), lambda b, t: (b, jnp.maximum(t - 1, 0), 0)),
                  pl.BlockSpec((1, TM, dm), lambda b, t: (b, 0, 0)),
                  _mod_spec(0, 0, row), _mod_spec(0, 1, row),
                  pl.BlockSpec((1, 1, dm), lambda b, t: (0, 0, 0)),
                  pl.BlockSpec((dm, EVEN_IN), lambda b, t: (0, 0)),
                  pl.BlockSpec((1, ATT_QK_W), lambda b, t: (0, 0)),
                  pl.BlockSpec((1, ATT_QK_W), lambda b, t: (0, 0)),
                  pl.BlockSpec((ATT_QK_W, ATT_QK_W), lambda b, t: (0, 0)),
                  tab_spec, tab_spec, tab_spec],
        out_specs=[out_spec] * 5,
        out_shape=[out_sds] * 5,
        compiler_params=_cparams(("arbitrary", "arbitrary")),
        name="inproj",
    )(x, ctx, modtab, modtab, norm_g.reshape(1, 1, dm), w_in.astype(BF16), tile_g(q_g), tile_g(k_g), ones_bd,
      cos, slo, shi)


def _lru_kernel(xa_ref, xp_ref, xn_ref, cw_ref, cb_ref, lam_ref, w_ref, bias_ref, h_ref,
                a_scr, b_scr, c_scr, *, ntb, reverse):
    s = pl.program_id(1)
    ti = jnp.where(s == 0, 0, ntb - s) if reverse else s
    lw = LRU_WIDTH
    x = xa_ref[0].astype(F32)
    row = lax.broadcasted_iota(I32, (TM, lw), 0)
    has_prev = jnp.where(ti > 1, 1.0, 0.0)
    has_next = jnp.where((ti > 0) & (ti < ntb - 1), 1.0, 0.0)
    prev = xp_ref[0].astype(F32) * has_prev
    nxt = xn_ref[0].astype(F32) * has_next
    xm1 = jnp.where(row == 0, prev[7:8], pltpu.roll(x, 1, 0))
    xm2 = jnp.where(row == 0, prev[6:7], jnp.where(row == 1, prev[7:8], pltpu.roll(x, 2, 0)))
    xp1 = jnp.where(row == TM - 1, nxt[0:1], pltpu.roll(x, TM - 1, 0))
    cw = cw_ref[...]
    xc = cw[0:1] * xm2 + cw[1:2] * xm1 + cw[2:3] * x + cw[3:4] * xp1 + cb_ref[...]

    z = jnp.dot(xc.astype(BF16), w_ref[...], preferred_element_type=F32) + bias_ref[...]
    r = jax.nn.sigmoid(z[:, :lw])
    ig = jax.nn.sigmoid(z[:, lw:])
    neg_lam = -lam_ref[...]
    softplus = jnp.maximum(neg_lam, 0.0) + jnp.log1p(jnp.exp(-jnp.abs(neg_lam)))
    log_a = (-LRU_C) * r * softplus
    a = jnp.exp(log_a)
    b = jnp.sqrt(-jnp.tanh(log_a) * (a * a + 1.0)) * (ig * xc)

    r8 = row & 7
    for sft in (1, 2, 4):
        if reverse:
            a_s, b_s, m = pltpu.roll(a, TM - sft, 0), pltpu.roll(b, TM - sft, 0), r8 < 8 - sft
        else:
            a_s, b_s, m = pltpu.roll(a, sft, 0), pltpu.roll(b, sft, 0), r8 >= sft
        b = jnp.where(m, a * b_s + b, b)
        a = jnp.where(m, a * a_s, a)
    a_scr[...] = a
    b_scr[...] = b

    @pl.when(s == 0)
    def _():
        c_scr[...] = jnp.zeros_like(c_scr)

    ng = TM // 8

    def body(j, carry):
        g = (ng - 1 - j) if reverse else j
        off = pl.multiple_of(g * 8, 8)
        h = a_scr[pl.ds(off, 8), :] * carry + b_scr[pl.ds(off, 8), :]
        h_ref[0, pl.ds(off, 8), :] = h
        last = h[0:1] if reverse else h[7:8]
        return jnp.broadcast_to(last, (8, lw))

    c_scr[...] = lax.fori_loop(0, ng, body, c_scr[...], unroll=4)


def _lru(xa, conv_w, conv_b, lam, r_w, r_b, i_w, i_b, reverse):
    bsz, s, lw = xa.shape
    ntb = s // TM
    d = 1 if reverse else 0
    bd = lambda w: jax.scipy.linalg.block_diag(*[w[i] for i in range(LRU_BLOCKS)])
    w = jnp.concatenate([bd(r_w[d]), bd(i_w[d])], axis=1).astype(BF16)
    bias = jnp.concatenate([r_b[d], i_b[d]])[None, :]
    cw = jnp.concatenate([conv_w, jnp.zeros((4, lw), F32)], axis=0)
    if reverse:
        tile = lambda t: jnp.where(t == 0, 0, ntb - t)
    else:
        tile = lambda t: t
    nb8 = s // 8
    return pl.pallas_call(
        functools.partial(_lru_kernel, ntb=ntb, reverse=reverse),
        grid=(bsz, ntb),
        in_specs=[pl.BlockSpec((1, TM, lw), lambda b, t: (b, tile(t), 0)),
                  pl.BlockSpec((1, 8, lw), lambda b, t: (b, jnp.maximum(tile(t) * (TM // 8) - 1, 0), 0)),
                  pl.BlockSpec((1, 8, lw), lambda b, t: (b, jnp.minimum((tile(t) + 1) * (TM // 8), nb8 - 1), 0)),
                  pl.BlockSpec((8, lw), lambda b, t: (0, 0)),
                  pl.BlockSpec((1, lw), lambda b, t: (0, 0)),
                  pl.BlockSpec((1, lw), lambda b, t: (0, 0)),
                  pl.BlockSpec((lw, 2 * lw), lambda b, t: (0, 0)),
                  pl.BlockSpec((1, 2 * lw), lambda b, t: (0, 0))],
        out_specs=pl.BlockSpec((1, TM, lw), lambda b, t: (b, tile(t), 0)),
        out_shape=jax.ShapeDtypeStruct((bsz, s, lw), F32),
        scratch_shapes=[pltpu.VMEM((TM, lw), F32), pltpu.VMEM((TM, lw), F32), pltpu.VMEM((8, lw), F32)],
        compiler_params=_cparams(("arbitrary", "arbitrary")),
        name="lru_rev" if reverse else "lru_fwd",
    )(xa, xa, xa, cw, conv_b[None, :], lam[d][None, :], w, bias)


def _attn_kernel(lamv_ref, q_ref, k_ref, v_ref, sg_ref, o_ref, m_scr, l_scr, acc_scr, al_scr, s_scr, p_scr, *,
                 nkv, lam_init):
    t = pl.program_id(2)
    q = q_ref[0]
    dh = ATT_DH
    dv = ATT_DV
    m_scr[...] = jnp.full_like(m_scr, -jnp.inf)
    l_scr[...] = jnp.zeros_like(l_scr)
    acc_scr[...] = jnp.zeros_like(acc_scr)

    def scores(buf, off, size):
        kc = k_ref[0, pl.ds(off, size), :]
        for mi in range(2):
            s_scr[buf, mi, :, 0:size] = lax.dot_general(
                q[:, mi * dh:(mi + 1) * dh], kc[:, mi * dh:(mi + 1) * dh], (((1,), (1,)), ((), ())),
                preferred_element_type=F32)

    def softmax(buf, size):
        nlb = size // 128
        groups = [(slice(r * 16, (r + 1) * 16), mi) for r in range(TM // 16) for mi in range(2)]
        for rows, mi in groups:
            sc = s_scr[buf, mi, rows, 0:size]
            mx = functools.reduce(jnp.maximum, [sc[:, i * 128:(i + 1) * 128] for i in range(nlb)])
            m_old = m_scr[mi, rows, :]
            m_new = jnp.maximum(m_old, jnp.max(mx, axis=1, keepdims=True))
            m_scr[mi, rows, :] = m_new
            al_scr[mi, rows, :] = jnp.exp2(m_old - m_new)
        for rows, mi in groups:
            m_new = m_scr[mi, rows, :]
            ps = [jnp.exp2(s_scr[buf, mi, rows, i * 128:(i + 1) * 128] - m_new) for i in range(nlb)]
            l_scr[mi, rows, :] = al_scr[mi, rows, :] * l_scr[mi, rows, :] + functools.reduce(jnp.add, ps)
            p_scr[buf, mi, rows, 0:size] = jnp.concatenate(ps, axis=1).astype(BF16)

    def values(buf, off, size):
        vc = v_ref[0, pl.ds(off, size), :]
        pv = jnp.dot(p_scr[buf, :, :, 0:size].reshape(2 * TM, size), vc, preferred_element_type=F32)
        acc_scr[...] = al_scr[...] * acc_scr[...] + pv.reshape(2, TM, dv)

    @pl.when(t == 0)
    def _():
        scores(0, 0, CTX_LEN)
        softmax(0, CTX_LEN)
        values(0, 0, CTX_LEN)

    @pl.when(t > 0)
    def _():
        tk = ATT_TK
        nchunk = nkv // tk
        scores(0, 0, tk)
        for j in range(nchunk):
            if j + 1 < nchunk:
                scores((j + 1) % 2, (j + 1) * tk, tk)
            softmax(j % 2, tk)
            values(j % 2, j * tk, tk)

    lv = lamv_ref[...]
    lam = (jnp.exp(jnp.sum(lv[0:1] * lv[1:2], axis=1, keepdims=True))
           - jnp.exp(jnp.sum(lv[2:3] * lv[3:4], axis=1, keepdims=True)) + lam_init)
    l0 = jnp.sum(l_scr[0], axis=1, keepdims=True)
    l1 = jnp.sum(l_scr[1], axis=1, keepdims=True)
    o = acc_scr[0] / l0 - lam * (acc_scr[1] / l1)
    y = o * lax.rsqrt(jnp.mean(o * o, axis=-1, keepdims=True) + EPS) * sg_ref[...] * (1.0 - lam_init)
    o_ref[0] = y.astype(BF16)


def _attention(q, k, v, lam_vecs, subln_g, lam_init):
    bsz, s, _ = q.shape
    ntb = s // TM
    assert s % ATT_TK == 0
    dv = ATT_DV
    lamv = jnp.concatenate([jnp.pad(lam_vecs, ((0, 0), (0, dv - ATT_DH))), jnp.zeros((4, dv), F32)], axis=0)
    return pl.pallas_call(
        functools.partial(_attn_kernel, nkv=s, lam_init=lam_init),
        grid=(bsz, ATT_HEADS, ntb),
        in_specs=[pl.BlockSpec((8, dv), lambda b, h, t: (0, 0)),
                  pl.BlockSpec((1, TM, dv), lambda b, h, t: (b, t, h)),
                  pl.BlockSpec((1, s, dv), lambda b, h, t: (b, 0, h)),
                  pl.BlockSpec((1, s, dv), lambda b, h, t: (b, 0, h)),
                  pl.BlockSpec((1, dv), lambda b, h, t: (0, 0))],
        out_specs=pl.BlockSpec((1, TM, dv), lambda b, h, t: (b, t, h)),
        out_shape=jax.ShapeDtypeStruct((bsz, s, ATT_WIDTH), BF16),
        scratch_shapes=[pltpu.VMEM((2, TM, dv), F32)] * 4
        + [pltpu.VMEM((2, 2, TM, ATT_TK), F32), pltpu.VMEM((2, 2, TM, ATT_TK), BF16)],
        compiler_params=_cparams(("arbitrary", "arbitrary", "arbitrary")),
        name="diff_attention",
    )(lamv, q, k, v, subln_g[None, :])


def _post_mixer(y, x, g1, sh2, sc2, n2g, rwt):
    x1 = x + g1 * y
    h2 = _norm_mod(x1, n2g, sh2, sc2)
    logits = lax.dot_general(rwt, h2, (((1,), (1,)), ((), ())), precision=HIGHEST, preferred_element_type=F32)
    return x1, h2.astype(BF16), logits


def _outproj_kernel(hf_ref, hr_ref, gate_ref, yb_ref, x_ref, c_ref, g1_ref, sh_ref, sc_ref, n2g_ref, w_ref,
                    rwt_ref, x1_ref, h2_ref, lg_ref):
    t = pl.program_id(1)
    lw = LRU_WIDTH
    for rows in _row_halves():
        x = jnp.where(t == 0, c_ref[0, rows], x_ref[0, rows])
        ya = ((hf_ref[0, rows] + hr_ref[0, rows]) * jax.nn.gelu(gate_ref[0, rows].astype(F32))).astype(BF16)
        y = (jnp.dot(ya, w_ref[0:lw, :], preferred_element_type=F32)
             + jnp.dot(yb_ref[0, rows], w_ref[lw:, :], preferred_element_type=F32))
        x1_ref[0, rows], h2_ref[0, rows], lg_ref[:, rows] = _post_mixer(
            y, x, g1_ref[0], sh_ref[0], sc_ref[0], n2g_ref[0], rwt_ref[...])


def _outproj(hf, hr, gate, yb, x, ctx, modtab, norm_g, w_out, router_w):
    bsz, s, lw = hf.shape
    dm = D_MODEL
    ntb = s // TM
    row = lambda b, t: jnp.where(t == 0, bsz, b)
    half = pl.BlockSpec((1, TM, lw), lambda b, t: (b, t, 0))
    return pl.pallas_call(
        _outproj_kernel,
        grid=(bsz, ntb),
        in_specs=[half, half, half, half,
                  pl.BlockSpec((1, TM, dm), lambda b, t: (b, jnp.maximum(t - 1, 0), 0)),
                  pl.BlockSpec((1, TM, dm), lambda b, t: (b, 0, 0)),
                  _mod_spec(0, 2, row), _mod_spec(0, 3, row), _mod_spec(0, 4, row),
                  pl.BlockSpec((1, 1, dm), lambda b, t: (0, 0, 0)),
                  pl.BlockSpec((2 * lw, dm), lambda b, t: (0, 0)),
                  pl.BlockSpec((N_EXPERTS, dm), lambda b, t: (0, 0))],
        out_specs=[pl.BlockSpec((1, TM, dm), lambda b, t: (b, t, 0)),
                   pl.BlockSpec((1, TM, dm), lambda b, t: (b, t, 0)),
                   pl.BlockSpec((N_EXPERTS, TM), lambda b, t: (0, b * ntb + t))],
        out_shape=[jax.ShapeDtypeStruct((bsz, s, dm), F32),
                   jax.ShapeDtypeStruct((bsz, s, dm), BF16),
                   jax.ShapeDtypeStruct((N_EXPERTS, bsz * s), F32)],
        compiler_params=_cparams(("arbitrary", "arbitrary")),
        name="outproj",
    )(hf, hr, gate, yb, x, ctx, modtab, modtab, modtab, norm_g.reshape(1, 1, dm), w_out.astype(BF16),
      router_w.T)


def _route_kernel(lg_ref, bias_ref, tri_ref, low_ref, pos_ref, gate_ref, cnt_ref):
    ne = N_EXPERTS
    aff = jax.nn.sigmoid(lg_ref[...])
    work = aff + bias_ref[:, 0:1]
    eidx = lax.broadcasted_iota(I32, (ne, TM), 0)
    sels = []
    for _ in range(TOP_K):
        mx = jnp.max(work, axis=0, keepdims=True)
        am = jnp.min(jnp.where(work == mx, eidx, ne), axis=0, keepdims=True)
        sk = eidx == am
        sels.append(sk)
        work = jnp.where(sk, -jnp.inf, work)
    sel = sels[0]
    for sk in sels[1:]:
        sel = sel | sk
    self = jnp.where(sel, 1.0, 0.0)
    s_sel = aff * self
    gates = s_sel / jnp.sum(s_sel, axis=0, keepdims=True) * ROUTED_SCALE
    rank = jnp.dot(self.astype(BF16), tri_ref[...], preferred_element_type=F32)
    cnt = jnp.sum(self, axis=1, keepdims=True)
    cnt_seg = jnp.floor((cnt + (SEG - 1)) * (1.0 / SEG)) * SEG
    cnt_b = jnp.broadcast_to(cnt_seg, (ne, 128))
    seg_off = jnp.dot(low_ref[...], cnt_b.astype(BF16), preferred_element_type=F32)
    lpos = seg_off[:, 0:1] + rank
    pos_rows, gate_rows = [], []
    for sk in sels:
        pos_rows.append(jnp.sum(jnp.where(sk, lpos, 0.0), axis=0, keepdims=True))
        gate_rows.append(jnp.sum(jnp.where(sk, gates, 0.0), axis=0, keepdims=True))
    for _ in range(8 - TOP_K):
        pos_rows.append(jnp.full((1, TM), -1.0, F32))
        gate_rows.append(jnp.zeros((1, TM), F32))
    pos_ref[...] = jnp.concatenate(pos_rows, axis=0).astype(I32)
    gate_ref[...] = jnp.concatenate(gate_rows, axis=0)
    cnt_ref[0] = cnt_b.astype(I32)


def _route(lgt, router_bias):
    ne, t = lgt.shape
    nt = t // TM
    tri = jnp.triu(jnp.ones((TM, TM), F32), k=1).astype(BF16)
    low = jnp.tril(jnp.ones((ne, ne), F32), k=-1).astype(BF16)
    bias = jnp.broadcast_to(router_bias[:, None], (ne, 128))
    pos, gate, cnt = pl.pallas_call(
        _route_kernel,
        grid=(nt,),
        in_specs=[pl.BlockSpec((ne, TM), lambda i: (0, i)),
                  pl.BlockSpec((ne, 128), lambda i: (0, 0)),
                  pl.BlockSpec((TM, TM), lambda i: (0, 0)),
                  pl.BlockSpec((ne, ne), lambda i: (0, 0))],
        out_specs=[pl.BlockSpec((8, TM), lambda i: (0, i)),
                   pl.BlockSpec((8, TM), lambda i: (0, i)),
                   pl.BlockSpec((1, ne, 128), lambda i: (i, 0, 0))],
        out_shape=[jax.ShapeDtypeStruct((8, t), I32),
                   jax.ShapeDtypeStruct((8, t), F32),
                   jax.ShapeDtypeStruct((nt, ne, 128), I32)],
        compiler_params=_cparams(("arbitrary",)),
        name="route",
    )(lgt, bias, tri, low)
    return pos, gate, cnt[:, :, 0]


def _moe_layout(cnt_seg, nblk_max):
    tot = jnp.sum(cnt_seg, axis=0)
    region = (tot + TR - 1) // TR * TR
    region_end = jnp.cumsum(region)
    goff = (region_end - region)[None, :] + jnp.cumsum(cnt_seg, axis=0) - cnt_seg
    packed = ((goff // SEG) << 5) | (cnt_seg // SEG)
    nblk = region_end[-1] // TR
    blk = jnp.arange(nblk_max, dtype=I32)
    blk_e = jnp.sum((region_end[None, :] // TR <= blk[:, None]).astype(I32), axis=1)
    blk_e = jnp.minimum(blk_e, N_EXPERTS - 1).astype(I32)
    tile_info = jnp.sum(cnt_seg, axis=1) | (jnp.any(cnt_seg >= 64, axis=1).astype(I32) << 16)
    tails = (((region_end - region + tot) // SEG) << TAIL_BITS) | ((region - tot) // SEG)
    tails = jnp.concatenate([tails, nblk[None]]).astype(I32)
    return packed.reshape(-1).astype(I32), tile_info.astype(I32), tails, blk_e, nblk.astype(I32).reshape(1)


def _segment_copy(stage, hbm, sem, to_sorted, lo, go, size):
    lo, go = pl.multiple_of(lo, SEG), pl.multiple_of(go, SEG)
    a, b = stage.at[pl.ds(lo, size)], hbm.at[pl.ds(go, size)]
    return pltpu.make_async_copy(a, b, sem) if to_sorted else pltpu.make_async_copy(b, a, sem)


def _segment_unpack(pk):
    return (pk & 31) * SEG, (pk >> 5) * SEG


def _segment_starts(pk_ref, tile, stage, hbm, sem, to_sorted, valid=None):
    loff = [0]

    def step(e):
        cnt, goff = _segment_unpack(pk_ref[tile * N_EXPERTS + e])
        done = (cnt >> 6) << 6
        for size in (32, 16):
            bit = (cnt & size) != 0
            if valid is not None:
                bit = bit & valid

            @pl.when(bit)
            def _(lo=loff[0] + done, go=goff + done, size=size):
                _segment_copy(stage, hbm, sem, to_sorted, lo, go, size).start()

            done = done + jnp.where(bit, size, 0)
        loff[0] = loff[0] + cnt

    return [functools.partial(step, e) for e in range(N_EXPERTS)]


def _interleave(*step_lists):
    total = max(len(steps) for steps in step_lists)
    done = [0] * len(step_lists)
    for t in range(1, total + 1):
        for k, steps in enumerate(step_lists):
            upto = len(steps) * t // total
            for step in steps[done[k]:upto]:
                step()
            done[k] = upto


def _segment_starts_long(pk_ref, tile, info, stage, hbm, sem, to_sorted):
    @pl.when((info >> 16) != 0)
    def _():
        def expert(e, loff):
            cnt, goff = _segment_unpack(pk_ref[tile * N_EXPERTS + e])

            def chunk(j, carry):
                _segment_copy(stage, hbm, sem, to_sorted, loff + j * 64, goff + j * 64, 64).start()
                return carry

            lax.fori_loop(0, cnt >> 6, chunk, 0)
            return loff + cnt

        lax.fori_loop(0, N_EXPERTS, expert, 0)


def _segment_wait(total, stage, hbm, sem, to_sorted):
    size = 1 << (RMAX.bit_length() - 1)
    while size >= SEG:
        @pl.when((total & size) != 0)
        def _(size=size):
            _segment_copy(stage, hbm, sem, to_sorted, 0, 0, size).wait()

        size //= 2


def _fill_slot_matrix(dst_ref, pos, weight_rows=None):
    ch = SLOT_CHUNK
    riota = lax.broadcasted_iota(I32, (ch, TM), 0)

    def step(c):
        local = pos - c * ch
        out = jnp.zeros((ch, TM), F32)
        for k in range(TOP_K):
            w = 1.0 if weight_rows is None else weight_rows[k:k + 1]
            out = jnp.where(riota == local[k:k + 1], w, out)
        dst_ref[c * ch:(c + 1) * ch, :] = out.astype(BF16)

    return [functools.partial(step, c) for c in range(RMAX // ch)]


def _zero_fill(tail_ref, zeros, xs_ref, sem, fn):
    def expert(e, carry):
        pk = tail_ref[e]
        off = (pk >> TAIL_BITS) * SEG

        def chunk(j, c):
            dst = xs_ref.at[pl.ds(pl.multiple_of(off + j * SEG, SEG), SEG)]
            fn(pltpu.make_async_copy(zeros.at[pl.ds(0, SEG)], dst, sem))
            return c

        return lax.fori_loop(0, pk & ((1 << TAIL_BITS) - 1), chunk, carry)

    lax.fori_loop(0, N_EXPERTS, expert, 0)

    def block(j, carry):
        fn(pltpu.make_async_copy(zeros, xs_ref.at[pl.ds(pl.multiple_of(j * TR, TR), TR)], sem))
        return carry

    lax.fori_loop(tail_ref[N_EXPERTS], xs_ref.shape[0] // TR, block, 0)


def _dispatch_kernel(pk_ref, tot_ref, tail_ref, pos_ref, h_ref, xs_ref, stage, zeros, slots, sem, zsem):
    i = pl.program_id(0)
    slot = i % 2

    @pl.when(i == 0)
    def _():
        zeros[...] = jnp.zeros_like(zeros)
        _zero_fill(tail_ref, zeros, xs_ref, zsem, lambda c: c.start())

    for step in _fill_slot_matrix(slots, pos_ref[...]):
        step()
    stage[slot] = jnp.dot(slots[...], h_ref[...], preferred_element_type=F32).astype(BF16)
    prev_rows = jnp.where(i > 0, tot_ref[jnp.maximum(i - 1, 0)] & 0xFFFF, 0)
    _segment_wait(prev_rows, stage.at[1 - slot], xs_ref, sem.at[1 - slot], True)
    this = (stage.at[slot], xs_ref, sem.at[slot], True)
    for step in _segment_starts(pk_ref, i, *this):
        step()
    _segment_starts_long(pk_ref, i, tot_ref[i], *this)

    @pl.when(i == pl.num_programs(0) - 1)
    def _():
        _segment_wait(tot_ref[i] & 0xFFFF, *this)
        _zero_fill(tail_ref, zeros, xs_ref, zsem, lambda c: c.wait())


def _dispatch(packed, tile_rows, tails, pos, h2, nrows):
    t, dm = h2.shape
    nt = t // TM
    return pl.pallas_call(
        _dispatch_kernel,
        grid_spec=pltpu.PrefetchScalarGridSpec(
            num_scalar_prefetch=3,
            grid=(nt,),
            in_specs=[pl.BlockSpec((8, TM), lambda i, pk, tot, tail: (0, i)),
                      pl.BlockSpec((TM, dm), lambda i, pk, tot, tail: (i, 0))],
            out_specs=pl.BlockSpec(memory_space=pl.ANY),
            scratch_shapes=[pltpu.VMEM((2, RMAX, dm), BF16), pltpu.VMEM((TR, dm), BF16),
                            pltpu.VMEM((RMAX, TM), BF16),
                            pltpu.SemaphoreType.DMA((2,)), pltpu.SemaphoreType.DMA]),
        out_shape=jax.ShapeDtypeStruct((nrows, dm), BF16),
        compiler_params=_cparams(("arbitrary",)),
        name="moe_dispatch",
    )(packed, tile_rows, tails, pos, h2)


def _expert_kernel(be_ref, nb_ref, x_ref, wg_ref, wu_ref, wd_ref, y_ref, wg_s, wu_s, wd_s):
    j = pl.program_id(0)
    changed = be_ref[j] != be_ref[jnp.maximum(j - 1, 0)]

    @pl.when((j == 0) | changed)
    def _():
        wg_s[...] = wg_ref[0].astype(BF16)
        wu_s[...] = wu_ref[0].astype(BF16)
        wd_s[...] = wd_ref[0].astype(BF16)

    @pl.when(j < nb_ref[0])
    def _():
        x = x_ref[...]
        g = jnp.dot(x, wg_s[...], preferred_element_type=F32)
        u = jnp.dot(x, wu_s[...], preferred_element_type=F32)
        a = (g * jax.nn.sigmoid(g) * u).astype(BF16)
        y_ref[...] = jnp.dot(a, wd_s[...], preferred_element_type=F32).astype(BF16)


def _experts(blk_e, nblk, xs, layer, w_gate, w_up, w_down):
    nrows, dm = xs.shape
    nblk_max = nrows // TR
    de = D_EXPERT
    row_blk = lambda j, be, nb: (jnp.minimum(j, nb[0] - 1), 0)
    return pl.pallas_call(
        _expert_kernel,
        grid_spec=pltpu.PrefetchScalarGridSpec(
            num_scalar_prefetch=2,
            grid=(nblk_max,),
            in_specs=[pl.BlockSpec((TR, dm), row_blk),
                      pl.BlockSpec((None, 1, dm, de), lambda j, be, nb: (layer, be[j], 0, 0)),
                      pl.BlockSpec((None, 1, dm, de), lambda j, be, nb: (layer, be[j], 0, 0)),
                      pl.BlockSpec((None, 1, de, dm), lambda j, be, nb: (layer, be[j], 0, 0))],
            out_specs=pl.BlockSpec((TR, dm), row_blk),
            scratch_shapes=[pltpu.VMEM((dm, de), BF16), pltpu.VMEM((dm, de), BF16), pltpu.VMEM((de, dm), BF16)]),
        out_shape=jax.ShapeDtypeStruct((nrows, dm), BF16),
        input_output_aliases={2: 0},
        compiler_params=_cparams(("arbitrary",)),
        name="moe_experts",
    )(blk_e, nblk, xs, w_gate, w_up, w_down)


def _combine_kernel(pk_ref, tot_ref, pos_ref, gate_ref, h_ref, x1_ref, g2_ref, wsg_ref, wsu_ref, wsd_ref, ys_ref,
                    o_ref, stage, gates, sem):
    i = pl.program_id(0)
    slot = i % 2
    last = pl.num_programs(0) - 1

    @pl.when(i == 0)
    def _():
        stage[...] = jnp.zeros_like(stage)
        for step in _segment_starts(pk_ref, 0, stage.at[0], ys_ref, sem.at[0], False):
            step()
        _segment_starts_long(pk_ref, 0, tot_ref[0], stage.at[0], ys_ref, sem.at[0], False)

    nxt = jnp.minimum(i + 1, last)
    _interleave(_fill_slot_matrix(gates, pos_ref[...], gate_ref[...]),
                _segment_starts(pk_ref, nxt, stage.at[1 - slot], ys_ref, sem.at[1 - slot], False))
    h = h_ref[...]
    g = jnp.dot(h, wsg_ref[...], preferred_element_type=F32)
    u = jnp.dot(h, wsu_ref[...], preferred_element_type=F32)
    shared = jnp.dot((g * jax.nn.sigmoid(g) * u).astype(BF16), wsd_ref[...], preferred_element_type=F32)
    _segment_wait(tot_ref[i] & 0xFFFF, stage.at[slot], ys_ref, sem.at[slot], False)
    routed = lax.dot_general(gates[...], stage[slot], (((0,), (0,)), ((), ())), preferred_element_type=F32)
    o_ref[...] = x1_ref[...] + g2_ref[0] * (routed + shared)
    _segment_starts_long(pk_ref, nxt, tot_ref[nxt], stage.at[1 - slot], ys_ref, sem.at[1 - slot], False)

    @pl.when(i == last)
    def _():
        _segment_wait(tot_ref[i] & 0xFFFF, stage.at[1 - slot], ys_ref, sem.at[1 - slot], False)


def _combine(packed, tile_rows, pos, gate, h2, x1, modtab, layer, row_fn, ys, ws_gate, ws_up, ws_down):
    t, dm = h2.shape
    nt = t // TM
    de = D_EXPERT
    const = lambda shape: pl.BlockSpec(shape, lambda i, pk, tot: (0,) * len(shape))
    return pl.pallas_call(
        _combine_kernel,
        grid_spec=pltpu.PrefetchScalarGridSpec(
            num_scalar_prefetch=2,
            grid=(nt,),
            in_specs=[pl.BlockSpec((8, TM), lambda i, pk, tot: (0, i)),
                      pl.BlockSpec((8, TM), lambda i, pk, tot: (0, i)),
                      pl.BlockSpec((TM, dm), lambda i, pk, tot: (i, 0)),
                      pl.BlockSpec((TM, dm), lambda i, pk, tot: (i, 0)),
                      pl.BlockSpec((1, 1, dm), lambda i, pk, tot: ((layer * 8 + row_fn(i)) * 6 + 5, 0, 0)),
                      const((dm, de)), const((dm, de)), const((de, dm)),
                      pl.BlockSpec(memory_space=pl.ANY)],
            out_specs=pl.BlockSpec((TM, dm), lambda i, pk, tot: (i, 0)),
            scratch_shapes=[pltpu.VMEM((2, RMAX, dm), BF16), pltpu.VMEM((RMAX, TM), BF16),
                            pltpu.SemaphoreType.DMA((2,))]),
        out_shape=jax.ShapeDtypeStruct((t, dm), F32),
        compiler_params=_cparams(("arbitrary",)),
        name="moe_combine",
    )(packed, tile_rows, pos, gate, h2, x1, modtab, ws_gate.astype(BF16), ws_up.astype(BF16),
      ws_down.astype(BF16), ys)


def _moe(h2, lgt, x1, modtab, layer, row_fn, router_bias, w_gate, w_up, w_down, ws_gate, ws_up, ws_down):
    t = h2.shape[0]
    nt = t // TM
    max_rows = t * TOP_K + nt * N_EXPERTS * (SEG - 1) + N_EXPERTS * (TR - 1)
    nblk_max = -(-max_rows // TR)
    pos, gate, cnt_seg = _route(lgt, router_bias)
    packed, tile_rows, tails, blk_e, nblk = _moe_layout(cnt_seg, nblk_max)
    xs = _dispatch(packed, tile_rows, tails, pos, h2, nblk_max * TR)
    ys = _experts(blk_e, nblk, xs, layer, w_gate, w_up, w_down)
    return _combine(packed, tile_rows, pos, gate, h2, x1, modtab, layer, row_fn, ys, ws_gate, ws_up, ws_down)


S5_LANE_VREGS = D_MODEL // 128


def _segment_transpose(p):
    row = lax.broadcasted_iota(I32, (8, 128), 0)
    seg = lax.broadcasted_iota(I32, (8, 128), 1) // S5_GROUP
    for s in (4, 2, 1):
        m_up = ((row & s) == 0) & ((seg & s) != 0)
        m_dn = ((row & s) != 0) & ((seg & s) == 0)

        def swap(x):
            if s == 4:
                return jnp.where(m_up | m_dn, pltpu.roll(pltpu.roll(x, 4, 0), 4 * S5_GROUP, 1), x)
            up = pltpu.roll(pltpu.roll(x, 8 - s, 0), S5_GROUP * s, 1)
            dn = pltpu.roll(pltpu.roll(x, s, 0), 128 - S5_GROUP * s, 1)
            return jnp.where(m_up, up, jnp.where(m_dn, dn, x))

        p = [[swap(x) for x in half] for half in p]
    return [[p[1 - h][j - 1 + 2 * h] if (j % 2) != h else p[h][j] for j in range(S5_LANE_VREGS)] for h in range(2)]


def _s5_pack_kernel(x_ref, mod_ref, g_ref, xt_ref, u_scr, slab):
    t = pl.program_id(0)
    bsz = x_ref.shape[0]
    nchunk = TM // S5_TC
    for b in range(bsz):
        shift = jnp.where(t == 0, mod_ref[bsz, 0:1, :], mod_ref[b, 0:1, :])
        scale = jnp.where(t == 0, mod_ref[bsz, 1:2, :], mod_ref[b, 1:2, :])
        u_scr[b] = _norm_mod(x_ref[b], g_ref[0], shift, scale)

    def chunk(c, carry):
        r0 = pl.multiple_of(c * S5_TC, S5_TC)
        for b in range(bsz):
            p = [[u_scr[b, pl.ds(r0 + 8 * h, 8), 128 * j:128 * j + 128] for j in range(S5_LANE_VREGS)]
                 for h in range(2)]
            q = _segment_transpose(p)
            s0 = pl.multiple_of((c * bsz + b) * S5_TC, S5_TC)
            for h in range(2):
                for j in range(S5_LANE_VREGS):
                    slab[j, pl.ds(s0 + 8 * h, 8), :] = q[h][j]
        return carry

    lax.fori_loop(0, nchunk, chunk, 0)
    for j in range(S5_LANE_VREGS):
        for gl in range(S5_TC):
            g = (j // 2) * S5_TC + gl
            rows = slab[j, pl.ds(gl, nchunk * bsz, stride=S5_TC), :]
            xt_ref[g, :, 128 * (j % 2):128 * (j % 2) + 128] = rows.astype(BF16)


def _s5_pack(xall, modtab, layer, norm_g):
    bsz, s, dm = xall.shape
    ntb = s // TM
    rows = TM // S5_TC * bsz
    return pl.pallas_call(
        _s5_pack_kernel,
        grid=(ntb,),
        in_specs=[pl.BlockSpec((bsz, TM, dm), lambda t: (0, t, 0)),
                  pl.BlockSpec((8, 6, dm), lambda t: (layer, 0, 0)),
                  pl.BlockSpec((1, 1, dm), lambda t: (0, 0, 0))],
        out_specs=pl.BlockSpec((S5_GROUPS, rows, S5_TC * S5_GROUP), lambda t: (0, t, 0)),
        out_shape=jax.ShapeDtypeStruct((S5_GROUPS, ntb * rows, S5_TC * S5_GROUP), BF16),
        scratch_shapes=[pltpu.VMEM((bsz, TM, dm), F32), pltpu.VMEM((S5_LANE_VREGS, TM * bsz, 128), F32)],
        compiler_params=_cparams(("arbitrary",)),
        name="s5_pack",
    )(xall, modtab.reshape(DEPTH * 8, 6, dm), norm_g.reshape(1, 1, dm))


def _cmul(x, y):
    return x[0] * y[0] - x[1] * y[1], x[0] * y[1] + x[1] * y[0]


def _s5_weights(a_re, a_im, log_step, b_re, b_im, c_re, c_im, d_skip):
    tc, g, p, ch = S5_TC, S5_GROUPS, S5_STATE, S5_GROUP
    lam = (jnp.minimum(a_re, -1e-4), a_im)
    step = jnp.exp(log_step)
    mag = jnp.exp(lam[0] * step)
    lam_bar = (mag * jnp.cos(lam[1] * step), mag * jnp.sin(lam[1] * step))
    inv = 1.0 / (lam[0] * lam[0] + lam[1] * lam[1])
    coef = _cmul((lam_bar[0] - 1.0, lam_bar[1]), (lam[0] * inv, -lam[1] * inv))
    b_bar = _cmul((coef[0][..., None], coef[1][..., None]), (b_re, b_im))
    pw = [(jnp.ones_like(mag), jnp.zeros_like(mag))]
    for _ in range(tc):
        pw.append(_cmul(pw[-1], lam_bar))
    pw = (jnp.stack([q[0] for q in pw], axis=1), jnp.stack([q[1] for q in pw], axis=1))
    at = lambda d, idx: (pw[0][d, idx], pw[1][d, idx])
    lead = lambda z: jnp.moveaxis(z, -1, 0)
    cp = _cmul((lead(c_re)[:, :, :, None, :], lead(c_im)[:, :, :, None, :]),
               (jnp.transpose(pw[0][:, :tc], (3, 0, 2, 1))[..., None],
                jnp.transpose(pw[1][:, :tc], (3, 0, 2, 1))[..., None]))
    cp = tuple(z.reshape(p, 2, g, 1, tc * ch) for z in cp)
    bb = tuple(jnp.transpose(z, (2, 0, 1, 3))[..., None] for z in b_bar)
    kern = jnp.sum(bb[0] * cp[0] - bb[1] * cp[1], axis=0)
    rev = jnp.flip(kern[1].reshape(g, ch, tc, ch), axis=2).reshape(g, ch, tc * ch)
    zeros = lambda width: jnp.zeros((g, ch, width), F32)
    rows = []
    for s_ in range(tc):
        fwd_row = jnp.concatenate([zeros(s_ * ch), kern[0][..., :(tc - s_) * ch]], axis=-1)
        rev_row = jnp.concatenate([rev[..., (tc - 1 - s_) * ch:], zeros((tc - 1 - s_) * ch)], axis=-1)
        rows.append(fwd_row + rev_row)
    m = jnp.stack(rows, axis=1).reshape(g, tc * ch, tc * ch)
    skip = jnp.tile(d_skip.reshape(g, 1, ch), (1, tc, 1)).reshape(g, tc * ch)
    m = m + jnp.eye(tc * ch, dtype=F32)[None] * skip[:, :, None]
    steps = jnp.arange(tc)
    lift = lambda z: (z[0][..., None], z[1][..., None])
    e_f = _cmul(lift(at(0, tc - 1 - steps)), (b_bar[0][0][None], b_bar[1][0][None]))
    e_r = _cmul(lift(at(1, steps)), (b_bar[0][1][None], b_bar[1][1][None]))
    w_in = jnp.stack([e_f[0], e_r[0], e_f[1], e_r[1]], axis=0)
    w_in = jnp.transpose(w_in, (2, 1, 4, 0, 3)).reshape(g, tc * ch, 4 * p)
    mid = lambda z: (z[0][:, :, None, :], z[1][:, :, None, :])
    g_f = _cmul((c_re[0][None], c_im[0][None]), mid(at(0, 1 + steps)))
    g_r = _cmul((c_re[1][None], c_im[1][None]), mid(at(1, tc - steps)))
    w_re = jnp.stack([g_f[0], g_r[0]], axis=0)
    w_im = -jnp.stack([g_f[1], g_r[1]], axis=0)
    to_rows = lambda w: jnp.transpose(w, (2, 0, 4, 1, 3)).reshape(g, 2 * p, tc * ch)
    a1 = (jnp.concatenate([pw[0][0, tc], pw[0][1, tc]], axis=-1),
          jnp.concatenate([pw[1][0, tc], pw[1][1, tc]], axis=-1))
    a2 = _cmul(a1, a1)
    second = _s5_second_rows(8)
    par = tuple(jnp.where(second[None], a2[k][:, None, :], a1[k][:, None, :]) for k in range(2))
    one = tuple(jnp.broadcast_to(a1[k][:, None, :], (g, 8, 2 * p)) for k in range(2))
    a_rows = jnp.concatenate([jnp.concatenate(par, axis=-1), jnp.concatenate(one, axis=-1)], axis=1)
    return m.astype(BF16), w_in.astype(BF16), to_rows(w_re).astype(BF16), to_rows(w_im).astype(BF16), a_rows


def _s5_second_rows(nrows):
    row = lax.broadcasted_iota(I32, (nrows, 2 * S5_STATE), 0)
    lane = lax.broadcasted_iota(I32, (nrows, 2 * S5_STATE), 1)
    return (lane < S5_STATE) != ((row & 7) < 4)


def _s5_kernel(x_ref, m_ref, win_ref, wre_ref, wim_ref, a_ref, y_ref, ure_scr, uim_scr, sre_scr, sim_scr, *,
               nblock, nctx):
    p2 = 2 * S5_STATE
    nrows = nblock * 8
    second = _s5_second_rows(nrows)
    fwd = lax.broadcasted_iota(I32, (nrows, p2), 1) < S5_STATE
    for g in range(S5_GB):
        v = jnp.dot(x_ref[g], win_ref[g], preferred_element_type=F32)
        vre, vim = v[:, 0:p2], v[:, p2:2 * p2]
        a_re, a_im = a_ref[g, 8:9, 0:p2], a_ref[g, 8:9, p2:2 * p2]
        fre = jnp.where(fwd, pltpu.roll(vre, 4, 0), pltpu.roll(vre, nrows - 4, 0))
        fim = jnp.where(fwd, pltpu.roll(vim, 4, 0), pltpu.roll(vim, nrows - 4, 0))
        ure_scr[g] = vre + jnp.where(second, a_re * fre - a_im * fim, 0.0)
        uim_scr[g] = vim + jnp.where(second, a_re * fim + a_im * fre, 0.0)
    second8 = second[0:8]
    fwd8 = fwd[0:8]
    ap_re = [a_ref[g, 0:8, 0:p2] for g in range(S5_GB)]
    ap_im = [a_ref[g, 0:8, p2:2 * p2] for g in range(S5_GB)]

    def body(s, carry):
        jr = jnp.where(s < nctx, nctx - 1 - s, nblock - 1 + nctx - s)
        of = pl.multiple_of(s * 8, 8)
        orv = pl.multiple_of(jr * 8, 8)
        new = []
        for g in range(S5_GB):
            cre, cim = carry[g]
            ure = jnp.where(fwd8, ure_scr[g, pl.ds(of, 8), :], ure_scr[g, pl.ds(orv, 8), :])
            uim = jnp.where(fwd8, uim_scr[g, pl.ds(of, 8), :], uim_scr[g, pl.ds(orv, 8), :])
            zre = ap_re[g] * cre - ap_im[g] * cim + ure
            zim = ap_re[g] * cim + ap_im[g] * cre + uim
            rre, rim = pltpu.roll(zre, 4, 0), pltpu.roll(zim, 4, 0)
            ere, eim = jnp.where(second8, rre, cre), jnp.where(second8, rim, cim)
            sre_scr[g, pl.ds(of, 8), 0:S5_STATE] = ere[:, 0:S5_STATE]
            sre_scr[g, pl.ds(orv, 8), S5_STATE:p2] = ere[:, S5_STATE:p2]
            sim_scr[g, pl.ds(of, 8), 0:S5_STATE] = eim[:, 0:S5_STATE]
            sim_scr[g, pl.ds(orv, 8), S5_STATE:p2] = eim[:, S5_STATE:p2]
            new.append((jnp.where(second8, zre, rre), jnp.where(second8, zim, rim)))
        return tuple(new)

    zero = jnp.zeros((8, p2), F32)
    lax.fori_loop(0, nblock, body, tuple((zero, zero) for _ in range(S5_GB)))
    for g in range(S5_GB):
        y = (jnp.dot(x_ref[g], m_ref[g], preferred_element_type=F32)
             + jnp.dot(sre_scr[g].astype(BF16), wre_ref[g], preferred_element_type=F32)
             + jnp.dot(sim_scr[g].astype(BF16), wim_ref[g], preferred_element_type=F32))
        y_ref[g] = y.astype(BF16)


def _s5(xt, bsz, weights):
    g, rows, lanes = xt.shape
    assert 2 * bsz == 8
    nchunk = rows // bsz
    nctx = CTX_LEN // S5_TC
    assert nchunk % 2 == 0 and nctx % 2 == 0
    m, w_in, w_re, w_im, a_rows = weights
    p2 = 2 * S5_STATE
    gb = S5_GB
    wspec = lambda r, c: pl.BlockSpec((gb, r, c), lambda i: (i, 0, 0))
    return pl.pallas_call(
        functools.partial(_s5_kernel, nblock=nchunk // 2, nctx=nctx // 2),
        grid=(g // gb,),
        in_specs=[wspec(rows, lanes), wspec(lanes, lanes), wspec(lanes, 2 * p2), wspec(p2, lanes),
                  wspec(p2, lanes), wspec(16, 2 * p2)],
        out_specs=wspec(rows, lanes),
        out_shape=jax.ShapeDtypeStruct((g, rows, lanes), BF16),
        scratch_shapes=[pltpu.VMEM((gb, rows, p2), F32)] * 4,
        compiler_params=_cparams(("arbitrary",)),
        name="s5",
    )(xt, m, w_in, w_re, w_im, a_rows)


GLU_SAMPLES = 2


def _glu_kernel(y_ref, x_ref, mod_ref, n2g_ref, w_ref, b_ref, rwt_ref, x1_ref, h2_ref, lg_ref, slab, y_scr, *,
                bsz):
    dm = D_MODEL
    half = pl.program_id(1)
    nchunk = TM // S5_TC

    @pl.when(half == 0)
    def _():
        for j in range(S5_LANE_VREGS):
            for gl in range(S5_TC):
                g = (j // 2) * S5_TC + gl
                rows = y_ref[g, :, 128 * (j % 2):128 * (j % 2) + 128].astype(F32)
                slab[j, pl.ds(gl, nchunk * bsz, stride=S5_TC), :] = rows

    for k in range(GLU_SAMPLES):
        b = half * GLU_SAMPLES + k

        def chunk(c, carry):
            s0 = pl.multiple_of((c * bsz + b) * S5_TC, S5_TC)
            p = [[slab[j, pl.ds(s0 + 8 * h, 8), :] for j in range(S5_LANE_VREGS)] for h in range(2)]
            q = _segment_transpose(p)
            r0 = pl.multiple_of(c * S5_TC, S5_TC)
            for h in range(2):
                for j in range(S5_LANE_VREGS):
                    y_scr[pl.ds(r0 + 8 * h, 8), 128 * j:128 * j + 128] = q[h][j]
            return carry

        lax.fori_loop(0, nchunk, chunk, 0, unroll=8)
        for rows in _row_halves():
            z = jax.nn.gelu(y_scr[rows]).astype(BF16)
            zz = jnp.dot(z, w_ref[...], preferred_element_type=F32) + b_ref[...]
            glu = zz[:, :dm] * jax.nn.sigmoid(zz[:, dm:])
            x1_ref[k, rows], h2_ref[k, rows], lg_ref[k, :, rows] = _post_mixer(
                glu, x_ref[k, rows], mod_ref[b, 2:3, :], mod_ref[b, 3:4, :], mod_ref[b, 4:5, :], n2g_ref[0],
                rwt_ref[...])


def _glu(y, xall, modtab, layer, norm_g, glu_w, glu_b, router_w):
    bsz, s, dm = xall.shape
    n = s - CTX_LEN
    ntl = n // TM
    assert bsz == 2 * GLU_SAMPLES
    gs = GLU_SAMPLES
    rows = TM // S5_TC * bsz
    ctx_tiles = CTX_LEN // TM
    tok = pl.BlockSpec((gs, TM, dm), lambda t, h: (h, t, 0))
    x1, h2, lg = pl.pallas_call(
        functools.partial(_glu_kernel, bsz=bsz),
        grid=(ntl, bsz // gs),
        in_specs=[pl.BlockSpec((S5_GROUPS, rows, S5_TC * S5_GROUP), lambda t, h: (0, t + ctx_tiles, 0)),
                  pl.BlockSpec((gs, TM, dm), lambda t, h: (h, t + ctx_tiles, 0)),
                  pl.BlockSpec((8, 6, dm), lambda t, h: (layer, 0, 0)),
                  pl.BlockSpec((1, 1, dm), lambda t, h: (0, 0, 0)),
                  pl.BlockSpec((dm, 2 * dm), lambda t, h: (0, 0)),
                  pl.BlockSpec((1, 2 * dm), lambda t, h: (0, 0)),
                  pl.BlockSpec((N_EXPERTS, dm), lambda t, h: (0, 0))],
        out_specs=[tok, tok, pl.BlockSpec((gs, N_EXPERTS, TM), lambda t, h: (h, 0, t))],
        out_shape=[jax.ShapeDtypeStruct((bsz, n, dm), F32),
                   jax.ShapeDtypeStruct((bsz, n, dm), BF16),
                   jax.ShapeDtypeStruct((bsz, N_EXPERTS, n), F32)],
        scratch_shapes=[pltpu.VMEM((S5_LANE_VREGS, TM * bsz, 128), F32), pltpu.VMEM((TM, dm), F32)],
        compiler_params=_cparams(("arbitrary", "arbitrary")),
        name="glu",
    )(y, xall, modtab.reshape(DEPTH * 8, 6, dm), norm_g.reshape(1, 1, dm), glu_w.astype(BF16), glu_b[None, :],
      router_w.T)
    return x1, h2, jnp.transpose(lg, (1, 0, 2)).reshape(N_EXPERTS, bsz * n)


def kernel(x, c, ctx, c_ctx, mod_w, mod_b, norm1_g, norm2_g, ar_w_in, ar_w_out, lru_conv_w, lru_conv_b, lru_lam, lru_r_w, lru_r_b, lru_i_w, lru_i_b, attn_q_g, attn_k_g, attn_lam_q1, attn_lam_k1, attn_lam_q2, attn_lam_k2, attn_subln_g, s5_a_re, s5_a_im, s5_log_step, s5_b_re, s5_b_im, s5_c_re, s5_c_im, s5_d, s5_glu_w, s5_glu_b, router_w, router_bias, exp_w_gate, exp_w_up, exp_w_down, sh_w_gate, sh_w_up, sh_w_down):
    bsz, n, dm = x.shape
    assert dm == D_MODEL and ctx.shape[1] == CTX_LEN == TM and n % TM == 0 and bsz < 8
    assert mod_w.shape[0] == DEPTH == 2
    s = CTX_LEN + n
    ntb = s // TM
    modtab = _modulation(c, c_ctx, mod_w, mod_b)

    gate, xa, q, k, v = _inproj(x, ctx, modtab, norm1_g[0], ar_w_in[0], attn_q_g[0], attn_k_g[0])
    lru_args = (lru_conv_w[0], lru_conv_b[0], lru_lam[0], lru_r_w[0], lru_r_b[0], lru_i_w[0], lru_i_b[0])
    hf = _lru(xa, *lru_args, reverse=False)
    hr = _lru(xa, *lru_args, reverse=True)
    lam_init = 0.8 - 0.6 * math.exp(-0.3 * 0)
    lam_vecs = jnp.stack([attn_lam_q1[0], attn_lam_k1[0], attn_lam_q2[0], attn_lam_k2[0]], axis=0)
    yb = _attention(q, k, v, lam_vecs, attn_subln_g[0], lam_init)
    x1, h2, lgt = _outproj(hf, hr, gate, yb, x, ctx, modtab, norm2_g[0], ar_w_out[0], router_w[0])
    row0 = lambda i: jnp.where(i % ntb == 0, bsz, i // ntb)
    xall = _moe(h2.reshape(bsz * s, dm), lgt, x1.reshape(bsz * s, dm), modtab, 0, row0, router_bias[0],
                exp_w_gate, exp_w_up, exp_w_down, sh_w_gate[0], sh_w_up[0], sh_w_down[0])
    xall = xall.reshape(bsz, s, dm)

    xt = _s5_pack(xall, modtab, 1, norm1_g[1])
    weights = _s5_weights(s5_a_re[0], s5_a_im[0], s5_log_step[0], s5_b_re[0], s5_b_im[0], s5_c_re[0],
                          s5_c_im[0], s5_d[0])
    y = _s5(xt, bsz, weights)
    x1, h2, lgt = _glu(y, xall, modtab, 1, norm2_g[1], s5_glu_w[0], s5_glu_b[0], router_w[1])
    ntl = n // TM
    row1 = lambda i: i // ntl
    out = _moe(h2.reshape(bsz * n, dm), lgt, x1.reshape(bsz * n, dm), modtab, 1, row1, router_bias[1],
               exp_w_gate, exp_w_up, exp_w_down, sh_w_gate[1], sh_w_up[1], sh_w_down[1])
    return out.reshape(bsz, n, dm)
```
